```python
import math
import jax, jax.numpy as jnp
from jax import lax
import numpy as np

D_MODEL = 1024
BATCH = 8
SEQ = 2048
DEPTH = 4
DEC_BATCH = 128
DEC_SEQ = 4
PAST_LEN = 8192
PAGE_SIZE = 128

MIX_WIDTH = D_MODEL
POOL_WIDTH = MIX_WIDTH // 4
HG_WIDTH = MIX_WIDTH // 4
ATT_WIDTH = MIX_WIDTH - POOL_WIDTH - HG_WIDTH
POOL_WINDOWS = (2, 4, 8, 16)
POOL_GROUPS = len(POOL_WINDOWS)
POOL_GW = POOL_WIDTH // POOL_GROUPS
POOL_BUF = max(POOL_WINDOWS) - 1
HG_HEADS = 4
HG_DK = HG_WIDTH // HG_HEADS
HG_DV = HG_WIDTH // HG_HEADS
HG_CHUNK = 16
HEAD_DIM = 64
ATT_HEADS = ATT_WIDTH // HEAD_DIM
KV_HEADS = 2
GQA_GROUP = ATT_HEADS // KV_HEADS
WINDOW = 128
REL_BUCKETS = 32
REL_MAX_DIST = 128
D_FF = 2816
N_MOD = 9
EPS = 1e-6
OFF_POOL = 0
OFF_HQ = OFF_POOL + POOL_WIDTH
OFF_HF = OFF_HQ + HG_WIDTH
OFF_HI = OFF_HF + HG_WIDTH
OFF_HG = OFF_HI + HG_WIDTH
OFF_AQ = OFF_HG + HG_WIDTH
OFF_AK = OFF_AQ + ATT_WIDTH
OFF_AV = OFF_AK + KV_HEADS * HEAD_DIM
IN_WIDTH = OFF_AV + KV_HEADS * HEAD_DIM

kernel_name = 'hybrid_pool_hgrn2_swa_macaron_adaln_step'


def _rmsnorm(x, g):
    xf = x.astype(jnp.float32)
    r = lax.rsqrt(jnp.mean(xf * xf, axis=-1, keepdims=True) + EPS)
    return (xf * r).astype(x.dtype) * g


def _modulate(h, shift, scale):
    return h * (1.0 + scale[:, None, :]) + shift[:, None, :]


def _swiglu(h, wg, wu, wd):
    return (jax.nn.silu(h @ wg) * (h @ wu)) @ wd


def _pool_mixer(u, prefix, pos0, pool_w, pool_scale):
    B, T, _ = u.shape
    P = POOL_BUF
    ext = jnp.concatenate([prefix, u], axis=1).astype(jnp.float32)
    cs = jnp.concatenate([jnp.zeros((B, 1, POOL_WIDTH), jnp.float32), jnp.cumsum(ext, axis=1)], axis=1)
    pos = pos0 + jnp.arange(T)
    means = []
    for gi, w in enumerate(POOL_WINDOWS):
        sl = slice(gi * POOL_GW, (gi + 1) * POOL_GW)
        win_sum = cs[:, P + 1:P + T + 1, sl] - cs[:, P + 1 - w:P + T + 1 - w, sl]
        cnt = jnp.minimum(pos + 1, w).astype(jnp.float32)[None, :, None]
        means.append(win_sum / cnt)
    d = (jnp.concatenate(means, axis=-1) - ext[:, P:]).reshape(B, T, POOL_GROUPS, POOL_GW)
    y = jnp.einsum('btgc,gcd->btgd', d, pool_w.astype(jnp.float32)).reshape(B, T, POOL_WIDTH)
    y = y * pool_scale.astype(jnp.float32)
    return y.astype(u.dtype), ext[:, -P:].astype(u.dtype)


def _hgrn2(q, k, v, logf, S0):
    B, T, H, K = q.shape
    C = HG_CHUNK if T % HG_CHUNK == 0 else T
    n = T // C

    def blk(a):
        return a.reshape(B, n, C, *a.shape[2:]).swapaxes(0, 1)

    q, k, v, logf = blk(q), blk(k), blk(v), blk(logf)
    b = jnp.cumsum(logf, axis=2)
    causal = jnp.tril(jnp.ones((C, C), dtype=bool))
    diff = b[:, :, :, None] - b[:, :, None, :]
    decay = jnp.exp(jnp.where(causal[None, None, :, :, None, None], diff, -jnp.inf))
    A = jnp.einsum('nbthk,nbshk,nbtshk->nbtsh', q, k, decay)
    o_intra = jnp.einsum('nbtsh,nbshv->nbthv', A, v)
    b_last = b[:, :, -1]
    q_in = q * jnp.exp(b)
    k_out = k * jnp.exp(b_last[:, :, None] - b)
    g_last = jnp.exp(b_last)

    def step(S, xs):
        qi, ko, vc, gl = xs
        o = jnp.einsum('bthk,bhkv->bthv', qi, S)
        S = gl[..., None] * S + jnp.einsum('bthk,bthv->bhkv', ko, vc)
        return S, o

    S, o_inter = lax.scan(step, S0, (q_in, k_out, v, g_last))
    o = (o_intra + o_inter).swapaxes(0, 1).reshape(B, T, H, v.shape[-1])
    return o, S


def _t5_bucket(rel):
    n = jnp.maximum(-rel, 0)
    exact = REL_BUCKETS // 2
    nf = jnp.maximum(n, 1).astype(jnp.float32)
    large = exact + (jnp.log(nf / exact) / math.log(REL_MAX_DIST / exact) * (REL_BUCKETS - exact)).astype(jnp.int32)
    large = jnp.minimum(large, REL_BUCKETS - 1)
    return jnp.where(n < exact, n, large)


def _swa_core(q, k, v, qp, kp, sinks, rel_bias):
    N, Tq = qp.shape
    Tk = kp.shape[1]
    logits = jnp.einsum('bnqhgd,bnkhd->bnhgqk', q.astype(jnp.float32), k.astype(jnp.float32)) * (HEAD_DIM ** -0.5)
    rel = kp[:, None, :] - qp[:, :, None]
    bias = rel_bias.astype(jnp.float32)[_t5_bucket(rel)]
    bias = bias.reshape(N, Tq, Tk, KV_HEADS, GQA_GROUP).transpose(0, 3, 4, 1, 2)
    valid = (rel <= 0) & (rel > -WINDOW) & (kp >= 0)[:, None, :]
    logits = jnp.where(valid[None, :, None, None], logits + bias[None], -jnp.inf)
    sink = sinks.astype(jnp.float32).reshape(KV_HEADS, GQA_GROUP)[None, None, :, :, None]
    m = jnp.maximum(jnp.max(logits, axis=-1), sink)
    p = jnp.exp(logits - m[..., None])
    denom = jnp.sum(p, axis=-1) + jnp.exp(sink - m)
    probs = p / denom[..., None]
    return jnp.einsum('bnhgqk,bnkhd->bnqhgd', probs, v.astype(jnp.float32))


def _swa(q, k, v, k_buf, v_buf, pos0, sinks, rel_bias):
    B, T = q.shape[:2]
    P = WINDOW
    k_ext = jnp.concatenate([k_buf, k], axis=1)
    v_ext = jnp.concatenate([v_buf, v], axis=1)
    k_pos = pos0 - P + jnp.arange(P + T)
    q_pos = pos0 + jnp.arange(T)
    if T % WINDOW == 0:
        nb = T // WINDOW
        qb = q.reshape(B, nb, WINDOW, KV_HEADS, GQA_GROUP, HEAD_DIM)
        kb = k_ext.reshape(B, nb + 1, WINDOW, KV_HEADS, HEAD_DIM)
        vb = v_ext.reshape(B, nb + 1, WINDOW, KV_HEADS, HEAD_DIM)
        kband = jnp.concatenate([kb[:, :-1], kb[:, 1:]], axis=2)
        vband = jnp.concatenate([vb[:, :-1], vb[:, 1:]], axis=2)
        qp = q_pos.reshape(nb, WINDOW)
        kpb = k_pos.reshape(nb + 1, WINDOW)
        kp = jnp.concatenate([kpb[:-1], kpb[1:]], axis=1)
    else:
        qb = q.reshape(B, 1, T, KV_HEADS, GQA_GROUP, HEAD_DIM)
        kband = k_ext[:, None]
        vband = v_ext[:, None]
        qp = q_pos[None]
        kp = k_pos[None]
    out = _swa_core(qb, kband, vband, qp, kp, sinks, rel_bias)
    return out.reshape(B, T, ATT_WIDTH), k_ext[:, -P:], v_ext[:, -P:]


def _mixers(h, pool_buf, hg_state, k_buf, v_buf, pos0, lb, w_in, w_out, pool_w, pool_scale, hg_norm, sinks, rel_bias):
    B, T, _ = h.shape
    f32 = jnp.float32
    z = h @ w_in
    y_pool, new_pool = _pool_mixer(z[..., OFF_POOL:OFF_HQ], pool_buf, pos0, pool_w, pool_scale)
    hs = (B, T, HG_HEADS, HG_DK)
    hq = jax.nn.silu(z[..., OFF_HQ:OFF_HF].astype(f32)).reshape(hs)
    f = lb + (1.0 - lb) * jax.nn.sigmoid(z[..., OFF_HF:OFF_HI].astype(f32))
    logf = jnp.log(f).reshape(hs)
    hk = (1.0 - f).reshape(hs)
    hv = z[..., OFF_HI:OFF_HG].astype(f32).reshape(B, T, HG_HEADS, HG_DV)
    o, new_S = _hgrn2(hq, hk, hv, logf, hg_state.astype(f32))
    o = _rmsnorm(o, hg_norm.astype(f32)) * jax.nn.silu(z[..., OFF_HG:OFF_AQ].astype(f32).reshape(B, T, HG_HEADS, HG_DV))
    y_hg = o.reshape(B, T, HG_WIDTH).astype(h.dtype)
    aq = z[..., OFF_AQ:OFF_AK].reshape(B, T, ATT_HEADS, HEAD_DIM)
    ak = z[..., OFF_AK:OFF_AV].reshape(B, T, KV_HEADS, HEAD_DIM)
    av = z[..., OFF_AV:IN_WIDTH].reshape(B, T, KV_HEADS, HEAD_DIM)
    y_att, new_k, new_v = _swa(aq, ak, av, k_buf, v_buf, pos0, sinks, rel_bias)
    y = jnp.concatenate([y_pool, y_hg, y_att.astype(h.dtype)], axis=-1) @ w_out
    return y, new_pool, new_S.astype(hg_state.dtype), new_k, new_v


def _layer(x, c, pool_buf, hg_state, k_buf, v_buf, pos0, lb, rel_bias,
           n1, n2, n3, w_mod, b_mod, f1g, f1u, f1d, w_in, w_out, pool_w, pool_scale,
           hg_norm, sinks, f2g, f2u, f2d):
    B = x.shape[0]
    mod = (jax.nn.silu(c) @ w_mod + b_mod).reshape(B, N_MOD, D_MODEL)
    h = _modulate(_rmsnorm(x, n1), mod[:, 0], mod[:, 1])
    x = x + 0.5 * mod[:, 2][:, None, :] * _swiglu(h, f1g, f1u, f1d)
    h = _modulate(_rmsnorm(x, n2), mod[:, 3], mod[:, 4])
    y, new_pool, new_S, new_k, new_v = _mixers(h, pool_buf, hg_state, k_buf, v_buf, pos0, lb,
                                                w_in, w_out, pool_w, pool_scale, hg_norm, sinks, rel_bias)
    x = x + mod[:, 5][:, None, :] * y
    h = _modulate(_rmsnorm(x, n3), mod[:, 6], mod[:, 7])
    x = x + 0.5 * mod[:, 8][:, None, :] * _swiglu(h, f2g, f2u, f2d)
    return x, new_pool, new_S, new_k, new_v


def setup_inputs(seed: int = 0) -> dict:
    key = jax.random.key(seed)
    ks = jax.random.split(key, 32)
    f32 = jnp.float32
    D = D_MODEL

    def nrm(k, shape, scale):
        return jax.random.normal(k, shape, f32) * scale

    def gain(k, shape):
        return 1.0 + 0.05 * jax.random.normal(k, shape, f32)

    return {
        'x_prompt': nrm(ks[0], (BATCH, SEQ, D), 1.0),
        'x_sample': nrm(ks[1], (DEC_BATCH, DEC_SEQ, D), 1.0),
        'c_prompt': nrm(ks[2], (BATCH, D), 1.0),
        'c_sample': nrm(ks[3], (DEC_BATCH, D), 1.0),
        'state_pool': nrm(ks[4], (DEPTH, DEC_BATCH, POOL_BUF, POOL_WIDTH), 1.0),
        'state_hgrn': nrm(ks[5], (DEPTH, DEC_BATCH, HG_HEADS, HG_DK, HG_DV), 0.5),
        'cache_k_win': nrm(ks[6], (DEPTH, DEC_BATCH, WINDOW, KV_HEADS, HEAD_DIM), 1.0),
        'cache_v_win': nrm(ks[7], (DEPTH, DEC_BATCH, WINDOW, KV_HEADS, HEAD_DIM), 1.0),
        'norm_ffn1': gain(ks[8], (DEPTH, D)),
        'norm_mix': gain(ks[9], (DEPTH, D)),
        'norm_ffn2': gain(ks[10], (DEPTH, D)),
        'w_mod': nrm(ks[11], (DEPTH, D, N_MOD * D), D ** -0.5),
        'b_mod': nrm(ks[12], (DEPTH, N_MOD * D), 0.02),
        'ffn1_w_gate': nrm(ks[13], (DEPTH, D, D_FF), D ** -0.5),
        'ffn1_w_up': nrm(ks[14], (DEPTH, D, D_FF), D ** -0.5),
        'ffn1_w_down': nrm(ks[15], (DEPTH, D_FF, D), D_FF ** -0.5),
        'w_in': nrm(ks[16], (DEPTH, D, IN_WIDTH), D ** -0.5),
        'w_out': nrm(ks[17], (DEPTH, MIX_WIDTH, D), MIX_WIDTH ** -0.5),
        'pool_w': nrm(ks[18], (DEPTH, POOL_GROUPS, POOL_GW, POOL_GW), POOL_GW ** -0.5),
        'pool_scale': gain(ks[19], (DEPTH, POOL_WIDTH)),
        'hgrn_lower': nrm(ks[20], (DEPTH, HG_WIDTH), 0.1),
        'hgrn_norm': gain(ks[21], (DEPTH, HG_DV)),
        'attn_sinks': nrm(ks[22], (DEPTH, ATT_HEADS), 0.5),
        'rel_bias': nrm(ks[23], (REL_BUCKETS, ATT_HEADS), 0.5),
        'ffn2_w_gate': nrm(ks[24], (DEPTH, D, D_FF), D ** -0.5),
        'ffn2_w_up': nrm(ks[25], (DEPTH, D, D_FF), D ** -0.5),
        'ffn2_w_down': nrm(ks[26], (DEPTH, D_FF, D), D_FF ** -0.5),
        'norm_final': gain(ks[27], (D,)),
    }


def reference(x_prompt, x_sample, c_prompt, c_sample, state_pool, state_hgrn, cache_k_win, cache_v_win,
              norm_ffn1, norm_mix, norm_ffn2, w_mod, b_mod, ffn1_w_gate, ffn1_w_up, ffn1_w_down,
              w_in, w_out, pool_w, pool_scale, hgrn_lower, hgrn_norm, attn_sinks, rel_bias,
              ffn2_w_gate, ffn2_w_up, ffn2_w_down, norm_final):
    lb_sm = jax.nn.softmax(hgrn_lower.astype(jnp.float32), axis=0)
    lbs = jnp.cumsum(lb_sm, axis=0) - lb_sm[0]

    xp, xs = x_prompt, x_sample
    dt = x_prompt.dtype
    zp_pool = jnp.zeros((BATCH, POOL_BUF, POOL_WIDTH), dt)
    zp_hg = jnp.zeros((BATCH, HG_HEADS, HG_DK, HG_DV), dt)
    zp_kv = jnp.zeros((BATCH, WINDOW, KV_HEADS, HEAD_DIM), dt)
    pool_p, hg_p, k_p, v_p = [], [], [], []
    pool_s, hg_s, k_s, v_s = [], [], [], []
    for l in range(DEPTH):
        w = (norm_ffn1[l], norm_mix[l], norm_ffn2[l], w_mod[l], b_mod[l],
             ffn1_w_gate[l], ffn1_w_up[l], ffn1_w_down[l], w_in[l], w_out[l],
             pool_w[l], pool_scale[l], hgrn_norm[l], attn_sinks[l],
             ffn2_w_gate[l], ffn2_w_up[l], ffn2_w_down[l])
        xp, a, b, c, d = _layer(xp, c_prompt, zp_pool, zp_hg, zp_kv, zp_kv, 0, lbs[l], rel_bias, *w)
        pool_p.append(a); hg_p.append(b); k_p.append(c); v_p.append(d)
        xs, a, b, c, d = _layer(xs, c_sample, state_pool[l], state_hgrn[l], cache_k_win[l], cache_v_win[l],
                                PAST_LEN, lbs[l], rel_bias, *w)
        pool_s.append(a); hg_s.append(b); k_s.append(c); v_s.append(d)

    y_prompt = _rmsnorm(xp, norm_final)
    y_sample = _rmsnorm(xs, norm_final)
    new_pool_p = jnp.stack(pool_p)
    new_hgrn_p = jnp.stack(hg_p)
    new_k_p = jnp.stack(k_p)
    new_v_p = jnp.stack(v_p)
    new_pool_s = jnp.stack(pool_s)
    new_hgrn_s = jnp.stack(hg_s)
    new_k_s = jnp.stack(k_s)
    new_v_s = jnp.stack(v_s)
    return (y_prompt, y_sample, new_pool_p, new_hgrn_p, new_k_p, new_v_p, new_pool_s, new_hgrn_s, new_k_s, new_v_s)
```

```python
import functools
import math

import numpy as np
import jax
import jax.numpy as jnp
from jax import lax
from jax.experimental import pallas as pl
from jax.experimental.pallas import tpu as pltpu

F32 = jnp.float32
BF16 = jnp.bfloat16

D_MODEL = 1024
DEPTH = 4
N_MOD = 9
D_FF = 2816
EPS = 1e-6
POOL_WIDTH = 256
POOL_WINDOWS = (2, 4, 8, 16)
POOL_GW = 64
POOL_BUF = 15
HG_WIDTH = 256
HG_HEADS = 4
HG_DK = 64
ATT_WIDTH = 512
ATT_HEADS = 8
KV_HEADS = 2
GQA_GROUP = 4
HEAD_DIM = 64
WINDOW = 128
REL_BUCKETS = 32
REL_MAX_DIST = 128
OFF_POOL, OFF_HQ, OFF_HF, OFF_HI, OFF_HG, OFF_AQ, OFF_AK, OFF_AV = 0, 256, 512, 768, 1024, 1280, 1792, 1920
IN_WIDTH = 2048
NEG = -1e30

VMEM_LIMIT = 56 * 1024 * 1024


def _dot(a, b):
    return jnp.dot(a, b, preferred_element_type=F32)


def _dot_nt(a, b):
    return lax.dot_general(a, b, (((1,), (1,)), ((), ())), preferred_element_type=F32)


def _dot_tn(a, b):
    return lax.dot_general(a, b, (((0,), (0,)), ((), ())), preferred_element_type=F32)


def _sigmoid(x):
    return 1.0 / (1.0 + jnp.exp(-x))


def _silu(x):
    return x * _sigmoid(x)


def _norm_mod(x, g, shift, scale):
    r = lax.rsqrt(jnp.mean(x * x, axis=-1, keepdims=True) + EPS)
    return (x * r) * g * (1.0 + scale) + shift


def _split3(x):
    hi = x.astype(BF16)
    r1 = x - hi.astype(F32)
    mid = r1.astype(BF16)
    lo = (r1 - mid.astype(F32)).astype(BF16)
    return hi, mid, lo


def _mod_kernel(c_ref, w_ref, b_ref, o_ref):
    sc = _silu(c_ref[...]).astype(BF16)
    o_ref[...] = _dot(sc, w_ref[...].astype(BF16)) + b_ref[...]


def _modulation(c_all, w_mod, b_mod):
    rows = c_all.shape[0]
    return pl.pallas_call(
        _mod_kernel,
        grid=(DEPTH, N_MOD),
        in_specs=[
            pl.BlockSpec((rows, D_MODEL), lambda l, j: (0, 0)),
            pl.BlockSpec((None, D_MODEL, D_MODEL), lambda l, j: (l, 0, j)),
            pl.BlockSpec((None, None, 1, D_MODEL), lambda l, j: (l, j, 0, 0)),
        ],
        out_specs=pl.BlockSpec((None, None, rows, D_MODEL), lambda l, j: (l, j, 0, 0)),
        out_shape=jax.ShapeDtypeStruct((DEPTH, N_MOD, rows, D_MODEL), F32),
        compiler_params=pltpu.CompilerParams(dimension_semantics=("arbitrary", "arbitrary"),
                                             vmem_limit_bytes=VMEM_LIMIT),
        name="adaln_mod",
    )(c_all, w_mod, b_mod.reshape(DEPTH, N_MOD, 1, D_MODEL))


def _mod_rows(mod_ref, j, row, per_row):
    if per_row:
        return mod_ref[j]
    return mod_ref[j, pl.ds(row, 1), :]


def _ffn_kernel(x_ref, mod_ref, n_ref, wg_ref, wu_ref, wd_ref, o_ref, *, j0, per_row, tiles_per_seq):
    row = pl.program_id(0) // tiles_per_seq
    shift = _mod_rows(mod_ref, j0, row, per_row)
    scale = _mod_rows(mod_ref, j0 + 1, row, per_row)
    gate = _mod_rows(mod_ref, j0 + 2, row, per_row)
    x = x_ref[...]
    h = _norm_mod(x, n_ref[...], shift, scale).astype(BF16)
    g = _dot(h, wg_ref[...])
    u = _dot(h, wu_ref[...])
    a = (_silu(g) * u).astype(BF16)
    o_ref[...] = x + (0.5 * gate) * _dot(a, wd_ref[...])


def _ffn(x, mod, norm, wg, wu, wd, layer, j0, *, tm, per_row, mod_rows, mod_row_block):
    rows = x.shape[0]
    tiles_per_seq = 1 if per_row else (rows // mod_rows) // tm
    kern = functools.partial(_ffn_kernel, j0=j0, per_row=per_row, tiles_per_seq=tiles_per_seq)
    return pl.pallas_call(
        kern,
        grid=(rows // tm,),
        in_specs=[
            pl.BlockSpec((tm, D_MODEL), lambda i: (i, 0)),
            pl.BlockSpec((None, N_MOD, mod_rows, D_MODEL), lambda i: (layer, 0, mod_row_block, 0)),
            pl.BlockSpec((None, 1, D_MODEL), lambda i: (layer, 0, 0)),
            pl.BlockSpec((None, D_MODEL, D_FF), lambda i: (layer, 0, 0)),
            pl.BlockSpec((None, D_MODEL, D_FF), lambda i: (layer, 0, 0)),
            pl.BlockSpec((None, D_FF, D_MODEL), lambda i: (layer, 0, 0)),
        ],
        out_specs=pl.BlockSpec((tm, D_MODEL), lambda i: (i, 0)),
        out_shape=jax.ShapeDtypeStruct((rows, D_MODEL), F32),
        compiler_params=pltpu.CompilerParams(dimension_semantics=("arbitrary",), vmem_limit_bytes=VMEM_LIMIT),
        name="swiglu_half_step",
    )(x, mod, norm.reshape(DEPTH, 1, D_MODEL), wg, wu, wd)


def _final_norm_kernel(x_ref, g_ref, o_ref):
    x = x_ref[...]
    r = lax.rsqrt(jnp.mean(x * x, axis=-1, keepdims=True) + EPS)
    o_ref[...] = (x * r) * g_ref[...]


def _final_norm(x, g, tm):
    rows = x.shape[0]
    return pl.pallas_call(
        _final_norm_kernel,
        grid=(rows // tm,),
        in_specs=[pl.BlockSpec((tm, D_MODEL), lambda i: (i, 0)), pl.BlockSpec((1, D_MODEL), lambda i: (0, 0))],
        out_specs=pl.BlockSpec((tm, D_MODEL), lambda i: (i, 0)),
        out_shape=jax.ShapeDtypeStruct((rows, D_MODEL), F32),
        compiler_params=pltpu.CompilerParams(dimension_semantics=("arbitrary",)),
        name="final_rmsnorm",
    )(x, g.reshape(1, D_MODEL))


def _lower_bound_kernel(x_ref, o_ref):
    rows = [x_ref[l:l + 1, :] for l in range(DEPTH)]
    m = functools.reduce(jnp.maximum, rows)
    e = [jnp.exp(r - m) for r in rows]
    s = functools.reduce(lambda a, b: a + b, e)
    sm = [ei / s for ei in e]
    acc = sm[0]
    for l in range(DEPTH):
        if l > 0:
            acc = acc + sm[l]
        o_ref[l:l + 1, :] = acc - sm[0]


def _lower_bounds(hgrn_lower):
    return pl.pallas_call(
        _lower_bound_kernel,
        out_shape=jax.ShapeDtypeStruct((DEPTH, HG_WIDTH), F32),
        name="hgrn_lower_bounds",
    )(hgrn_lower)


def _bucket_map(rows, cols, valid_rows, col_lo, col_hi):
    r = np.arange(rows)[:, None]
    c = np.arange(cols)[None, :]
    rel = c - WINDOW - r
    n = np.maximum(-rel, 0)
    exact = REL_BUCKETS // 2
    nf = np.maximum(n, 1).astype(np.float32)
    large = exact + (np.log(nf / np.float32(exact)) / np.float32(math.log(REL_MAX_DIST / exact))
                     * np.float32(REL_BUCKETS - exact)).astype(np.int32)
    large = np.minimum(large, REL_BUCKETS - 1)
    bucket = np.where(n < exact, n, large)
    valid = (rel <= 0) & (rel > -WINDOW) & (r < valid_rows) & (c >= col_lo) & (c < col_hi)
    return np.where(valid, bucket, -1).astype(np.int32)


def _bias_table_kernel(bkt_ref, rb_ref, o_ref):
    h = pl.program_id(1)
    bkt = bkt_ref[...]
    acc = jnp.full(bkt.shape, NEG, F32)
    for b in range(REL_BUCKETS):
        acc = jnp.where(bkt == b, rb_ref[b, h], acc)
    o_ref[...] = acc


def _bias_tables(bucket_maps, rel_bias):
    m, r, c = bucket_maps.shape
    return pl.pallas_call(
        _bias_table_kernel,
        grid=(m, ATT_HEADS),
        in_specs=[pl.BlockSpec((None, r, c), lambda i, h: (i, 0, 0)),
                  pl.BlockSpec(memory_space=pltpu.SMEM)],
        out_specs=pl.BlockSpec((None, None, r, c), lambda i, h: (i, h, 0, 0)),
        out_shape=jax.ShapeDtypeStruct((m, ATT_HEADS, r, c), F32),
        compiler_params=pltpu.CompilerParams(dimension_semantics=("arbitrary", "arbitrary")),
        name="rel_bias_tables",
    )(bucket_maps, rel_bias)


def _block_ones(n, blk):
    i = np.arange(n)
    return (i[:, None] // blk == i[None, :] // blk)


def _mixer_consts(tt):
    tri = jnp.asarray(np.tril(np.ones((tt, tt), np.float32)), BF16)
    blk = _block_ones(HG_WIDTH, HG_DK)
    return tri, jnp.asarray(blk, BF16), jnp.asarray(blk, F32)


def _pool_block_diag(pool_w):
    out = jnp.zeros((DEPTH, POOL_WIDTH, POOL_WIDTH), BF16)
    for g in range(len(POOL_WINDOWS)):
        sl = slice(g * POOL_GW, (g + 1) * POOL_GW)
        out = out.at[:, sl, sl].set(pool_w[:, g].astype(BF16))
    return out


def _head_masks():
    lane = lax.broadcasted_iota(jnp.int32, (1, HG_WIDTH), 1)
    return [jnp.where((lane >= HG_DK * h) & (lane < HG_DK * (h + 1)), 1.0, 0.0).astype(F32) for h in range(HG_HEADS)]


def _prompt_mixer_kernel(x_ref, mod_ref, n_ref, win_ref, wout_ref, pbd_ref, pscale_ref, lb_ref, hgn_ref,
                         bm_ref, sink_ref, tri_ref, ones_ref, bdm_ref,
                         xo_ref, pool_o, s_o, k_o, v_o,
                         z_ref, ubuf, kbuf, vbuf, st_ref, oacc, cat_ref, *, layer, tt):
    b = pl.program_id(0)
    t = pl.program_id(1)
    last = pl.num_programs(1) - 1

    @pl.when(t == 0)
    def _():
        ubuf[0:16, :] = jnp.zeros((16, POOL_WIDTH), F32)
        kbuf[:, 0:WINDOW, :] = jnp.zeros((KV_HEADS, WINDOW, HEAD_DIM), BF16)
        vbuf[:, 0:WINDOW, :] = jnp.zeros((KV_HEADS, WINDOW, HEAD_DIM), BF16)
        st_ref[...] = jnp.zeros((HG_WIDTH, HG_WIDTH), F32)

    x = x_ref[...]
    shift = mod_ref[3, pl.ds(b, 1), :]
    scale = mod_ref[4, pl.ds(b, 1), :]
    gate = mod_ref[5, pl.ds(b, 1), :]
    h = _norm_mod(x, n_ref[...], shift, scale).astype(BF16)
    z_ref[...] = _dot(h, win_ref[...])

    lane = lax.broadcasted_iota(jnp.int32, (1, POOL_WIDTH), 1)

    u = z_ref[:, OFF_POOL:OFF_POOL + POOL_WIDTH]
    ubuf[16:16 + tt, :] = u
    e = ubuf[...]
    s2 = e + pltpu.roll(e, 1, 0)
    s4 = s2 + pltpu.roll(s2, 2, 0)
    s8 = s4 + pltpu.roll(s4, 4, 0)
    s16 = s8 + pltpu.roll(s8, 8, 0)
    sel = jnp.where(lane < 64, s2, jnp.where(lane < 128, s4, jnp.where(lane < 192, s8, s16)))[16:]
    wl = jnp.where(lane < 64, 2.0, jnp.where(lane < 128, 4.0, jnp.where(lane < 192, 8.0, 16.0))).astype(F32)
    pos1 = (t * tt + 1 + lax.broadcasted_iota(jnp.int32, (tt, 1), 0)).astype(F32)
    dpool = sel / jnp.minimum(pos1, wl) - u
    ypool = _dot(dpool.astype(BF16), pbd_ref[...]) * pscale_ref[...]
    cat_ref[:, 0:POOL_WIDTH] = ypool.astype(BF16)
    ubuf[0:16, :] = ubuf[tt:tt + 16, :]

    @pl.when(t == last)
    def _():
        pool_o[...] = ubuf[pl.ds(1, POOL_BUF), :]

    lb = lb_ref[...]
    q = _silu(z_ref[:, OFF_HQ:OFF_HQ + HG_WIDTH])
    f = lb + (1.0 - lb) * _sigmoid(z_ref[:, OFF_HF:OFF_HF + HG_WIDTH])
    k = 1.0 - f
    v = z_ref[:, OFF_HI:OFF_HI + HG_WIDTH]
    vb = v.astype(BF16)
    tri = tri_ref[...]
    hi, mid, lo = _split3(jnp.log(f))
    bcum = _dot(tri, hi) + _dot(tri, mid) + _dot(tri, lo)
    ones_blk = ones_ref[...]
    hm = _head_masks()

    n16 = tt // 16
    q3 = q.reshape(n16, 16, HG_WIDTH)
    b3 = bcum.reshape(n16, 16, HG_WIDTH)
    c3 = (bcum - jnp.log(k)).reshape(n16, 16, HG_WIDTH)
    v3 = v.reshape(n16, 16, HG_WIDTH)
    rowi = lax.broadcasted_iota(jnp.int32, (1, 16, 1), 1)
    o3 = jnp.zeros((n16, 16, HG_WIDTH), F32)
    for s in range(16):
        xs = q3 * jnp.exp(b3 - c3[:, s:s + 1, :])
        xs = jnp.where(rowi >= s, xs, 0.0)
        r = _dot(xs.reshape(tt, HG_WIDTH).astype(BF16), ones_blk)
        o3 = o3 + r.reshape(n16, 16, HG_WIDTH) * v3[:, s:s + 1, :]
    oacc[...] = o3.reshape(tt, HG_WIDTH)

    blk = 32
    while blk <= tt:
        half = blk // 2
        for i in range(tt // blk):
            lo_s, mid_s, hi_s = i * blk, i * blk + half, (i + 1) * blk
            ref = bcum[mid_s - 1:mid_s, :]
            qt = q[mid_s:hi_s] * jnp.exp(bcum[mid_s:hi_s] - ref)
            kt = (k[lo_s:mid_s] * jnp.exp(ref - bcum[lo_s:mid_s])).astype(BF16)
            qs = jnp.concatenate([qt * hm[hh] for hh in range(HG_HEADS)], axis=0).astype(BF16)
            a = _dot_nt(qs, kt).astype(BF16)
            ov = _dot(a, vb[lo_s:mid_s])
            oi = ov[0:half] * hm[0]
            for hh in range(1, HG_HEADS):
                oi = oi + ov[hh * half:(hh + 1) * half] * hm[hh]
            oacc[mid_s:hi_s, :] += oi
        blk *= 2

    st = st_ref[...]
    qin = (q * jnp.exp(bcum)).astype(BF16)
    o_tot = oacc[...] + _dot_nt(qin, st.astype(BF16))
    blast = bcum[tt - 1:tt, :]
    kout = (k * jnp.exp(blast - bcum)).astype(BF16)
    st_new = st * jnp.exp(blast) + _dot_tn(vb, kout) * bdm_ref[...]
    st_ref[...] = st_new

    @pl.when(t == last)
    def _():
        s_kv = st_new.T
        for hh in range(HG_HEADS):
            s_o[hh] = s_kv[HG_DK * hh:HG_DK * (hh + 1), HG_DK * hh:HG_DK * (hh + 1)]

    sq_hi, sq_mid, _ = _split3(o_tot * o_tot)
    ms = (_dot(sq_hi, ones_blk) + _dot(sq_mid, ones_blk)) * (1.0 / HG_DK)
    y_hg = o_tot * lax.rsqrt(ms + EPS) * hgn_ref[...] * _silu(z_ref[:, OFF_HG:OFF_HG + HG_WIDTH])
    cat_ref[:, POOL_WIDTH:POOL_WIDTH + HG_WIDTH] = y_hg.astype(BF16)

    for g in range(KV_HEADS):
        kbuf[g, WINDOW:WINDOW + tt, :] = z_ref[:, OFF_AK + HEAD_DIM * g:OFF_AK + HEAD_DIM * (g + 1)].astype(BF16)
        vbuf[g, WINDOW:WINDOW + tt, :] = z_ref[:, OFF_AV + HEAD_DIM * g:OFF_AV + HEAD_DIM * (g + 1)].astype(BF16)
    first = jnp.where(t == 0, 1, 0)
    for j in range(tt // WINDOW):
        r0 = j * WINDOW
        for g in range(KV_HEADS):
            kg = kbuf[g, r0:r0 + 2 * WINDOW, :]
            vg = vbuf[g, r0:r0 + 2 * WINDOW, :]
            for i in range(GQA_GROUP):
                hd = g * GQA_GROUP + i
                c0 = OFF_AQ + HEAD_DIM * hd
                qh = (z_ref[r0:r0 + WINDOW, c0:c0 + HEAD_DIM] * (HEAD_DIM ** -0.5)).astype(BF16)
                bm = bm_ref[first, hd] if j == 0 else bm_ref[0, hd]
                lg = _dot_nt(qh, kg) + bm
                sink = sink_ref[layer, hd]
                m = jnp.maximum(jnp.max(lg, axis=-1, keepdims=True), sink)
                p = jnp.exp(lg - m)
                den = jnp.sum(p, axis=-1, keepdims=True) + jnp.exp(sink - m)
                oh = _dot(p.astype(BF16), vg) / den
                cc = POOL_WIDTH + HG_WIDTH + HEAD_DIM * hd
                cat_ref[r0:r0 + WINDOW, cc:cc + HEAD_DIM] = oh.astype(BF16)
    for g in range(KV_HEADS):
        kbuf[g, 0:WINDOW, :] = kbuf[g, tt:tt + WINDOW, :]
        vbuf[g, 0:WINDOW, :] = vbuf[g, tt:tt + WINDOW, :]

    @pl.when(t == last)
    def _():
        k_o[...] = z_ref[tt - WINDOW:tt, OFF_AK:OFF_AV]
        v_o[...] = z_ref[tt - WINDOW:tt, OFF_AV:IN_WIDTH]

    xo_ref[...] = x + gate * _dot(cat_ref[...], wout_ref[...])


def _prompt_mixer(x, mod, norm, w_in, w_out, pool_bd, pool_scale, lbs, hg_norm, bm, sinks, consts, layer, *,
                  tt, mod_row_block):
    bsz, seq, _ = x.shape
    tri, ones_blk, bdm = consts
    kern = functools.partial(_prompt_mixer_kernel, layer=layer, tt=tt)
    full = lambda *shape: pl.BlockSpec(shape, lambda b, t: (0,) * len(shape))
    lsel = lambda *shape: pl.BlockSpec((None,) + shape, lambda b, t: (layer,) + (0,) * len(shape))
    return pl.pallas_call(
        kern,
        grid=(bsz, seq // tt),
        in_specs=[
            pl.BlockSpec((None, tt, D_MODEL), lambda b, t: (b, t, 0)),
            pl.BlockSpec((None, N_MOD, 8, D_MODEL), lambda b, t: (layer, 0, mod_row_block, 0)),
            lsel(1, D_MODEL),
            lsel(D_MODEL, IN_WIDTH),
            lsel(D_MODEL, D_MODEL),
            lsel(POOL_WIDTH, POOL_WIDTH),
            lsel(1, POOL_WIDTH),
            lsel(1, HG_WIDTH),
            lsel(1, HG_WIDTH),
            full(2, ATT_HEADS, WINDOW, 2 * WINDOW),
            pl.BlockSpec(memory_space=pltpu.SMEM),
            full(tt, tt),
            full(HG_WIDTH, HG_WIDTH),
            full(HG_WIDTH, HG_WIDTH),
        ],
        out_specs=[
            pl.BlockSpec((None, tt, D_MODEL), lambda b, t: (b, t, 0)),
            pl.BlockSpec((None, POOL_BUF, POOL_WIDTH), lambda b, t: (b, 0, 0)),
            pl.BlockSpec((None, HG_HEADS, HG_DK, HG_DK), lambda b, t: (b, 0, 0, 0)),
            pl.BlockSpec((None, WINDOW, KV_HEADS * HEAD_DIM), lambda b, t: (b, 0, 0)),
            pl.BlockSpec((None, WINDOW, KV_HEADS * HEAD_DIM), lambda b, t: (b, 0, 0)),
        ],
        out_shape=[
            jax.ShapeDtypeStruct((bsz, seq, D_MODEL), F32),
            jax.ShapeDtypeStruct((bsz, POOL_BUF, POOL_WIDTH), F32),
            jax.ShapeDtypeStruct((bsz, HG_HEADS, HG_DK, HG_DK), F32),
            jax.ShapeDtypeStruct((bsz, WINDOW, KV_HEADS * HEAD_DIM), F32),
            jax.ShapeDtypeStruct((bsz, WINDOW, KV_HEADS * HEAD_DIM), F32),
        ],
        scratch_shapes=[
            pltpu.VMEM((tt, IN_WIDTH), F32),
            pltpu.VMEM((16 + tt, POOL_WIDTH), F32),
            pltpu.VMEM((KV_HEADS, WINDOW + tt, HEAD_DIM), BF16),
            pltpu.VMEM((KV_HEADS, WINDOW + tt, HEAD_DIM), BF16),
            pltpu.VMEM((HG_WIDTH, HG_WIDTH), F32),
            pltpu.VMEM((tt, HG_WIDTH), F32),
            pltpu.VMEM((tt, D_MODEL), BF16),
        ],
        compiler_params=pltpu.CompilerParams(dimension_semantics=("arbitrary", "arbitrary"),
                                             vmem_limit_bytes=VMEM_LIMIT),
        name="prompt_mixer",
    )(x, mod, norm.reshape(DEPTH, 1, D_MODEL), w_in, w_out, pool_bd, pool_scale.reshape(DEPTH, 1, POOL_WIDTH),
      lbs.reshape(DEPTH, 1, HG_WIDTH), hg_norm, bm, sinks, tri, ones_blk, bdm)


def _sample_mixer_kernel(x_ref, mod_ref, n_ref, win_ref, wout_ref, pbd_ref, pscale_ref, lb_ref, hgn_ref,
                         bms_ref, sinkc_ref, ones_ref, sp_ref, s0_ref, kc_ref, vc_ref,
                         xo_ref, pool_o, s_o, k_o, v_o,
                         z_ref, cat_ref, qs_ref, os_ref, qh_ref, kb_ref, vs_ref, g8_ref, oi_ref, kn_ref, vn_ref,
                         *, steps, nb, gb, pos0):
    g = pl.program_id(0)
    last = pl.num_programs(0) - 1
    base = pl.multiple_of(g * gb, gb)

    @pl.when(g == 0)
    def _():
        for t in range(steps):
            h = _norm_mod(x_ref[t], n_ref[...], mod_ref[3], mod_ref[4]).astype(BF16)
            z_ref[t * nb:(t + 1) * nb, :] = _dot(h, win_ref[...])
        g8_ref[...] = jnp.zeros(g8_ref.shape, F32)
        kn_ref[...] = jnp.zeros(kn_ref.shape, F32)
        vn_ref[...] = jnp.zeros(vn_ref.shape, F32)

    def zt(t, c0, w):
        return z_ref[pl.ds(t * nb + base, gb), c0:c0 + w]

    def cat_store(t, c0, val):
        cat_ref[pl.ds(t * nb + base, gb), c0:c0 + val.shape[1]] = val

    lane = lax.broadcasted_iota(jnp.int32, (1, POOL_WIDTH), 1)
    ones_blk = ones_ref[...]
    hm = _head_masks()

    ext = [sp_ref[:, r, :] for r in range(POOL_BUF)] + [zt(t, OFF_POOL, POOL_WIDTH) for t in range(steps)]
    for r in range(POOL_BUF):
        pool_o[:, r, :] = ext[r + steps]
    sums = {1: ext}
    w = 1
    while w < 16:
        prev = sums[w]
        sums[2 * w] = [None if (r < 2 * w - 1) else prev[r] + prev[r - w] for r in range(len(ext))]
        w *= 2
    dl = []
    for t in range(steps):
        r = POOL_BUF + t
        sel = jnp.where(lane < 64, sums[2][r], jnp.where(lane < 128, sums[4][r], jnp.where(lane < 192, sums[8][r], sums[16][r])))
        cnt = [float(min(pos0 + t + 1, wd)) for wd in POOL_WINDOWS]
        cl = jnp.where(lane < 64, cnt[0], jnp.where(lane < 128, cnt[1], jnp.where(lane < 192, cnt[2], cnt[3]))).astype(F32)
        dl.append(sel / cl - ext[r])
    ypool = _dot(jnp.concatenate(dl, axis=0).astype(BF16), pbd_ref[...]) * pscale_ref[...]
    for t in range(steps):
        cat_store(t, 0, ypool[t * gb:(t + 1) * gb])

    lb = lb_ref[...]
    qv, kv_, vv, bv = [], [], [], []
    bacc = None
    for t in range(steps):
        qv.append(_silu(zt(t, OFF_HQ, HG_WIDTH)))
        f = lb + (1.0 - lb) * _sigmoid(zt(t, OFF_HF, HG_WIDTH))
        kv_.append(1.0 - f)
        vv.append(zt(t, OFF_HI, HG_WIDTH))
        bacc = jnp.log(f) if bacc is None else bacc + jnp.log(f)
        bv.append(bacc)
    pairs = [(t, s) for t in range(steps) for s in range(t + 1)]
    xs = jnp.concatenate([qv[t] * kv_[s] * jnp.exp(bv[t] - bv[s]) for t, s in pairs], axis=0).astype(BF16)
    rr = _dot(xs, ones_blk)
    o_intra = [None] * steps
    for i, (t, s) in enumerate(pairs):
        term = rr[i * gb:(i + 1) * gb] * vv[s]
        o_intra[t] = term if o_intra[t] is None else o_intra[t] + term
    blast = bv[steps - 1]
    ghi, gmid, glo = _split3(jnp.exp(blast))
    g8_ref[:, 0, :] = ghi.astype(F32)
    g8_ref[:, 1, :] = gmid.astype(F32)
    g8_ref[:, 2, :] = glo.astype(F32)
    for t in range(steps):
        qin = qv[t] * jnp.exp(bv[t])
        kout = kv_[t] * jnp.exp(blast - bv[t])
        for hh in range(HG_HEADS):
            qh_ref[:, HG_HEADS * t + hh, :] = qin * hm[hh]
            kb_ref[:, HG_HEADS * t + hh, :] = kout * hm[hh]
            vs_ref[:, HG_HEADS * t + hh, :] = vv[t][:, HG_DK * hh:HG_DK * (hh + 1)]

    zeros64 = jnp.zeros((gb, HEAD_DIM), F32)
    for t in range(steps):
        kn_ref[:, t, :] = zt(t, OFF_AK, KV_HEADS * HEAD_DIM)
        vn_ref[:, t, :] = zt(t, OFF_AV, KV_HEADS * HEAD_DIM)
        for hd in range(ATT_HEADS):
            qsl = zt(t, OFF_AQ + HEAD_DIM * hd, HEAD_DIM) * (HEAD_DIM ** -0.5)
            parts = [qsl, zeros64] if hd < GQA_GROUP else [zeros64, qsl]
            qs_ref[:, ATT_HEADS * t + hd, :] = jnp.concatenate(parts, axis=1)
    k_o[:, 0:WINDOW - steps, :] = kc_ref[:, steps:WINDOW, :]
    v_o[:, 0:WINDOW - steps, :] = vc_ref[:, steps:WINDOW, :]
    for t in range(steps):
        k_o[:, WINDOW - steps + t, :] = zt(t, OFF_AK, KV_HEADS * HEAD_DIM)
        v_o[:, WINDOW - steps + t, :] = zt(t, OFF_AV, KV_HEADS * HEAD_DIM)

    bm_old = bms_ref[:, 0:WINDOW]
    bm_new = bms_ref[:, WINDOW:WINDOW + 8]
    sinkc = sinkc_ref[...]
    ones8 = jnp.ones((8, HG_DK), BF16)

    def per_seq(n, carry):
        s0 = s0_ref[n]
        oi_ref[n] = _dot(qh_ref[n].astype(BF16), s0.astype(BF16))
        gcol = _dot_tn(g8_ref[n].astype(BF16), ones8)
        s_o[n] = gcol * s0 + _dot_tn(kb_ref[n].astype(BF16), vs_ref[n].astype(BF16))
        qn = qs_ref[n].astype(BF16)
        l_old = _dot_nt(qn, kc_ref[n].astype(BF16)) + bm_old
        l_new = _dot_nt(qn, kn_ref[n].astype(BF16)) + bm_new
        m = jnp.maximum(jnp.maximum(jnp.max(l_old, axis=-1, keepdims=True), jnp.max(l_new, axis=-1, keepdims=True)), sinkc)
        p_old = jnp.exp(l_old - m)
        p_new = jnp.exp(l_new - m)
        den = jnp.sum(p_old, axis=-1, keepdims=True) + jnp.sum(p_new, axis=-1, keepdims=True) + jnp.exp(sinkc - m)
        o = _dot(p_old.astype(BF16), vc_ref[n].astype(BF16)) + _dot(p_new.astype(BF16), vn_ref[n].astype(BF16))
        os_ref[n] = o / den
        return carry

    lax.fori_loop(0, gb, per_seq, 0)

    ol, gl = [], []
    for t in range(steps):
        o_inter = jnp.concatenate([oi_ref[:, HG_HEADS * t + hh, :] for hh in range(HG_HEADS)], axis=1)
        ol.append(o_intra[t] + o_inter)
        gl.append(_silu(zt(t, OFF_HG, HG_WIDTH)))
        for hd in range(ATT_HEADS):
            c0 = HEAD_DIM * (hd // GQA_GROUP)
            cat_store(t, POOL_WIDTH + HG_WIDTH + HEAD_DIM * hd, os_ref[:, ATT_HEADS * t + hd, c0:c0 + HEAD_DIM])
    o_all = jnp.concatenate(ol, axis=0)
    sq_hi, sq_mid, _ = _split3(o_all * o_all)
    ms = (_dot(sq_hi, ones_blk) + _dot(sq_mid, ones_blk)) * (1.0 / HG_DK)
    y_hg = o_all * lax.rsqrt(ms + EPS) * hgn_ref[...] * jnp.concatenate(gl, axis=0)
    for t in range(steps):
        cat_store(t, POOL_WIDTH, y_hg[t * gb:(t + 1) * gb])

    @pl.when(g == last)
    def _():
        for t in range(steps):
            y = _dot(cat_ref[t * nb:(t + 1) * nb, :].astype(BF16), wout_ref[...])
            xo_ref[t] = x_ref[t] + mod_ref[5] * y


def _sample_mixer(x, mod, norm, w_in, w_out, pool_bd, pool_scale, lbs, hg_norm, bms, sink_col, ones_blk,
                  state_pool, state_hgrn, cache_k, cache_v, layer, *, gb, pos0):
    steps, nb, _ = x.shape
    rows_th = steps * ATT_HEADS
    kern = functools.partial(_sample_mixer_kernel, steps=steps, nb=nb, gb=gb, pos0=pos0)
    full = lambda *shape: pl.BlockSpec(shape, lambda g: (0,) * len(shape))
    lsel = lambda *shape: pl.BlockSpec((None,) + shape, lambda g: (layer,) + (0,) * len(shape))
    grp = lambda *shape: pl.BlockSpec((None, gb) + shape, lambda g: (layer, g) + (0,) * len(shape))
    ogrp = lambda *shape: pl.BlockSpec((gb,) + shape, lambda g: (g,) + (0,) * len(shape))
    kvw = KV_HEADS * HEAD_DIM
    return pl.pallas_call(
        kern,
        grid=(nb // gb,),
        in_specs=[
            full(steps, nb, D_MODEL),
            pl.BlockSpec((None, N_MOD, nb, D_MODEL), lambda g: (layer, 0, 0, 0)),
            lsel(1, D_MODEL),
            lsel(D_MODEL, IN_WIDTH),
            lsel(D_MODEL, D_MODEL),
            lsel(POOL_WIDTH, POOL_WIDTH),
            lsel(1, POOL_WIDTH),
            lsel(1, HG_WIDTH),
            lsel(1, HG_WIDTH),
            full(rows_th, 2 * WINDOW),
            lsel(rows_th, 1),
            full(HG_WIDTH, HG_WIDTH),
            grp(POOL_BUF, POOL_WIDTH),
            grp(HG_WIDTH, HG_DK),
            grp(WINDOW, kvw),
            grp(WINDOW, kvw),
        ],
        out_specs=[
            full(steps, nb, D_MODEL),
            ogrp(POOL_BUF, POOL_WIDTH),
            ogrp(HG_WIDTH, HG_DK),
            ogrp(WINDOW, kvw),
            ogrp(WINDOW, kvw),
        ],
        out_shape=[
            jax.ShapeDtypeStruct((steps, nb, D_MODEL), F32),
            jax.ShapeDtypeStruct((nb, POOL_BUF, POOL_WIDTH), F32),
            jax.ShapeDtypeStruct((nb, HG_WIDTH, HG_DK), F32),
            jax.ShapeDtypeStruct((nb, WINDOW, kvw), F32),
            jax.ShapeDtypeStruct((nb, WINDOW, kvw), F32),
        ],
        scratch_shapes=[
            pltpu.VMEM((steps * nb, IN_WIDTH), F32),
            pltpu.VMEM((steps * nb, D_MODEL), F32),
            pltpu.VMEM((gb, rows_th, kvw), F32),
            pltpu.VMEM((gb, rows_th, kvw), F32),
            pltpu.VMEM((gb, steps * HG_HEADS, HG_WIDTH), F32),
            pltpu.VMEM((gb, steps * HG_HEADS, HG_WIDTH), F32),
            pltpu.VMEM((gb, steps * HG_HEADS, HG_DK), F32),
            pltpu.VMEM((gb, 8, HG_WIDTH), F32),
            pltpu.VMEM((gb, steps * HG_HEADS, HG_DK), F32),
            pltpu.VMEM((gb, 8, kvw), F32),
            pltpu.VMEM((gb, 8, kvw), F32),
        ],
        compiler_params=pltpu.CompilerParams(dimension_semantics=("arbitrary",), vmem_limit_bytes=VMEM_LIMIT),
        name="sample_mixer",
    )(x, mod, norm.reshape(DEPTH, 1, D_MODEL), w_in, w_out, pool_bd, pool_scale.reshape(DEPTH, 1, POOL_WIDTH),
      lbs.reshape(DEPTH, 1, HG_WIDTH), hg_norm, bms, sink_col, ones_blk, state_pool, state_hgrn, cache_k, cache_v)


FFN_TILE = 256
MIXER_TILE = 256
SAMPLE_GROUP = 8
PAST_LEN = 8192


def kernel(x_prompt, x_sample, c_prompt, c_sample, state_pool, state_hgrn, cache_k_win, cache_v_win, norm_ffn1, norm_mix, norm_ffn2, w_mod, b_mod, ffn1_w_gate, ffn1_w_up, ffn1_w_down, w_in, w_out, pool_w, pool_scale, hgrn_lower, hgrn_norm, attn_sinks, rel_bias, ffn2_w_gate, ffn2_w_up, ffn2_w_down, norm_final):
    bsz, seq, _ = x_prompt.shape
    nb, steps, _ = x_sample.shape
    kvw = KV_HEADS * HEAD_DIM

    mod = _modulation(jnp.concatenate([c_sample, c_prompt], axis=0), w_mod, b_mod)
    prompt_mod_block = nb // 8
    lbs = _lower_bounds(hgrn_lower)
    maps_p = np.stack([_bucket_map(WINDOW, 2 * WINDOW, WINDOW, 0, 2 * WINDOW),
                       _bucket_map(WINDOW, 2 * WINDOW, WINDOW, WINDOW, 2 * WINDOW)])
    bm_p = _bias_tables(jnp.asarray(maps_p), rel_bias)
    maps_s = _bucket_map(8, 2 * WINDOW, steps, 0, WINDOW + steps)[None]
    bm_s = _bias_tables(jnp.asarray(maps_s), rel_bias)[0]
    bm_s = jnp.swapaxes(bm_s, 0, 1).reshape(8 * ATT_HEADS, 2 * WINDOW)[:steps * ATT_HEADS]
    sink_col = jnp.tile(attn_sinks, (1, steps)).reshape(DEPTH, steps * ATT_HEADS, 1)
    consts = _mixer_consts(MIXER_TILE)
    hgn = jnp.tile(hgrn_norm, (1, HG_HEADS)).reshape(DEPTH, 1, HG_WIDTH)
    pool_bd = _pool_block_diag(pool_w)
    w_in_b, w_out_b = w_in.astype(BF16), w_out.astype(BF16)
    f1 = (ffn1_w_gate.astype(BF16), ffn1_w_up.astype(BF16), ffn1_w_down.astype(BF16))
    f2 = (ffn2_w_gate.astype(BF16), ffn2_w_up.astype(BF16), ffn2_w_down.astype(BF16))
    s_hgrn = state_hgrn.reshape(DEPTH, nb, HG_WIDTH, HG_DK)
    c_k = cache_k_win.reshape(DEPTH, nb, WINDOW, kvw)
    c_v = cache_v_win.reshape(DEPTH, nb, WINDOW, kvw)

    xp = x_prompt.reshape(bsz * seq, D_MODEL)
    xs = jnp.swapaxes(x_sample, 0, 1).reshape(steps * nb, D_MODEL)
    ffn_p = functools.partial(_ffn, tm=FFN_TILE, per_row=False, mod_rows=8, mod_row_block=prompt_mod_block)
    ffn_s = functools.partial(_ffn, tm=nb, per_row=True, mod_rows=nb, mod_row_block=0)
    st_p, st_s = [], []
    for l in range(DEPTH):
        xp = ffn_p(xp, mod, norm_ffn1, *f1, l, 0)
        xs = ffn_s(xs, mod, norm_ffn1, *f1, l, 0)
        xp, *sp = _prompt_mixer(xp.reshape(bsz, seq, D_MODEL), mod, norm_mix, w_in_b, w_out_b, pool_bd, pool_scale,
                                lbs, hgn, bm_p, attn_sinks, consts, l, tt=MIXER_TILE, mod_row_block=prompt_mod_block)
        xs, *ss = _sample_mixer(xs.reshape(steps, nb, D_MODEL), mod, norm_mix, w_in_b, w_out_b, pool_bd, pool_scale,
                                lbs, hgn, bm_s, sink_col, consts[1], state_pool, s_hgrn, c_k, c_v, l,
                                gb=SAMPLE_GROUP, pos0=PAST_LEN)
        st_p.append(sp)
        st_s.append(ss)
        xp = ffn_p(xp.reshape(bsz * seq, D_MODEL), mod, norm_ffn2, *f2, l, 6)
        xs = ffn_s(xs.reshape(steps * nb, D_MODEL), mod, norm_ffn2, *f2, l, 6)

    y_prompt = _final_norm(xp, norm_final, FFN_TILE).reshape(bsz, seq, D_MODEL)
    y_sample = jnp.swapaxes(_final_norm(xs, norm_final, nb).reshape(steps, nb, D_MODEL), 0, 1)

    def stacked(states, i, shape):
        return jnp.stack([s[i] for s in states]).reshape((DEPTH,) + shape)

    outs = []
    for states, n in ((st_p, bsz), (st_s, nb)):
        outs += [stacked(states, 0, (n, POOL_BUF, POOL_WIDTH)),
                 stacked(states, 1, (n, HG_HEADS, HG_DK, HG_DK)),
                 stacked(states, 2, (n, WINDOW, KV_HEADS, HEAD_DIM)),
                 stacked(states, 3, (n, WINDOW, KV_HEADS, HEAD_DIM))]
    return (y_prompt, y_sample, *outs)
```

```python
import functools
import math

import numpy as np
import jax
import jax.numpy as jnp
from jax import lax
from jax.experimental import pallas as pl
from jax.experimental.pallas import tpu as pltpu

F32 = jnp.float32
BF16 = jnp.bfloat16

D_MODEL = 1024
DEPTH = 4
N_MOD = 9
D_FF = 2816
EPS = 1e-6
POOL_WIDTH = 256
POOL_WINDOWS = (2, 4, 8, 16)
POOL_GW = 64
POOL_BUF = 15
HG_WIDTH = 256
HG_HEADS = 4
HG_DK = 64
ATT_WIDTH = 512
ATT_HEADS = 8
KV_HEADS = 2
GQA_GROUP = 4
HEAD_DIM = 64
WINDOW = 128
REL_BUCKETS = 32
REL_MAX_DIST = 128
OFF_POOL, OFF_HQ, OFF_HF, OFF_HI, OFF_HG, OFF_AQ, OFF_AK, OFF_AV = 0, 256, 512, 768, 1024, 1280, 1792, 1920
IN_WIDTH = 2048
NEG = -1e30

VMEM_LIMIT = 56 * 1024 * 1024


def _dot(a, b):
    return jnp.dot(a, b, preferred_element_type=F32)


def _dot_nt(a, b):
    return lax.dot_general(a, b, (((1,), (1,)), ((), ())), preferred_element_type=F32)


def _dot_tn(a, b):
    return lax.dot_general(a, b, (((0,), (0,)), ((), ())), preferred_element_type=F32)


def _sigmoid(x):
    return 1.0 / (1.0 + jnp.exp(-x))


def _silu(x):
    return x * _sigmoid(x)


def _norm_mod(x, g, shift, scale):
    r = lax.rsqrt(jnp.mean(x * x, axis=-1, keepdims=True) + EPS)
    return (x * r) * g * (1.0 + scale) + shift


def _split3(x):
    hi = x.astype(BF16)
    r1 = x - hi.astype(F32)
    mid = r1.astype(BF16)
    lo = (r1 - mid.astype(F32)).astype(BF16)
    return hi, mid, lo


def _mod_kernel(c_ref, w_ref, b_ref, o_ref):
    sc = _silu(c_ref[...]).astype(BF16)
    o_ref[...] = _dot(sc, w_ref[...].astype(BF16)) + b_ref[...]


def _modulation(c_all, w_mod, b_mod):
    rows = c_all.shape[0]
    return pl.pallas_call(
        _mod_kernel,
        grid=(DEPTH, N_MOD),
        in_specs=[
            pl.BlockSpec((rows, D_MODEL), lambda l, j: (0, 0)),
            pl.BlockSpec((None, D_MODEL, D_MODEL), lambda l, j: (l, 0, j)),
            pl.BlockSpec((None, None, 1, D_MODEL), lambda l, j: (l, j, 0, 0)),
        ],
        out_specs=pl.BlockSpec((None, None, rows, D_MODEL), lambda l, j: (l, j, 0, 0)),
        out_shape=jax.ShapeDtypeStruct((DEPTH, N_MOD, rows, D_MODEL), F32),
        compiler_params=pltpu.CompilerParams(dimension_semantics=("arbitrary", "arbitrary"),
                                             vmem_limit_bytes=VMEM_LIMIT),
        name="adaln_mod",
    )(c_all, w_mod, b_mod.reshape(DEPTH, N_MOD, 1, D_MODEL))


def _mod_rows(mod_ref, j, row, per_row):
    if per_row:
        return mod_ref[j]
    return mod_ref[j, pl.ds(row, 1), :]


def _ffn_kernel(x_ref, mod_ref, n_ref, wg_ref, wu_ref, wd_ref, o_ref, *, j0, per_row, tiles_per_seq):
    row = pl.program_id(0) // tiles_per_seq
    shift = _mod_rows(mod_ref, j0, row, per_row)
    scale = _mod_rows(mod_ref, j0 + 1, row, per_row)
    gate = _mod_rows(mod_ref, j0 + 2, row, per_row)
    x = x_ref[...]
    h = _norm_mod(x, n_ref[...], shift, scale).astype(BF16)
    g = _dot(h, wg_ref[...])
    u = _dot(h, wu_ref[...])
    a = (_silu(g) * u).astype(BF16)
    o_ref[...] = x + (0.5 * gate) * _dot(a, wd_ref[...])


def _ffn(x, mod, norm, wg, wu, wd, layer, j0, *, tm, per_row, mod_rows, mod_row_block):
    rows = x.shape[0]
    tiles_per_seq = 1 if per_row else (rows // mod_rows) // tm
    kern = functools.partial(_ffn_kernel, j0=j0, per_row=per_row, tiles_per_seq=tiles_per_seq)
    return pl.pallas_call(
        kern,
        grid=(rows // tm,),
        in_specs=[
            pl.BlockSpec((tm, D_MODEL), lambda i: (i, 0)),
            pl.BlockSpec((None, N_MOD, mod_rows, D_MODEL), lambda i: (layer, 0, mod_row_block, 0)),
            pl.BlockSpec((None, 1, D_MODEL), lambda i: (layer, 0, 0)),
            pl.BlockSpec((None, D_MODEL, D_FF), lambda i: (layer, 0, 0)),
            pl.BlockSpec((None, D_MODEL, D_FF), lambda i: (layer, 0, 0)),
            pl.BlockSpec((None, D_FF, D_MODEL), lambda i: (layer, 0, 0)),
        ],
        out_specs=pl.BlockSpec((tm, D_MODEL), lambda i: (i, 0)),
        out_shape=jax.ShapeDtypeStruct((rows, D_MODEL), F32),
        compiler_params=pltpu.CompilerParams(dimension_semantics=("arbitrary",), vmem_limit_bytes=VMEM_LIMIT),
        name="swiglu_half_step",
    )(x, mod, norm.reshape(DEPTH, 1, D_MODEL), wg, wu, wd)


def _final_norm_kernel(x_ref, g_ref, o_ref):
    x = x_ref[...]
    r = lax.rsqrt(jnp.mean(x * x, axis=-1, keepdims=True) + EPS)
    o_ref[...] = (x * r) * g_ref[...]


def _final_norm(x, g, tm):
    rows = x.shape[0]
    return pl.pallas_call(
        _final_norm_kernel,
        grid=(rows // tm,),
        in_specs=[pl.BlockSpec((tm, D_MODEL), lambda i: (i, 0)), pl.BlockSpec((1, D_MODEL), lambda i: (0, 0))],
        out_specs=pl.BlockSpec((tm, D_MODEL), lambda i: (i, 0)),
        out_shape=jax.ShapeDtypeStruct((rows, D_MODEL), F32),
        compiler_params=pltpu.CompilerParams(dimension_semantics=("arbitrary",)),
        name="final_rmsnorm",
    )(x, g.reshape(1, D_MODEL))


def _lower_bound_kernel(x_ref, o_ref):
    rows = [x_ref[l:l + 1, :] for l in range(DEPTH)]
    m = functools.reduce(jnp.maximum, rows)
    e = [jnp.exp(r - m) for r in rows]
    s = functools.reduce(lambda a, b: a + b, e)
    sm = [ei / s for ei in e]
    acc = sm[0]
    for l in range(DEPTH):
        if l > 0:
            acc = acc + sm[l]
        o_ref[l:l + 1, :] = acc - sm[0]


def _lower_bounds(hgrn_lower):
    return pl.pallas_call(
        _lower_bound_kernel,
        out_shape=jax.ShapeDtypeStruct((DEPTH, HG_WIDTH), F32),
        name="hgrn_lower_bounds",
    )(hgrn_lower)


def _bucket_map(rows, cols, valid_rows, col_lo, col_hi):
    r = np.arange(rows)[:, None]
    c = np.arange(cols)[None, :]
    rel = c - WINDOW - r
    n = np.maximum(-rel, 0)
    exact = REL_BUCKETS // 2
    nf = np.maximum(n, 1).astype(np.float32)
    large = exact + (np.log(nf / np.float32(exact)) / np.float32(math.log(REL_MAX_DIST / exact))
                     * np.float32(REL_BUCKETS - exact)).astype(np.int32)
    large = np.minimum(large, REL_BUCKETS - 1)
    bucket = np.where(n < exact, n, large)
    valid = (rel <= 0) & (rel > -WINDOW) & (r < valid_rows) & (c >= col_lo) & (c < col_hi)
    return np.where(valid, bucket, -1).astype(np.int32)


def _bias_table_kernel(bkt_ref, rb_ref, o_ref):
    h = pl.program_id(1)
    bkt = bkt_ref[...]
    acc = jnp.full(bkt.shape, NEG, F32)
    for b in range(REL_BUCKETS):
        acc = jnp.where(bkt == b, rb_ref[b, h], acc)
    o_ref[...] = acc


def _bias_tables(bucket_maps, rel_bias):
    m, r, c = bucket_maps.shape
    return pl.pallas_call(
        _bias_table_kernel,
        grid=(m, ATT_HEADS),
        in_specs=[pl.BlockSpec((None, r, c), lambda i, h: (i, 0, 0)),
                  pl.BlockSpec(memory_space=pltpu.SMEM)],
        out_specs=pl.BlockSpec((None, None, r, c), lambda i, h: (i, h, 0, 0)),
        out_shape=jax.ShapeDtypeStruct((m, ATT_HEADS, r, c), F32),
        compiler_params=pltpu.CompilerParams(dimension_semantics=("arbitrary", "arbitrary")),
        name="rel_bias_tables",
    )(bucket_maps, rel_bias)


def _block_ones(n, blk):
    i = np.arange(n)
    return (i[:, None] // blk == i[None, :] // blk)


def _mixer_consts(tt):
    tri = jnp.asarray(np.tril(np.ones((tt, tt), np.float32)), BF16)
    blk = _block_ones(HG_WIDTH, HG_DK)
    return tri, jnp.asarray(blk, BF16), jnp.asarray(blk, F32)


def _pool_block_diag(pool_w):
    out = jnp.zeros((DEPTH, POOL_WIDTH, POOL_WIDTH), BF16)
    for g in range(len(POOL_WINDOWS)):
        sl = slice(g * POOL_GW, (g + 1) * POOL_GW)
        out = out.at[:, sl, sl].set(pool_w[:, g].astype(BF16))
    return out


def _head_masks():
    lane = lax.broadcasted_iota(jnp.int32, (1, HG_WIDTH), 1)
    return [jnp.where((lane >= HG_DK * h) & (lane < HG_DK * (h + 1)), 1.0, 0.0).astype(F32) for h in range(HG_HEADS)]


def _prompt_mixer_kernel(x_ref, mod_ref, n_ref, win_ref, wout_ref, pbd_ref, pscale_ref, lb_ref, hgn_ref,
                         bm_ref, sink_ref, tri_ref, ones_ref, bdm_ref,
                         xo_ref, pool_o, s_o, k_o, v_o,
                         z_ref, ubuf, kbuf, vbuf, st_ref, cat_ref, *, layer, tt):
    b = pl.program_id(0)
    t = pl.program_id(1)
    last = pl.num_programs(1) - 1

    @pl.when(t == 0)
    def _():
        ubuf[0:16, :] = jnp.zeros((16, POOL_WIDTH), F32)
        kbuf[:, 0:WINDOW, :] = jnp.zeros((KV_HEADS, WINDOW, HEAD_DIM), BF16)
        vbuf[:, 0:WINDOW, :] = jnp.zeros((KV_HEADS, WINDOW, HEAD_DIM), BF16)
        st_ref[...] = jnp.zeros((HG_WIDTH, HG_WIDTH), F32)

    x = x_ref[...]
    shift = mod_ref[3, pl.ds(b, 1), :]
    scale = mod_ref[4, pl.ds(b, 1), :]
    gate = mod_ref[5, pl.ds(b, 1), :]
    h = _norm_mod(x, n_ref[...], shift, scale).astype(BF16)
    z_ref[...] = _dot(h, win_ref[...])

    lane = lax.broadcasted_iota(jnp.int32, (1, POOL_WIDTH), 1)

    u = z_ref[:, OFF_POOL:OFF_POOL + POOL_WIDTH]
    ubuf[16:16 + tt, :] = u
    e = ubuf[...]
    s2 = e + pltpu.roll(e, 1, 0)
    s4 = s2 + pltpu.roll(s2, 2, 0)
    s8 = s4 + pltpu.roll(s4, 4, 0)
    s16 = s8 + pltpu.roll(s8, 8, 0)
    sel = jnp.where(lane < 64, s2, jnp.where(lane < 128, s4, jnp.where(lane < 192, s8, s16)))[16:]
    wl = jnp.where(lane < 64, 2.0, jnp.where(lane < 128, 4.0, jnp.where(lane < 192, 8.0, 16.0))).astype(F32)
    pos1 = (t * tt + 1 + lax.broadcasted_iota(jnp.int32, (tt, 1), 0)).astype(F32)
    dpool = sel / jnp.minimum(pos1, wl) - u
    ypool = _dot(dpool.astype(BF16), pbd_ref[...]) * pscale_ref[...]
    cat_ref[:, 0:POOL_WIDTH] = ypool.astype(BF16)
    ubuf[0:16, :] = ubuf[tt:tt + 16, :]

    lb = lb_ref[...]
    q = _silu(z_ref[:, OFF_HQ:OFF_HQ + HG_WIDTH])
    f = lb + (1.0 - lb) * _sigmoid(z_ref[:, OFF_HF:OFF_HF + HG_WIDTH])
    k = 1.0 - f
    v = z_ref[:, OFF_HI:OFF_HI + HG_WIDTH]
    vb = v.astype(BF16)
    tri = tri_ref[...]
    hi, mid, lo = _split3(jnp.log(f))
    bcum = _dot(tri, hi) + _dot(tri, mid) + _dot(tri, lo)
    ones_blk = ones_ref[...]
    hm = _head_masks()

    n16 = tt // 16
    half_shape = (n16, 2, 8, HG_WIDTH)
    q4 = q.reshape(half_shape)
    b4 = bcum.reshape(half_shape)
    c4 = (bcum - jnp.log(k)).reshape(half_shape)
    v4 = v.reshape(half_shape)
    rowi = lax.broadcasted_iota(jnp.int32, (1, 8, 1), 1)
    o_up = jnp.zeros((n16, 8, HG_WIDTH), F32)
    o_dn = jnp.zeros((n16, 8, HG_WIDTH), F32)
    nh = n16 * 8
    for s in range(8):
        cs = c4[:, 0, s:s + 1, :]
        vs = v4[:, 0, s:s + 1, :]
        x_up = jnp.where(rowi >= s, q4[:, 0] * jnp.exp(b4[:, 0] - cs), 0.0)
        x_dn = q4[:, 1] * jnp.exp(b4[:, 1] - cs)
        xs = jnp.concatenate([x_up.reshape(nh, HG_WIDTH), x_dn.reshape(nh, HG_WIDTH)], axis=0)
        r = _dot(xs.astype(BF16), ones_blk)
        o_up = o_up + r[0:nh].reshape(n16, 8, HG_WIDTH) * vs
        o_dn = o_dn + r[nh:2 * nh].reshape(n16, 8, HG_WIDTH) * vs
    for s in range(8):
        cs = c4[:, 1, s:s + 1, :]
        x_dn = jnp.where(rowi >= s, q4[:, 1] * jnp.exp(b4[:, 1] - cs), 0.0)
        r = _dot(x_dn.reshape(nh, HG_WIDTH).astype(BF16), ones_blk)
        o_dn = o_dn + r.reshape(n16, 8, HG_WIDTH) * v4[:, 1, s:s + 1, :]
    o_tot = jnp.concatenate([o_up[:, None], o_dn[:, None]], axis=1).reshape(tt, HG_WIDTH)

    blk = 32
    while blk <= tt:
        half = blk // 2
        parts = []
        for i in range(tt // blk):
            lo_s, mid_s, hi_s = i * blk, i * blk + half, (i + 1) * blk
            ref = bcum[mid_s - 1:mid_s, :]
            qt = q[mid_s:hi_s] * jnp.exp(bcum[mid_s:hi_s] - ref)
            kt = (k[lo_s:mid_s] * jnp.exp(ref - bcum[lo_s:mid_s])).astype(BF16)
            qs = jnp.concatenate([qt * hm[hh] for hh in range(HG_HEADS)], axis=0).astype(BF16)
            a = _dot_nt(qs, kt).astype(BF16)
            ov = _dot(a, vb[lo_s:mid_s])
            oi = ov[0:half] * hm[0]
            for hh in range(1, HG_HEADS):
                oi = oi + ov[hh * half:(hh + 1) * half] * hm[hh]
            parts += [jnp.zeros((half, HG_WIDTH), F32), oi]
        o_tot = o_tot + jnp.concatenate(parts, axis=0)
        blk *= 2

    st = st_ref[...]
    qin = (q * jnp.exp(bcum)).astype(BF16)
    o_tot = o_tot + _dot_nt(qin, st.astype(BF16))
    blast = bcum[tt - 1:tt, :]
    kout = (k * jnp.exp(blast - bcum)).astype(BF16)
    st_new = st * jnp.exp(blast) + _dot_tn(vb, kout) * bdm_ref[...]
    st_ref[...] = st_new

    sq_hi, sq_mid, _ = _split3(o_tot * o_tot)
    ms = (_dot(sq_hi, ones_blk) + _dot(sq_mid, ones_blk)) * (1.0 / HG_DK)
    y_hg = o_tot * lax.rsqrt(ms + EPS) * hgn_ref[...] * _silu(z_ref[:, OFF_HG:OFF_HG + HG_WIDTH])
    cat_ref[:, POOL_WIDTH:POOL_WIDTH + HG_WIDTH] = y_hg.astype(BF16)

    for g in range(KV_HEADS):
        kbuf[g, WINDOW:WINDOW + tt, :] = z_ref[:, OFF_AK + HEAD_DIM * g:OFF_AK + HEAD_DIM * (g + 1)].astype(BF16)
        vbuf[g, WINDOW:WINDOW + tt, :] = z_ref[:, OFF_AV + HEAD_DIM * g:OFF_AV + HEAD_DIM * (g + 1)].astype(BF16)
    first = jnp.where(t == 0, 1, 0)
    for j in range(tt // WINDOW):
        r0 = j * WINDOW
        for g in range(KV_HEADS):
            kg = kbuf[g, r0:r0 + 2 * WINDOW, :]
            vg = vbuf[g, r0:r0 + 2 * WINDOW, :]
            h0 = g * GQA_GROUP
            c0 = OFF_AQ + HEAD_DIM * h0
            qg = jnp.concatenate([z_ref[r0:r0 + WINDOW, c0 + HEAD_DIM * i:c0 + HEAD_DIM * (i + 1)]
                                  for i in range(GQA_GROUP)], axis=0)
            qg = (qg * (HEAD_DIM ** -0.5)).astype(BF16)
            bm = bm_ref[first, h0:h0 + GQA_GROUP] if j == 0 else bm_ref[0, h0:h0 + GQA_GROUP]
            lg_all = _dot_nt(qg, kg)
            ps, dens = [], []
            for i in range(GQA_GROUP):
                lg = lg_all[i * WINDOW:(i + 1) * WINDOW] + bm[i]
                sink = sink_ref[layer, h0 + i]
                m = jnp.maximum(jnp.max(lg, axis=-1, keepdims=True), sink)
                p = jnp.exp(lg - m)
                dens.append(jnp.sum(p, axis=-1, keepdims=True) + jnp.exp(sink - m))
                ps.append(p.astype(BF16))
            og = _dot(jnp.concatenate(ps, axis=0), vg)
            for i in range(GQA_GROUP):
                cc = POOL_WIDTH + HG_WIDTH + HEAD_DIM * (h0 + i)
                cat_ref[r0:r0 + WINDOW, cc:cc + HEAD_DIM] = (og[i * WINDOW:(i + 1) * WINDOW] / dens[i]).astype(BF16)
    for g in range(KV_HEADS):
        kbuf[g, 0:WINDOW, :] = kbuf[g, tt:tt + WINDOW, :]
        vbuf[g, 0:WINDOW, :] = vbuf[g, tt:tt + WINDOW, :]

    xo_ref[...] = x + gate * _dot(cat_ref[...], wout_ref[...])

    @pl.when(t == last)
    def _():
        pool_o[...] = ubuf[pl.ds(1, POOL_BUF), :]
        s_kv = st_ref[...].T
        for hh in range(HG_HEADS):
            s_o[hh] = s_kv[HG_DK * hh:HG_DK * (hh + 1), HG_DK * hh:HG_DK * (hh + 1)]
        k_o[...] = z_ref[tt - WINDOW:tt, OFF_AK:OFF_AV]
        v_o[...] = z_ref[tt - WINDOW:tt, OFF_AV:IN_WIDTH]


def _prompt_mixer(x, mod, norm, w_in, w_out, pool_bd, pool_scale, lbs, hg_norm, bm, sinks, consts, layer, *,
                  tt, mod_row_block):
    bsz, seq, _ = x.shape
    tri, ones_blk, bdm = consts
    kern = functools.partial(_prompt_mixer_kernel, layer=layer, tt=tt)
    full = lambda *shape: pl.BlockSpec(shape, lambda b, t: (0,) * len(shape))
    lsel = lambda *shape: pl.BlockSpec((None,) + shape, lambda b, t: (layer,) + (0,) * len(shape))
    mix_tile = lambda b, t: (b, t, 0)
    mix_seq = lambda *z: (lambda b, t: (b,) + z)
    return pl.pallas_call(
        kern,
        grid=(bsz, seq // tt),
        in_specs=[
            pl.BlockSpec((None, tt, D_MODEL), mix_tile),
            pl.BlockSpec((None, N_MOD, 8, D_MODEL), lambda b, t: (layer, 0, mod_row_block, 0)),
            lsel(1, D_MODEL),
            lsel(D_MODEL, IN_WIDTH),
            lsel(D_MODEL, D_MODEL),
            lsel(POOL_WIDTH, POOL_WIDTH),
            lsel(1, POOL_WIDTH),
            lsel(1, HG_WIDTH),
            lsel(1, HG_WIDTH),
            full(2, ATT_HEADS, WINDOW, 2 * WINDOW),
            pl.BlockSpec(memory_space=pltpu.SMEM),
            full(tt, tt),
            full(HG_WIDTH, HG_WIDTH),
            full(HG_WIDTH, HG_WIDTH),
        ],
        out_specs=[
            pl.BlockSpec((None, tt, D_MODEL), mix_tile),
            pl.BlockSpec((None, POOL_BUF, POOL_WIDTH), mix_seq(0, 0)),
            pl.BlockSpec((None, HG_HEADS, HG_DK, HG_DK), mix_seq(0, 0, 0)),
            pl.BlockSpec((None, WINDOW, KV_HEADS * HEAD_DIM), mix_seq(0, 0)),
            pl.BlockSpec((None, WINDOW, KV_HEADS * HEAD_DIM), mix_seq(0, 0)),
        ],
        out_shape=[
            jax.ShapeDtypeStruct((bsz, seq, D_MODEL), F32),
            jax.ShapeDtypeStruct((bsz, POOL_BUF, POOL_WIDTH), F32),
            jax.ShapeDtypeStruct((bsz, HG_HEADS, HG_DK, HG_DK), F32),
            jax.ShapeDtypeStruct((bsz, WINDOW, KV_HEADS * HEAD_DIM), F32),
            jax.ShapeDtypeStruct((bsz, WINDOW, KV_HEADS * HEAD_DIM), F32),
        ],
        scratch_shapes=[
            pltpu.VMEM((tt, IN_WIDTH), F32),
            pltpu.VMEM((16 + tt, POOL_WIDTH), F32),
            pltpu.VMEM((KV_HEADS, WINDOW + tt, HEAD_DIM), BF16),
            pltpu.VMEM((KV_HEADS, WINDOW + tt, HEAD_DIM), BF16),
            pltpu.VMEM((HG_WIDTH, HG_WIDTH), F32),
            pltpu.VMEM((tt, D_MODEL), BF16),
        ],
        compiler_params=pltpu.CompilerParams(dimension_semantics=("arbitrary", "arbitrary"),
                                             vmem_limit_bytes=VMEM_LIMIT),
        name="prompt_mixer",
    )(x, mod, norm.reshape(DEPTH, 1, D_MODEL), w_in, w_out, pool_bd, pool_scale.reshape(DEPTH, 1, POOL_WIDTH),
      lbs.reshape(DEPTH, 1, HG_WIDTH), hg_norm, bm, sinks, tri, ones_blk, bdm)


def _sample_mixer_kernel(x_ref, mod_ref, n_ref, win_ref, wout_ref, pbd_ref, pscale_ref, lb_ref, hgn_ref,
                         bms_ref, sinkc_ref, ones_ref, sp_ref, s0_ref, kc_ref, vc_ref,
                         xo_ref, pool_o, s_o, k_o, v_o,
                         z_ref, cat_ref, qs_ref, os_ref, qh_ref, kb_ref, vs_ref, g8_ref, oi_ref, kn_ref, vn_ref,
                         *, steps, nb, gb, pos0):
    g = pl.program_id(0)
    last = pl.num_programs(0) - 1
    base = pl.multiple_of(g * gb, gb)

    @pl.when(g == 0)
    def _():
        for t in range(steps):
            h = _norm_mod(x_ref[t], n_ref[...], mod_ref[3], mod_ref[4]).astype(BF16)
            z_ref[t * nb:(t + 1) * nb, :] = _dot(h, win_ref[...])
        g8_ref[...] = jnp.zeros(g8_ref.shape, F32)
        kn_ref[...] = jnp.zeros(kn_ref.shape, F32)
        vn_ref[...] = jnp.zeros(vn_ref.shape, F32)

    def zt(t, c0, w):
        return z_ref[pl.ds(t * nb + base, gb), c0:c0 + w]

    def cat_store(t, c0, val):
        cat_ref[pl.ds(t * nb + base, gb), c0:c0 + val.shape[1]] = val

    lane = lax.broadcasted_iota(jnp.int32, (1, POOL_WIDTH), 1)
    ones_blk = ones_ref[...]
    hm = _head_masks()

    ext = [sp_ref[:, r, :] for r in range(POOL_BUF)] + [zt(t, OFF_POOL, POOL_WIDTH) for t in range(steps)]
    for r in range(POOL_BUF):
        pool_o[:, r, :] = ext[r + steps]
    sums = {1: ext}
    w = 1
    while w < 16:
        prev = sums[w]
        sums[2 * w] = [None if (r < 2 * w - 1) else prev[r] + prev[r - w] for r in range(len(ext))]
        w *= 2
    dl = []
    for t in range(steps):
        r = POOL_BUF + t
        sel = jnp.where(lane < 64, sums[2][r], jnp.where(lane < 128, sums[4][r], jnp.where(lane < 192, sums[8][r], sums[16][r])))
        cnt = [float(min(pos0 + t + 1, wd)) for wd in POOL_WINDOWS]
        cl = jnp.where(lane < 64, cnt[0], jnp.where(lane < 128, cnt[1], jnp.where(lane < 192, cnt[2], cnt[3]))).astype(F32)
        dl.append(sel / cl - ext[r])
    ypool = _dot(jnp.concatenate(dl, axis=0).astype(BF16), pbd_ref[...]) * pscale_ref[...]
    for t in range(steps):
        cat_store(t, 0, ypool[t * gb:(t + 1) * gb])

    lb = lb_ref[...]
    qv, kv_, vv, bv = [], [], [], []
    bacc = None
    for t in range(steps):
        qv.append(_silu(zt(t, OFF_HQ, HG_WIDTH)))
        f = lb + (1.0 - lb) * _sigmoid(zt(t, OFF_HF, HG_WIDTH))
        kv_.append(1.0 - f)
        vv.append(zt(t, OFF_HI, HG_WIDTH))
        bacc = jnp.log(f) if bacc is None else bacc + jnp.log(f)
        bv.append(bacc)
    pairs = [(t, s) for t in range(steps) for s in range(t + 1)]
    xs = jnp.concatenate([qv[t] * kv_[s] * jnp.exp(bv[t] - bv[s]) for t, s in pairs], axis=0).astype(BF16)
    rr = _dot(xs, ones_blk)
    o_intra = [None] * steps
    for i, (t, s) in enumerate(pairs):
        term = rr[i * gb:(i + 1) * gb] * vv[s]
        o_intra[t] = term if o_intra[t] is None else o_intra[t] + term
    blast = bv[steps - 1]
    ghi, gmid, glo = _split3(jnp.exp(blast))
    g8_ref[:, 0, :] = ghi.astype(F32)
    g8_ref[:, 1, :] = gmid.astype(F32)
    g8_ref[:, 2, :] = glo.astype(F32)
    for t in range(steps):
        qin = qv[t] * jnp.exp(bv[t])
        kout = kv_[t] * jnp.exp(blast - bv[t])
        for hh in range(HG_HEADS):
            qh_ref[:, HG_HEADS * t + hh, :] = qin * hm[hh]
            kb_ref[:, HG_HEADS * t + hh, :] = kout * hm[hh]
            vs_ref[:, HG_HEADS * t + hh, :] = vv[t][:, HG_DK * hh:HG_DK * (hh + 1)]

    zeros64 = jnp.zeros((gb, HEAD_DIM), F32)
    for t in range(steps):
        kn_ref[:, t, :] = zt(t, OFF_AK, KV_HEADS * HEAD_DIM)
        vn_ref[:, t, :] = zt(t, OFF_AV, KV_HEADS * HEAD_DIM)
        for hd in range(ATT_HEADS):
            qsl = zt(t, OFF_AQ + HEAD_DIM * hd, HEAD_DIM) * (HEAD_DIM ** -0.5)
            parts = [qsl, zeros64] if hd < GQA_GROUP else [zeros64, qsl]
            qs_ref[:, ATT_HEADS * t + hd, :] = jnp.concatenate(parts, axis=1)
    k_o[:, 0:WINDOW - steps, :] = kc_ref[:, steps:WINDOW, :]
    v_o[:, 0:WINDOW - steps, :] = vc_ref[:, steps:WINDOW, :]
    for t in range(steps):
        k_o[:, WINDOW - steps + t, :] = zt(t, OFF_AK, KV_HEADS * HEAD_DIM)
        v_o[:, WINDOW - steps + t, :] = zt(t, OFF_AV, KV_HEADS * HEAD_DIM)

    bm_old = bms_ref[:, 0:WINDOW]
    bm_new = bms_ref[:, WINDOW:WINDOW + 8]
    sinkc = sinkc_ref[...]
    ones8 = jnp.ones((8, HG_DK), BF16)

    def per_seq(n, carry):
        s0 = s0_ref[n]
        oi_ref[n] = _dot(qh_ref[n].astype(BF16), s0.astype(BF16))
        gcol = _dot_tn(g8_ref[n].astype(BF16), ones8)
        s_o[n] = gcol * s0 + _dot_tn(kb_ref[n].astype(BF16), vs_ref[n].astype(BF16))
        qn = qs_ref[n].astype(BF16)
        l_old = _dot_nt(qn, kc_ref[n].astype(BF16)) + bm_old
        l_new = _dot_nt(qn, kn_ref[n].astype(BF16)) + bm_new
        m = jnp.maximum(jnp.maximum(jnp.max(l_old, axis=-1, keepdims=True), jnp.max(l_new, axis=-1, keepdims=True)), sinkc)
        p_old = jnp.exp(l_old - m)
        p_new = jnp.exp(l_new - m)
        den = jnp.sum(p_old, axis=-1, keepdims=True) + jnp.sum(p_new, axis=-1, keepdims=True) + jnp.exp(sinkc - m)
        o = _dot(p_old.astype(BF16), vc_ref[n].astype(BF16)) + _dot(p_new.astype(BF16), vn_ref[n].astype(BF16))
        os_ref[n] = o / den
        return carry

    lax.fori_loop(0, gb, per_seq, 0)

    ol, gl = [], []
    for t in range(steps):
        o_inter = jnp.concatenate([oi_ref[:, HG_HEADS * t + hh, :] for hh in range(HG_HEADS)], axis=1)
        ol.append(o_intra[t] + o_inter)
        gl.append(_silu(zt(t, OFF_HG, HG_WIDTH)))
        for hd in range(ATT_HEADS):
            c0 = HEAD_DIM * (hd // GQA_GROUP)
            cat_store(t, POOL_WIDTH + HG_WIDTH + HEAD_DIM * hd, os_ref[:, ATT_HEADS * t + hd, c0:c0 + HEAD_DIM])
    o_all = jnp.concatenate(ol, axis=0)
    sq_hi, sq_mid, _ = _split3(o_all * o_all)
    ms = (_dot(sq_hi, ones_blk) + _dot(sq_mid, ones_blk)) * (1.0 / HG_DK)
    y_hg = o_all * lax.rsqrt(ms + EPS) * hgn_ref[...] * jnp.concatenate(gl, axis=0)
    for t in range(steps):
        cat_store(t, POOL_WIDTH, y_hg[t * gb:(t + 1) * gb])

    @pl.when(g == last)
    def _():
        for t in range(steps):
            y = _dot(cat_ref[t * nb:(t + 1) * nb, :].astype(BF16), wout_ref[...])
            xo_ref[t] = x_ref[t] + mod_ref[5] * y


def _sample_mixer(x, mod, norm, w_in, w_out, pool_bd, pool_scale, lbs, hg_norm, bms, sink_col, ones_blk,
                  state_pool, state_hgrn, cache_k, cache_v, layer, *, gb, pos0):
    steps, nb, _ = x.shape
    rows_th = steps * ATT_HEADS
    kern = functools.partial(_sample_mixer_kernel, steps=steps, nb=nb, gb=gb, pos0=pos0)
    full = lambda *shape: pl.BlockSpec(shape, lambda g: (0,) * len(shape))
    lsel = lambda *shape: pl.BlockSpec((None,) + shape, lambda g: (layer,) + (0,) * len(shape))
    grp = lambda *shape: pl.BlockSpec((None, gb) + shape, lambda g: (layer, g) + (0,) * len(shape))
    ogrp = lambda *shape: pl.BlockSpec((gb,) + shape, lambda g: (g,) + (0,) * len(shape))
    kvw = KV_HEADS * HEAD_DIM
    return pl.pallas_call(
        kern,
        grid=(nb // gb,),
        in_specs=[
            full(steps, nb, D_MODEL),
            pl.BlockSpec((None, N_MOD, nb, D_MODEL), lambda g: (layer, 0, 0, 0)),
            lsel(1, D_MODEL),
            lsel(D_MODEL, IN_WIDTH),
            lsel(D_MODEL, D_MODEL),
            lsel(POOL_WIDTH, POOL_WIDTH),
            lsel(1, POOL_WIDTH),
            lsel(1, HG_WIDTH),
            lsel(1, HG_WIDTH),
            full(rows_th, 2 * WINDOW),
            lsel(rows_th, 1),
            full(HG_WIDTH, HG_WIDTH),
            grp(POOL_BUF, POOL_WIDTH),
            grp(HG_WIDTH, HG_DK),
            grp(WINDOW, kvw),
            grp(WINDOW, kvw),
        ],
        out_specs=[
            full(steps, nb, D_MODEL),
            ogrp(POOL_BUF, POOL_WIDTH),
            ogrp(HG_WIDTH, HG_DK),
            ogrp(WINDOW, kvw),
            ogrp(WINDOW, kvw),
        ],
        out_shape=[
            jax.ShapeDtypeStruct((steps, nb, D_MODEL), F32),
            jax.ShapeDtypeStruct((nb, POOL_BUF, POOL_WIDTH), F32),
            jax.ShapeDtypeStruct((nb, HG_WIDTH, HG_DK), F32),
            jax.ShapeDtypeStruct((nb, WINDOW, kvw), F32),
            jax.ShapeDtypeStruct((nb, WINDOW, kvw), F32),
        ],
        scratch_shapes=[
            pltpu.VMEM((steps * nb, IN_WIDTH), F32),
            pltpu.VMEM((steps * nb, D_MODEL), F32),
            pltpu.VMEM((gb, rows_th, kvw), F32),
            pltpu.VMEM((gb, rows_th, kvw), F32),
            pltpu.VMEM((gb, steps * HG_HEADS, HG_WIDTH), F32),
            pltpu.VMEM((gb, steps * HG_HEADS, HG_WIDTH), F32),
            pltpu.VMEM((gb, steps * HG_HEADS, HG_DK), F32),
            pltpu.VMEM((gb, 8, HG_WIDTH), F32),
            pltpu.VMEM((gb, steps * HG_HEADS, HG_DK), F32),
            pltpu.VMEM((gb, 8, kvw), F32),
            pltpu.VMEM((gb, 8, kvw), F32),
        ],
        compiler_params=pltpu.CompilerParams(dimension_semantics=("arbitrary",), vmem_limit_bytes=VMEM_LIMIT),
        name="sample_mixer",
    )(x, mod, norm.reshape(DEPTH, 1, D_MODEL), w_in, w_out, pool_bd, pool_scale.reshape(DEPTH, 1, POOL_WIDTH),
      lbs.reshape(DEPTH, 1, HG_WIDTH), hg_norm, bms, sink_col, ones_blk, state_pool, state_hgrn, cache_k, cache_v)


FFN_TILE = 256
MIXER_TILE = 256
SAMPLE_GROUP = 8
PAST_LEN = 8192


def kernel(x_prompt, x_sample, c_prompt, c_sample, state_pool, state_hgrn, cache_k_win, cache_v_win, norm_ffn1, norm_mix, norm_ffn2, w_mod, b_mod, ffn1_w_gate, ffn1_w_up, ffn1_w_down, w_in, w_out, pool_w, pool_scale, hgrn_lower, hgrn_norm, attn_sinks, rel_bias, ffn2_w_gate, ffn2_w_up, ffn2_w_down, norm_final):
    bsz, seq, _ = x_prompt.shape
    nb, steps, _ = x_sample.shape
    kvw = KV_HEADS * HEAD_DIM

    mod = _modulation(jnp.concatenate([c_sample, c_prompt], axis=0), w_mod, b_mod)
    prompt_mod_block = nb // 8
    lbs = _lower_bounds(hgrn_lower)
    maps_p = np.stack([_bucket_map(WINDOW, 2 * WINDOW, WINDOW, 0, 2 * WINDOW),
                       _bucket_map(WINDOW, 2 * WINDOW, WINDOW, WINDOW, 2 * WINDOW)])
    bm_p = _bias_tables(jnp.asarray(maps_p), rel_bias)
    maps_s = _bucket_map(8, 2 * WINDOW, steps, 0, WINDOW + steps)[None]
    bm_s = _bias_tables(jnp.asarray(maps_s), rel_bias)[0]
    bm_s = jnp.swapaxes(bm_s, 0, 1).reshape(8 * ATT_HEADS, 2 * WINDOW)[:steps * ATT_HEADS]
    sink_col = jnp.tile(attn_sinks, (1, steps)).reshape(DEPTH, steps * ATT_HEADS, 1)
    consts = _mixer_consts(MIXER_TILE)
    hgn = jnp.tile(hgrn_norm, (1, HG_HEADS)).reshape(DEPTH, 1, HG_WIDTH)
    pool_bd = _pool_block_diag(pool_w)
    w_in_b, w_out_b = w_in.astype(BF16), w_out.astype(BF16)
    f1 = (ffn1_w_gate.astype(BF16), ffn1_w_up.astype(BF16), ffn1_w_down.astype(BF16))
    f2 = (ffn2_w_gate.astype(BF16), ffn2_w_up.astype(BF16), ffn2_w_down.astype(BF16))
    s_hgrn = state_hgrn.reshape(DEPTH, nb, HG_WIDTH, HG_DK)
    c_k = cache_k_win.reshape(DEPTH, nb, WINDOW, kvw)
    c_v = cache_v_win.reshape(DEPTH, nb, WINDOW, kvw)

    xp = x_prompt.reshape(bsz * seq, D_MODEL)
    xs = jnp.swapaxes(x_sample, 0, 1).reshape(steps * nb, D_MODEL)
    ffn_p = functools.partial(_ffn, tm=FFN_TILE, per_row=False, mod_rows=8, mod_row_block=prompt_mod_block)
    ffn_s = functools.partial(_ffn, tm=nb, per_row=True, mod_rows=nb, mod_row_block=0)
    st_p, st_s = [], []
    for l in range(DEPTH):
        xp = ffn_p(xp, mod, norm_ffn1, *f1, l, 0)
        xs = ffn_s(xs, mod, norm_ffn1, *f1, l, 0)
        xp, *sp = _prompt_mixer(xp.reshape(bsz, seq, D_MODEL), mod, norm_mix, w_in_b, w_out_b, pool_bd, pool_scale,
                                lbs, hgn, bm_p, attn_sinks, consts, l, tt=MIXER_TILE, mod_row_block=prompt_mod_block)
        xs, *ss = _sample_mixer(xs.reshape(steps, nb, D_MODEL), mod, norm_mix, w_in_b, w_out_b, pool_bd, pool_scale,
                                lbs, hgn, bm_s, sink_col, consts[1], state_pool, s_hgrn, c_k, c_v, l,
                                gb=SAMPLE_GROUP, pos0=PAST_LEN)
        st_p.append(sp)
        st_s.append(ss)
        xp = ffn_p(xp.reshape(bsz * seq, D_MODEL), mod, norm_ffn2, *f2, l, 6)
        xs = ffn_s(xs.reshape(steps * nb, D_MODEL), mod, norm_ffn2, *f2, l, 6)

    y_prompt = _final_norm(xp, norm_final, FFN_TILE).reshape(bsz, seq, D_MODEL)
    y_sample = jnp.swapaxes(_final_norm(xs, norm_final, nb).reshape(steps, nb, D_MODEL), 0, 1)

    def stacked(states, i, shape):
        return jnp.stack([s[i] for s in states]).reshape((DEPTH,) + shape)

    outs = []
    for states, n in ((st_p, bsz), (st_s, nb)):
        outs += [stacked(states, 0, (n, POOL_BUF, POOL_WIDTH)),
                 stacked(states, 1, (n, HG_HEADS, HG_DK, HG_DK)),
                 stacked(states, 2, (n, WINDOW, KV_HEADS, HEAD_DIM)),
                 stacked(states, 3, (n, WINDOW, KV_HEADS, HEAD_DIM))]
    return (y_prompt, y_sample, *outs)
```

```python
import functools
import math

import numpy as np
import jax
import jax.numpy as jnp
from jax import lax
from jax.experimental import pallas as pl
from jax.experimental.pallas import tpu as pltpu

F32 = jnp.float32
BF16 = jnp.bfloat16

D_MODEL = 1024
DEPTH = 4
N_MOD = 9
D_FF = 2816
EPS = 1e-6
POOL_WIDTH = 256
POOL_WINDOWS = (2, 4, 8, 16)
POOL_GW = 64
POOL_BUF = 15
HG_WIDTH = 256
HG_HEADS = 4
HG_DK = 64
ATT_WIDTH = 512
ATT_HEADS = 8
KV_HEADS = 2
GQA_GROUP = 4
HEAD_DIM = 64
WINDOW = 128
REL_BUCKETS = 32
REL_MAX_DIST = 128
OFF_POOL, OFF_HQ, OFF_HF, OFF_HI, OFF_HG, OFF_AQ, OFF_AK, OFF_AV = 0, 256, 512, 768, 1024, 1280, 1792, 1920
IN_WIDTH = 2048
NEG = -1e30

VMEM_LIMIT = 56 * 1024 * 1024


def _dot(a, b):
    return jnp.dot(a, b, preferred_element_type=F32)


def _dot_nt(a, b):
    return lax.dot_general(a, b, (((1,), (1,)), ((), ())), preferred_element_type=F32)


def _dot_tn(a, b):
    return lax.dot_general(a, b, (((0,), (0,)), ((), ())), preferred_element_type=F32)


def _sigmoid(x):
    return 1.0 / (1.0 + jnp.exp(-x))


def _silu(x):
    return x * _sigmoid(x)


def _norm_mod(x, g, shift, scale):
    r = lax.rsqrt(jnp.mean(x * x, axis=-1, keepdims=True) + EPS)
    return (x * r) * g * (1.0 + scale) + shift


def _split3(x):
    hi = x.astype(BF16)
    r1 = x - hi.astype(F32)
    mid = r1.astype(BF16)
    lo = (r1 - mid.astype(F32)).astype(BF16)
    return hi, mid, lo


def _mod_kernel(c_ref, w_ref, b_ref, o_ref):
    sc = _silu(c_ref[...]).astype(BF16)
    o_ref[...] = _dot(sc, w_ref[...].astype(BF16)) + b_ref[...]


def _modulation(c_all, w_mod, b_mod):
    rows = c_all.shape[0]
    return pl.pallas_call(
        _mod_kernel,
        grid=(DEPTH, N_MOD),
        in_specs=[
            pl.BlockSpec((rows, D_MODEL), lambda l, j: (0, 0)),
            pl.BlockSpec((None, D_MODEL, D_MODEL), lambda l, j: (l, 0, j)),
            pl.BlockSpec((None, None, 1, D_MODEL), lambda l, j: (l, j, 0, 0)),
        ],
        out_specs=pl.BlockSpec((None, None, rows, D_MODEL), lambda l, j: (l, j, 0, 0)),
        out_shape=jax.ShapeDtypeStruct((DEPTH, N_MOD, rows, D_MODEL), F32),
        compiler_params=pltpu.CompilerParams(dimension_semantics=("arbitrary", "arbitrary"),
                                             vmem_limit_bytes=VMEM_LIMIT),
        name="adaln_mod",
    )(c_all, w_mod, b_mod.reshape(DEPTH, N_MOD, 1, D_MODEL))


def _mod_rows(mod_ref, j, row, per_row):
    if per_row:
        return mod_ref[j]
    return mod_ref[j, pl.ds(row, 1), :]


def _ffn_kernel(x_ref, mod_ref, n_ref, wg_ref, wu_ref, wd_ref, nf_ref, o_ref, *, j0, per_row, tiles_per_seq,
                final):
    row = pl.program_id(0) // tiles_per_seq
    shift = _mod_rows(mod_ref, j0, row, per_row)
    scale = _mod_rows(mod_ref, j0 + 1, row, per_row)
    gate = _mod_rows(mod_ref, j0 + 2, row, per_row)
    x = x_ref[...]
    h = _norm_mod(x, n_ref[...], shift, scale).astype(BF16)
    g = _dot(h, wg_ref[...])
    u = _dot(h, wu_ref[...])
    a = (_silu(g) * u).astype(BF16)
    y = x + (0.5 * gate) * _dot(a, wd_ref[...])
    if final:
        y = (y * lax.rsqrt(jnp.mean(y * y, axis=-1, keepdims=True) + EPS)) * nf_ref[...]
    o_ref[...] = y


def _ffn(x, mod, norm, wg, wu, wd, norm_final, layer, j0, *, tm, per_row, mod_rows, mod_row_block, final=False):
    rows = x.shape[0]
    tiles_per_seq = 1 if per_row else (rows // mod_rows) // tm
    kern = functools.partial(_ffn_kernel, j0=j0, per_row=per_row, tiles_per_seq=tiles_per_seq, final=final)
    return pl.pallas_call(
        kern,
        grid=(rows // tm,),
        in_specs=[
            pl.BlockSpec((tm, D_MODEL), lambda i: (i, 0)),
            pl.BlockSpec((None, N_MOD, mod_rows, D_MODEL), lambda i: (layer, 0, mod_row_block, 0)),
            pl.BlockSpec((None, 1, D_MODEL), lambda i: (layer, 0, 0)),
            pl.BlockSpec((None, D_MODEL, D_FF), lambda i: (layer, 0, 0)),
            pl.BlockSpec((None, D_MODEL, D_FF), lambda i: (layer, 0, 0)),
            pl.BlockSpec((None, D_FF, D_MODEL), lambda i: (layer, 0, 0)),
            pl.BlockSpec((1, D_MODEL), lambda i: (0, 0)),
        ],
        out_specs=pl.BlockSpec((tm, D_MODEL), lambda i: (i, 0)),
        out_shape=jax.ShapeDtypeStruct((rows, D_MODEL), F32),
        compiler_params=pltpu.CompilerParams(dimension_semantics=("arbitrary",), vmem_limit_bytes=VMEM_LIMIT),
        name="swiglu_half_step",
    )(x, mod, norm.reshape(DEPTH, 1, D_MODEL), wg, wu, wd, norm_final.reshape(1, D_MODEL))


def _lower_bound_kernel(x_ref, o_ref):
    rows = [x_ref[l:l + 1, :] for l in range(DEPTH)]
    m = functools.reduce(jnp.maximum, rows)
    e = [jnp.exp(r - m) for r in rows]
    s = functools.reduce(lambda a, b: a + b, e)
    sm = [ei / s for ei in e]
    acc = sm[0]
    for l in range(DEPTH):
        if l > 0:
            acc = acc + sm[l]
        o_ref[l:l + 1, :] = acc - sm[0]


def _lower_bounds(hgrn_lower):
    return pl.pallas_call(
        _lower_bound_kernel,
        out_shape=jax.ShapeDtypeStruct((DEPTH, HG_WIDTH), F32),
        name="hgrn_lower_bounds",
    )(hgrn_lower)


def _bucket_map(rows, cols, valid_rows, col_lo, col_hi):
    r = np.arange(rows)[:, None]
    c = np.arange(cols)[None, :]
    rel = c - WINDOW - r
    n = np.maximum(-rel, 0)
    exact = REL_BUCKETS // 2
    nf = np.maximum(n, 1).astype(np.float32)
    large = exact + (np.log(nf / np.float32(exact)) / np.float32(math.log(REL_MAX_DIST / exact))
                     * np.float32(REL_BUCKETS - exact)).astype(np.int32)
    large = np.minimum(large, REL_BUCKETS - 1)
    bucket = np.where(n < exact, n, large)
    valid = (rel <= 0) & (rel > -WINDOW) & (r < valid_rows) & (c >= col_lo) & (c < col_hi)
    return np.where(valid, bucket, -1).astype(np.int32)


def _bias_table_kernel(bkt_ref, rb_ref, o_ref):
    h = pl.program_id(1)
    bkt = bkt_ref[...]
    acc = jnp.full(bkt.shape, NEG, F32)
    for b in range(REL_BUCKETS):
        acc = jnp.where(bkt == b, rb_ref[b, h], acc)
    o_ref[...] = acc


def _bias_tables(bucket_maps, rel_bias):
    m, r, c = bucket_maps.shape
    return pl.pallas_call(
        _bias_table_kernel,
        grid=(m, ATT_HEADS),
        in_specs=[pl.BlockSpec((None, r, c), lambda i, h: (i, 0, 0)),
                  pl.BlockSpec(memory_space=pltpu.SMEM)],
        out_specs=pl.BlockSpec((None, None, r, c), lambda i, h: (i, h, 0, 0)),
        out_shape=jax.ShapeDtypeStruct((m, ATT_HEADS, r, c), F32),
        compiler_params=pltpu.CompilerParams(dimension_semantics=("arbitrary", "arbitrary")),
        name="rel_bias_tables",
    )(bucket_maps, rel_bias)


def _block_ones(n, blk):
    i = np.arange(n)
    return (i[:, None] // blk == i[None, :] // blk)


def _mixer_consts(tt):
    tri = jnp.asarray(np.tril(np.ones((tt, tt), np.float32)), BF16)
    blk = _block_ones(HG_WIDTH, HG_DK)
    return tri, jnp.asarray(blk, BF16), jnp.asarray(blk, F32)


def _pool_block_diag(pool_w):
    out = jnp.zeros((DEPTH, POOL_WIDTH, POOL_WIDTH), BF16)
    for g in range(len(POOL_WINDOWS)):
        sl = slice(g * POOL_GW, (g + 1) * POOL_GW)
        out = out.at[:, sl, sl].set(pool_w[:, g].astype(BF16))
    return out


def _head_masks():
    lane = lax.broadcasted_iota(jnp.int32, (1, HG_WIDTH), 1)
    return [jnp.where((lane >= HG_DK * h) & (lane < HG_DK * (h + 1)), 1.0, 0.0).astype(F32) for h in range(HG_HEADS)]


def _prompt_mixer_kernel(x_ref, mod_ref, n_ref, win_ref, wout_ref, pbd_ref, pscale_ref, lb_ref, hgn_ref,
                         bm_ref, sink_ref, tri_ref, ones_ref, bdm_ref,
                         xo_ref, pool_o, s_o, k_o, v_o,
                         z_ref, ubuf, kbuf, vbuf, st_ref, cat_ref, *, layer, tt):
    b = pl.program_id(0)
    t = pl.program_id(1)
    last = pl.num_programs(1) - 1

    @pl.when(t == 0)
    def _():
        ubuf[0:16, :] = jnp.zeros((16, POOL_WIDTH), F32)
        kbuf[:, 0:WINDOW, :] = jnp.zeros((KV_HEADS, WINDOW, HEAD_DIM), BF16)
        vbuf[:, 0:WINDOW, :] = jnp.zeros((KV_HEADS, WINDOW, HEAD_DIM), BF16)
        st_ref[...] = jnp.zeros((HG_WIDTH, HG_WIDTH), F32)

    x = x_ref[...]
    shift = mod_ref[3, pl.ds(b, 1), :]
    scale = mod_ref[4, pl.ds(b, 1), :]
    gate = mod_ref[5, pl.ds(b, 1), :]
    h = _norm_mod(x, n_ref[...], shift, scale).astype(BF16)
    z_ref[...] = _dot(h, win_ref[...])

    lane = lax.broadcasted_iota(jnp.int32, (1, POOL_WIDTH), 1)

    u = z_ref[:, OFF_POOL:OFF_POOL + POOL_WIDTH]
    ubuf[16:16 + tt, :] = u
    e = ubuf[...]
    s2 = e + pltpu.roll(e, 1, 0)
    s4 = s2 + pltpu.roll(s2, 2, 0)
    s8 = s4 + pltpu.roll(s4, 4, 0)
    s16 = s8 + pltpu.roll(s8, 8, 0)
    sel = jnp.where(lane < 64, s2, jnp.where(lane < 128, s4, jnp.where(lane < 192, s8, s16)))[16:]
    wl = jnp.where(lane < 64, 2.0, jnp.where(lane < 128, 4.0, jnp.where(lane < 192, 8.0, 16.0))).astype(F32)
    pos1 = (t * tt + 1 + lax.broadcasted_iota(jnp.int32, (tt, 1), 0)).astype(F32)
    dpool = sel / jnp.minimum(pos1, wl) - u
    ypool = _dot(dpool.astype(BF16), pbd_ref[...]) * pscale_ref[...]
    cat_ref[:, 0:POOL_WIDTH] = ypool.astype(BF16)
    ubuf[0:16, :] = ubuf[tt:tt + 16, :]


    for g in range(KV_HEADS):
        kbuf[g, WINDOW:WINDOW + tt, :] = z_ref[:, OFF_AK + HEAD_DIM * g:OFF_AK + HEAD_DIM * (g + 1)].astype(BF16)
        vbuf[g, WINDOW:WINDOW + tt, :] = z_ref[:, OFF_AV + HEAD_DIM * g:OFF_AV + HEAD_DIM * (g + 1)].astype(BF16)
    first = jnp.where(t == 0, 1, 0)
    bands = [(j, g) for j in range(tt // WINDOW) for g in range(KV_HEADS)]
    logits = []
    for j, g in bands:
        r0 = j * WINDOW
        c0 = OFF_AQ + HEAD_DIM * g * GQA_GROUP
        qg = jnp.concatenate([z_ref[r0:r0 + WINDOW, c0 + HEAD_DIM * i:c0 + HEAD_DIM * (i + 1)]
                              for i in range(GQA_GROUP)], axis=0)
        qg = (qg * (HEAD_DIM ** -0.5)).astype(BF16)
        logits.append(_dot_nt(qg, kbuf[g, r0:r0 + 2 * WINDOW, :]))

    lb = lb_ref[...]
    q = _silu(z_ref[:, OFF_HQ:OFF_HQ + HG_WIDTH])
    f = lb + (1.0 - lb) * _sigmoid(z_ref[:, OFF_HF:OFF_HF + HG_WIDTH])
    k = 1.0 - f
    v = z_ref[:, OFF_HI:OFF_HI + HG_WIDTH]
    vb = v.astype(BF16)
    tri = tri_ref[...]
    hi, mid, lo = _split3(jnp.log(f))
    bcum = _dot(tri, hi) + _dot(tri, mid) + _dot(tri, lo)
    ones_blk = ones_ref[...]
    hm = _head_masks()

    n16 = tt // 16
    half_shape = (n16, 2, 8, HG_WIDTH)
    q4 = q.reshape(half_shape)
    b4 = bcum.reshape(half_shape)
    c4 = (bcum - jnp.log(k)).reshape(half_shape)
    v4 = v.reshape(half_shape)
    rowi = lax.broadcasted_iota(jnp.int32, (1, 8, 1), 1)
    o_up = jnp.zeros((n16, 8, HG_WIDTH), F32)
    o_dn = jnp.zeros((n16, 8, HG_WIDTH), F32)
    nh = n16 * 8
    for s in range(8):
        cs = c4[:, 0, s:s + 1, :]
        vs = v4[:, 0, s:s + 1, :]
        x_up = jnp.where(rowi >= s, q4[:, 0] * jnp.exp(b4[:, 0] - cs), 0.0)
        x_dn = q4[:, 1] * jnp.exp(b4[:, 1] - cs)
        xs = jnp.concatenate([x_up.reshape(nh, HG_WIDTH), x_dn.reshape(nh, HG_WIDTH)], axis=0)
        r = _dot(xs.astype(BF16), ones_blk)
        o_up = o_up + r[0:nh].reshape(n16, 8, HG_WIDTH) * vs
        o_dn = o_dn + r[nh:2 * nh].reshape(n16, 8, HG_WIDTH) * vs
    for s in range(8):
        cs = c4[:, 1, s:s + 1, :]
        x_dn = jnp.where(rowi >= s, q4[:, 1] * jnp.exp(b4[:, 1] - cs), 0.0)
        r = _dot(x_dn.reshape(nh, HG_WIDTH).astype(BF16), ones_blk)
        o_dn = o_dn + r.reshape(n16, 8, HG_WIDTH) * v4[:, 1, s:s + 1, :]
    o_tot = jnp.concatenate([o_up[:, None], o_dn[:, None]], axis=1).reshape(tt, HG_WIDTH)

    blocks = []
    blk = 32
    while blk <= tt:
        blocks += [(i * blk, i * blk + blk // 2, (i + 1) * blk) for i in range(tt // blk)]
        blk *= 2
    scores = []
    for lo_s, mid_s, hi_s in blocks:
        ref = bcum[mid_s - 1:mid_s, :]
        qt = q[mid_s:hi_s] * jnp.exp(bcum[mid_s:hi_s] - ref)
        kt = (k[lo_s:mid_s] * jnp.exp(ref - bcum[lo_s:mid_s])).astype(BF16)
        qs = jnp.concatenate([qt * hm[hh] for hh in range(HG_HEADS)], axis=0).astype(BF16)
        scores.append(_dot_nt(qs, kt).astype(BF16))

    st = st_ref[...]
    qin = (q * jnp.exp(bcum)).astype(BF16)
    o_tot = o_tot + _dot_nt(qin, st.astype(BF16))
    blast = bcum[tt - 1:tt, :]
    kout = (k * jnp.exp(blast - bcum)).astype(BF16)
    st_ref[...] = st * jnp.exp(blast) + _dot_tn(vb, kout) * bdm_ref[...]

    probs, dens = [], []
    for (j, g), lg_all in zip(bands, logits):
        h0 = g * GQA_GROUP
        bm = bm_ref[first, h0:h0 + GQA_GROUP] if j == 0 else bm_ref[0, h0:h0 + GQA_GROUP]
        ps = []
        for i in range(GQA_GROUP):
            lg = lg_all[i * WINDOW:(i + 1) * WINDOW] + bm[i]
            sink = sink_ref[layer, h0 + i]
            m = jnp.maximum(jnp.max(lg, axis=-1, keepdims=True), sink)
            p = jnp.exp(lg - m)
            dens.append(jnp.sum(p, axis=-1, keepdims=True) + jnp.exp(sink - m))
            ps.append(p.astype(BF16))
        probs.append(jnp.concatenate(ps, axis=0))
    for n, (j, g) in enumerate(bands):
        r0 = j * WINDOW
        og = _dot(probs[n], vbuf[g, r0:r0 + 2 * WINDOW, :])
        for i in range(GQA_GROUP):
            cc = POOL_WIDTH + HG_WIDTH + HEAD_DIM * (g * GQA_GROUP + i)
            oh = og[i * WINDOW:(i + 1) * WINDOW] / dens[n * GQA_GROUP + i]
            cat_ref[r0:r0 + WINDOW, cc:cc + HEAD_DIM] = oh.astype(BF16)
    for g in range(KV_HEADS):
        kbuf[g, 0:WINDOW, :] = kbuf[g, tt:tt + WINDOW, :]
        vbuf[g, 0:WINDOW, :] = vbuf[g, tt:tt + WINDOW, :]

    adds = {}
    for (lo_s, mid_s, hi_s), a in zip(blocks, scores):
        half = mid_s - lo_s
        ov = _dot(a, vb[lo_s:mid_s])
        oi = ov[0:half] * hm[0]
        for hh in range(1, HG_HEADS):
            oi = oi + ov[hh * half:(hh + 1) * half] * hm[hh]
        adds.setdefault(2 * half, []).extend([jnp.zeros((half, HG_WIDTH), F32), oi])
    for parts in adds.values():
        o_tot = o_tot + jnp.concatenate(parts, axis=0)

    sq_hi, sq_mid, _ = _split3(o_tot * o_tot)
    ms = (_dot(sq_hi, ones_blk) + _dot(sq_mid, ones_blk)) * (1.0 / HG_DK)
    y_hg = o_tot * lax.rsqrt(ms + EPS) * hgn_ref[...] * _silu(z_ref[:, OFF_HG:OFF_HG + HG_WIDTH])
    cat_ref[:, POOL_WIDTH:POOL_WIDTH + HG_WIDTH] = y_hg.astype(BF16)

    xo_ref[...] = x + gate * _dot(cat_ref[...], wout_ref[...])

    @pl.when(t == last)
    def _():
        pool_o[...] = ubuf[pl.ds(1, POOL_BUF), :]
        s_kv = st_ref[...].T
        for hh in range(HG_HEADS):
            s_o[hh] = s_kv[HG_DK * hh:HG_DK * (hh + 1), HG_DK * hh:HG_DK * (hh + 1)]
        k_o[...] = z_ref[tt - WINDOW:tt, OFF_AK:OFF_AV]
        v_o[...] = z_ref[tt - WINDOW:tt, OFF_AV:IN_WIDTH]


def _prompt_mixer(x, mod, norm, w_in, w_out, pool_bd, pool_scale, lbs, hg_norm, bm, sinks, consts, layer, *,
                  tt, mod_row_block):
    bsz, seq, _ = x.shape
    tri, ones_blk, bdm = consts
    kern = functools.partial(_prompt_mixer_kernel, layer=layer, tt=tt)
    full = lambda *shape: pl.BlockSpec(shape, lambda b, t: (0,) * len(shape))
    lsel = lambda *shape: pl.BlockSpec((None,) + shape, lambda b, t: (layer,) + (0,) * len(shape))
    mix_tile = lambda b, t: (b, t, 0)
    mix_seq = lambda *z: (lambda b, t: (b,) + z)
    return pl.pallas_call(
        kern,
        grid=(bsz, seq // tt),
        in_specs=[
            pl.BlockSpec((None, tt, D_MODEL), mix_tile),
            pl.BlockSpec((None, N_MOD, 8, D_MODEL), lambda b, t: (layer, 0, mod_row_block, 0)),
            lsel(1, D_MODEL),
            lsel(D_MODEL, IN_WIDTH),
            lsel(D_MODEL, D_MODEL),
            lsel(POOL_WIDTH, POOL_WIDTH),
            lsel(1, POOL_WIDTH),
            lsel(1, HG_WIDTH),
            lsel(1, HG_WIDTH),
            full(2, ATT_HEADS, WINDOW, 2 * WINDOW),
            pl.BlockSpec(memory_space=pltpu.SMEM),
            full(tt, tt),
            full(HG_WIDTH, HG_WIDTH),
            full(HG_WIDTH, HG_WIDTH),
        ],
        out_specs=[
            pl.BlockSpec((None, tt, D_MODEL), mix_tile),
            pl.BlockSpec((None, POOL_BUF, POOL_WIDTH), mix_seq(0, 0)),
            pl.BlockSpec((None, HG_HEADS, HG_DK, HG_DK), mix_seq(0, 0, 0)),
            pl.BlockSpec((None, WINDOW, KV_HEADS * HEAD_DIM), mix_seq(0, 0)),
            pl.BlockSpec((None, WINDOW, KV_HEADS * HEAD_DIM), mix_seq(0, 0)),
        ],
        out_shape=[
            jax.ShapeDtypeStruct((bsz, seq, D_MODEL), F32),
            jax.ShapeDtypeStruct((bsz, POOL_BUF, POOL_WIDTH), F32),
            jax.ShapeDtypeStruct((bsz, HG_HEADS, HG_DK, HG_DK), F32),
            jax.ShapeDtypeStruct((bsz, WINDOW, KV_HEADS * HEAD_DIM), F32),
            jax.ShapeDtypeStruct((bsz, WINDOW, KV_HEADS * HEAD_DIM), F32),
        ],
        scratch_shapes=[
            pltpu.VMEM((tt, IN_WIDTH), F32),
            pltpu.VMEM((16 + tt, POOL_WIDTH), F32),
            pltpu.VMEM((KV_HEADS, WINDOW + tt, HEAD_DIM), BF16),
            pltpu.VMEM((KV_HEADS, WINDOW + tt, HEAD_DIM), BF16),
            pltpu.VMEM((HG_WIDTH, HG_WIDTH), F32),
            pltpu.VMEM((tt, D_MODEL), BF16),
        ],
        compiler_params=pltpu.CompilerParams(dimension_semantics=("arbitrary", "arbitrary"),
                                             vmem_limit_bytes=VMEM_LIMIT),
        name="prompt_mixer",
    )(x, mod, norm.reshape(DEPTH, 1, D_MODEL), w_in, w_out, pool_bd, pool_scale.reshape(DEPTH, 1, POOL_WIDTH),
      lbs.reshape(DEPTH, 1, HG_WIDTH), hg_norm, bm, sinks, tri, ones_blk, bdm)


def _sample_mixer_kernel(x_ref, mod_ref, n_ref, win_ref, wout_ref, pbd_ref, pscale_ref, lb_ref, hgn_ref,
                         bms_ref, sinkc_ref, ones_ref, sp_ref, s0_ref, kc_ref, vc_ref,
                         xo_ref, pool_o, s_o, k_o, v_o,
                         z_ref, cat_ref, qs_ref, os_ref, qh_ref, kb_ref, vs_ref, g8_ref, oi_ref, kn_ref, vn_ref,
                         *, steps, nb, gb, pos0):
    g = pl.program_id(0)
    last = pl.num_programs(0) - 1
    base = pl.multiple_of(g * gb, gb)

    @pl.when(g == 0)
    def _():
        for t in range(steps):
            h = _norm_mod(x_ref[t], n_ref[...], mod_ref[3], mod_ref[4]).astype(BF16)
            z_ref[t * nb:(t + 1) * nb, :] = _dot(h, win_ref[...])
        g8_ref[...] = jnp.zeros(g8_ref.shape, F32)
        kn_ref[...] = jnp.zeros(kn_ref.shape, F32)
        vn_ref[...] = jnp.zeros(vn_ref.shape, F32)

    def zt(t, c0, w):
        return z_ref[pl.ds(t * nb + base, gb), c0:c0 + w]

    def cat_store(t, c0, val):
        cat_ref[pl.ds(t * nb + base, gb), c0:c0 + val.shape[1]] = val

    lane = lax.broadcasted_iota(jnp.int32, (1, POOL_WIDTH), 1)
    ones_blk = ones_ref[...]
    hm = _head_masks()

    ext = [sp_ref[:, r, :] for r in range(POOL_BUF)] + [zt(t, OFF_POOL, POOL_WIDTH) for t in range(steps)]
    for r in range(POOL_BUF):
        pool_o[:, r, :] = ext[r + steps]
    sums = {1: ext}
    w = 1
    while w < 16:
        prev = sums[w]
        sums[2 * w] = [None if (r < 2 * w - 1) else prev[r] + prev[r - w] for r in range(len(ext))]
        w *= 2
    dl = []
    for t in range(steps):
        r = POOL_BUF + t
        sel = jnp.where(lane < 64, sums[2][r], jnp.where(lane < 128, sums[4][r], jnp.where(lane < 192, sums[8][r], sums[16][r])))
        cnt = [float(min(pos0 + t + 1, wd)) for wd in POOL_WINDOWS]
        cl = jnp.where(lane < 64, cnt[0], jnp.where(lane < 128, cnt[1], jnp.where(lane < 192, cnt[2], cnt[3]))).astype(F32)
        dl.append(sel / cl - ext[r])
    ypool = _dot(jnp.concatenate(dl, axis=0).astype(BF16), pbd_ref[...]) * pscale_ref[...]
    for t in range(steps):
        cat_store(t, 0, ypool[t * gb:(t + 1) * gb])

    lb = lb_ref[...]
    qv, kv_, vv, bv = [], [], [], []
    bacc = None
    for t in range(steps):
        qv.append(_silu(zt(t, OFF_HQ, HG_WIDTH)))
        f = lb + (1.0 - lb) * _sigmoid(zt(t, OFF_HF, HG_WIDTH))
        kv_.append(1.0 - f)
        vv.append(zt(t, OFF_HI, HG_WIDTH))
        bacc = jnp.log(f) if bacc is None else bacc + jnp.log(f)
        bv.append(bacc)
    pairs = [(t, s) for t in range(steps) for s in range(t + 1)]
    xs = jnp.concatenate([qv[t] * kv_[s] * jnp.exp(bv[t] - bv[s]) for t, s in pairs], axis=0).astype(BF16)
    rr = _dot(xs, ones_blk)
    o_intra = [None] * steps
    for i, (t, s) in enumerate(pairs):
        term = rr[i * gb:(i + 1) * gb] * vv[s]
        o_intra[t] = term if o_intra[t] is None else o_intra[t] + term
    blast = bv[steps - 1]
    ghi, gmid, glo = _split3(jnp.exp(blast))
    g8_ref[:, 0, :] = ghi.astype(F32)
    g8_ref[:, 1, :] = gmid.astype(F32)
    g8_ref[:, 2, :] = glo.astype(F32)
    for t in range(steps):
        qin = qv[t] * jnp.exp(bv[t])
        kout = kv_[t] * jnp.exp(blast - bv[t])
        for hh in range(HG_HEADS):
            qh_ref[:, HG_HEADS * t + hh, :] = qin * hm[hh]
            kb_ref[:, HG_HEADS * t + hh, :] = kout * hm[hh]
            vs_ref[:, HG_HEADS * t + hh, :] = vv[t][:, HG_DK * hh:HG_DK * (hh + 1)]

    zeros64 = jnp.zeros((gb, HEAD_DIM), F32)
    for t in range(steps):
        kn_ref[:, t, :] = zt(t, OFF_AK, KV_HEADS * HEAD_DIM)
        vn_ref[:, t, :] = zt(t, OFF_AV, KV_HEADS * HEAD_DIM)
        for hd in range(ATT_HEADS):
            qsl = zt(t, OFF_AQ + HEAD_DIM * hd, HEAD_DIM) * (HEAD_DIM ** -0.5)
            parts = [qsl, zeros64] if hd < GQA_GROUP else [zeros64, qsl]
            qs_ref[:, ATT_HEADS * t + hd, :] = jnp.concatenate(parts, axis=1)
    k_o[:, 0:WINDOW - steps, :] = kc_ref[:, steps:WINDOW, :]
    v_o[:, 0:WINDOW - steps, :] = vc_ref[:, steps:WINDOW, :]
    for t in range(steps):
        k_o[:, WINDOW - steps + t, :] = zt(t, OFF_AK, KV_HEADS * HEAD_DIM)
        v_o[:, WINDOW - steps + t, :] = zt(t, OFF_AV, KV_HEADS * HEAD_DIM)

    bm_old = bms_ref[:, 0:WINDOW]
    bm_new = bms_ref[:, WINDOW:WINDOW + 8]
    sinkc = sinkc_ref[...]
    ones8 = jnp.ones((8, HG_DK), BF16)

    seqs = range(gb)
    qn = [qs_ref[n].astype(BF16) for n in seqs]
    l_old = [_dot_nt(qn[n], kc_ref[n].astype(BF16)) for n in seqs]
    l_new = [_dot_nt(qn[n], kn_ref[n].astype(BF16)) for n in seqs]
    for n in seqs:
        oi_ref[n] = _dot(qh_ref[n].astype(BF16), s0_ref[n].astype(BF16))
    gcol = [_dot_tn(g8_ref[n].astype(BF16), ones8) for n in seqs]
    ds = [_dot_tn(kb_ref[n].astype(BF16), vs_ref[n].astype(BF16)) for n in seqs]
    for n in seqs:
        s_o[n] = gcol[n] * s0_ref[n] + ds[n]
    p_old, p_new, den = [], [], []
    for n in seqs:
        lo_, ln_ = l_old[n] + bm_old, l_new[n] + bm_new
        m = jnp.maximum(jnp.maximum(jnp.max(lo_, axis=-1, keepdims=True), jnp.max(ln_, axis=-1, keepdims=True)), sinkc)
        po, pn = jnp.exp(lo_ - m), jnp.exp(ln_ - m)
        den.append(jnp.sum(po, axis=-1, keepdims=True) + jnp.sum(pn, axis=-1, keepdims=True) + jnp.exp(sinkc - m))
        p_old.append(po.astype(BF16))
        p_new.append(pn.astype(BF16))
    for n in seqs:
        o = _dot(p_old[n], vc_ref[n].astype(BF16)) + _dot(p_new[n], vn_ref[n].astype(BF16))
        os_ref[n] = o / den[n]

    ol, gl = [], []
    for t in range(steps):
        o_inter = jnp.concatenate([oi_ref[:, HG_HEADS * t + hh, :] for hh in range(HG_HEADS)], axis=1)
        ol.append(o_intra[t] + o_inter)
        gl.append(_silu(zt(t, OFF_HG, HG_WIDTH)))
        for hd in range(ATT_HEADS):
            c0 = HEAD_DIM * (hd // GQA_GROUP)
            cat_store(t, POOL_WIDTH + HG_WIDTH + HEAD_DIM * hd, os_ref[:, ATT_HEADS * t + hd, c0:c0 + HEAD_DIM])
    o_all = jnp.concatenate(ol, axis=0)
    sq_hi, sq_mid, _ = _split3(o_all * o_all)
    ms = (_dot(sq_hi, ones_blk) + _dot(sq_mid, ones_blk)) * (1.0 / HG_DK)
    y_hg = o_all * lax.rsqrt(ms + EPS) * hgn_ref[...] * jnp.concatenate(gl, axis=0)
    for t in range(steps):
        cat_store(t, POOL_WIDTH, y_hg[t * gb:(t + 1) * gb])

    @pl.when(g == last)
    def _():
        for t in range(steps):
            y = _dot(cat_ref[t * nb:(t + 1) * nb, :].astype(BF16), wout_ref[...])
            xo_ref[t] = x_ref[t] + mod_ref[5] * y


def _sample_mixer(x, mod, norm, w_in, w_out, pool_bd, pool_scale, lbs, hg_norm, bms, sink_col, ones_blk,
                  state_pool, state_hgrn, cache_k, cache_v, layer, *, gb, pos0):
    steps, nb, _ = x.shape
    rows_th = steps * ATT_HEADS
    kern = functools.partial(_sample_mixer_kernel, steps=steps, nb=nb, gb=gb, pos0=pos0)
    full = lambda *shape: pl.BlockSpec(shape, lambda g: (0,) * len(shape))
    lsel = lambda *shape: pl.BlockSpec((None,) + shape, lambda g: (layer,) + (0,) * len(shape))
    grp = lambda *shape: pl.BlockSpec((None, gb) + shape, lambda g: (layer, g) + (0,) * len(shape))
    ogrp = lambda *shape: pl.BlockSpec((gb,) + shape, lambda g: (g,) + (0,) * len(shape))
    kvw = KV_HEADS * HEAD_DIM
    return pl.pallas_call(
        kern,
        grid=(nb // gb,),
        in_specs=[
            full(steps, nb, D_MODEL),
            pl.BlockSpec((None, N_MOD, nb, D_MODEL), lambda g: (layer, 0, 0, 0)),
            lsel(1, D_MODEL),
            lsel(D_MODEL, IN_WIDTH),
            lsel(D_MODEL, D_MODEL),
            lsel(POOL_WIDTH, POOL_WIDTH),
            lsel(1, POOL_WIDTH),
            lsel(1, HG_WIDTH),
            lsel(1, HG_WIDTH),
            full(rows_th, 2 * WINDOW),
            lsel(rows_th, 1),
            full(HG_WIDTH, HG_WIDTH),
            grp(POOL_BUF, POOL_WIDTH),
            grp(HG_WIDTH, HG_DK),
            grp(WINDOW, kvw),
            grp(WINDOW, kvw),
        ],
        out_specs=[
            full(steps, nb, D_MODEL),
            ogrp(POOL_BUF, POOL_WIDTH),
            ogrp(HG_WIDTH, HG_DK),
            ogrp(WINDOW, kvw),
            ogrp(WINDOW, kvw),
        ],
        out_shape=[
            jax.ShapeDtypeStruct((steps, nb, D_MODEL), F32),
            jax.ShapeDtypeStruct((nb, POOL_BUF, POOL_WIDTH), F32),
            jax.ShapeDtypeStruct((nb, HG_WIDTH, HG_DK), F32),
            jax.ShapeDtypeStruct((nb, WINDOW, kvw), F32),
            jax.ShapeDtypeStruct((nb, WINDOW, kvw), F32),
        ],
        scratch_shapes=[
            pltpu.VMEM((steps * nb, IN_WIDTH), F32),
            pltpu.VMEM((steps * nb, D_MODEL), F32),
            pltpu.VMEM((gb, rows_th, kvw), F32),
            pltpu.VMEM((gb, rows_th, kvw), F32),
            pltpu.VMEM((gb, steps * HG_HEADS, HG_WIDTH), F32),
            pltpu.VMEM((gb, steps * HG_HEADS, HG_WIDTH), F32),
            pltpu.VMEM((gb, steps * HG_HEADS, HG_DK), F32),
            pltpu.VMEM((gb, 8, HG_WIDTH), F32),
            pltpu.VMEM((gb, steps * HG_HEADS, HG_DK), F32),
            pltpu.VMEM((gb, 8, kvw), F32),
            pltpu.VMEM((gb, 8, kvw), F32),
        ],
        compiler_params=pltpu.CompilerParams(dimension_semantics=("arbitrary",), vmem_limit_bytes=VMEM_LIMIT),
        name="sample_mixer",
    )(x, mod, norm.reshape(DEPTH, 1, D_MODEL), w_in, w_out, pool_bd, pool_scale.reshape(DEPTH, 1, POOL_WIDTH),
      lbs.reshape(DEPTH, 1, HG_WIDTH), hg_norm, bms, sink_col, ones_blk, state_pool, state_hgrn, cache_k, cache_v)


FFN_TILE = 256
MIXER_TILE = 256
SAMPLE_GROUP = 8
PAST_LEN = 8192


def kernel(x_prompt, x_sample, c_prompt, c_sample, state_pool, state_hgrn, cache_k_win, cache_v_win, norm_ffn1, norm_mix, norm_ffn2, w_mod, b_mod, ffn1_w_gate, ffn1_w_up, ffn1_w_down, w_in, w_out, pool_w, pool_scale, hgrn_lower, hgrn_norm, attn_sinks, rel_bias, ffn2_w_gate, ffn2_w_up, ffn2_w_down, norm_final):
    bsz, seq, _ = x_prompt.shape
    nb, steps, _ = x_sample.shape
    kvw = KV_HEADS * HEAD_DIM

    mod = _modulation(jnp.concatenate([c_sample, c_prompt], axis=0), w_mod, b_mod)
    prompt_mod_block = nb // 8
    lbs = _lower_bounds(hgrn_lower)
    maps_p = np.stack([_bucket_map(WINDOW, 2 * WINDOW, WINDOW, 0, 2 * WINDOW),
                       _bucket_map(WINDOW, 2 * WINDOW, WINDOW, WINDOW, 2 * WINDOW)])
    bm_p = _bias_tables(jnp.asarray(maps_p), rel_bias)
    maps_s = _bucket_map(8, 2 * WINDOW, steps, 0, WINDOW + steps)[None]
    bm_s = _bias_tables(jnp.asarray(maps_s), rel_bias)[0]
    bm_s = jnp.swapaxes(bm_s, 0, 1).reshape(8 * ATT_HEADS, 2 * WINDOW)[:steps * ATT_HEADS]
    sink_col = jnp.tile(attn_sinks, (1, steps)).reshape(DEPTH, steps * ATT_HEADS, 1)
    consts = _mixer_consts(MIXER_TILE)
    hgn = jnp.tile(hgrn_norm, (1, HG_HEADS)).reshape(DEPTH, 1, HG_WIDTH)
    pool_bd = _pool_block_diag(pool_w)
    w_in_b, w_out_b = w_in.astype(BF16), w_out.astype(BF16)
    f1 = (ffn1_w_gate.astype(BF16), ffn1_w_up.astype(BF16), ffn1_w_down.astype(BF16))
    f2 = (ffn2_w_gate.astype(BF16), ffn2_w_up.astype(BF16), ffn2_w_down.astype(BF16))
    s_hgrn = state_hgrn.reshape(DEPTH, nb, HG_WIDTH, HG_DK)
    c_k = cache_k_win.reshape(DEPTH, nb, WINDOW, kvw)
    c_v = cache_v_win.reshape(DEPTH, nb, WINDOW, kvw)

    xp = x_prompt.reshape(bsz * seq, D_MODEL)
    xs = jnp.swapaxes(x_sample, 0, 1).reshape(steps * nb, D_MODEL)
    ffn_p = functools.partial(_ffn, tm=FFN_TILE, per_row=False, mod_rows=8, mod_row_block=prompt_mod_block)
    ffn_s = functools.partial(_ffn, tm=nb, per_row=True, mod_rows=nb, mod_row_block=0)
    st_p, st_s = [], []
    for l in range(DEPTH):
        xp = ffn_p(xp, mod, norm_ffn1, *f1, norm_final, l, 0)
        xs = ffn_s(xs, mod, norm_ffn1, *f1, norm_final, l, 0)
        xp, *sp = _prompt_mixer(xp.reshape(bsz, seq, D_MODEL), mod, norm_mix, w_in_b, w_out_b, pool_bd, pool_scale,
                                lbs, hgn, bm_p, attn_sinks, consts, l, tt=MIXER_TILE, mod_row_block=prompt_mod_block)
        xs, *ss = _sample_mixer(xs.reshape(steps, nb, D_MODEL), mod, norm_mix, w_in_b, w_out_b, pool_bd, pool_scale,
                                lbs, hgn, bm_s, sink_col, consts[1], state_pool, s_hgrn, c_k, c_v, l,
                                gb=SAMPLE_GROUP, pos0=PAST_LEN)
        st_p.append(sp)
        st_s.append(ss)
        last = l == DEPTH - 1
        xp = ffn_p(xp.reshape(bsz * seq, D_MODEL), mod, norm_ffn2, *f2, norm_final, l, 6, final=last)
        xs = ffn_s(xs.reshape(steps * nb, D_MODEL), mod, norm_ffn2, *f2, norm_final, l, 6, final=last)

    y_prompt = xp.reshape(bsz, seq, D_MODEL)
    y_sample = jnp.swapaxes(xs.reshape(steps, nb, D_MODEL), 0, 1)

    def stacked(states, i, shape):
        return jnp.stack([s[i] for s in states]).reshape((DEPTH,) + shape)

    outs = []
    for states, n in ((st_p, bsz), (st_s, nb)):
        outs += [stacked(states, 0, (n, POOL_BUF, POOL_WIDTH)),
                 stacked(states, 1, (n, HG_HEADS, HG_DK, HG_DK)),
                 stacked(states, 2, (n, WINDOW, KV_HEADS, HEAD_DIM)),
                 stacked(states, 3, (n, WINDOW, KV_HEADS, HEAD_DIM))]
    return (y_prompt, y_sample, *outs)
```

```python
import functools
import math

import numpy as np
import jax
import jax.numpy as jnp
from jax import lax
from jax.experimental import pallas as pl
from jax.experimental.pallas import tpu as pltpu

F32 = jnp.float32
BF16 = jnp.bfloat16

D_MODEL = 1024
DEPTH = 4
N_MOD = 9
D_FF = 2816
EPS = 1e-6
POOL_WIDTH = 256
POOL_WINDOWS = (2, 4, 8, 16)
POOL_GW = 64
POOL_BUF = 15
HG_WIDTH = 256
HG_HEADS = 4
HG_DK = 64
ATT_WIDTH = 512
ATT_HEADS = 8
KV_HEADS = 2
GQA_GROUP = 4
HEAD_DIM = 64
WINDOW = 128
REL_BUCKETS = 32
REL_MAX_DIST = 128
OFF_POOL, OFF_HQ, OFF_HF, OFF_HI, OFF_HG, OFF_AQ, OFF_AK, OFF_AV = 0, 256, 512, 768, 1024, 1280, 1792, 1920
IN_WIDTH = 2048
NEG = -1e30

VMEM_LIMIT = 56 * 1024 * 1024


def _dot(a, b):
    return jnp.dot(a, b, preferred_element_type=F32)


def _dot_nt(a, b):
    return lax.dot_general(a, b, (((1,), (1,)), ((), ())), preferred_element_type=F32)


def _dot_tn(a, b):
    return lax.dot_general(a, b, (((0,), (0,)), ((), ())), preferred_element_type=F32)


def _sigmoid(x):
    return 1.0 / (1.0 + jnp.exp(-x))


def _silu(x):
    return x * _sigmoid(x)


def _norm_mod(x, g, shift, scale):
    r = lax.rsqrt(jnp.mean(x * x, axis=-1, keepdims=True) + EPS)
    return (x * r) * g * (1.0 + scale) + shift


def _split3(x):
    hi = x.astype(BF16)
    r1 = x - hi.astype(F32)
    mid = r1.astype(BF16)
    lo = (r1 - mid.astype(F32)).astype(BF16)
    return hi, mid, lo


def _mod_kernel(c_ref, w_ref, b_ref, o_ref):
    sc = _silu(c_ref[...]).astype(BF16)
    o_ref[...] = _dot(sc, w_ref[...].astype(BF16)) + b_ref[...]


def _modulation(c_all, w_mod, b_mod):
    rows = c_all.shape[0]
    return pl.pallas_call(
        _mod_kernel,
        grid=(DEPTH, N_MOD),
        in_specs=[
            pl.BlockSpec((rows, D_MODEL), lambda l, j: (0, 0)),
            pl.BlockSpec((None, D_MODEL, D_MODEL), lambda l, j: (l, 0, j)),
            pl.BlockSpec((None, None, 1, D_MODEL), lambda l, j: (l, j, 0, 0)),
        ],
        out_specs=pl.BlockSpec((None, None, rows, D_MODEL), lambda l, j: (l, j, 0, 0)),
        out_shape=jax.ShapeDtypeStruct((DEPTH, N_MOD, rows, D_MODEL), F32),
        compiler_params=pltpu.CompilerParams(dimension_semantics=("arbitrary", "arbitrary"),
                                             vmem_limit_bytes=VMEM_LIMIT),
        name="adaln_mod",
    )(c_all, w_mod, b_mod.reshape(DEPTH, N_MOD, 1, D_MODEL))


def _mod_rows(mod_ref, j, row, per_row):
    if per_row:
        return mod_ref[j]
    return mod_ref[j, pl.ds(row, 1), :]


def _ffn_kernel(x_ref, mod_ref, n_ref, wg_ref, wu_ref, wd_ref, nf_ref, o_ref, *, j0, per_row, tiles_per_seq,
                final):
    row = pl.program_id(0) // tiles_per_seq
    shift = _mod_rows(mod_ref, j0, row, per_row)
    scale = _mod_rows(mod_ref, j0 + 1, row, per_row)
    gate = _mod_rows(mod_ref, j0 + 2, row, per_row)
    x = x_ref[...]
    h = _norm_mod(x, n_ref[...], shift, scale).astype(BF16)
    g = _dot(h, wg_ref[...])
    u = _dot(h, wu_ref[...])
    a = (_silu(g) * u).astype(BF16)
    y = x + (0.5 * gate) * _dot(a, wd_ref[...])
    if final:
        y = (y * lax.rsqrt(jnp.mean(y * y, axis=-1, keepdims=True) + EPS)) * nf_ref[...]
    o_ref[...] = y


def _ffn(x, mod, norm, wg, wu, wd, norm_final, layer, j0, *, tm, per_row, mod_rows, mod_row_block, final=False):
    rows = x.shape[0]
    tiles_per_seq = 1 if per_row else (rows // mod_rows) // tm
    kern = functools.partial(_ffn_kernel, j0=j0, per_row=per_row, tiles_per_seq=tiles_per_seq, final=final)
    return pl.pallas_call(
        kern,
        grid=(rows // tm,),
        in_specs=[
            pl.BlockSpec((tm, D_MODEL), lambda i: (i, 0)),
            pl.BlockSpec((None, N_MOD, mod_rows, D_MODEL), lambda i: (layer, 0, mod_row_block, 0)),
            pl.BlockSpec((None, 1, D_MODEL), lambda i: (layer, 0, 0)),
            pl.BlockSpec((None, D_MODEL, D_FF), lambda i: (layer, 0, 0), pipeline_mode=pl.Buffered(1)),
            pl.BlockSpec((None, D_MODEL, D_FF), lambda i: (layer, 0, 0), pipeline_mode=pl.Buffered(1)),
            pl.BlockSpec((None, D_FF, D_MODEL), lambda i: (layer, 0, 0), pipeline_mode=pl.Buffered(1)),
            pl.BlockSpec((1, D_MODEL), lambda i: (0, 0)),
        ],
        out_specs=pl.BlockSpec((tm, D_MODEL), lambda i: (i, 0)),
        out_shape=jax.ShapeDtypeStruct((rows, D_MODEL), F32),
        compiler_params=pltpu.CompilerParams(dimension_semantics=("arbitrary",), vmem_limit_bytes=VMEM_LIMIT),
        name="swiglu_half_step",
    )(x, mod, norm.reshape(DEPTH, 1, D_MODEL), wg, wu, wd, norm_final.reshape(1, D_MODEL))


def _lower_bound_kernel(x_ref, o_ref):
    rows = [x_ref[l:l + 1, :] for l in range(DEPTH)]
    m = functools.reduce(jnp.maximum, rows)
    e = [jnp.exp(r - m) for r in rows]
    s = functools.reduce(lambda a, b: a + b, e)
    sm = [ei / s for ei in e]
    acc = sm[0]
    for l in range(DEPTH):
        if l > 0:
            acc = acc + sm[l]
        o_ref[l:l + 1, :] = acc - sm[0]


def _lower_bounds(hgrn_lower):
    return pl.pallas_call(
        _lower_bound_kernel,
        out_shape=jax.ShapeDtypeStruct((DEPTH, HG_WIDTH), F32),
        name="hgrn_lower_bounds",
    )(hgrn_lower)


def _bucket_map(rows, cols, valid_rows, col_lo, col_hi):
    r = np.arange(rows)[:, None]
    c = np.arange(cols)[None, :]
    rel = c - WINDOW - r
    n = np.maximum(-rel, 0)
    exact = REL_BUCKETS // 2
    nf = np.maximum(n, 1).astype(np.float32)
    large = exact + (np.log(nf / np.float32(exact)) / np.float32(math.log(REL_MAX_DIST / exact))
                     * np.float32(REL_BUCKETS - exact)).astype(np.int32)
    large = np.minimum(large, REL_BUCKETS - 1)
    bucket = np.where(n < exact, n, large)
    valid = (rel <= 0) & (rel > -WINDOW) & (r < valid_rows) & (c >= col_lo) & (c < col_hi)
    return np.where(valid, bucket, -1).astype(np.int32)


def _bias_table_kernel(bkt_ref, rb_ref, o_ref):
    h = pl.program_id(1)
    bkt = bkt_ref[...]
    acc = jnp.full(bkt.shape, NEG, F32)
    for b in range(REL_BUCKETS):
        acc = jnp.where(bkt == b, rb_ref[b, h], acc)
    o_ref[...] = acc


def _bias_tables(bucket_maps, rel_bias):
    m, r, c = bucket_maps.shape
    return pl.pallas_call(
        _bias_table_kernel,
        grid=(m, ATT_HEADS),
        in_specs=[pl.BlockSpec((None, r, c), lambda i, h: (i, 0, 0)),
                  pl.BlockSpec(memory_space=pltpu.SMEM)],
        out_specs=pl.BlockSpec((None, None, r, c), lambda i, h: (i, h, 0, 0)),
        out_shape=jax.ShapeDtypeStruct((m, ATT_HEADS, r, c), F32),
        compiler_params=pltpu.CompilerParams(dimension_semantics=("arbitrary", "arbitrary")),
        name="rel_bias_tables",
    )(bucket_maps, rel_bias)


def _block_ones(n, blk):
    i = np.arange(n)
    return (i[:, None] // blk == i[None, :] // blk)


def _mixer_consts(tt):
    tri = jnp.asarray(np.tril(np.ones((tt, tt), np.float32)), BF16)
    blk = _block_ones(HG_WIDTH, HG_DK)
    return tri, jnp.asarray(blk, BF16), jnp.asarray(blk, F32)


def _pool_block_diag(pool_w):
    out = jnp.zeros((DEPTH, POOL_WIDTH, POOL_WIDTH), BF16)
    for g in range(len(POOL_WINDOWS)):
        sl = slice(g * POOL_GW, (g + 1) * POOL_GW)
        out = out.at[:, sl, sl].set(pool_w[:, g].astype(BF16))
    return out


def _head_masks():
    lane = lax.broadcasted_iota(jnp.int32, (1, HG_WIDTH), 1)
    return [jnp.where((lane >= HG_DK * h) & (lane < HG_DK * (h + 1)), 1.0, 0.0).astype(F32) for h in range(HG_HEADS)]


def _prompt_mixer_kernel(x_ref, xn_ref, mod_ref, n_ref, win_ref, wout_ref, pbd_ref, pscale_ref, lb_ref, hgn_ref,
                         bm_ref, sink_ref, tri_ref, ones_ref, bdm_ref,
                         xo_ref, pool_o, s_o, k_o, v_o,
                         za_ref, zb_ref, ubuf, kbuf, vbuf, st_ref, cat_ref, *, layer, tt, nt, ntiles):
    b = pl.program_id(0)
    p = pl.program_id(1)
    last_pair = pl.num_programs(1) - 1
    nxt = jnp.minimum(b * nt + 2 * p + 2, ntiles - 1)
    bn = nxt // nt
    n_chunks = 8
    cw = IN_WIDTH // n_chunks

    def projector(x_rows, bsel, dst_ref):
        h = _norm_mod(x_rows, n_ref[...], mod_ref[3, pl.ds(bsel, 1), :], mod_ref[4, pl.ds(bsel, 1), :]).astype(BF16)
        state = {"k": 0}

        def step():
            k = state["k"]
            if k < n_chunks:
                dst_ref[:, k * cw:(k + 1) * cw] = _dot(h, win_ref[:, k * cw:(k + 1) * cw])
                state["k"] = k + 1

        return step

    @pl.when((b == 0) & (p == 0))
    def _():
        first_proj = projector(x_ref[0:tt, :], b, za_ref)
        for _ in range(n_chunks):
            first_proj()

    @pl.when(p == 0)
    def _():
        ubuf[0:16, :] = jnp.zeros((16, POOL_WIDTH), F32)
        kbuf[:, 0:WINDOW, :] = jnp.zeros((KV_HEADS, WINDOW, HEAD_DIM), BF16)
        vbuf[:, 0:WINDOW, :] = jnp.zeros((KV_HEADS, WINDOW, HEAD_DIM), BF16)
        st_ref[...] = jnp.zeros((HG_WIDTH, HG_WIDTH), F32)

    gate = mod_ref[5, pl.ds(b, 1), :]

    def mix(z_ref, x, row0, t, side):
        lane = lax.broadcasted_iota(jnp.int32, (1, POOL_WIDTH), 1)

        u = z_ref[:, OFF_POOL:OFF_POOL + POOL_WIDTH]
        ubuf[16:16 + tt, :] = u
        e = ubuf[...]
        s2 = e + pltpu.roll(e, 1, 0)
        s4 = s2 + pltpu.roll(s2, 2, 0)
        s8 = s4 + pltpu.roll(s4, 4, 0)
        s16 = s8 + pltpu.roll(s8, 8, 0)
        sel = jnp.where(lane < 64, s2, jnp.where(lane < 128, s4, jnp.where(lane < 192, s8, s16)))[16:]
        wl = jnp.where(lane < 64, 2.0, jnp.where(lane < 128, 4.0, jnp.where(lane < 192, 8.0, 16.0))).astype(F32)
        pos1 = (t * tt + 1 + lax.broadcasted_iota(jnp.int32, (tt, 1), 0)).astype(F32)
        dpool = sel / jnp.minimum(pos1, wl) - u
        ypool = _dot(dpool.astype(BF16), pbd_ref[...]) * pscale_ref[...]
        cat_ref[:, 0:POOL_WIDTH] = ypool.astype(BF16)
        ubuf[0:16, :] = ubuf[tt:tt + 16, :]
        side()


        for g in range(KV_HEADS):
            kbuf[g, WINDOW:WINDOW + tt, :] = z_ref[:, OFF_AK + HEAD_DIM * g:OFF_AK + HEAD_DIM * (g + 1)].astype(BF16)
            vbuf[g, WINDOW:WINDOW + tt, :] = z_ref[:, OFF_AV + HEAD_DIM * g:OFF_AV + HEAD_DIM * (g + 1)].astype(BF16)
        first = jnp.where(t == 0, 1, 0)
        bands = [(j, g) for j in range(tt // WINDOW) for g in range(KV_HEADS)]
        logits = []
        for j, g in bands:
            r0 = j * WINDOW
            c0 = OFF_AQ + HEAD_DIM * g * GQA_GROUP
            qg = jnp.concatenate([z_ref[r0:r0 + WINDOW, c0 + HEAD_DIM * i:c0 + HEAD_DIM * (i + 1)]
                                  for i in range(GQA_GROUP)], axis=0)
            qg = (qg * (HEAD_DIM ** -0.5)).astype(BF16)
            logits.append(_dot_nt(qg, kbuf[g, r0:r0 + 2 * WINDOW, :]))
        side()

        lb = lb_ref[...]
        q = _silu(z_ref[:, OFF_HQ:OFF_HQ + HG_WIDTH])
        f = lb + (1.0 - lb) * _sigmoid(z_ref[:, OFF_HF:OFF_HF + HG_WIDTH])
        k = 1.0 - f
        v = z_ref[:, OFF_HI:OFF_HI + HG_WIDTH]
        vb = v.astype(BF16)
        tri = tri_ref[...]
        hi, mid, lo = _split3(jnp.log(f))
        bcum = _dot(tri, hi) + _dot(tri, mid) + _dot(tri, lo)
        side()
        ones_blk = ones_ref[...]
        hm = _head_masks()

        n16 = tt // 16
        half_shape = (n16, 2, 8, HG_WIDTH)
        q4 = q.reshape(half_shape)
        b4 = bcum.reshape(half_shape)
        c4 = (bcum - jnp.log(k)).reshape(half_shape)
        v4 = v.reshape(half_shape)
        rowi = lax.broadcasted_iota(jnp.int32, (1, 8, 1), 1)
        o_up = jnp.zeros((n16, 8, HG_WIDTH), F32)
        o_dn = jnp.zeros((n16, 8, HG_WIDTH), F32)
        nh = n16 * 8
        for s in range(8):
            cs = c4[:, 0, s:s + 1, :]
            vs = v4[:, 0, s:s + 1, :]
            x_up = jnp.where(rowi >= s, q4[:, 0] * jnp.exp(b4[:, 0] - cs), 0.0)
            x_dn = q4[:, 1] * jnp.exp(b4[:, 1] - cs)
            xs = jnp.concatenate([x_up.reshape(nh, HG_WIDTH), x_dn.reshape(nh, HG_WIDTH)], axis=0)
            r = _dot(xs.astype(BF16), ones_blk)
            o_up = o_up + r[0:nh].reshape(n16, 8, HG_WIDTH) * vs
            o_dn = o_dn + r[nh:2 * nh].reshape(n16, 8, HG_WIDTH) * vs
            if s % 4 == 3:
                side()
        for s in range(8):
            cs = c4[:, 1, s:s + 1, :]
            x_dn = jnp.where(rowi >= s, q4[:, 1] * jnp.exp(b4[:, 1] - cs), 0.0)
            r = _dot(x_dn.reshape(nh, HG_WIDTH).astype(BF16), ones_blk)
            o_dn = o_dn + r.reshape(n16, 8, HG_WIDTH) * v4[:, 1, s:s + 1, :]
            if s % 4 == 3:
                side()
        o_tot = jnp.concatenate([o_up[:, None], o_dn[:, None]], axis=1).reshape(tt, HG_WIDTH)

        blocks = []
        blk = 32
        while blk <= tt:
            blocks += [(i * blk, i * blk + blk // 2, (i + 1) * blk) for i in range(tt // blk)]
            blk *= 2
        scores = []
        for lo_s, mid_s, hi_s in blocks:
            ref = bcum[mid_s - 1:mid_s, :]
            qt = q[mid_s:hi_s] * jnp.exp(bcum[mid_s:hi_s] - ref)
            kt = (k[lo_s:mid_s] * jnp.exp(ref - bcum[lo_s:mid_s])).astype(BF16)
            qs = jnp.concatenate([qt * hm[hh] for hh in range(HG_HEADS)], axis=0).astype(BF16)
            scores.append(_dot_nt(qs, kt).astype(BF16))

        st = st_ref[...]
        qin = (q * jnp.exp(bcum)).astype(BF16)
        o_tot = o_tot + _dot_nt(qin, st.astype(BF16))
        blast = bcum[tt - 1:tt, :]
        kout = (k * jnp.exp(blast - bcum)).astype(BF16)
        st_ref[...] = st * jnp.exp(blast) + _dot_tn(vb, kout) * bdm_ref[...]
        side()

        probs, dens = [], []
        for (j, g), lg_all in zip(bands, logits):
            h0 = g * GQA_GROUP
            bm = bm_ref[first, h0:h0 + GQA_GROUP] if j == 0 else bm_ref[0, h0:h0 + GQA_GROUP]
            ps = []
            for i in range(GQA_GROUP):
                lg = lg_all[i * WINDOW:(i + 1) * WINDOW] + bm[i]
                sink = sink_ref[layer, h0 + i]
                m = jnp.maximum(jnp.max(lg, axis=-1, keepdims=True), sink)
                p_ = jnp.exp(lg - m)
                dens.append(jnp.sum(p_, axis=-1, keepdims=True) + jnp.exp(sink - m))
                ps.append(p_.astype(BF16))
            probs.append(jnp.concatenate(ps, axis=0))
        for n, (j, g) in enumerate(bands):
            r0 = j * WINDOW
            og = _dot(probs[n], vbuf[g, r0:r0 + 2 * WINDOW, :])
            for i in range(GQA_GROUP):
                cc = POOL_WIDTH + HG_WIDTH + HEAD_DIM * (g * GQA_GROUP + i)
                oh = og[i * WINDOW:(i + 1) * WINDOW] / dens[n * GQA_GROUP + i]
                cat_ref[r0:r0 + WINDOW, cc:cc + HEAD_DIM] = oh.astype(BF16)
        for g in range(KV_HEADS):
            kbuf[g, 0:WINDOW, :] = kbuf[g, tt:tt + WINDOW, :]
            vbuf[g, 0:WINDOW, :] = vbuf[g, tt:tt + WINDOW, :]
        side()

        adds = {}
        for (lo_s, mid_s, hi_s), a in zip(blocks, scores):
            half = mid_s - lo_s
            ov = _dot(a, vb[lo_s:mid_s])
            oi = ov[0:half] * hm[0]
            for hh in range(1, HG_HEADS):
                oi = oi + ov[hh * half:(hh + 1) * half] * hm[hh]
            adds.setdefault(2 * half, []).extend([jnp.zeros((half, HG_WIDTH), F32), oi])
        for parts in adds.values():
            o_tot = o_tot + jnp.concatenate(parts, axis=0)

        sq_hi, sq_mid, _ = _split3(o_tot * o_tot)
        ms = (_dot(sq_hi, ones_blk) + _dot(sq_mid, ones_blk)) * (1.0 / HG_DK)
        y_hg = o_tot * lax.rsqrt(ms + EPS) * hgn_ref[...] * _silu(z_ref[:, OFF_HG:OFF_HG + HG_WIDTH])
        cat_ref[:, POOL_WIDTH:POOL_WIDTH + HG_WIDTH] = y_hg.astype(BF16)

        xo_ref[row0:row0 + tt, :] = x + gate * _dot(cat_ref[...], wout_ref[...])

        for _ in range(n_chunks):
            side()

    mix(za_ref, x_ref[0:tt, :], 0, 2 * p, projector(x_ref[tt:2 * tt, :], b, zb_ref))
    mix(zb_ref, x_ref[tt:2 * tt, :], tt, 2 * p + 1, projector(xn_ref[...], bn, za_ref))

    @pl.when(p == last_pair)
    def _():
        pool_o[...] = ubuf[pl.ds(1, POOL_BUF), :]
        s_kv = st_ref[...].T
        for hh in range(HG_HEADS):
            s_o[hh] = s_kv[HG_DK * hh:HG_DK * (hh + 1), HG_DK * hh:HG_DK * (hh + 1)]
        k_o[...] = zb_ref[tt - WINDOW:tt, OFF_AK:OFF_AV]
        v_o[...] = zb_ref[tt - WINDOW:tt, OFF_AV:IN_WIDTH]


def _prompt_mixer(x, mod, norm, w_in, w_out, pool_bd, pool_scale, lbs, hg_norm, bm, sinks, consts, layer, *,
                  tt, mod_row_block):
    bsz, seq, _ = x.shape
    tri, ones_blk, bdm = consts
    nt = seq // tt
    ntiles = bsz * nt
    kern = functools.partial(_prompt_mixer_kernel, layer=layer, tt=tt, nt=nt, ntiles=ntiles)
    full = lambda *shape: pl.BlockSpec(shape, lambda b, p: (0,) * len(shape))
    lsel = lambda *shape: pl.BlockSpec((None,) + shape, lambda b, p: (layer,) + (0,) * len(shape))
    mix_pair = lambda b, p: (b, p, 0)
    mix_seq = lambda *z: (lambda b, p: (b,) + z)

    def next_tile(b, p):
        nxt = jnp.minimum(b * nt + 2 * p + 2, ntiles - 1)
        return (nxt // nt, nxt % nt, 0)

    return pl.pallas_call(
        kern,
        grid=(bsz, nt // 2),
        in_specs=[
            pl.BlockSpec((None, 2 * tt, D_MODEL), mix_pair),
            pl.BlockSpec((None, tt, D_MODEL), next_tile),
            pl.BlockSpec((None, N_MOD, 8, D_MODEL), lambda b, p: (layer, 0, mod_row_block, 0)),
            lsel(1, D_MODEL),
            lsel(D_MODEL, IN_WIDTH),
            lsel(D_MODEL, D_MODEL),
            lsel(POOL_WIDTH, POOL_WIDTH),
            lsel(1, POOL_WIDTH),
            lsel(1, HG_WIDTH),
            lsel(1, HG_WIDTH),
            full(2, ATT_HEADS, WINDOW, 2 * WINDOW),
            pl.BlockSpec(memory_space=pltpu.SMEM),
            full(tt, tt),
            full(HG_WIDTH, HG_WIDTH),
            full(HG_WIDTH, HG_WIDTH),
        ],
        out_specs=[
            pl.BlockSpec((None, 2 * tt, D_MODEL), mix_pair),
            pl.BlockSpec((None, POOL_BUF, POOL_WIDTH), mix_seq(0, 0)),
            pl.BlockSpec((None, HG_HEADS, HG_DK, HG_DK), mix_seq(0, 0, 0)),
            pl.BlockSpec((None, WINDOW, KV_HEADS * HEAD_DIM), mix_seq(0, 0)),
            pl.BlockSpec((None, WINDOW, KV_HEADS * HEAD_DIM), mix_seq(0, 0)),
        ],
        out_shape=[
            jax.ShapeDtypeStruct((bsz, seq, D_MODEL), F32),
            jax.ShapeDtypeStruct((bsz, POOL_BUF, POOL_WIDTH), F32),
            jax.ShapeDtypeStruct((bsz, HG_HEADS, HG_DK, HG_DK), F32),
            jax.ShapeDtypeStruct((bsz, WINDOW, KV_HEADS * HEAD_DIM), F32),
            jax.ShapeDtypeStruct((bsz, WINDOW, KV_HEADS * HEAD_DIM), F32),
        ],
        scratch_shapes=[
            pltpu.VMEM((tt, IN_WIDTH), F32),
            pltpu.VMEM((tt, IN_WIDTH), F32),
            pltpu.VMEM((16 + tt, POOL_WIDTH), F32),
            pltpu.VMEM((KV_HEADS, WINDOW + tt, HEAD_DIM), BF16),
            pltpu.VMEM((KV_HEADS, WINDOW + tt, HEAD_DIM), BF16),
            pltpu.VMEM((HG_WIDTH, HG_WIDTH), F32),
            pltpu.VMEM((tt, D_MODEL), BF16),
        ],
        compiler_params=pltpu.CompilerParams(dimension_semantics=("arbitrary", "arbitrary"),
                                             vmem_limit_bytes=VMEM_LIMIT),
        name="prompt_mixer",
    )(x, x, mod, norm.reshape(DEPTH, 1, D_MODEL), w_in, w_out, pool_bd, pool_scale.reshape(DEPTH, 1, POOL_WIDTH),
      lbs.reshape(DEPTH, 1, HG_WIDTH), hg_norm, bm, sinks, tri, ones_blk, bdm)


def _sample_mixer_kernel(x_ref, mod_ref, n_ref, win_ref, wout_ref, pbd_ref, pscale_ref, lb_ref, hgn_ref,
                         bms_ref, sinkc_ref, ones_ref, sp_ref, s0_ref, kc_ref, vc_ref,
                         xo_ref, pool_o, s_o, k_o, v_o,
                         z_ref, cat_ref, qs_ref, os_ref, qh_ref, kb_ref, vs_ref, g8_ref, oi_ref, kn_ref, vn_ref,
                         *, steps, nb, gb, pos0):
    g = pl.program_id(0)
    last = pl.num_programs(0) - 1
    base = pl.multiple_of(g * gb, gb)

    @pl.when(g == 0)
    def _():
        for t in range(steps):
            h = _norm_mod(x_ref[t], n_ref[...], mod_ref[3], mod_ref[4]).astype(BF16)
            z_ref[t * nb:(t + 1) * nb, :] = _dot(h, win_ref[...])
        g8_ref[...] = jnp.zeros(g8_ref.shape, F32)
        kn_ref[...] = jnp.zeros(kn_ref.shape, F32)
        vn_ref[...] = jnp.zeros(vn_ref.shape, F32)

    def zt(t, c0, w):
        return z_ref[pl.ds(t * nb + base, gb), c0:c0 + w]

    def cat_store(t, c0, val):
        cat_ref[pl.ds(t * nb + base, gb), c0:c0 + val.shape[1]] = val

    lane = lax.broadcasted_iota(jnp.int32, (1, POOL_WIDTH), 1)
    ones_blk = ones_ref[...]
    hm = _head_masks()

    ext = [sp_ref[:, r, :] for r in range(POOL_BUF)] + [zt(t, OFF_POOL, POOL_WIDTH) for t in range(steps)]
    for r in range(POOL_BUF):
        pool_o[:, r, :] = ext[r + steps]
    sums = {1: ext}
    w = 1
    while w < 16:
        prev = sums[w]
        sums[2 * w] = [None if (r < 2 * w - 1) else prev[r] + prev[r - w] for r in range(len(ext))]
        w *= 2
    dl = []
    for t in range(steps):
        r = POOL_BUF + t
        sel = jnp.where(lane < 64, sums[2][r], jnp.where(lane < 128, sums[4][r], jnp.where(lane < 192, sums[8][r], sums[16][r])))
        cnt = [float(min(pos0 + t + 1, wd)) for wd in POOL_WINDOWS]
        cl = jnp.where(lane < 64, cnt[0], jnp.where(lane < 128, cnt[1], jnp.where(lane < 192, cnt[2], cnt[3]))).astype(F32)
        dl.append(sel / cl - ext[r])
    ypool = _dot(jnp.concatenate(dl, axis=0).astype(BF16), pbd_ref[...]) * pscale_ref[...]
    for t in range(steps):
        cat_store(t, 0, ypool[t * gb:(t + 1) * gb])

    lb = lb_ref[...]
    qv, kv_, vv, bv = [], [], [], []
    bacc = None
    for t in range(steps):
        qv.append(_silu(zt(t, OFF_HQ, HG_WIDTH)))
        f = lb + (1.0 - lb) * _sigmoid(zt(t, OFF_HF, HG_WIDTH))
        kv_.append(1.0 - f)
        vv.append(zt(t, OFF_HI, HG_WIDTH))
        bacc = jnp.log(f) if bacc is None else bacc + jnp.log(f)
        bv.append(bacc)
    pairs = [(t, s) for t in range(steps) for s in range(t + 1)]
    xs = jnp.concatenate([qv[t] * kv_[s] * jnp.exp(bv[t] - bv[s]) for t, s in pairs], axis=0).astype(BF16)
    rr = _dot(xs, ones_blk)
    o_intra = [None] * steps
    for i, (t, s) in enumerate(pairs):
        term = rr[i * gb:(i + 1) * gb] * vv[s]
        o_intra[t] = term if o_intra[t] is None else o_intra[t] + term
    blast = bv[steps - 1]
    ghi, gmid, glo = _split3(jnp.exp(blast))
    g8_ref[:, 0, :] = ghi.astype(F32)
    g8_ref[:, 1, :] = gmid.astype(F32)
    g8_ref[:, 2, :] = glo.astype(F32)
    for t in range(steps):
        qin = qv[t] * jnp.exp(bv[t])
        kout = kv_[t] * jnp.exp(blast - bv[t])
        for hh in range(HG_HEADS):
            qh_ref[:, HG_HEADS * t + hh, :] = qin * hm[hh]
            kb_ref[:, HG_HEADS * t + hh, :] = kout * hm[hh]
            vs_ref[:, HG_HEADS * t + hh, :] = vv[t][:, HG_DK * hh:HG_DK * (hh + 1)]

    zeros64 = jnp.zeros((gb, HEAD_DIM), F32)
    for t in range(steps):
        kn_ref[:, t, :] = zt(t, OFF_AK, KV_HEADS * HEAD_DIM)
        vn_ref[:, t, :] = zt(t, OFF_AV, KV_HEADS * HEAD_DIM)
        for hd in range(ATT_HEADS):
            qsl = zt(t, OFF_AQ + HEAD_DIM * hd, HEAD_DIM) * (HEAD_DIM ** -0.5)
            parts = [qsl, zeros64] if hd < GQA_GROUP else [zeros64, qsl]
            qs_ref[:, ATT_HEADS * t + hd, :] = jnp.concatenate(parts, axis=1)
    k_o[:, 0:WINDOW - steps, :] = kc_ref[:, steps:WINDOW, :]
    v_o[:, 0:WINDOW - steps, :] = vc_ref[:, steps:WINDOW, :]
    for t in range(steps):
        k_o[:, WINDOW - steps + t, :] = zt(t, OFF_AK, KV_HEADS * HEAD_DIM)
        v_o[:, WINDOW - steps + t, :] = zt(t, OFF_AV, KV_HEADS * HEAD_DIM)

    bm_old = bms_ref[:, 0:WINDOW]
    bm_new = bms_ref[:, WINDOW:WINDOW + 8]
    sinkc = sinkc_ref[...]
    ones8 = jnp.ones((8, HG_DK), BF16)

    seqs = range(gb)
    qn = [qs_ref[n].astype(BF16) for n in seqs]
    l_old = [_dot_nt(qn[n], kc_ref[n].astype(BF16)) for n in seqs]
    l_new = [_dot_nt(qn[n], kn_ref[n].astype(BF16)) for n in seqs]
    for n in seqs:
        oi_ref[n] = _dot(qh_ref[n].astype(BF16), s0_ref[n].astype(BF16))
    gcol = [_dot_tn(g8_ref[n].astype(BF16), ones8) for n in seqs]
    ds = [_dot_tn(kb_ref[n].astype(BF16), vs_ref[n].astype(BF16)) for n in seqs]
    for n in seqs:
        s_o[n] = gcol[n] * s0_ref[n] + ds[n]
    p_old, p_new, den = [], [], []
    for n in seqs:
        lo_, ln_ = l_old[n] + bm_old, l_new[n] + bm_new
        m = jnp.maximum(jnp.maximum(jnp.max(lo_, axis=-1, keepdims=True), jnp.max(ln_, axis=-1, keepdims=True)), sinkc)
        po, pn = jnp.exp(lo_ - m), jnp.exp(ln_ - m)
        den.append(jnp.sum(po, axis=-1, keepdims=True) + jnp.sum(pn, axis=-1, keepdims=True) + jnp.exp(sinkc - m))
        p_old.append(po.astype(BF16))
        p_new.append(pn.astype(BF16))
    for n in seqs:
        o = _dot(p_old[n], vc_ref[n].astype(BF16)) + _dot(p_new[n], vn_ref[n].astype(BF16))
        os_ref[n] = o / den[n]

    ol, gl = [], []
    for t in range(steps):
        o_inter = jnp.concatenate([oi_ref[:, HG_HEADS * t + hh, :] for hh in range(HG_HEADS)], axis=1)
        ol.append(o_intra[t] + o_inter)
        gl.append(_silu(zt(t, OFF_HG, HG_WIDTH)))
        for hd in range(ATT_HEADS):
            c0 = HEAD_DIM * (hd // GQA_GROUP)
            cat_store(t, POOL_WIDTH + HG_WIDTH + HEAD_DIM * hd, os_ref[:, ATT_HEADS * t + hd, c0:c0 + HEAD_DIM])
    o_all = jnp.concatenate(ol, axis=0)
    sq_hi, sq_mid, _ = _split3(o_all * o_all)
    ms = (_dot(sq_hi, ones_blk) + _dot(sq_mid, ones_blk)) * (1.0 / HG_DK)
    y_hg = o_all * lax.rsqrt(ms + EPS) * hgn_ref[...] * jnp.concatenate(gl, axis=0)
    for t in range(steps):
        cat_store(t, POOL_WIDTH, y_hg[t * gb:(t + 1) * gb])

    @pl.when(g == last)
    def _():
        for t in range(steps):
            y = _dot(cat_ref[t * nb:(t + 1) * nb, :].astype(BF16), wout_ref[...])
            xo_ref[t] = x_ref[t] + mod_ref[5] * y


def _sample_mixer(x, mod, norm, w_in, w_out, pool_bd, pool_scale, lbs, hg_norm, bms, sink_col, ones_blk,
                  state_pool, state_hgrn, cache_k, cache_v, layer, *, gb, pos0):
    steps, nb, _ = x.shape
    rows_th = steps * ATT_HEADS
    kern = functools.partial(_sample_mixer_kernel, steps=steps, nb=nb, gb=gb, pos0=pos0)
    full = lambda *shape: pl.BlockSpec(shape, lambda g: (0,) * len(shape))
    lsel = lambda *shape: pl.BlockSpec((None,) + shape, lambda g: (layer,) + (0,) * len(shape))
    grp = lambda *shape: pl.BlockSpec((None, gb) + shape, lambda g: (layer, g) + (0,) * len(shape))
    ogrp = lambda *shape: pl.BlockSpec((gb,) + shape, lambda g: (g,) + (0,) * len(shape))
    kvw = KV_HEADS * HEAD_DIM
    return pl.pallas_call(
        kern,
        grid=(nb // gb,),
        in_specs=[
            full(steps, nb, D_MODEL),
            pl.BlockSpec((None, N_MOD, nb, D_MODEL), lambda g: (layer, 0, 0, 0)),
            lsel(1, D_MODEL),
            lsel(D_MODEL, IN_WIDTH),
            lsel(D_MODEL, D_MODEL),
            lsel(POOL_WIDTH, POOL_WIDTH),
            lsel(1, POOL_WIDTH),
            lsel(1, HG_WIDTH),
            lsel(1, HG_WIDTH),
            full(rows_th, 2 * WINDOW),
            lsel(rows_th, 1),
            full(HG_WIDTH, HG_WIDTH),
            grp(POOL_BUF, POOL_WIDTH),
            grp(HG_WIDTH, HG_DK),
            grp(WINDOW, kvw),
            grp(WINDOW, kvw),
        ],
        out_specs=[
            full(steps, nb, D_MODEL),
            ogrp(POOL_BUF, POOL_WIDTH),
            ogrp(HG_WIDTH, HG_DK),
            ogrp(WINDOW, kvw),
            ogrp(WINDOW, kvw),
        ],
        out_shape=[
            jax.ShapeDtypeStruct((steps, nb, D_MODEL), F32),
            jax.ShapeDtypeStruct((nb, POOL_BUF, POOL_WIDTH), F32),
            jax.ShapeDtypeStruct((nb, HG_WIDTH, HG_DK), F32),
            jax.ShapeDtypeStruct((nb, WINDOW, kvw), F32),
            jax.ShapeDtypeStruct((nb, WINDOW, kvw), F32),
        ],
        scratch_shapes=[
            pltpu.VMEM((steps * nb, IN_WIDTH), F32),
            pltpu.VMEM((steps * nb, D_MODEL), F32),
            pltpu.VMEM((gb, rows_th, kvw), F32),
            pltpu.VMEM((gb, rows_th, kvw), F32),
            pltpu.VMEM((gb, steps * HG_HEADS, HG_WIDTH), F32),
            pltpu.VMEM((gb, steps * HG_HEADS, HG_WIDTH), F32),
            pltpu.VMEM((gb, steps * HG_HEADS, HG_DK), F32),
            pltpu.VMEM((gb, 8, HG_WIDTH), F32),
            pltpu.VMEM((gb, steps * HG_HEADS, HG_DK), F32),
            pltpu.VMEM((gb, 8, kvw), F32),
            pltpu.VMEM((gb, 8, kvw), F32),
        ],
        compiler_params=pltpu.CompilerParams(dimension_semantics=("arbitrary",), vmem_limit_bytes=VMEM_LIMIT),
        name="sample_mixer",
    )(x, mod, norm.reshape(DEPTH, 1, D_MODEL), w_in, w_out, pool_bd, pool_scale.reshape(DEPTH, 1, POOL_WIDTH),
      lbs.reshape(DEPTH, 1, HG_WIDTH), hg_norm, bms, sink_col, ones_blk, state_pool, state_hgrn, cache_k, cache_v)


FFN_TILE = 512
MIXER_TILE = 256
SAMPLE_GROUP = 8
PAST_LEN = 8192


def kernel(x_prompt, x_sample, c_prompt, c_sample, state_pool, state_hgrn, cache_k_win, cache_v_win, norm_ffn1, norm_mix, norm_ffn2, w_mod, b_mod, ffn1_w_gate, ffn1_w_up, ffn1_w_down, w_in, w_out, pool_w, pool_scale, hgrn_lower, hgrn_norm, attn_sinks, rel_bias, ffn2_w_gate, ffn2_w_up, ffn2_w_down, norm_final):
    bsz, seq, _ = x_prompt.shape
    nb, steps, _ = x_sample.shape
    kvw = KV_HEADS * HEAD_DIM

    mod = _modulation(jnp.concatenate([c_sample, c_prompt], axis=0), w_mod, b_mod)
    prompt_mod_block = nb // 8
    lbs = _lower_bounds(hgrn_lower)
    maps_p = np.stack([_bucket_map(WINDOW, 2 * WINDOW, WINDOW, 0, 2 * WINDOW),
                       _bucket_map(WINDOW, 2 * WINDOW, WINDOW, WINDOW, 2 * WINDOW)])
    bm_p = _bias_tables(jnp.asarray(maps_p), rel_bias)
    maps_s = _bucket_map(8, 2 * WINDOW, steps, 0, WINDOW + steps)[None]
    bm_s = _bias_tables(jnp.asarray(maps_s), rel_bias)[0]
    bm_s = jnp.swapaxes(bm_s, 0, 1).reshape(8 * ATT_HEADS, 2 * WINDOW)[:steps * ATT_HEADS]
    sink_col = jnp.tile(attn_sinks, (1, steps)).reshape(DEPTH, steps * ATT_HEADS, 1)
    consts = _mixer_consts(MIXER_TILE)
    hgn = jnp.tile(hgrn_norm, (1, HG_HEADS)).reshape(DEPTH, 1, HG_WIDTH)
    pool_bd = _pool_block_diag(pool_w)
    w_in_b, w_out_b = w_in.astype(BF16), w_out.astype(BF16)
    f1 = (ffn1_w_gate.astype(BF16), ffn1_w_up.astype(BF16), ffn1_w_down.astype(BF16))
    f2 = (ffn2_w_gate.astype(BF16), ffn2_w_up.astype(BF16), ffn2_w_down.astype(BF16))
    s_hgrn = state_hgrn.reshape(DEPTH, nb, HG_WIDTH, HG_DK)
    c_k = cache_k_win.reshape(DEPTH, nb, WINDOW, kvw)
    c_v = cache_v_win.reshape(DEPTH, nb, WINDOW, kvw)

    xp = x_prompt.reshape(bsz * seq, D_MODEL)
    xs = jnp.swapaxes(x_sample, 0, 1).reshape(steps * nb, D_MODEL)
    ffn_p = functools.partial(_ffn, tm=FFN_TILE, per_row=False, mod_rows=8, mod_row_block=prompt_mod_block)
    ffn_s = functools.partial(_ffn, tm=nb, per_row=True, mod_rows=nb, mod_row_block=0)
    st_p, st_s = [], []
    for l in range(DEPTH):
        xp = ffn_p(xp, mod, norm_ffn1, *f1, norm_final, l, 0)
        xs = ffn_s(xs, mod, norm_ffn1, *f1, norm_final, l, 0)
        xp, *sp = _prompt_mixer(xp.reshape(bsz, seq, D_MODEL), mod, norm_mix, w_in_b, w_out_b, pool_bd, pool_scale,
                                lbs, hgn, bm_p, attn_sinks, consts, l, tt=MIXER_TILE, mod_row_block=prompt_mod_block)
        xs, *ss = _sample_mixer(xs.reshape(steps, nb, D_MODEL), mod, norm_mix, w_in_b, w_out_b, pool_bd, pool_scale,
                                lbs, hgn, bm_s, sink_col, consts[1], state_pool, s_hgrn, c_k, c_v, l,
                                gb=SAMPLE_GROUP, pos0=PAST_LEN)
        st_p.append(sp)
        st_s.append(ss)
        last = l == DEPTH - 1
        xp = ffn_p(xp.reshape(bsz * seq, D_MODEL), mod, norm_ffn2, *f2, norm_final, l, 6, final=last)
        xs = ffn_s(xs.reshape(steps * nb, D_MODEL), mod, norm_ffn2, *f2, norm_final, l, 6, final=last)

    y_prompt = xp.reshape(bsz, seq, D_MODEL)
    y_sample = jnp.swapaxes(xs.reshape(steps, nb, D_MODEL), 0, 1)

    def stacked(states, i, shape):
        return jnp.stack([s[i] for s in states]).reshape((DEPTH,) + shape)

    outs = []
    for states, n in ((st_p, bsz), (st_s, nb)):
        outs += [stacked(states, 0, (n, POOL_BUF, POOL_WIDTH)),
                 stacked(states, 1, (n, HG_HEADS, HG_DK, HG_DK)),
                 stacked(states, 2, (n, WINDOW, KV_HEADS, HEAD_DIM)),
                 stacked(states, 3, (n, WINDOW, KV_HEADS, HEAD_DIM))]
    return (y_prompt, y_sample, *outs)
```

```python
import functools
import math

import numpy as np
import jax
import jax.numpy as jnp
from jax import lax
from jax.experimental import pallas as pl
from jax.experimental.pallas import tpu as pltpu

F32 = jnp.float32
BF16 = jnp.bfloat16

D_MODEL = 1024
DEPTH = 4
N_MOD = 9
D_FF = 2816
EPS = 1e-6
POOL_WIDTH = 256
POOL_WINDOWS = (2, 4, 8, 16)
POOL_GW = 64
POOL_BUF = 15
HG_WIDTH = 256
HG_HEADS = 4
HG_DK = 64
ATT_WIDTH = 512
ATT_HEADS = 8
KV_HEADS = 2
GQA_GROUP = 4
HEAD_DIM = 64
WINDOW = 128
REL_BUCKETS = 32
REL_MAX_DIST = 128
OFF_POOL, OFF_HQ, OFF_HF, OFF_HI, OFF_HG, OFF_AQ, OFF_AK, OFF_AV = 0, 256, 512, 768, 1024, 1280, 1792, 1920
IN_WIDTH = 2048
NEG = -1e30

VMEM_LIMIT = 56 * 1024 * 1024


def _dot(a, b):
    return jnp.dot(a, b, preferred_element_type=F32)


def _dot_nt(a, b):
    return lax.dot_general(a, b, (((1,), (1,)), ((), ())), preferred_element_type=F32)


def _dot_tn(a, b):
    return lax.dot_general(a, b, (((0,), (0,)), ((), ())), preferred_element_type=F32)


def _sigmoid(x):
    return 1.0 / (1.0 + jnp.exp(-x))


def _silu(x):
    return x * _sigmoid(x)


def _norm_mod(x, g, shift, scale):
    r = lax.rsqrt(jnp.mean(x * x, axis=-1, keepdims=True) + EPS)
    return (x * r) * g * (1.0 + scale) + shift


def _split3(x):
    hi = x.astype(BF16)
    r1 = x - hi.astype(F32)
    mid = r1.astype(BF16)
    lo = (r1 - mid.astype(F32)).astype(BF16)
    return hi, mid, lo


def _mod_kernel(c_ref, w_ref, b_ref, o_ref):
    sc = _silu(c_ref[...]).astype(BF16)
    o_ref[...] = _dot(sc, w_ref[...].astype(BF16)) + b_ref[...]


def _modulation(c_all, w_mod, b_mod):
    rows = c_all.shape[0]
    return pl.pallas_call(
        _mod_kernel,
        grid=(DEPTH, N_MOD),
        in_specs=[
            pl.BlockSpec((rows, D_MODEL), lambda l, j: (0, 0)),
            pl.BlockSpec((None, D_MODEL, D_MODEL), lambda l, j: (l, 0, j)),
            pl.BlockSpec((None, None, 1, D_MODEL), lambda l, j: (l, j, 0, 0)),
        ],
        out_specs=pl.BlockSpec((None, None, rows, D_MODEL), lambda l, j: (l, j, 0, 0)),
        out_shape=jax.ShapeDtypeStruct((DEPTH, N_MOD, rows, D_MODEL), F32),
        compiler_params=pltpu.CompilerParams(dimension_semantics=("arbitrary", "arbitrary"),
                                             vmem_limit_bytes=VMEM_LIMIT),
        name="adaln_mod",
    )(c_all, w_mod, b_mod.reshape(DEPTH, N_MOD, 1, D_MODEL))


def _mod_rows(mod_ref, j, row, per_row):
    if per_row:
        return mod_ref[j]
    return mod_ref[j, pl.ds(row, 1), :]


def _ffn_kernel(x_ref, mod_ref, n_ref, wg_ref, wu_ref, wd_ref, nf_ref, o_ref, *, j0, per_row, tiles_per_seq,
                final):
    row = pl.program_id(0) // tiles_per_seq
    shift = _mod_rows(mod_ref, j0, row, per_row)
    scale = _mod_rows(mod_ref, j0 + 1, row, per_row)
    gate = _mod_rows(mod_ref, j0 + 2, row, per_row)
    x = x_ref[...]
    h = _norm_mod(x, n_ref[...], shift, scale).astype(BF16)
    g = _dot(h, wg_ref[...])
    u = _dot(h, wu_ref[...])
    a = (_silu(g) * u).astype(BF16)
    y = x + (0.5 * gate) * _dot(a, wd_ref[...])
    if final:
        y = (y * lax.rsqrt(jnp.mean(y * y, axis=-1, keepdims=True) + EPS)) * nf_ref[...]
    o_ref[...] = y


def _ffn(x, mod, norm, wg, wu, wd, norm_final, layer, j0, *, tm, per_row, mod_rows, mod_row_block, final=False):
    rows = x.shape[0]
    tiles_per_seq = 1 if per_row else (rows // mod_rows) // tm
    kern = functools.partial(_ffn_kernel, j0=j0, per_row=per_row, tiles_per_seq=tiles_per_seq, final=final)
    return pl.pallas_call(
        kern,
        grid=(rows // tm,),
        in_specs=[
            pl.BlockSpec((tm, D_MODEL), lambda i: (i, 0)),
            pl.BlockSpec((None, N_MOD, mod_rows, D_MODEL), lambda i: (layer, 0, mod_row_block, 0)),
            pl.BlockSpec((None, 1, D_MODEL), lambda i: (layer, 0, 0)),
            pl.BlockSpec((None, D_MODEL, D_FF), lambda i: (layer, 0, 0), pipeline_mode=pl.Buffered(1)),
            pl.BlockSpec((None, D_MODEL, D_FF), lambda i: (layer, 0, 0), pipeline_mode=pl.Buffered(1)),
            pl.BlockSpec((None, D_FF, D_MODEL), lambda i: (layer, 0, 0), pipeline_mode=pl.Buffered(1)),
            pl.BlockSpec((1, D_MODEL), lambda i: (0, 0)),
        ],
        out_specs=pl.BlockSpec((tm, D_MODEL), lambda i: (i, 0)),
        out_shape=jax.ShapeDtypeStruct((rows, D_MODEL), F32),
        compiler_params=pltpu.CompilerParams(dimension_semantics=("arbitrary",), vmem_limit_bytes=VMEM_LIMIT),
        name="swiglu_half_step",
    )(x, mod, norm.reshape(DEPTH, 1, D_MODEL), wg, wu, wd, norm_final.reshape(1, D_MODEL))


def _lower_bound_kernel(x_ref, o_ref):
    rows = [x_ref[l:l + 1, :] for l in range(DEPTH)]
    m = functools.reduce(jnp.maximum, rows)
    e = [jnp.exp(r - m) for r in rows]
    s = functools.reduce(lambda a, b: a + b, e)
    sm = [ei / s for ei in e]
    acc = sm[0]
    for l in range(DEPTH):
        if l > 0:
            acc = acc + sm[l]
        o_ref[l:l + 1, :] = acc - sm[0]


def _lower_bounds(hgrn_lower):
    return pl.pallas_call(
        _lower_bound_kernel,
        out_shape=jax.ShapeDtypeStruct((DEPTH, HG_WIDTH), F32),
        name="hgrn_lower_bounds",
    )(hgrn_lower)


def _bucket_map(rows, cols, valid_rows, col_lo, col_hi):
    r = np.arange(rows)[:, None]
    c = np.arange(cols)[None, :]
    rel = c - WINDOW - r
    n = np.maximum(-rel, 0)
    exact = REL_BUCKETS // 2
    nf = np.maximum(n, 1).astype(np.float32)
    large = exact + (np.log(nf / np.float32(exact)) / np.float32(math.log(REL_MAX_DIST / exact))
                     * np.float32(REL_BUCKETS - exact)).astype(np.int32)
    large = np.minimum(large, REL_BUCKETS - 1)
    bucket = np.where(n < exact, n, large)
    valid = (rel <= 0) & (rel > -WINDOW) & (r < valid_rows) & (c >= col_lo) & (c < col_hi)
    return np.where(valid, bucket, -1).astype(np.int32)


def _bias_table_kernel(bkt_ref, rb_ref, o_ref):
    h = pl.program_id(1)
    bkt = bkt_ref[...]
    acc = jnp.full(bkt.shape, NEG, F32)
    for b in range(REL_BUCKETS):
        acc = jnp.where(bkt == b, rb_ref[b, h], acc)
    o_ref[...] = acc


def _bias_tables(bucket_maps, rel_bias):
    m, r, c = bucket_maps.shape
    return pl.pallas_call(
        _bias_table_kernel,
        grid=(m, ATT_HEADS),
        in_specs=[pl.BlockSpec((None, r, c), lambda i, h: (i, 0, 0)),
                  pl.BlockSpec(memory_space=pltpu.SMEM)],
        out_specs=pl.BlockSpec((None, None, r, c), lambda i, h: (i, h, 0, 0)),
        out_shape=jax.ShapeDtypeStruct((m, ATT_HEADS, r, c), F32),
        compiler_params=pltpu.CompilerParams(dimension_semantics=("arbitrary", "arbitrary")),
        name="rel_bias_tables",
    )(bucket_maps, rel_bias)


def _block_ones(n, blk):
    i = np.arange(n)
    return (i[:, None] // blk == i[None, :] // blk)


def _mixer_consts(tt):
    tri = jnp.asarray(np.tril(np.ones((tt, tt), np.float32)), BF16)
    blk = _block_ones(HG_WIDTH, HG_DK)
    return tri, jnp.asarray(blk, BF16), jnp.asarray(blk, F32)


def _pool_block_diag(pool_w):
    out = jnp.zeros((DEPTH, POOL_WIDTH, POOL_WIDTH), BF16)
    for g in range(len(POOL_WINDOWS)):
        sl = slice(g * POOL_GW, (g + 1) * POOL_GW)
        out = out.at[:, sl, sl].set(pool_w[:, g].astype(BF16))
    return out


def _head_masks():
    lane = lax.broadcasted_iota(jnp.int32, (1, HG_WIDTH), 1)
    return [jnp.where((lane >= HG_DK * h) & (lane < HG_DK * (h + 1)), 1.0, 0.0).astype(F32) for h in range(HG_HEADS)]


def _prompt_mixer_kernel(x_ref, xn_ref, mod_ref, n_ref, win_ref, wout_ref, pbd_ref, pscale_ref, lb_ref, hgn_ref,
                         bm_ref, sink_ref, tri_ref, ones_ref, bdm_ref,
                         xo_ref, pool_o, s_o, k_o, v_o,
                         za_ref, zb_ref, ubuf, kbuf, vbuf, st_ref, cat_ref, *, layer, tt, nt, ntiles):
    b = pl.program_id(0)
    p = pl.program_id(1)
    last_pair = pl.num_programs(1) - 1
    nxt = jnp.minimum(b * nt + 2 * p + 2, ntiles - 1)
    bn = nxt // nt
    n_chunks = 8
    cw = IN_WIDTH // n_chunks

    def projector(x_rows, bsel, dst_ref):
        h = _norm_mod(x_rows, n_ref[...], mod_ref[3, pl.ds(bsel, 1), :], mod_ref[4, pl.ds(bsel, 1), :]).astype(BF16)
        state = {"k": 0}

        def step():
            k = state["k"]
            if k < n_chunks:
                dst_ref[:, k * cw:(k + 1) * cw] = _dot(h, win_ref[:, k * cw:(k + 1) * cw])
                state["k"] = k + 1

        return step

    @pl.when((b == 0) & (p == 0))
    def _():
        first_proj = projector(x_ref[0:tt, :], b, za_ref)
        for _ in range(n_chunks):
            first_proj()

    @pl.when(p == 0)
    def _():
        ubuf[0:16, :] = jnp.zeros((16, POOL_WIDTH), F32)
        kbuf[:, 0:WINDOW, :] = jnp.zeros((KV_HEADS, WINDOW, HEAD_DIM), BF16)
        vbuf[:, 0:WINDOW, :] = jnp.zeros((KV_HEADS, WINDOW, HEAD_DIM), BF16)
        st_ref[...] = jnp.zeros((HG_WIDTH, HG_WIDTH), F32)

    gate = mod_ref[5, pl.ds(b, 1), :]

    def mix(z_ref, x, row0, t, side):
        lane = lax.broadcasted_iota(jnp.int32, (1, POOL_WIDTH), 1)

        u = z_ref[:, OFF_POOL:OFF_POOL + POOL_WIDTH]
        ubuf[16:16 + tt, :] = u
        e = ubuf[...]
        s2 = e + pltpu.roll(e, 1, 0)
        s4 = s2 + pltpu.roll(s2, 2, 0)
        s8 = s4 + pltpu.roll(s4, 4, 0)
        s16 = s8 + pltpu.roll(s8, 8, 0)
        sel = jnp.where(lane < 64, s2, jnp.where(lane < 128, s4, jnp.where(lane < 192, s8, s16)))[16:]
        wl = jnp.where(lane < 64, 2.0, jnp.where(lane < 128, 4.0, jnp.where(lane < 192, 8.0, 16.0))).astype(F32)
        pos1 = (t * tt + 1 + lax.broadcasted_iota(jnp.int32, (tt, 1), 0)).astype(F32)
        dpool = sel / jnp.minimum(pos1, wl) - u
        ypool = _dot(dpool.astype(BF16), pbd_ref[...]) * pscale_ref[...]
        cat_ref[:, 0:POOL_WIDTH] = ypool.astype(BF16)
        ubuf[0:16, :] = ubuf[tt:tt + 16, :]
        side()


        for g in range(KV_HEADS):
            kbuf[g, WINDOW:WINDOW + tt, :] = z_ref[:, OFF_AK + HEAD_DIM * g:OFF_AK + HEAD_DIM * (g + 1)].astype(BF16)
            vbuf[g, WINDOW:WINDOW + tt, :] = z_ref[:, OFF_AV + HEAD_DIM * g:OFF_AV + HEAD_DIM * (g + 1)].astype(BF16)
        first = jnp.where(t == 0, 1, 0)
        bands = [(j, g) for j in range(tt // WINDOW) for g in range(KV_HEADS)]
        logits = []
        for j, g in bands:
            r0 = j * WINDOW
            c0 = OFF_AQ + HEAD_DIM * g * GQA_GROUP
            qg = jnp.concatenate([z_ref[r0:r0 + WINDOW, c0 + HEAD_DIM * i:c0 + HEAD_DIM * (i + 1)]
                                  for i in range(GQA_GROUP)], axis=0)
            qg = (qg * (HEAD_DIM ** -0.5)).astype(BF16)
            logits.append(_dot_nt(qg, kbuf[g, r0:r0 + 2 * WINDOW, :]))
        side()

        lb = lb_ref[...]
        q = _silu(z_ref[:, OFF_HQ:OFF_HQ + HG_WIDTH])
        f = lb + (1.0 - lb) * _sigmoid(z_ref[:, OFF_HF:OFF_HF + HG_WIDTH])
        k = 1.0 - f
        v = z_ref[:, OFF_HI:OFF_HI + HG_WIDTH]
        vb = v.astype(BF16)
        tri = tri_ref[...]
        hi, mid, lo = _split3(jnp.log(f))
        bcum = _dot(tri, hi) + _dot(tri, mid) + _dot(tri, lo)
        side()
        ones_blk = ones_ref[...]
        hm = _head_masks()

        n16 = tt // 16
        half_shape = (n16, 2, 8, HG_WIDTH)
        q4 = q.reshape(half_shape)
        b4 = bcum.reshape(half_shape)
        c4 = (bcum - jnp.log(k)).reshape(half_shape)
        v4 = v.reshape(half_shape)
        rowi = lax.broadcasted_iota(jnp.int32, (1, 8, 1), 1)
        o_up = jnp.zeros((n16, 8, HG_WIDTH), F32)
        o_dn = jnp.zeros((n16, 8, HG_WIDTH), F32)
        nh = n16 * 8
        for s in range(8):
            cs = c4[:, 0, s:s + 1, :]
            vs = v4[:, 0, s:s + 1, :]
            x_up = jnp.where(rowi >= s, q4[:, 0] * jnp.exp(b4[:, 0] - cs), 0.0)
            x_dn = q4[:, 1] * jnp.exp(b4[:, 1] - cs)
            xs = jnp.concatenate([x_up.reshape(nh, HG_WIDTH), x_dn.reshape(nh, HG_WIDTH)], axis=0)
            r = _dot(xs.astype(BF16), ones_blk)
            o_up = o_up + r[0:nh].reshape(n16, 8, HG_WIDTH) * vs
            o_dn = o_dn + r[nh:2 * nh].reshape(n16, 8, HG_WIDTH) * vs
            if s % 4 == 3:
                side()
        for s in range(8):
            cs = c4[:, 1, s:s + 1, :]
            x_dn = jnp.where(rowi >= s, q4[:, 1] * jnp.exp(b4[:, 1] - cs), 0.0)
            r = _dot(x_dn.reshape(nh, HG_WIDTH).astype(BF16), ones_blk)
            o_dn = o_dn + r.reshape(n16, 8, HG_WIDTH) * v4[:, 1, s:s + 1, :]
            if s % 4 == 3:
                side()
        o_tot = jnp.concatenate([o_up[:, None], o_dn[:, None]], axis=1).reshape(tt, HG_WIDTH)

        blocks = []
        blk = 32
        while blk <= tt:
            blocks += [(i * blk, i * blk + blk // 2, (i + 1) * blk) for i in range(tt // blk)]
            blk *= 2
        scores = []
        for lo_s, mid_s, hi_s in blocks:
            ref = bcum[mid_s - 1:mid_s, :]
            qt = q[mid_s:hi_s] * jnp.exp(bcum[mid_s:hi_s] - ref)
            kt = (k[lo_s:mid_s] * jnp.exp(ref - bcum[lo_s:mid_s])).astype(BF16)
            qs = jnp.concatenate([qt * hm[hh] for hh in range(HG_HEADS)], axis=0).astype(BF16)
            scores.append(_dot_nt(qs, kt).astype(BF16))

        st = st_ref[...]
        qin = (q * jnp.exp(bcum)).astype(BF16)
        o_tot = o_tot + _dot_nt(qin, st.astype(BF16))
        blast = bcum[tt - 1:tt, :]
        kout = (k * jnp.exp(blast - bcum)).astype(BF16)
        st_ref[...] = st * jnp.exp(blast) + _dot_tn(vb, kout) * bdm_ref[...]
        side()

        probs, dens = [], []
        for (j, g), lg_all in zip(bands, logits):
            h0 = g * GQA_GROUP
            bm = bm_ref[first, h0:h0 + GQA_GROUP] if j == 0 else bm_ref[0, h0:h0 + GQA_GROUP]
            ps = []
            for i in range(GQA_GROUP):
                lg = lg_all[i * WINDOW:(i + 1) * WINDOW] + bm[i]
                sink = sink_ref[layer, h0 + i]
                m = jnp.maximum(jnp.max(lg, axis=-1, keepdims=True), sink)
                p_ = jnp.exp(lg - m)
                dens.append(jnp.sum(p_, axis=-1, keepdims=True) + jnp.exp(sink - m))
                ps.append(p_.astype(BF16))
            probs.append(jnp.concatenate(ps, axis=0))
        for n, (j, g) in enumerate(bands):
            r0 = j * WINDOW
            og = _dot(probs[n], vbuf[g, r0:r0 + 2 * WINDOW, :])
            for i in range(GQA_GROUP):
                cc = POOL_WIDTH + HG_WIDTH + HEAD_DIM * (g * GQA_GROUP + i)
                oh = og[i * WINDOW:(i + 1) * WINDOW] / dens[n * GQA_GROUP + i]
                cat_ref[r0:r0 + WINDOW, cc:cc + HEAD_DIM] = oh.astype(BF16)
        for g in range(KV_HEADS):
            kbuf[g, 0:WINDOW, :] = kbuf[g, tt:tt + WINDOW, :]
            vbuf[g, 0:WINDOW, :] = vbuf[g, tt:tt + WINDOW, :]
        side()

        adds = {}
        for (lo_s, mid_s, hi_s), a in zip(blocks, scores):
            half = mid_s - lo_s
            ov = _dot(a, vb[lo_s:mid_s])
            oi = ov[0:half] * hm[0]
            for hh in range(1, HG_HEADS):
                oi = oi + ov[hh * half:(hh + 1) * half] * hm[hh]
            adds.setdefault(2 * half, []).extend([jnp.zeros((half, HG_WIDTH), F32), oi])
        for parts in adds.values():
            o_tot = o_tot + jnp.concatenate(parts, axis=0)

        sq_hi, sq_mid, _ = _split3(o_tot * o_tot)
        ms = (_dot(sq_hi, ones_blk) + _dot(sq_mid, ones_blk)) * (1.0 / HG_DK)
        y_hg = o_tot * lax.rsqrt(ms + EPS) * hgn_ref[...] * _silu(z_ref[:, OFF_HG:OFF_HG + HG_WIDTH])
        cat_ref[:, POOL_WIDTH:POOL_WIDTH + HG_WIDTH] = y_hg.astype(BF16)

        xo_ref[row0:row0 + tt, :] = x + gate * _dot(cat_ref[...], wout_ref[...])

        for _ in range(n_chunks):
            side()

    mix(za_ref, x_ref[0:tt, :], 0, 2 * p, projector(x_ref[tt:2 * tt, :], b, zb_ref))
    mix(zb_ref, x_ref[tt:2 * tt, :], tt, 2 * p + 1, projector(xn_ref[...], bn, za_ref))

    @pl.when(p == last_pair)
    def _():
        pool_o[...] = ubuf[pl.ds(1, POOL_BUF), :]
        s_kv = st_ref[...].T
        for hh in range(HG_HEADS):
            s_o[hh] = s_kv[HG_DK * hh:HG_DK * (hh + 1), HG_DK * hh:HG_DK * (hh + 1)]
        k_o[...] = zb_ref[tt - WINDOW:tt, OFF_AK:OFF_AV]
        v_o[...] = zb_ref[tt - WINDOW:tt, OFF_AV:IN_WIDTH]


def _prompt_mixer(x, mod, norm, w_in, w_out, pool_bd, pool_scale, lbs, hg_norm, bm, sinks, consts, layer, *,
                  tt, mod_row_block):
    bsz, seq, _ = x.shape
    tri, ones_blk, bdm = consts
    nt = seq // tt
    ntiles = bsz * nt
    kern = functools.partial(_prompt_mixer_kernel, layer=layer, tt=tt, nt=nt, ntiles=ntiles)
    full = lambda *shape: pl.BlockSpec(shape, lambda b, p: (0,) * len(shape))
    lsel = lambda *shape: pl.BlockSpec((None,) + shape, lambda b, p: (layer,) + (0,) * len(shape))
    mix_pair = lambda b, p: (b, p, 0)
    mix_seq = lambda *z: (lambda b, p: (b,) + z)

    def next_tile(b, p):
        nxt = jnp.minimum(b * nt + 2 * p + 2, ntiles - 1)
        return (nxt // nt, nxt % nt, 0)

    return pl.pallas_call(
        kern,
        grid=(bsz, nt // 2),
        in_specs=[
            pl.BlockSpec((None, 2 * tt, D_MODEL), mix_pair),
            pl.BlockSpec((None, tt, D_MODEL), next_tile),
            pl.BlockSpec((None, N_MOD, 8, D_MODEL), lambda b, p: (layer, 0, mod_row_block, 0)),
            lsel(1, D_MODEL),
            lsel(D_MODEL, IN_WIDTH),
            lsel(D_MODEL, D_MODEL),
            lsel(POOL_WIDTH, POOL_WIDTH),
            lsel(1, POOL_WIDTH),
            lsel(1, HG_WIDTH),
            lsel(1, HG_WIDTH),
            full(2, ATT_HEADS, WINDOW, 2 * WINDOW),
            pl.BlockSpec(memory_space=pltpu.SMEM),
            full(tt, tt),
            full(HG_WIDTH, HG_WIDTH),
            full(HG_WIDTH, HG_WIDTH),
        ],
        out_specs=[
            pl.BlockSpec((None, 2 * tt, D_MODEL), mix_pair),
            pl.BlockSpec((None, POOL_BUF, POOL_WIDTH), mix_seq(0, 0)),
            pl.BlockSpec((None, HG_HEADS, HG_DK, HG_DK), mix_seq(0, 0, 0)),
            pl.BlockSpec((None, WINDOW, KV_HEADS * HEAD_DIM), mix_seq(0, 0)),
            pl.BlockSpec((None, WINDOW, KV_HEADS * HEAD_DIM), mix_seq(0, 0)),
        ],
        out_shape=[
            jax.ShapeDtypeStruct((bsz, seq, D_MODEL), F32),
            jax.ShapeDtypeStruct((bsz, POOL_BUF, POOL_WIDTH), F32),
            jax.ShapeDtypeStruct((bsz, HG_HEADS, HG_DK, HG_DK), F32),
            jax.ShapeDtypeStruct((bsz, WINDOW, KV_HEADS * HEAD_DIM), F32),
            jax.ShapeDtypeStruct((bsz, WINDOW, KV_HEADS * HEAD_DIM), F32),
        ],
        scratch_shapes=[
            pltpu.VMEM((tt, IN_WIDTH), F32),
            pltpu.VMEM((tt, IN_WIDTH), F32),
            pltpu.VMEM((16 + tt, POOL_WIDTH), F32),
            pltpu.VMEM((KV_HEADS, WINDOW + tt, HEAD_DIM), BF16),
            pltpu.VMEM((KV_HEADS, WINDOW + tt, HEAD_DIM), BF16),
            pltpu.VMEM((HG_WIDTH, HG_WIDTH), F32),
            pltpu.VMEM((tt, D_MODEL), BF16),
        ],
        compiler_params=pltpu.CompilerParams(dimension_semantics=("arbitrary", "arbitrary"),
                                             vmem_limit_bytes=VMEM_LIMIT),
        name="prompt_mixer",
    )(x, x, mod, norm.reshape(DEPTH, 1, D_MODEL), w_in, w_out, pool_bd, pool_scale.reshape(DEPTH, 1, POOL_WIDTH),
      lbs.reshape(DEPTH, 1, HG_WIDTH), hg_norm, bm, sinks, tri, ones_blk, bdm)


def _sample_mixer_kernel(x_ref, mod_ref, n_ref, win_ref, wout_ref, pbd_ref, pscale_ref, lb_ref, hgn_ref,
                         bms_ref, sinkc_ref, ones_ref, sp_ref, s0_ref, kt_ref, vt_ref,
                         xo_ref, pool_o, s_o, kt_o, vt_o,
                         z_ref, cat_ref, qs_ref, os_ref, kn_ref, vn_ref, qint_ref, koutt_ref, vtt_ref, gt_ref,
                         oin_ref, oacc_ref, *, steps, nb, gb, rb, pos0):
    g = pl.program_id(0)
    last = pl.num_programs(0) - 1
    base = pl.multiple_of(g * gb, gb)
    kvw = KV_HEADS * HEAD_DIM

    def rows(t):
        return slice(t * nb, (t + 1) * nb)

    @pl.when(g == 0)
    def _():
        for t in range(steps):
            h = _norm_mod(x_ref[t], n_ref[...], mod_ref[0], mod_ref[1]).astype(BF16)
            z_ref[rows(t), :] = _dot(h, win_ref[...])
        lane = lax.broadcasted_iota(jnp.int32, (1, POOL_WIDTH), 1)

        ext = [sp_ref[r] for r in range(POOL_BUF)] + [z_ref[rows(t), OFF_POOL:OFF_POOL + POOL_WIDTH] for t in range(steps)]
        for r in range(POOL_BUF):
            pool_o[r] = ext[r + steps]
        sums = {1: ext}
        w = 1
        while w < 16:
            prev = sums[w]
            sums[2 * w] = [None if (r < 2 * w - 1) else prev[r] + prev[r - w] for r in range(len(ext))]
            w *= 2
        dl = []
        for t in range(steps):
            r = POOL_BUF + t
            sel = jnp.where(lane < 64, sums[2][r], jnp.where(lane < 128, sums[4][r], jnp.where(lane < 192, sums[8][r], sums[16][r])))
            cnt = [float(min(pos0 + t + 1, wd)) for wd in POOL_WINDOWS]
            cl = jnp.where(lane < 64, cnt[0], jnp.where(lane < 128, cnt[1], jnp.where(lane < 192, cnt[2], cnt[3]))).astype(F32)
            dl.append(sel / cl - ext[r])
        ypool = _dot(jnp.concatenate(dl, axis=0).astype(BF16), pbd_ref[...]) * pscale_ref[...]
        cat_ref[:, 0:POOL_WIDTH] = ypool

        lb = lb_ref[...]
        qv, kv_, vv, bv = [], [], [], []
        bacc = None
        for t in range(steps):
            qv.append(_silu(z_ref[rows(t), OFF_HQ:OFF_HQ + HG_WIDTH]))
            f = lb + (1.0 - lb) * _sigmoid(z_ref[rows(t), OFF_HF:OFF_HF + HG_WIDTH])
            kv_.append(1.0 - f)
            vv.append(z_ref[rows(t), OFF_HI:OFF_HI + HG_WIDTH])
            bacc = jnp.log(f) if bacc is None else bacc + jnp.log(f)
            bv.append(bacc)
        pairs = [(t, s) for t in range(steps) for s in range(t + 1)]
        xs = jnp.concatenate([qv[t] * kv_[s] * jnp.exp(bv[t] - bv[s]) for t, s in pairs], axis=0).astype(BF16)
        rr = _dot(xs, ones_ref[...])
        o_intra = [None] * steps
        for i, (t, s) in enumerate(pairs):
            term = rr[i * nb:(i + 1) * nb] * vv[s]
            o_intra[t] = term if o_intra[t] is None else o_intra[t] + term
        blast = bv[steps - 1]
        gt_ref[...] = jnp.exp(blast).T
        for t in range(steps):
            oin_ref[t] = o_intra[t]
            qint_ref[t] = (qv[t] * jnp.exp(bv[t])).T
            koutt_ref[t] = (kv_[t] * jnp.exp(blast - bv[t])).T
            vtt_ref[t] = vv[t].T
        oacc_ref[...] = jnp.zeros(oacc_ref.shape, F32)

        zeros64 = jnp.zeros((nb, HEAD_DIM), F32)
        kn_ref[...] = jnp.zeros(kn_ref.shape, F32)
        vn_ref[...] = jnp.zeros(vn_ref.shape, F32)
        for t in range(steps):
            kn_ref[:, t, :] = z_ref[rows(t), OFF_AK:OFF_AK + kvw]
            vn_ref[:, t, :] = z_ref[rows(t), OFF_AV:OFF_AV + kvw]
            for hd in range(ATT_HEADS):
                c0 = OFF_AQ + HEAD_DIM * hd
                qsl = z_ref[rows(t), c0:c0 + HEAD_DIM] * (HEAD_DIM ** -0.5)
                parts = [qsl, zeros64] if hd < GQA_GROUP else [zeros64, qsl]
                qs_ref[:, ATT_HEADS * t + hd, :] = jnp.concatenate(parts, axis=1)

    r0 = pl.multiple_of(g * rb, rb)
    hrow = pl.multiple_of((g * rb // HG_DK) * HG_DK, HG_DK)
    vts = [vtt_ref[t, pl.ds(hrow, HG_DK), :] for t in range(steps)]
    accs = [oacc_ref[t, pl.ds(hrow, HG_DK), :] for t in range(steps)]
    for i in range(rb):
        s0 = s0_ref[i]
        snew = gt_ref[pl.ds(r0 + i, 1), :] * s0
        for t in range(steps):
            accs[t] = accs[t] + qint_ref[t, pl.ds(r0 + i, 1), :] * s0
            snew = snew + koutt_ref[t, pl.ds(r0 + i, 1), :] * vts[t]
        s_o[i] = snew
    for t in range(steps):
        oacc_ref[t, pl.ds(hrow, HG_DK), :] = accs[t]

    bm_old = bms_ref[:, 0:WINDOW]
    bm_new = bms_ref[:, WINDOW:WINDOW + 8]
    sinkc = sinkc_ref[...]
    seqs = range(gb)
    qn = [qs_ref[base + n].astype(BF16) for n in seqs]
    l_old = [_dot(qn[n], kt_ref[n].astype(BF16)) for n in seqs]
    l_new = [_dot_nt(qn[n], kn_ref[base + n].astype(BF16)) for n in seqs]
    pad = jnp.zeros((WINDOW - gb * 8, kvw), F32)
    knt = jnp.concatenate([kn_ref[pl.ds(base, gb)].reshape(gb * 8, kvw), pad], axis=0).T
    vnt = jnp.concatenate([vn_ref[pl.ds(base, gb)].reshape(gb * 8, kvw), pad], axis=0).T
    for n in seqs:
        kt_o[n] = jnp.concatenate([kt_ref[n][:, steps:], knt[:, 8 * n:8 * n + steps]], axis=1)
        vt_o[n] = jnp.concatenate([vt_ref[n][:, steps:], vnt[:, 8 * n:8 * n + steps]], axis=1)
    p_old, p_new, den = [], [], []
    for n in seqs:
        lo_, ln_ = l_old[n] + bm_old, l_new[n] + bm_new
        m = jnp.maximum(jnp.maximum(jnp.max(lo_, axis=-1, keepdims=True), jnp.max(ln_, axis=-1, keepdims=True)), sinkc)
        po, pn = jnp.exp(lo_ - m), jnp.exp(ln_ - m)
        den.append(jnp.sum(po, axis=-1, keepdims=True) + jnp.sum(pn, axis=-1, keepdims=True) + jnp.exp(sinkc - m))
        p_old.append(po.astype(BF16))
        p_new.append(pn.astype(BF16))
    for n in seqs:
        o = _dot_nt(p_old[n], vt_ref[n].astype(BF16)) + _dot(p_new[n], vn_ref[base + n].astype(BF16))
        os_ref[n] = o / den[n]
    for t in range(steps):
        for hd in range(ATT_HEADS):
            c0 = HEAD_DIM * (hd // GQA_GROUP)
            cc = POOL_WIDTH + HG_WIDTH + HEAD_DIM * hd
            cat_ref[pl.ds(t * nb + base, gb), cc:cc + HEAD_DIM] = os_ref[:, ATT_HEADS * t + hd, c0:c0 + HEAD_DIM]

    @pl.when(g == last)
    def _():
        ones_blk = ones_ref[...]
        for t in range(steps):
            o = oin_ref[t] + oacc_ref[t].T
            sq_hi, sq_mid, _ = _split3(o * o)
            ms = (_dot(sq_hi, ones_blk) + _dot(sq_mid, ones_blk)) * (1.0 / HG_DK)
            gate_t = _silu(z_ref[rows(t), OFF_HG:OFF_HG + HG_WIDTH])
            cat_ref[rows(t), POOL_WIDTH:POOL_WIDTH + HG_WIDTH] = o * lax.rsqrt(ms + EPS) * hgn_ref[...] * gate_t
        for t in range(steps):
            y = _dot(cat_ref[rows(t), :].astype(BF16), wout_ref[...])
            xo_ref[t] = x_ref[t] + mod_ref[2] * y


def _sample_mixer(x, mod, norm, w_in, w_out, pool_bd, pool_scale, lbs, hg_norm, bms, sink_col, ones_blk,
                  pool_hist, hgrn_state, cache_kt, cache_vt, layer, *, gb, pos0):
    steps, nb, _ = x.shape
    rows_th = steps * ATT_HEADS
    ngrp = nb // gb
    rb = HG_WIDTH // ngrp
    kvw = KV_HEADS * HEAD_DIM
    kern = functools.partial(_sample_mixer_kernel, steps=steps, nb=nb, gb=gb, rb=rb, pos0=pos0)
    full = lambda *shape: pl.BlockSpec(shape, lambda g: (0,) * len(shape))
    lsel = lambda *shape: pl.BlockSpec((None,) + shape, lambda g: (layer,) + (0,) * len(shape))
    grp = lambda *shape: pl.BlockSpec((None,) + shape, lambda g: (layer, g) + (0,) * (len(shape) - 1))
    ogrp = lambda *shape: pl.BlockSpec(shape, lambda g: (g,) + (0,) * (len(shape) - 1))
    return pl.pallas_call(
        kern,
        grid=(ngrp,),
        in_specs=[
            full(steps, nb, D_MODEL),
            pl.BlockSpec((None, 3, nb, D_MODEL), lambda g: (layer, 1, 0, 0)),
            lsel(1, D_MODEL),
            lsel(D_MODEL, IN_WIDTH),
            lsel(D_MODEL, D_MODEL),
            lsel(POOL_WIDTH, POOL_WIDTH),
            lsel(1, POOL_WIDTH),
            lsel(1, HG_WIDTH),
            lsel(1, HG_WIDTH),
            full(rows_th, 2 * WINDOW),
            lsel(rows_th, 1),
            full(HG_WIDTH, HG_WIDTH),
            lsel(POOL_BUF, nb, POOL_WIDTH),
            grp(rb, HG_DK, nb),
            grp(gb, kvw, WINDOW),
            grp(gb, kvw, WINDOW),
        ],
        out_specs=[
            full(steps, nb, D_MODEL),
            full(POOL_BUF, nb, POOL_WIDTH),
            ogrp(rb, HG_DK, nb),
            ogrp(gb, kvw, WINDOW),
            ogrp(gb, kvw, WINDOW),
        ],
        out_shape=[
            jax.ShapeDtypeStruct((steps, nb, D_MODEL), F32),
            jax.ShapeDtypeStruct((POOL_BUF, nb, POOL_WIDTH), F32),
            jax.ShapeDtypeStruct((HG_WIDTH, HG_DK, nb), F32),
            jax.ShapeDtypeStruct((nb, kvw, WINDOW), F32),
            jax.ShapeDtypeStruct((nb, kvw, WINDOW), F32),
        ],
        scratch_shapes=[
            pltpu.VMEM((steps * nb, IN_WIDTH), F32),
            pltpu.VMEM((steps * nb, D_MODEL), F32),
            pltpu.VMEM((nb, rows_th, kvw), F32),
            pltpu.VMEM((gb, rows_th, kvw), F32),
            pltpu.VMEM((nb, 8, kvw), F32),
            pltpu.VMEM((nb, 8, kvw), F32),
            pltpu.VMEM((steps, HG_WIDTH, nb), F32),
            pltpu.VMEM((steps, HG_WIDTH, nb), F32),
            pltpu.VMEM((steps, HG_WIDTH, nb), F32),
            pltpu.VMEM((HG_WIDTH, nb), F32),
            pltpu.VMEM((steps, nb, HG_WIDTH), F32),
            pltpu.VMEM((steps, HG_WIDTH, nb), F32),
        ],
        compiler_params=pltpu.CompilerParams(dimension_semantics=("arbitrary",), vmem_limit_bytes=VMEM_LIMIT),
        name="sample_mixer",
    )(x, mod, norm.reshape(DEPTH, 1, D_MODEL), w_in, w_out, pool_bd, pool_scale.reshape(DEPTH, 1, POOL_WIDTH),
      lbs.reshape(DEPTH, 1, HG_WIDTH), hg_norm, bms, sink_col, ones_blk, pool_hist, hgrn_state, cache_kt, cache_vt)


FFN_TILE = 512
MIXER_TILE = 256
SAMPLE_GROUP = 8
PAST_LEN = 8192


def kernel(x_prompt, x_sample, c_prompt, c_sample, state_pool, state_hgrn, cache_k_win, cache_v_win, norm_ffn1, norm_mix, norm_ffn2, w_mod, b_mod, ffn1_w_gate, ffn1_w_up, ffn1_w_down, w_in, w_out, pool_w, pool_scale, hgrn_lower, hgrn_norm, attn_sinks, rel_bias, ffn2_w_gate, ffn2_w_up, ffn2_w_down, norm_final):
    bsz, seq, _ = x_prompt.shape
    nb, steps, _ = x_sample.shape
    kvw = KV_HEADS * HEAD_DIM

    mod = _modulation(jnp.concatenate([c_sample, c_prompt], axis=0), w_mod, b_mod)
    prompt_mod_block = nb // 8
    lbs = _lower_bounds(hgrn_lower)
    maps_p = np.stack([_bucket_map(WINDOW, 2 * WINDOW, WINDOW, 0, 2 * WINDOW),
                       _bucket_map(WINDOW, 2 * WINDOW, WINDOW, WINDOW, 2 * WINDOW)])
    bm_p = _bias_tables(jnp.asarray(maps_p), rel_bias)
    maps_s = _bucket_map(8, 2 * WINDOW, steps, 0, WINDOW + steps)[None]
    bm_s = _bias_tables(jnp.asarray(maps_s), rel_bias)[0]
    bm_s = jnp.swapaxes(bm_s, 0, 1).reshape(8 * ATT_HEADS, 2 * WINDOW)[:steps * ATT_HEADS]
    sink_col = jnp.tile(attn_sinks, (1, steps)).reshape(DEPTH, steps * ATT_HEADS, 1)
    consts = _mixer_consts(MIXER_TILE)
    hgn = jnp.tile(hgrn_norm, (1, HG_HEADS)).reshape(DEPTH, 1, HG_WIDTH)
    pool_bd = _pool_block_diag(pool_w)
    w_in_b, w_out_b = w_in.astype(BF16), w_out.astype(BF16)
    f1 = (ffn1_w_gate.astype(BF16), ffn1_w_up.astype(BF16), ffn1_w_down.astype(BF16))
    f2 = (ffn2_w_gate.astype(BF16), ffn2_w_up.astype(BF16), ffn2_w_down.astype(BF16))
    s_pool = jnp.transpose(state_pool, (0, 2, 1, 3))
    s_hgrn = jnp.transpose(state_hgrn, (0, 2, 3, 4, 1)).reshape(DEPTH, HG_WIDTH, HG_DK, nb)
    c_k = jnp.transpose(cache_k_win, (0, 1, 3, 4, 2)).reshape(DEPTH, nb, kvw, WINDOW)
    c_v = jnp.transpose(cache_v_win, (0, 1, 3, 4, 2)).reshape(DEPTH, nb, kvw, WINDOW)

    xp = x_prompt.reshape(bsz * seq, D_MODEL)
    xs = jnp.swapaxes(x_sample, 0, 1).reshape(steps * nb, D_MODEL)
    ffn_p = functools.partial(_ffn, tm=FFN_TILE, per_row=False, mod_rows=8, mod_row_block=prompt_mod_block)
    ffn_s = functools.partial(_ffn, tm=nb, per_row=True, mod_rows=nb, mod_row_block=0)
    st_p, st_s = [], []
    for l in range(DEPTH):
        xp = ffn_p(xp, mod, norm_ffn1, *f1, norm_final, l, 0)
        xs = ffn_s(xs, mod, norm_ffn1, *f1, norm_final, l, 0)
        xp, *sp = _prompt_mixer(xp.reshape(bsz, seq, D_MODEL), mod, norm_mix, w_in_b, w_out_b, pool_bd, pool_scale,
                                lbs, hgn, bm_p, attn_sinks, consts, l, tt=MIXER_TILE, mod_row_block=prompt_mod_block)
        xs, *ss = _sample_mixer(xs.reshape(steps, nb, D_MODEL), mod, norm_mix, w_in_b, w_out_b, pool_bd, pool_scale,
                                lbs, hgn, bm_s, sink_col, consts[1], s_pool, s_hgrn, c_k, c_v, l,
                                gb=SAMPLE_GROUP, pos0=PAST_LEN)
        st_p.append(sp)
        st_s.append(ss)
        last = l == DEPTH - 1
        xp = ffn_p(xp.reshape(bsz * seq, D_MODEL), mod, norm_ffn2, *f2, norm_final, l, 6, final=last)
        xs = ffn_s(xs.reshape(steps * nb, D_MODEL), mod, norm_ffn2, *f2, norm_final, l, 6, final=last)

    y_prompt = xp.reshape(bsz, seq, D_MODEL)
    y_sample = jnp.swapaxes(xs.reshape(steps, nb, D_MODEL), 0, 1)

    def stacked(states, i, shape):
        return jnp.stack([s[i] for s in states]).reshape((DEPTH,) + shape)

    outs = [stacked(st_p, 0, (bsz, POOL_BUF, POOL_WIDTH)),
            stacked(st_p, 1, (bsz, HG_HEADS, HG_DK, HG_DK)),
            stacked(st_p, 2, (bsz, WINDOW, KV_HEADS, HEAD_DIM)),
            stacked(st_p, 3, (bsz, WINDOW, KV_HEADS, HEAD_DIM)),
            jnp.transpose(stacked(st_s, 0, (POOL_BUF, nb, POOL_WIDTH)), (0, 2, 1, 3)),
            jnp.transpose(stacked(st_s, 1, (HG_HEADS, HG_DK, HG_DK, nb)), (0, 4, 1, 2, 3)),
            jnp.transpose(stacked(st_s, 2, (nb, KV_HEADS, HEAD_DIM, WINDOW)), (0, 1, 4, 2, 3)),
            jnp.transpose(stacked(st_s, 3, (nb, KV_HEADS, HEAD_DIM, WINDOW)), (0, 1, 4, 2, 3))]
    return (y_prompt, y_sample, *outs)
```

```python
import functools
import math

import numpy as np
import jax
import jax.numpy as jnp
from jax import lax
from jax.experimental import pallas as pl
from jax.experimental.pallas import tpu as pltpu

F32 = jnp.float32
BF16 = jnp.bfloat16

D_MODEL = 1024
DEPTH = 4
N_MOD = 9
D_FF = 2816
EPS = 1e-6
POOL_WIDTH = 256
POOL_WINDOWS = (2, 4, 8, 16)
POOL_GW = 64
POOL_BUF = 15
HG_WIDTH = 256
HG_HEADS = 4
HG_DK = 64
ATT_WIDTH = 512
ATT_HEADS = 8
KV_HEADS = 2
GQA_GROUP = 4
HEAD_DIM = 64
WINDOW = 128
REL_BUCKETS = 32
REL_MAX_DIST = 128
OFF_POOL, OFF_HQ, OFF_HF, OFF_HI, OFF_HG, OFF_AQ, OFF_AK, OFF_AV = 0, 256, 512, 768, 1024, 1280, 1792, 1920
IN_WIDTH = 2048
NEG = -1e30

VMEM_LIMIT = 56 * 1024 * 1024


def _dot(a, b):
    return jnp.dot(a, b, preferred_element_type=F32)


def _dot_nt(a, b):
    return lax.dot_general(a, b, (((1,), (1,)), ((), ())), preferred_element_type=F32)


def _dot_tn(a, b):
    return lax.dot_general(a, b, (((0,), (0,)), ((), ())), preferred_element_type=F32)


def _sigmoid(x):
    return 1.0 / (1.0 + jnp.exp(-x))


def _silu(x):
    return x * _sigmoid(x)


def _norm_mod(x, g, shift, scale):
    r = lax.rsqrt(jnp.mean(x * x, axis=-1, keepdims=True) + EPS)
    return (x * r) * g * (1.0 + scale) + shift


def _split3(x):
    hi = x.astype(BF16)
    r1 = x - hi.astype(F32)
    mid = r1.astype(BF16)
    lo = (r1 - mid.astype(F32)).astype(BF16)
    return hi, mid, lo


def _mod_kernel(c_ref, w_ref, b_ref, o_ref):
    sc = _silu(c_ref[...]).astype(BF16)
    o_ref[...] = _dot(sc, w_ref[...].astype(BF16)) + b_ref[...]


def _modulation(c_all, w_mod, b_mod):
    rows = c_all.shape[0]
    return pl.pallas_call(
        _mod_kernel,
        grid=(DEPTH, N_MOD),
        in_specs=[
            pl.BlockSpec((rows, D_MODEL), lambda l, j: (0, 0)),
            pl.BlockSpec((None, D_MODEL, D_MODEL), lambda l, j: (l, 0, j)),
            pl.BlockSpec((None, None, 1, D_MODEL), lambda l, j: (l, j, 0, 0)),
        ],
        out_specs=pl.BlockSpec((None, None, rows, D_MODEL), lambda l, j: (l, j, 0, 0)),
        out_shape=jax.ShapeDtypeStruct((DEPTH, N_MOD, rows, D_MODEL), F32),
        compiler_params=pltpu.CompilerParams(dimension_semantics=("arbitrary", "arbitrary"),
                                             vmem_limit_bytes=VMEM_LIMIT),
        name="adaln_mod",
    )(c_all, w_mod, b_mod.reshape(DEPTH, N_MOD, 1, D_MODEL))


def _mod_rows(mod_ref, j, row, per_row):
    if per_row:
        return mod_ref[j]
    return mod_ref[j, pl.ds(row, 1), :]


def _ffn_kernel(x_ref, mod_ref, n_ref, wg_ref, wu_ref, wd_ref, nf_ref, o_ref, *, j0, per_row, tiles_per_seq,
                final):
    row = pl.program_id(0) // tiles_per_seq
    shift = _mod_rows(mod_ref, j0, row, per_row)
    scale = _mod_rows(mod_ref, j0 + 1, row, per_row)
    gate = _mod_rows(mod_ref, j0 + 2, row, per_row)
    x = x_ref[...]
    h = _norm_mod(x, n_ref[...], shift, scale).astype(BF16)
    g = _dot(h, wg_ref[...])
    u = _dot(h, wu_ref[...])
    a = (_silu(g) * u).astype(BF16)
    y = x + (0.5 * gate) * _dot(a, wd_ref[...])
    if final:
        y = (y * lax.rsqrt(jnp.mean(y * y, axis=-1, keepdims=True) + EPS)) * nf_ref[...]
    o_ref[...] = y


def _ffn(x, mod, norm, wg, wu, wd, norm_final, layer, j0, *, tm, per_row, mod_rows, mod_row_block, final=False):
    rows = x.shape[0]
    tiles_per_seq = 1 if per_row else (rows // mod_rows) // tm
    kern = functools.partial(_ffn_kernel, j0=j0, per_row=per_row, tiles_per_seq=tiles_per_seq, final=final)
    return pl.pallas_call(
        kern,
        grid=(rows // tm,),
        in_specs=[
            pl.BlockSpec((tm, D_MODEL), lambda i: (i, 0)),
            pl.BlockSpec((None, N_MOD, mod_rows, D_MODEL), lambda i: (layer, 0, mod_row_block, 0)),
            pl.BlockSpec((None, 1, D_MODEL), lambda i: (layer, 0, 0)),
            pl.BlockSpec((None, D_MODEL, D_FF), lambda i: (layer, 0, 0), pipeline_mode=pl.Buffered(1)),
            pl.BlockSpec((None, D_MODEL, D_FF), lambda i: (layer, 0, 0), pipeline_mode=pl.Buffered(1)),
            pl.BlockSpec((None, D_FF, D_MODEL), lambda i: (layer, 0, 0), pipeline_mode=pl.Buffered(1)),
            pl.BlockSpec((1, D_MODEL), lambda i: (0, 0)),
        ],
        out_specs=pl.BlockSpec((tm, D_MODEL), lambda i: (i, 0)),
        out_shape=jax.ShapeDtypeStruct((rows, D_MODEL), F32),
        compiler_params=pltpu.CompilerParams(dimension_semantics=("arbitrary",), vmem_limit_bytes=VMEM_LIMIT),
        name="swiglu_half_step",
    )(x, mod, norm.reshape(DEPTH, 1, D_MODEL), wg, wu, wd, norm_final.reshape(1, D_MODEL))


def _lower_bound_kernel(x_ref, o_ref):
    rows = [x_ref[l:l + 1, :] for l in range(DEPTH)]
    m = functools.reduce(jnp.maximum, rows)
    e = [jnp.exp(r - m) for r in rows]
    s = functools.reduce(lambda a, b: a + b, e)
    sm = [ei / s for ei in e]
    acc = sm[0]
    for l in range(DEPTH):
        if l > 0:
            acc = acc + sm[l]
        o_ref[l:l + 1, :] = acc - sm[0]


def _lower_bounds(hgrn_lower):
    return pl.pallas_call(
        _lower_bound_kernel,
        out_shape=jax.ShapeDtypeStruct((DEPTH, HG_WIDTH), F32),
        name="hgrn_lower_bounds",
    )(hgrn_lower)


def _bucket_map(rows, cols, valid_rows, col_lo, col_hi):
    r = np.arange(rows)[:, None]
    c = np.arange(cols)[None, :]
    rel = c - WINDOW - r
    n = np.maximum(-rel, 0)
    exact = REL_BUCKETS // 2
    nf = np.maximum(n, 1).astype(np.float32)
    large = exact + (np.log(nf / np.float32(exact)) / np.float32(math.log(REL_MAX_DIST / exact))
                     * np.float32(REL_BUCKETS - exact)).astype(np.int32)
    large = np.minimum(large, REL_BUCKETS - 1)
    bucket = np.where(n < exact, n, large)
    valid = (rel <= 0) & (rel > -WINDOW) & (r < valid_rows) & (c >= col_lo) & (c < col_hi)
    return np.where(valid, bucket, -1).astype(np.int32)


def _bias_table_kernel(bkt_ref, rb_ref, o_ref):
    h = pl.program_id(1)
    bkt = bkt_ref[...]
    acc = jnp.full(bkt.shape, NEG, F32)
    for b in range(REL_BUCKETS):
        acc = jnp.where(bkt == b, rb_ref[b, h], acc)
    o_ref[...] = acc


def _bias_tables(bucket_maps, rel_bias):
    m, r, c = bucket_maps.shape
    return pl.pallas_call(
        _bias_table_kernel,
        grid=(m, ATT_HEADS),
        in_specs=[pl.BlockSpec((None, r, c), lambda i, h: (i, 0, 0)),
                  pl.BlockSpec(memory_space=pltpu.SMEM)],
        out_specs=pl.BlockSpec((None, None, r, c), lambda i, h: (i, h, 0, 0)),
        out_shape=jax.ShapeDtypeStruct((m, ATT_HEADS, r, c), F32),
        compiler_params=pltpu.CompilerParams(dimension_semantics=("arbitrary", "arbitrary")),
        name="rel_bias_tables",
    )(bucket_maps, rel_bias)


def _block_ones(n, blk):
    i = np.arange(n)
    return (i[:, None] // blk == i[None, :] // blk)


def _mixer_consts(tt):
    tri = jnp.asarray(np.tril(np.ones((tt, tt), np.float32)), BF16)
    blk = _block_ones(HG_WIDTH, HG_DK)
    return tri, jnp.asarray(blk, BF16), jnp.asarray(blk, F32)


def _pool_block_diag(pool_w):
    out = jnp.zeros((DEPTH, POOL_WIDTH, POOL_WIDTH), BF16)
    for g in range(len(POOL_WINDOWS)):
        sl = slice(g * POOL_GW, (g + 1) * POOL_GW)
        out = out.at[:, sl, sl].set(pool_w[:, g].astype(BF16))
    return out


def _head_masks():
    lane = lax.broadcasted_iota(jnp.int32, (1, HG_WIDTH), 1)
    return [jnp.where((lane >= HG_DK * h) & (lane < HG_DK * (h + 1)), 1.0, 0.0).astype(F32) for h in range(HG_HEADS)]


def _prompt_mixer_kernel(x_ref, xn_ref, mod_ref, n_ref, win_ref, wout_ref, pbd_ref, pscale_ref, lb_ref, hgn_ref,
                         bm_ref, sink_ref, tri_ref, ones_ref, bdm_ref,
                         xo_ref, pool_o, s_o, k_o, v_o,
                         za_ref, zb_ref, ubuf, kbuf, vbuf, st_ref, cat_ref, *, layer, tt, nt, ntiles):
    b = pl.program_id(0)
    p = pl.program_id(1)
    last_pair = pl.num_programs(1) - 1
    nxt = jnp.minimum(b * nt + 2 * p + 2, ntiles - 1)
    bn = nxt // nt
    n_chunks = 8
    cw = IN_WIDTH // n_chunks

    def projector(x_rows, bsel, dst_ref):
        h = _norm_mod(x_rows, n_ref[...], mod_ref[3, pl.ds(bsel, 1), :], mod_ref[4, pl.ds(bsel, 1), :]).astype(BF16)
        state = {"k": 0}

        def step():
            k = state["k"]
            if k < n_chunks:
                dst_ref[:, k * cw:(k + 1) * cw] = _dot(h, win_ref[:, k * cw:(k + 1) * cw])
                state["k"] = k + 1

        return step

    @pl.when((b == 0) & (p == 0))
    def _():
        first_proj = projector(x_ref[0:tt, :], b, za_ref)
        for _ in range(n_chunks):
            first_proj()

    @pl.when(p == 0)
    def _():
        ubuf[0:16, :] = jnp.zeros((16, POOL_WIDTH), F32)
        kbuf[:, 0:WINDOW, :] = jnp.zeros((KV_HEADS, WINDOW, HEAD_DIM), BF16)
        vbuf[:, 0:WINDOW, :] = jnp.zeros((KV_HEADS, WINDOW, HEAD_DIM), BF16)
        st_ref[...] = jnp.zeros((HG_WIDTH, HG_WIDTH), F32)

    gate = mod_ref[5, pl.ds(b, 1), :]

    def mix(z_ref, x, row0, t, side):
        lane = lax.broadcasted_iota(jnp.int32, (1, POOL_WIDTH), 1)

        u = z_ref[:, OFF_POOL:OFF_POOL + POOL_WIDTH]
        ubuf[16:16 + tt, :] = u
        e = ubuf[...]
        s2 = e + pltpu.roll(e, 1, 0)
        s4 = s2 + pltpu.roll(s2, 2, 0)
        s8 = s4 + pltpu.roll(s4, 4, 0)
        s16 = s8 + pltpu.roll(s8, 8, 0)
        sel = jnp.where(lane < 64, s2, jnp.where(lane < 128, s4, jnp.where(lane < 192, s8, s16)))[16:]
        wl = jnp.where(lane < 64, 2.0, jnp.where(lane < 128, 4.0, jnp.where(lane < 192, 8.0, 16.0))).astype(F32)
        pos1 = (t * tt + 1 + lax.broadcasted_iota(jnp.int32, (tt, 1), 0)).astype(F32)
        dpool = sel / jnp.minimum(pos1, wl) - u
        ypool = _dot(dpool.astype(BF16), pbd_ref[...]) * pscale_ref[...]
        cat_ref[:, 0:POOL_WIDTH] = ypool.astype(BF16)
        ubuf[0:16, :] = ubuf[tt:tt + 16, :]
        side()


        for g in range(KV_HEADS):
            kbuf[g, WINDOW:WINDOW + tt, :] = z_ref[:, OFF_AK + HEAD_DIM * g:OFF_AK + HEAD_DIM * (g + 1)].astype(BF16)
            vbuf[g, WINDOW:WINDOW + tt, :] = z_ref[:, OFF_AV + HEAD_DIM * g:OFF_AV + HEAD_DIM * (g + 1)].astype(BF16)
        first = jnp.where(t == 0, 1, 0)
        bands = [(j, g) for j in range(tt // WINDOW) for g in range(KV_HEADS)]
        logits = []
        for j, g in bands:
            r0 = j * WINDOW
            c0 = OFF_AQ + HEAD_DIM * g * GQA_GROUP
            qg = jnp.concatenate([z_ref[r0:r0 + WINDOW, c0 + HEAD_DIM * i:c0 + HEAD_DIM * (i + 1)]
                                  for i in range(GQA_GROUP)], axis=0)
            qg = (qg * (HEAD_DIM ** -0.5)).astype(BF16)
            logits.append(_dot_nt(qg, kbuf[g, r0:r0 + 2 * WINDOW, :]))
        side()

        lb = lb_ref[...]
        q = _silu(z_ref[:, OFF_HQ:OFF_HQ + HG_WIDTH])
        f = lb + (1.0 - lb) * _sigmoid(z_ref[:, OFF_HF:OFF_HF + HG_WIDTH])
        k = 1.0 - f
        v = z_ref[:, OFF_HI:OFF_HI + HG_WIDTH]
        vb = v.astype(BF16)
        tri = tri_ref[...]
        hi, mid, lo = _split3(jnp.log(f))
        bcum = _dot(tri, hi) + _dot(tri, mid) + _dot(tri, lo)
        side()
        ones_blk = ones_ref[...]
        hm = _head_masks()

        n16 = tt // 16
        half_shape = (n16, 2, 8, HG_WIDTH)
        q4 = q.reshape(half_shape)
        b4 = bcum.reshape(half_shape)
        c4 = (bcum - jnp.log(k)).reshape(half_shape)
        v4 = v.reshape(half_shape)
        rowi = lax.broadcasted_iota(jnp.int32, (1, 8, 1), 1)
        o_up = jnp.zeros((n16, 8, HG_WIDTH), F32)
        o_dn = jnp.zeros((n16, 8, HG_WIDTH), F32)
        nh = n16 * 8
        for s in range(8):
            cs = c4[:, 0, s:s + 1, :]
            vs = v4[:, 0, s:s + 1, :]
            x_up = jnp.where(rowi >= s, q4[:, 0] * jnp.exp(b4[:, 0] - cs), 0.0)
            x_dn = q4[:, 1] * jnp.exp(b4[:, 1] - cs)
            xs = jnp.concatenate([x_up.reshape(nh, HG_WIDTH), x_dn.reshape(nh, HG_WIDTH)], axis=0)
            r = _dot(xs.astype(BF16), ones_blk)
            o_up = o_up + r[0:nh].reshape(n16, 8, HG_WIDTH) * vs
            o_dn = o_dn + r[nh:2 * nh].reshape(n16, 8, HG_WIDTH) * vs
            if s % 4 == 3:
                side()
        for s in range(8):
            cs = c4[:, 1, s:s + 1, :]
            x_dn = jnp.where(rowi >= s, q4[:, 1] * jnp.exp(b4[:, 1] - cs), 0.0)
            r = _dot(x_dn.reshape(nh, HG_WIDTH).astype(BF16), ones_blk)
            o_dn = o_dn + r.reshape(n16, 8, HG_WIDTH) * v4[:, 1, s:s + 1, :]
            if s % 4 == 3:
                side()
        o_tot = jnp.concatenate([o_up[:, None], o_dn[:, None]], axis=1).reshape(tt, HG_WIDTH)

        blocks = []
        blk = 32
        while blk <= tt:
            blocks += [(i * blk, i * blk + blk // 2, (i + 1) * blk) for i in range(tt // blk)]
            blk *= 2
        scores = []
        for lo_s, mid_s, hi_s in blocks:
            ref = bcum[mid_s - 1:mid_s, :]
            qt = q[mid_s:hi_s] * jnp.exp(bcum[mid_s:hi_s] - ref)
            kt = (k[lo_s:mid_s] * jnp.exp(ref - bcum[lo_s:mid_s])).astype(BF16)
            qs = jnp.concatenate([qt * hm[hh] for hh in range(HG_HEADS)], axis=0).astype(BF16)
            scores.append(_dot_nt(qs, kt).astype(BF16))

        st = st_ref[...]
        qin = (q * jnp.exp(bcum)).astype(BF16)
        o_tot = o_tot + _dot_nt(qin, st.astype(BF16))
        blast = bcum[tt - 1:tt, :]
        kout = (k * jnp.exp(blast - bcum)).astype(BF16)
        st_ref[...] = st * jnp.exp(blast) + _dot_tn(vb, kout) * bdm_ref[...]
        side()

        probs, dens = [], []
        for (j, g), lg_all in zip(bands, logits):
            h0 = g * GQA_GROUP
            bm = bm_ref[first, h0:h0 + GQA_GROUP] if j == 0 else bm_ref[0, h0:h0 + GQA_GROUP]
            ps = []
            for i in range(GQA_GROUP):
                lg = lg_all[i * WINDOW:(i + 1) * WINDOW] + bm[i]
                sink = sink_ref[layer, h0 + i]
                m = jnp.maximum(jnp.max(lg, axis=-1, keepdims=True), sink)
                p_ = jnp.exp(lg - m)
                dens.append(jnp.sum(p_, axis=-1, keepdims=True) + jnp.exp(sink - m))
                ps.append(p_.astype(BF16))
            probs.append(jnp.concatenate(ps, axis=0))
        for n, (j, g) in enumerate(bands):
            r0 = j * WINDOW
            og = _dot(probs[n], vbuf[g, r0:r0 + 2 * WINDOW, :])
            for i in range(GQA_GROUP):
                cc = POOL_WIDTH + HG_WIDTH + HEAD_DIM * (g * GQA_GROUP + i)
                oh = og[i * WINDOW:(i + 1) * WINDOW] / dens[n * GQA_GROUP + i]
                cat_ref[r0:r0 + WINDOW, cc:cc + HEAD_DIM] = oh.astype(BF16)
        for g in range(KV_HEADS):
            kbuf[g, 0:WINDOW, :] = kbuf[g, tt:tt + WINDOW, :]
            vbuf[g, 0:WINDOW, :] = vbuf[g, tt:tt + WINDOW, :]
        side()

        adds = {}
        for (lo_s, mid_s, hi_s), a in zip(blocks, scores):
            half = mid_s - lo_s
            ov = _dot(a, vb[lo_s:mid_s])
            oi = ov[0:half] * hm[0]
            for hh in range(1, HG_HEADS):
                oi = oi + ov[hh * half:(hh + 1) * half] * hm[hh]
            adds.setdefault(2 * half, []).extend([jnp.zeros((half, HG_WIDTH), F32), oi])
        for parts in adds.values():
            o_tot = o_tot + jnp.concatenate(parts, axis=0)

        sq_hi, sq_mid, _ = _split3(o_tot * o_tot)
        ms = (_dot(sq_hi, ones_blk) + _dot(sq_mid, ones_blk)) * (1.0 / HG_DK)
        y_hg = o_tot * lax.rsqrt(ms + EPS) * hgn_ref[...] * _silu(z_ref[:, OFF_HG:OFF_HG + HG_WIDTH])
        cat_ref[:, POOL_WIDTH:POOL_WIDTH + HG_WIDTH] = y_hg.astype(BF16)

        xo_ref[row0:row0 + tt, :] = x + gate * _dot(cat_ref[...], wout_ref[...])

        for _ in range(n_chunks):
            side()

    mix(za_ref, x_ref[0:tt, :], 0, 2 * p, projector(x_ref[tt:2 * tt, :], b, zb_ref))
    mix(zb_ref, x_ref[tt:2 * tt, :], tt, 2 * p + 1, projector(xn_ref[...], bn, za_ref))

    @pl.when(p == last_pair)
    def _():
        pool_o[...] = ubuf[pl.ds(1, POOL_BUF), :]
        s_kv = st_ref[...].T
        for hh in range(HG_HEADS):
            s_o[hh] = s_kv[HG_DK * hh:HG_DK * (hh + 1), HG_DK * hh:HG_DK * (hh + 1)]
        k_o[...] = zb_ref[tt - WINDOW:tt, OFF_AK:OFF_AV]
        v_o[...] = zb_ref[tt - WINDOW:tt, OFF_AV:IN_WIDTH]


def _prompt_mixer(x, mod, norm, w_in, w_out, pool_bd, pool_scale, lbs, hg_norm, bm, sinks, consts, layer, *,
                  tt, mod_row_block):
    bsz, seq, _ = x.shape
    tri, ones_blk, bdm = consts
    nt = seq // tt
    ntiles = bsz * nt
    kern = functools.partial(_prompt_mixer_kernel, layer=layer, tt=tt, nt=nt, ntiles=ntiles)
    full = lambda *shape: pl.BlockSpec(shape, lambda b, p: (0,) * len(shape))
    lsel = lambda *shape: pl.BlockSpec((None,) + shape, lambda b, p: (layer,) + (0,) * len(shape))
    mix_pair = lambda b, p: (b, p, 0)
    mix_seq = lambda *z: (lambda b, p: (b,) + z)

    def next_tile(b, p):
        nxt = jnp.minimum(b * nt + 2 * p + 2, ntiles - 1)
        return (nxt // nt, nxt % nt, 0)

    return pl.pallas_call(
        kern,
        grid=(bsz, nt // 2),
        in_specs=[
            pl.BlockSpec((None, 2 * tt, D_MODEL), mix_pair),
            pl.BlockSpec((None, tt, D_MODEL), next_tile),
            pl.BlockSpec((None, N_MOD, 8, D_MODEL), lambda b, p: (layer, 0, mod_row_block, 0)),
            lsel(1, D_MODEL),
            lsel(D_MODEL, IN_WIDTH),
            lsel(D_MODEL, D_MODEL),
            lsel(POOL_WIDTH, POOL_WIDTH),
            lsel(1, POOL_WIDTH),
            lsel(1, HG_WIDTH),
            lsel(1, HG_WIDTH),
            full(2, ATT_HEADS, WINDOW, 2 * WINDOW),
            pl.BlockSpec(memory_space=pltpu.SMEM),
            full(tt, tt),
            full(HG_WIDTH, HG_WIDTH),
            full(HG_WIDTH, HG_WIDTH),
        ],
        out_specs=[
            pl.BlockSpec((None, 2 * tt, D_MODEL), mix_pair),
            pl.BlockSpec((None, POOL_BUF, POOL_WIDTH), mix_seq(0, 0)),
            pl.BlockSpec((None, HG_HEADS, HG_DK, HG_DK), mix_seq(0, 0, 0)),
            pl.BlockSpec((None, WINDOW, KV_HEADS * HEAD_DIM), mix_seq(0, 0)),
            pl.BlockSpec((None, WINDOW, KV_HEADS * HEAD_DIM), mix_seq(0, 0)),
        ],
        out_shape=[
            jax.ShapeDtypeStruct((bsz, seq, D_MODEL), F32),
            jax.ShapeDtypeStruct((bsz, POOL_BUF, POOL_WIDTH), F32),
            jax.ShapeDtypeStruct((bsz, HG_HEADS, HG_DK, HG_DK), F32),
            jax.ShapeDtypeStruct((bsz, WINDOW, KV_HEADS * HEAD_DIM), F32),
            jax.ShapeDtypeStruct((bsz, WINDOW, KV_HEADS * HEAD_DIM), F32),
        ],
        scratch_shapes=[
            pltpu.VMEM((tt, IN_WIDTH), F32),
            pltpu.VMEM((tt, IN_WIDTH), F32),
            pltpu.VMEM((16 + tt, POOL_WIDTH), F32),
            pltpu.VMEM((KV_HEADS, WINDOW + tt, HEAD_DIM), BF16),
            pltpu.VMEM((KV_HEADS, WINDOW + tt, HEAD_DIM), BF16),
            pltpu.VMEM((HG_WIDTH, HG_WIDTH), F32),
            pltpu.VMEM((tt, D_MODEL), BF16),
        ],
        compiler_params=pltpu.CompilerParams(dimension_semantics=("arbitrary", "arbitrary"),
                                             vmem_limit_bytes=VMEM_LIMIT),
        name="prompt_mixer",
    )(x, x, mod, norm.reshape(DEPTH, 1, D_MODEL), w_in, w_out, pool_bd, pool_scale.reshape(DEPTH, 1, POOL_WIDTH),
      lbs.reshape(DEPTH, 1, HG_WIDTH), hg_norm, bm, sinks, tri, ones_blk, bdm)


def _sample_mixer_kernel(x_ref, mod_ref, n_ref, win_ref, wout_ref, pbd_ref, pscale_ref, lb_ref, hgn_ref,
                         bms_ref, sinkc_ref, ones_ref, sp_ref, s0_ref, kt_ref, vt_ref,
                         xo_ref, pool_o, s_o, kt_o, vt_o,
                         z_ref, cat_ref, qs_ref, os_ref, kn_ref, vn_ref, qint_ref, koutt_ref, vtt_ref, gt_ref,
                         oin_ref, oacc_ref, *, steps, nb, gb, rb, pos0):
    g = pl.program_id(0)
    last = pl.num_programs(0) - 1
    base = pl.multiple_of(g * gb, gb)
    kvw = KV_HEADS * HEAD_DIM

    def rows(t):
        return slice(t * nb, (t + 1) * nb)

    @pl.when(g == 0)
    def _():
        for t in range(steps):
            h = _norm_mod(x_ref[t], n_ref[...], mod_ref[0], mod_ref[1]).astype(BF16)
            z_ref[rows(t), :] = _dot(h, win_ref[...])
        lane = lax.broadcasted_iota(jnp.int32, (1, POOL_WIDTH), 1)

        ext = [sp_ref[r] for r in range(POOL_BUF)] + [z_ref[rows(t), OFF_POOL:OFF_POOL + POOL_WIDTH] for t in range(steps)]
        for r in range(POOL_BUF):
            pool_o[r] = ext[r + steps]
        sums = {1: ext}
        w = 1
        while w < 16:
            prev = sums[w]
            sums[2 * w] = [None if (r < 2 * w - 1) else prev[r] + prev[r - w] for r in range(len(ext))]
            w *= 2
        dl = []
        for t in range(steps):
            r = POOL_BUF + t
            sel = jnp.where(lane < 64, sums[2][r], jnp.where(lane < 128, sums[4][r], jnp.where(lane < 192, sums[8][r], sums[16][r])))
            cnt = [float(min(pos0 + t + 1, wd)) for wd in POOL_WINDOWS]
            cl = jnp.where(lane < 64, cnt[0], jnp.where(lane < 128, cnt[1], jnp.where(lane < 192, cnt[2], cnt[3]))).astype(F32)
            dl.append(sel / cl - ext[r])
        ypool = _dot(jnp.concatenate(dl, axis=0).astype(BF16), pbd_ref[...]) * pscale_ref[...]
        cat_ref[:, 0:POOL_WIDTH] = ypool

        lb = lb_ref[...]
        qv, kv_, vv, bv = [], [], [], []
        bacc = None
        for t in range(steps):
            qv.append(_silu(z_ref[rows(t), OFF_HQ:OFF_HQ + HG_WIDTH]))
            f = lb + (1.0 - lb) * _sigmoid(z_ref[rows(t), OFF_HF:OFF_HF + HG_WIDTH])
            kv_.append(1.0 - f)
            vv.append(z_ref[rows(t), OFF_HI:OFF_HI + HG_WIDTH])
            bacc = jnp.log(f) if bacc is None else bacc + jnp.log(f)
            bv.append(bacc)
        pairs = [(t, s) for t in range(steps) for s in range(t + 1)]
        xs = jnp.concatenate([qv[t] * kv_[s] * jnp.exp(bv[t] - bv[s]) for t, s in pairs], axis=0).astype(BF16)
        rr = _dot(xs, ones_ref[...])
        o_intra = [None] * steps
        for i, (t, s) in enumerate(pairs):
            term = rr[i * nb:(i + 1) * nb] * vv[s]
            o_intra[t] = term if o_intra[t] is None else o_intra[t] + term
        blast = bv[steps - 1]
        gt_ref[...] = jnp.exp(blast).T
        for t in range(steps):
            oin_ref[t] = o_intra[t]
            qint_ref[t] = (qv[t] * jnp.exp(bv[t])).T
            koutt_ref[t] = (kv_[t] * jnp.exp(blast - bv[t])).T
            vtt_ref[t] = vv[t].T
        oacc_ref[...] = jnp.zeros(oacc_ref.shape, F32)

        zeros64 = jnp.zeros((nb, HEAD_DIM), F32)
        kn_ref[...] = jnp.zeros(kn_ref.shape, F32)
        vn_ref[...] = jnp.zeros(vn_ref.shape, F32)
        for t in range(steps):
            kn_ref[:, t, :] = z_ref[rows(t), OFF_AK:OFF_AK + kvw]
            vn_ref[:, t, :] = z_ref[rows(t), OFF_AV:OFF_AV + kvw]
            for hd in range(ATT_HEADS):
                c0 = OFF_AQ + HEAD_DIM * hd
                qsl = z_ref[rows(t), c0:c0 + HEAD_DIM] * (HEAD_DIM ** -0.5)
                parts = [qsl, zeros64] if hd < GQA_GROUP else [zeros64, qsl]
                qs_ref[:, ATT_HEADS * t + hd, :] = jnp.concatenate(parts, axis=1)

    r0 = pl.multiple_of(g * rb, rb)
    hrow = pl.multiple_of((g * rb // HG_DK) * HG_DK, HG_DK)
    vts = [vtt_ref[t, pl.ds(hrow, HG_DK), :] for t in range(steps)]
    accs = [oacc_ref[t, pl.ds(hrow, HG_DK), :] for t in range(steps)]
    for i in range(rb):
        s0 = s0_ref[i]
        snew = gt_ref[pl.ds(r0 + i, 1), :] * s0
        for t in range(steps):
            accs[t] = accs[t] + qint_ref[t, pl.ds(r0 + i, 1), :] * s0
            snew = snew + koutt_ref[t, pl.ds(r0 + i, 1), :] * vts[t]
        s_o[i] = snew
    for t in range(steps):
        oacc_ref[t, pl.ds(hrow, HG_DK), :] = accs[t]

    bm_old = bms_ref[:, 0:WINDOW]
    bm_new = bms_ref[:, WINDOW:WINDOW + 8]
    sinkc = sinkc_ref[...]
    seqs = range(gb)
    qn = [qs_ref[base + n].astype(BF16) for n in seqs]
    l_old = [_dot(qn[n], kt_ref[n].astype(BF16)) for n in seqs]
    l_new = [_dot_nt(qn[n], kn_ref[base + n].astype(BF16)) for n in seqs]
    def new_columns(ref):
        new = ref[pl.ds(base, gb)].reshape(gb * 8, kvw)
        if gb * 8 < WINDOW:
            new = jnp.concatenate([new, jnp.zeros((WINDOW - gb * 8, kvw), F32)], axis=0)
        return new.T

    knt = new_columns(kn_ref)
    vnt = new_columns(vn_ref)
    for n in seqs:
        kt_o[n] = jnp.concatenate([kt_ref[n][:, steps:], knt[:, 8 * n:8 * n + steps]], axis=1)
        vt_o[n] = jnp.concatenate([vt_ref[n][:, steps:], vnt[:, 8 * n:8 * n + steps]], axis=1)
    p_old, p_new, den = [], [], []
    for n in seqs:
        lo_, ln_ = l_old[n] + bm_old, l_new[n] + bm_new
        m = jnp.maximum(jnp.maximum(jnp.max(lo_, axis=-1, keepdims=True), jnp.max(ln_, axis=-1, keepdims=True)), sinkc)
        po, pn = jnp.exp(lo_ - m), jnp.exp(ln_ - m)
        den.append(jnp.sum(po, axis=-1, keepdims=True) + jnp.sum(pn, axis=-1, keepdims=True) + jnp.exp(sinkc - m))
        p_old.append(po.astype(BF16))
        p_new.append(pn.astype(BF16))
    for n in seqs:
        o = _dot_nt(p_old[n], vt_ref[n].astype(BF16)) + _dot(p_new[n], vn_ref[base + n].astype(BF16))
        os_ref[n] = o / den[n]
    for t in range(steps):
        for hd in range(ATT_HEADS):
            c0 = HEAD_DIM * (hd // GQA_GROUP)
            cc = POOL_WIDTH + HG_WIDTH + HEAD_DIM * hd
            cat_ref[pl.ds(t * nb + base, gb), cc:cc + HEAD_DIM] = os_ref[:, ATT_HEADS * t + hd, c0:c0 + HEAD_DIM]

    @pl.when(g == last)
    def _():
        ones_blk = ones_ref[...]
        for t in range(steps):
            o = oin_ref[t] + oacc_ref[t].T
            sq_hi, sq_mid, _ = _split3(o * o)
            ms = (_dot(sq_hi, ones_blk) + _dot(sq_mid, ones_blk)) * (1.0 / HG_DK)
            gate_t = _silu(z_ref[rows(t), OFF_HG:OFF_HG + HG_WIDTH])
            cat_ref[rows(t), POOL_WIDTH:POOL_WIDTH + HG_WIDTH] = o * lax.rsqrt(ms + EPS) * hgn_ref[...] * gate_t
        for t in range(steps):
            y = _dot(cat_ref[rows(t), :].astype(BF16), wout_ref[...])
            xo_ref[t] = x_ref[t] + mod_ref[2] * y


N_SAMPLE_MIXER_INPUTS = 16


def _sample_mixer_kernel_into(*refs, n_stacks, **kw):
    _sample_mixer_kernel(*refs[:N_SAMPLE_MIXER_INPUTS], *refs[N_SAMPLE_MIXER_INPUTS + n_stacks:], **kw)


def _sample_mixer(x, mod, norm, w_in, w_out, pool_bd, pool_scale, lbs, hg_norm, bms, sink_col, ones_blk,
                  pool_hist, hgrn_state, cache_kt, cache_vt, stacks, layer, *, gb, pos0):
    steps, nb, _ = x.shape
    rows_th = steps * ATT_HEADS
    ngrp = nb // gb
    rb = HG_WIDTH // ngrp
    kvw = KV_HEADS * HEAD_DIM
    stacks = () if stacks is None else tuple(stacks)
    kern = functools.partial(_sample_mixer_kernel_into, n_stacks=len(stacks), steps=steps, nb=nb, gb=gb, rb=rb,
                             pos0=pos0)
    full = lambda *shape: pl.BlockSpec(shape, lambda g: (0,) * len(shape))
    lsel = lambda *shape: pl.BlockSpec((None,) + shape, lambda g: (layer,) + (0,) * len(shape))
    grp = lambda *shape: pl.BlockSpec((None,) + shape, lambda g: (layer, g) + (0,) * (len(shape) - 1))
    return pl.pallas_call(
        kern,
        grid=(ngrp,),
        in_specs=[
            full(steps, nb, D_MODEL),
            pl.BlockSpec((None, 3, nb, D_MODEL), lambda g: (layer, 1, 0, 0)),
            lsel(1, D_MODEL),
            lsel(D_MODEL, IN_WIDTH),
            lsel(D_MODEL, D_MODEL),
            lsel(POOL_WIDTH, POOL_WIDTH),
            lsel(1, POOL_WIDTH),
            lsel(1, HG_WIDTH),
            lsel(1, HG_WIDTH),
            full(rows_th, 2 * WINDOW),
            lsel(rows_th, 1),
            full(HG_WIDTH, HG_WIDTH),
            lsel(POOL_BUF, nb, POOL_WIDTH),
            grp(rb, HG_DK, nb),
            grp(gb, kvw, WINDOW),
            grp(gb, kvw, WINDOW),
        ] + [pl.BlockSpec(memory_space=pl.ANY)] * len(stacks),
        out_specs=[
            full(steps, nb, D_MODEL),
            lsel(POOL_BUF, nb, POOL_WIDTH),
            grp(rb, HG_DK, nb),
            grp(gb, kvw, WINDOW),
            grp(gb, kvw, WINDOW),
        ],
        out_shape=[
            jax.ShapeDtypeStruct((steps, nb, D_MODEL), F32),
            jax.ShapeDtypeStruct((DEPTH, POOL_BUF, nb, POOL_WIDTH), F32),
            jax.ShapeDtypeStruct((DEPTH, HG_WIDTH, HG_DK, nb), F32),
            jax.ShapeDtypeStruct((DEPTH, nb, kvw, WINDOW), F32),
            jax.ShapeDtypeStruct((DEPTH, nb, kvw, WINDOW), F32),
        ],
        input_output_aliases={N_SAMPLE_MIXER_INPUTS + i: 1 + i for i in range(len(stacks))},
        scratch_shapes=[
            pltpu.VMEM((steps * nb, IN_WIDTH), F32),
            pltpu.VMEM((steps * nb, D_MODEL), F32),
            pltpu.VMEM((nb, rows_th, kvw), F32),
            pltpu.VMEM((gb, rows_th, kvw), F32),
            pltpu.VMEM((nb, 8, kvw), F32),
            pltpu.VMEM((nb, 8, kvw), F32),
            pltpu.VMEM((steps, HG_WIDTH, nb), F32),
            pltpu.VMEM((steps, HG_WIDTH, nb), F32),
            pltpu.VMEM((steps, HG_WIDTH, nb), F32),
            pltpu.VMEM((HG_WIDTH, nb), F32),
            pltpu.VMEM((steps, nb, HG_WIDTH), F32),
            pltpu.VMEM((steps, HG_WIDTH, nb), F32),
        ],
        compiler_params=pltpu.CompilerParams(dimension_semantics=("arbitrary",), vmem_limit_bytes=VMEM_LIMIT),
        name="sample_mixer",
    )(x, mod, norm.reshape(DEPTH, 1, D_MODEL), w_in, w_out, pool_bd, pool_scale.reshape(DEPTH, 1, POOL_WIDTH),
      lbs.reshape(DEPTH, 1, HG_WIDTH), hg_norm, bms, sink_col, ones_blk, pool_hist, hgrn_state, cache_kt, cache_vt,
      *stacks)


FFN_TILE = 512
MIXER_TILE = 256
SAMPLE_GROUP = 8
PAST_LEN = 8192


def kernel(x_prompt, x_sample, c_prompt, c_sample, state_pool, state_hgrn, cache_k_win, cache_v_win, norm_ffn1, norm_mix, norm_ffn2, w_mod, b_mod, ffn1_w_gate, ffn1_w_up, ffn1_w_down, w_in, w_out, pool_w, pool_scale, hgrn_lower, hgrn_norm, attn_sinks, rel_bias, ffn2_w_gate, ffn2_w_up, ffn2_w_down, norm_final):
    bsz, seq, _ = x_prompt.shape
    nb, steps, _ = x_sample.shape
    kvw = KV_HEADS * HEAD_DIM

    mod = _modulation(jnp.concatenate([c_sample, c_prompt], axis=0), w_mod, b_mod)
    prompt_mod_block = nb // 8
    lbs = _lower_bounds(hgrn_lower)
    maps_p = np.stack([_bucket_map(WINDOW, 2 * WINDOW, WINDOW, 0, 2 * WINDOW),
                       _bucket_map(WINDOW, 2 * WINDOW, WINDOW, WINDOW, 2 * WINDOW)])
    bm_p = _bias_tables(jnp.asarray(maps_p), rel_bias)
    maps_s = _bucket_map(8, 2 * WINDOW, steps, 0, WINDOW + steps)[None]
    bm_s = _bias_tables(jnp.asarray(maps_s), rel_bias)[0]
    bm_s = jnp.swapaxes(bm_s, 0, 1).reshape(8 * ATT_HEADS, 2 * WINDOW)[:steps * ATT_HEADS]
    sink_col = jnp.tile(attn_sinks, (1, steps)).reshape(DEPTH, steps * ATT_HEADS, 1)
    consts = _mixer_consts(MIXER_TILE)
    hgn = jnp.tile(hgrn_norm, (1, HG_HEADS)).reshape(DEPTH, 1, HG_WIDTH)
    pool_bd = _pool_block_diag(pool_w)
    w_in_b, w_out_b = w_in.astype(BF16), w_out.astype(BF16)
    f1 = (ffn1_w_gate.astype(BF16), ffn1_w_up.astype(BF16), ffn1_w_down.astype(BF16))
    f2 = (ffn2_w_gate.astype(BF16), ffn2_w_up.astype(BF16), ffn2_w_down.astype(BF16))
    s_pool = jnp.transpose(state_pool, (0, 2, 1, 3))
    s_hgrn = jnp.transpose(state_hgrn, (0, 2, 3, 4, 1)).reshape(DEPTH, HG_WIDTH, HG_DK, nb)
    c_k = jnp.transpose(cache_k_win, (0, 1, 3, 4, 2)).reshape(DEPTH, nb, kvw, WINDOW)
    c_v = jnp.transpose(cache_v_win, (0, 1, 3, 4, 2)).reshape(DEPTH, nb, kvw, WINDOW)

    xp = x_prompt.reshape(bsz * seq, D_MODEL)
    xs = jnp.swapaxes(x_sample, 0, 1).reshape(steps * nb, D_MODEL)
    ffn_p = functools.partial(_ffn, tm=FFN_TILE, per_row=False, mod_rows=8, mod_row_block=prompt_mod_block)
    ffn_s = functools.partial(_ffn, tm=nb, per_row=True, mod_rows=nb, mod_row_block=0)
    st_p, stacks_s = [], None
    for l in range(DEPTH):
        xp = ffn_p(xp, mod, norm_ffn1, *f1, norm_final, l, 0)
        xs = ffn_s(xs, mod, norm_ffn1, *f1, norm_final, l, 0)
        xp, *sp = _prompt_mixer(xp.reshape(bsz, seq, D_MODEL), mod, norm_mix, w_in_b, w_out_b, pool_bd, pool_scale,
                                lbs, hgn, bm_p, attn_sinks, consts, l, tt=MIXER_TILE, mod_row_block=prompt_mod_block)
        xs, *stacks_s = _sample_mixer(xs.reshape(steps, nb, D_MODEL), mod, norm_mix, w_in_b, w_out_b, pool_bd,
                                      pool_scale, lbs, hgn, bm_s, sink_col, consts[1], s_pool, s_hgrn, c_k, c_v,
                                      stacks_s, l, gb=SAMPLE_GROUP, pos0=PAST_LEN)
        st_p.append(sp)
        last = l == DEPTH - 1
        xp = ffn_p(xp.reshape(bsz * seq, D_MODEL), mod, norm_ffn2, *f2, norm_final, l, 6, final=last)
        xs = ffn_s(xs.reshape(steps * nb, D_MODEL), mod, norm_ffn2, *f2, norm_final, l, 6, final=last)

    y_prompt = xp.reshape(bsz, seq, D_MODEL)
    y_sample = jnp.swapaxes(xs.reshape(steps, nb, D_MODEL), 0, 1)

    def stacked(states, i, shape):
        return jnp.stack([s[i] for s in states]).reshape((DEPTH,) + shape)

    outs = [stacked(st_p, 0, (bsz, POOL_BUF, POOL_WIDTH)),
            stacked(st_p, 1, (bsz, HG_HEADS, HG_DK, HG_DK)),
            stacked(st_p, 2, (bsz, WINDOW, KV_HEADS, HEAD_DIM)),
            stacked(st_p, 3, (bsz, WINDOW, KV_HEADS, HEAD_DIM)),
            jnp.transpose(stacks_s[0], (0, 2, 1, 3)),
            jnp.transpose(stacks_s[1].reshape(DEPTH, HG_HEADS, HG_DK, HG_DK, nb), (0, 4, 1, 2, 3)),
            jnp.transpose(stacks_s[2].reshape(DEPTH, nb, KV_HEADS, HEAD_DIM, WINDOW), (0, 1, 4, 2, 3)),
            jnp.transpose(stacks_s[3].reshape(DEPTH, nb, KV_HEADS, HEAD_DIM, WINDOW), (0, 1, 4, 2, 3))]
    return (y_prompt, y_sample, *outs)
```

```python
import functools
import math

import numpy as np
import jax
import jax.numpy as jnp
from jax import lax
from jax.experimental import pallas as pl
from jax.experimental.pallas import tpu as pltpu

F32 = jnp.float32
BF16 = jnp.bfloat16

D_MODEL = 1024
DEPTH = 4
N_MOD = 9
D_FF = 2816
EPS = 1e-6
POOL_WIDTH = 256
POOL_WINDOWS = (2, 4, 8, 16)
POOL_GW = 64
POOL_BUF = 15
HG_WIDTH = 256
HG_HEADS = 4
HG_DK = 64
ATT_WIDTH = 512
ATT_HEADS = 8
KV_HEADS = 2
GQA_GROUP = 4
HEAD_DIM = 64
WINDOW = 128
REL_BUCKETS = 32
REL_MAX_DIST = 128
OFF_POOL, OFF_HQ, OFF_HF, OFF_HI, OFF_HG, OFF_AQ, OFF_AK, OFF_AV = 0, 256, 512, 768, 1024, 1280, 1792, 1920
IN_WIDTH = 2048
NEG = -1e30

VMEM_LIMIT = 56 * 1024 * 1024


def _dot(a, b):
    return jnp.dot(a, b, preferred_element_type=F32)


def _dot_nt(a, b):
    return lax.dot_general(a, b, (((1,), (1,)), ((), ())), preferred_element_type=F32)


def _dot_tn(a, b):
    return lax.dot_general(a, b, (((0,), (0,)), ((), ())), preferred_element_type=F32)


def _sigmoid(x):
    return 1.0 / (1.0 + jnp.exp(-x))


def _silu(x):
    return x * _sigmoid(x)


def _norm_mod(x, g, shift, scale):
    r = lax.rsqrt(jnp.mean(x * x, axis=-1, keepdims=True) + EPS)
    return (x * r) * g * (1.0 + scale) + shift


def _split3(x):
    hi = x.astype(BF16)
    r1 = x - hi.astype(F32)
    mid = r1.astype(BF16)
    lo = (r1 - mid.astype(F32)).astype(BF16)
    return hi, mid, lo


def _mod_kernel(c_ref, w_ref, b_ref, o_ref):
    sc = _silu(c_ref[...]).astype(BF16)
    o_ref[...] = _dot(sc, w_ref[...].astype(BF16)) + b_ref[...]


def _modulation(c_all, w_mod, b_mod):
    rows = c_all.shape[0]
    return pl.pallas_call(
        _mod_kernel,
        grid=(DEPTH, N_MOD),
        in_specs=[
            pl.BlockSpec((rows, D_MODEL), lambda l, j: (0, 0)),
            pl.BlockSpec((None, D_MODEL, D_MODEL), lambda l, j: (l, 0, j)),
            pl.BlockSpec((None, None, 1, D_MODEL), lambda l, j: (l, j, 0, 0)),
        ],
        out_specs=pl.BlockSpec((None, None, rows, D_MODEL), lambda l, j: (l, j, 0, 0)),
        out_shape=jax.ShapeDtypeStruct((DEPTH, N_MOD, rows, D_MODEL), F32),
        compiler_params=pltpu.CompilerParams(dimension_semantics=("arbitrary", "arbitrary"),
                                             vmem_limit_bytes=VMEM_LIMIT),
        name="adaln_mod",
    )(c_all, w_mod, b_mod.reshape(DEPTH, N_MOD, 1, D_MODEL))


def _mod_rows(mod_ref, j, row, per_row):
    if per_row:
        return mod_ref[j]
    return mod_ref[j, pl.ds(row, 1), :]


FFN_CHUNK = 256
FFN_NCHUNK = D_FF // FFN_CHUNK


def _ffn_kernel(x_ref, mod_ref, n_ref, wg_hbm, wu_hbm, wd_hbm, nf_ref, o_ref,
                wg_ref, wu_ref, wd_ref, sg_ref, su_ref, sd_ref, sem, *, layer, j0, per_row, tiles_per_seq, final):
    i = pl.program_id(0)
    row = i // tiles_per_seq
    shift = _mod_rows(mod_ref, j0, row, per_row)
    scale = _mod_rows(mod_ref, j0 + 1, row, per_row)
    gate = _mod_rows(mod_ref, j0 + 2, row, per_row)
    x = x_ref[...]
    h = _norm_mod(x, n_ref[...], shift, scale).astype(BF16)

    def chunk_copies(c, slot):
        cols = pl.ds(c * FFN_CHUNK, FFN_CHUNK)
        return (pltpu.make_async_copy(wg_hbm.at[layer, :, cols], sg_ref.at[slot], sem.at[0, slot]),
                pltpu.make_async_copy(wu_hbm.at[layer, :, cols], su_ref.at[slot], sem.at[1, slot]),
                pltpu.make_async_copy(wd_hbm.at[layer, cols, :], sd_ref.at[slot], sem.at[2, slot]))

    def finish(acc):
        y = x + (0.5 * gate) * acc
        if final:
            y = (y * lax.rsqrt(jnp.mean(y * y, axis=-1, keepdims=True) + EPS)) * nf_ref[...]
        o_ref[...] = y

    @pl.when(i == 0)
    def _():
        for cp in chunk_copies(0, 0):
            cp.start()
        acc = jnp.zeros(x.shape, F32)
        for c in range(FFN_NCHUNK):
            slot = c % 2
            if c + 1 < FFN_NCHUNK:
                for cp in chunk_copies(c + 1, 1 - slot):
                    cp.start()
            for cp in chunk_copies(c, slot):
                cp.wait()
            cs = slice(c * FFN_CHUNK, (c + 1) * FFN_CHUNK)
            wg = sg_ref[slot].astype(BF16)
            wu = su_ref[slot].astype(BF16)
            wd = sd_ref[slot].astype(BF16)
            wg_ref[:, cs] = wg
            wu_ref[:, cs] = wu
            wd_ref[cs, :] = wd
            a = (_silu(_dot(h, wg)) * _dot(h, wu)).astype(BF16)
            acc = acc + _dot(a, wd)
        finish(acc)

    @pl.when(i != 0)
    def _():
        a = (_silu(_dot(h, wg_ref[...])) * _dot(h, wu_ref[...])).astype(BF16)
        finish(_dot(a, wd_ref[...]))


def _ffn(x, mod, norm, wg, wu, wd, norm_final, layer, j0, *, tm, per_row, mod_rows, mod_row_block, final=False):
    rows = x.shape[0]
    tiles_per_seq = 1 if per_row else (rows // mod_rows) // tm
    kern = functools.partial(_ffn_kernel, layer=layer, j0=j0, per_row=per_row, tiles_per_seq=tiles_per_seq,
                             final=final)
    return pl.pallas_call(
        kern,
        grid=(rows // tm,),
        in_specs=[
            pl.BlockSpec((tm, D_MODEL), lambda i: (i, 0)),
            pl.BlockSpec((None, N_MOD, mod_rows, D_MODEL), lambda i: (layer, 0, mod_row_block, 0)),
            pl.BlockSpec((None, 1, D_MODEL), lambda i: (layer, 0, 0)),
            pl.BlockSpec(memory_space=pl.ANY),
            pl.BlockSpec(memory_space=pl.ANY),
            pl.BlockSpec(memory_space=pl.ANY),
            pl.BlockSpec((1, D_MODEL), lambda i: (0, 0)),
        ],
        out_specs=pl.BlockSpec((tm, D_MODEL), lambda i: (i, 0)),
        out_shape=jax.ShapeDtypeStruct((rows, D_MODEL), F32),
        scratch_shapes=[
            pltpu.VMEM((D_MODEL, D_FF), BF16),
            pltpu.VMEM((D_MODEL, D_FF), BF16),
            pltpu.VMEM((D_FF, D_MODEL), BF16),
            pltpu.VMEM((2, D_MODEL, FFN_CHUNK), F32),
            pltpu.VMEM((2, D_MODEL, FFN_CHUNK), F32),
            pltpu.VMEM((2, FFN_CHUNK, D_MODEL), F32),
            pltpu.SemaphoreType.DMA((3, 2)),
        ],
        compiler_params=pltpu.CompilerParams(dimension_semantics=("arbitrary",), vmem_limit_bytes=VMEM_LIMIT),
        name="swiglu_half_step",
    )(x, mod, norm.reshape(DEPTH, 1, D_MODEL), wg, wu, wd, norm_final.reshape(1, D_MODEL))


def _lower_bound_kernel(x_ref, o_ref):
    rows = [x_ref[l:l + 1, :] for l in range(DEPTH)]
    m = functools.reduce(jnp.maximum, rows)
    e = [jnp.exp(r - m) for r in rows]
    s = functools.reduce(lambda a, b: a + b, e)
    sm = [ei / s for ei in e]
    acc = sm[0]
    for l in range(DEPTH):
        if l > 0:
            acc = acc + sm[l]
        o_ref[l:l + 1, :] = acc - sm[0]


def _lower_bounds(hgrn_lower):
    return pl.pallas_call(
        _lower_bound_kernel,
        out_shape=jax.ShapeDtypeStruct((DEPTH, HG_WIDTH), F32),
        name="hgrn_lower_bounds",
    )(hgrn_lower)


def _bucket_map(rows, cols, valid_rows, col_lo, col_hi):
    r = np.arange(rows)[:, None]
    c = np.arange(cols)[None, :]
    rel = c - WINDOW - r
    n = np.maximum(-rel, 0)
    exact = REL_BUCKETS // 2
    nf = np.maximum(n, 1).astype(np.float32)
    large = exact + (np.log(nf / np.float32(exact)) / np.float32(math.log(REL_MAX_DIST / exact))
                     * np.float32(REL_BUCKETS - exact)).astype(np.int32)
    large = np.minimum(large, REL_BUCKETS - 1)
    bucket = np.where(n < exact, n, large)
    valid = (rel <= 0) & (rel > -WINDOW) & (r < valid_rows) & (c >= col_lo) & (c < col_hi)
    return np.where(valid, bucket, -1).astype(np.int32)


def _bias_table_kernel(bkt_ref, rb_ref, o_ref):
    h = pl.program_id(1)
    bkt = bkt_ref[...]
    acc = jnp.full(bkt.shape, NEG, F32)
    for b in range(REL_BUCKETS):
        acc = jnp.where(bkt == b, rb_ref[b, h], acc)
    o_ref[...] = acc


def _bias_tables(bucket_maps, rel_bias):
    m, r, c = bucket_maps.shape
    return pl.pallas_call(
        _bias_table_kernel,
        grid=(m, ATT_HEADS),
        in_specs=[pl.BlockSpec((None, r, c), lambda i, h: (i, 0, 0)),
                  pl.BlockSpec(memory_space=pltpu.SMEM)],
        out_specs=pl.BlockSpec((None, None, r, c), lambda i, h: (i, h, 0, 0)),
        out_shape=jax.ShapeDtypeStruct((m, ATT_HEADS, r, c), F32),
        compiler_params=pltpu.CompilerParams(dimension_semantics=("arbitrary", "arbitrary")),
        name="rel_bias_tables",
    )(bucket_maps, rel_bias)


def _block_ones(n, blk):
    i = np.arange(n)
    return (i[:, None] // blk == i[None, :] // blk)


def _mixer_consts(tt):
    tri = jnp.asarray(np.tril(np.ones((tt, tt), np.float32)), BF16)
    blk = _block_ones(HG_WIDTH, HG_DK)
    return tri, jnp.asarray(blk, BF16), jnp.asarray(blk, F32)


def _pool_block_diag(pool_w):
    out = jnp.zeros((DEPTH, POOL_WIDTH, POOL_WIDTH), BF16)
    for g in range(len(POOL_WINDOWS)):
        sl = slice(g * POOL_GW, (g + 1) * POOL_GW)
        out = out.at[:, sl, sl].set(pool_w[:, g].astype(BF16))
    return out


def _head_masks():
    lane = lax.broadcasted_iota(jnp.int32, (1, HG_WIDTH), 1)
    return [jnp.where((lane >= HG_DK * h) & (lane < HG_DK * (h + 1)), 1.0, 0.0).astype(F32) for h in range(HG_HEADS)]


def _prompt_mixer_kernel(x_ref, xn_ref, mod_ref, n_ref, win_ref, wout_ref, pbd_ref, pscale_ref, lb_ref, hgn_ref,
                         bm_ref, sink_ref, tri_ref, ones_ref, bdm_ref,
                         xo_ref, pool_o, s_o, k_o, v_o,
                         za_ref, zb_ref, ubuf, kbuf, vbuf, st_ref, cat_ref, *, layer, tt, nt, ntiles):
    b = pl.program_id(0)
    p = pl.program_id(1)
    last_pair = pl.num_programs(1) - 1
    nxt = jnp.minimum(b * nt + 2 * p + 2, ntiles - 1)
    bn = nxt // nt
    n_chunks = 8
    cw = IN_WIDTH // n_chunks

    def projector(x_rows, bsel, dst_ref):
        h = _norm_mod(x_rows, n_ref[...], mod_ref[3, pl.ds(bsel, 1), :], mod_ref[4, pl.ds(bsel, 1), :]).astype(BF16)
        state = {"k": 0}

        def step():
            k = state["k"]
            if k < n_chunks:
                dst_ref[:, k * cw:(k + 1) * cw] = _dot(h, win_ref[:, k * cw:(k + 1) * cw])
                state["k"] = k + 1

        return step

    @pl.when((b == 0) & (p == 0))
    def _():
        first_proj = projector(x_ref[0:tt, :], b, za_ref)
        for _ in range(n_chunks):
            first_proj()

    @pl.when(p == 0)
    def _():
        ubuf[0:16, :] = jnp.zeros((16, POOL_WIDTH), F32)
        kbuf[:, 0:WINDOW, :] = jnp.zeros((KV_HEADS, WINDOW, HEAD_DIM), BF16)
        vbuf[:, 0:WINDOW, :] = jnp.zeros((KV_HEADS, WINDOW, HEAD_DIM), BF16)
        st_ref[...] = jnp.zeros((HG_WIDTH, HG_WIDTH), F32)

    gate = mod_ref[5, pl.ds(b, 1), :]

    def mix(z_ref, x, row0, t, side):
        lane = lax.broadcasted_iota(jnp.int32, (1, POOL_WIDTH), 1)

        u = z_ref[:, OFF_POOL:OFF_POOL + POOL_WIDTH]
        ubuf[16:16 + tt, :] = u
        e = ubuf[...]
        s2 = e + pltpu.roll(e, 1, 0)
        s4 = s2 + pltpu.roll(s2, 2, 0)
        s8 = s4 + pltpu.roll(s4, 4, 0)
        s16 = s8 + pltpu.roll(s8, 8, 0)
        sel = jnp.where(lane < 64, s2, jnp.where(lane < 128, s4, jnp.where(lane < 192, s8, s16)))[16:]
        wl = jnp.where(lane < 64, 2.0, jnp.where(lane < 128, 4.0, jnp.where(lane < 192, 8.0, 16.0))).astype(F32)
        pos1 = (t * tt + 1 + lax.broadcasted_iota(jnp.int32, (tt, 1), 0)).astype(F32)
        dpool = sel / jnp.minimum(pos1, wl) - u
        ypool = _dot(dpool.astype(BF16), pbd_ref[...]) * pscale_ref[...]
        cat_ref[:, 0:POOL_WIDTH] = ypool.astype(BF16)
        ubuf[0:16, :] = ubuf[tt:tt + 16, :]
        side()


        for g in range(KV_HEADS):
            kbuf[g, WINDOW:WINDOW + tt, :] = z_ref[:, OFF_AK + HEAD_DIM * g:OFF_AK + HEAD_DIM * (g + 1)].astype(BF16)
            vbuf[g, WINDOW:WINDOW + tt, :] = z_ref[:, OFF_AV + HEAD_DIM * g:OFF_AV + HEAD_DIM * (g + 1)].astype(BF16)
        first = jnp.where(t == 0, 1, 0)
        bands = [(j, g) for j in range(tt // WINDOW) for g in range(KV_HEADS)]
        logits = []
        for j, g in bands:
            r0 = j * WINDOW
            c0 = OFF_AQ + HEAD_DIM * g * GQA_GROUP
            qg = jnp.concatenate([z_ref[r0:r0 + WINDOW, c0 + HEAD_DIM * i:c0 + HEAD_DIM * (i + 1)]
                                  for i in range(GQA_GROUP)], axis=0)
            qg = (qg * (HEAD_DIM ** -0.5)).astype(BF16)
            logits.append(_dot_nt(qg, kbuf[g, r0:r0 + 2 * WINDOW, :]))
        side()

        lb = lb_ref[...]
        q = _silu(z_ref[:, OFF_HQ:OFF_HQ + HG_WIDTH])
        f = lb + (1.0 - lb) * _sigmoid(z_ref[:, OFF_HF:OFF_HF + HG_WIDTH])
        k = 1.0 - f
        v = z_ref[:, OFF_HI:OFF_HI + HG_WIDTH]
        vb = v.astype(BF16)
        tri = tri_ref[...]
        hi, mid, lo = _split3(jnp.log(f))
        bcum = _dot(tri, hi) + _dot(tri, mid) + _dot(tri, lo)
        side()
        ones_blk = ones_ref[...]
        hm = _head_masks()

        n16 = tt // 16
        half_shape = (n16, 2, 8, HG_WIDTH)
        q4 = q.reshape(half_shape)
        b4 = bcum.reshape(half_shape)
        c4 = (bcum - jnp.log(k)).reshape(half_shape)
        v4 = v.reshape(half_shape)
        rowi = lax.broadcasted_iota(jnp.int32, (1, 8, 1), 1)
        o_up = jnp.zeros((n16, 8, HG_WIDTH), F32)
        o_dn = jnp.zeros((n16, 8, HG_WIDTH), F32)
        nh = n16 * 8
        for s in range(8):
            cs = c4[:, 0, s:s + 1, :]
            vs = v4[:, 0, s:s + 1, :]
            x_up = jnp.where(rowi >= s, q4[:, 0] * jnp.exp(b4[:, 0] - cs), 0.0)
            x_dn = q4[:, 1] * jnp.exp(b4[:, 1] - cs)
            xs = jnp.concatenate([x_up.reshape(nh, HG_WIDTH), x_dn.reshape(nh, HG_WIDTH)], axis=0)
            r = _dot(xs.astype(BF16), ones_blk)
            o_up = o_up + r[0:nh].reshape(n16, 8, HG_WIDTH) * vs
            o_dn = o_dn + r[nh:2 * nh].reshape(n16, 8, HG_WIDTH) * vs
            if s % 4 == 3:
                side()
        for s in range(8):
            cs = c4[:, 1, s:s + 1, :]
            x_dn = jnp.where(rowi >= s, q4[:, 1] * jnp.exp(b4[:, 1] - cs), 0.0)
            r = _dot(x_dn.reshape(nh, HG_WIDTH).astype(BF16), ones_blk)
            o_dn = o_dn + r.reshape(n16, 8, HG_WIDTH) * v4[:, 1, s:s + 1, :]
            if s % 4 == 3:
                side()
        o_tot = jnp.concatenate([o_up[:, None], o_dn[:, None]], axis=1).reshape(tt, HG_WIDTH)

        blocks = []
        blk = 32
        while blk <= tt:
            blocks += [(i * blk, i * blk + blk // 2, (i + 1) * blk) for i in range(tt // blk)]
            blk *= 2
        scores = []
        for lo_s, mid_s, hi_s in blocks:
            ref = bcum[mid_s - 1:mid_s, :]
            qt = q[mid_s:hi_s] * jnp.exp(bcum[mid_s:hi_s] - ref)
            kt = (k[lo_s:mid_s] * jnp.exp(ref - bcum[lo_s:mid_s])).astype(BF16)
            qs = jnp.concatenate([qt * hm[hh] for hh in range(HG_HEADS)], axis=0).astype(BF16)
            scores.append(_dot_nt(qs, kt).astype(BF16))

        st = st_ref[...]
        qin = (q * jnp.exp(bcum)).astype(BF16)
        o_tot = o_tot + _dot_nt(qin, st.astype(BF16))
        blast = bcum[tt - 1:tt, :]
        kout = (k * jnp.exp(blast - bcum)).astype(BF16)
        st_ref[...] = st * jnp.exp(blast) + _dot_tn(vb, kout) * bdm_ref[...]
        side()

        probs, dens = [], []
        for (j, g), lg_all in zip(bands, logits):
            h0 = g * GQA_GROUP
            bm = bm_ref[first, h0:h0 + GQA_GROUP] if j == 0 else bm_ref[0, h0:h0 + GQA_GROUP]
            ps = []
            for i in range(GQA_GROUP):
                lg = lg_all[i * WINDOW:(i + 1) * WINDOW] + bm[i]
                sink = sink_ref[layer, h0 + i]
                m = jnp.maximum(jnp.max(lg, axis=-1, keepdims=True), sink)
                p_ = jnp.exp(lg - m)
                dens.append(jnp.sum(p_, axis=-1, keepdims=True) + jnp.exp(sink - m))
                ps.append(p_.astype(BF16))
            probs.append(jnp.concatenate(ps, axis=0))
        for n, (j, g) in enumerate(bands):
            r0 = j * WINDOW
            og = _dot(probs[n], vbuf[g, r0:r0 + 2 * WINDOW, :])
            for i in range(GQA_GROUP):
                cc = POOL_WIDTH + HG_WIDTH + HEAD_DIM * (g * GQA_GROUP + i)
                oh = og[i * WINDOW:(i + 1) * WINDOW] / dens[n * GQA_GROUP + i]
                cat_ref[r0:r0 + WINDOW, cc:cc + HEAD_DIM] = oh.astype(BF16)
        for g in range(KV_HEADS):
            kbuf[g, 0:WINDOW, :] = kbuf[g, tt:tt + WINDOW, :]
            vbuf[g, 0:WINDOW, :] = vbuf[g, tt:tt + WINDOW, :]
        side()

        adds = {}
        for (lo_s, mid_s, hi_s), a in zip(blocks, scores):
            half = mid_s - lo_s
            ov = _dot(a, vb[lo_s:mid_s])
            oi = ov[0:half] * hm[0]
            for hh in range(1, HG_HEADS):
                oi = oi + ov[hh * half:(hh + 1) * half] * hm[hh]
            adds.setdefault(2 * half, []).extend([jnp.zeros((half, HG_WIDTH), F32), oi])
        for parts in adds.values():
            o_tot = o_tot + jnp.concatenate(parts, axis=0)

        sq_hi, sq_mid, _ = _split3(o_tot * o_tot)
        ms = (_dot(sq_hi, ones_blk) + _dot(sq_mid, ones_blk)) * (1.0 / HG_DK)
        y_hg = o_tot * lax.rsqrt(ms + EPS) * hgn_ref[...] * _silu(z_ref[:, OFF_HG:OFF_HG + HG_WIDTH])
        cat_ref[:, POOL_WIDTH:POOL_WIDTH + HG_WIDTH] = y_hg.astype(BF16)

        xo_ref[row0:row0 + tt, :] = x + gate * _dot(cat_ref[...], wout_ref[...])

        for _ in range(n_chunks):
            side()

    mix(za_ref, x_ref[0:tt, :], 0, 2 * p, projector(x_ref[tt:2 * tt, :], b, zb_ref))
    mix(zb_ref, x_ref[tt:2 * tt, :], tt, 2 * p + 1, projector(xn_ref[...], bn, za_ref))

    @pl.when(p == last_pair)
    def _():
        pool_o[...] = ubuf[pl.ds(1, POOL_BUF), :]
        s_kv = st_ref[...].T
        for hh in range(HG_HEADS):
            s_o[hh] = s_kv[HG_DK * hh:HG_DK * (hh + 1), HG_DK * hh:HG_DK * (hh + 1)]
        k_o[...] = zb_ref[tt - WINDOW:tt, OFF_AK:OFF_AV]
        v_o[...] = zb_ref[tt - WINDOW:tt, OFF_AV:IN_WIDTH]


def _prompt_mixer(x, mod, norm, w_in, w_out, pool_bd, pool_scale, lbs, hg_norm, bm, sinks, consts, layer, *,
                  tt, mod_row_block):
    bsz, seq, _ = x.shape
    tri, ones_blk, bdm = consts
    nt = seq // tt
    ntiles = bsz * nt
    kern = functools.partial(_prompt_mixer_kernel, layer=layer, tt=tt, nt=nt, ntiles=ntiles)
    full = lambda *shape: pl.BlockSpec(shape, lambda b, p: (0,) * len(shape))
    lsel = lambda *shape: pl.BlockSpec((None,) + shape, lambda b, p: (layer,) + (0,) * len(shape))
    mix_pair = lambda b, p: (b, p, 0)
    mix_seq = lambda *z: (lambda b, p: (b,) + z)

    def next_tile(b, p):
        nxt = jnp.minimum(b * nt + 2 * p + 2, ntiles - 1)
        return (nxt // nt, nxt % nt, 0)

    return pl.pallas_call(
        kern,
        grid=(bsz, nt // 2),
        in_specs=[
            pl.BlockSpec((None, 2 * tt, D_MODEL), mix_pair),
            pl.BlockSpec((None, tt, D_MODEL), next_tile),
            pl.BlockSpec((None, N_MOD, 8, D_MODEL), lambda b, p: (layer, 0, mod_row_block, 0)),
            lsel(1, D_MODEL),
            lsel(D_MODEL, IN_WIDTH),
            lsel(D_MODEL, D_MODEL),
            lsel(POOL_WIDTH, POOL_WIDTH),
            lsel(1, POOL_WIDTH),
            lsel(1, HG_WIDTH),
            lsel(1, HG_WIDTH),
            full(2, ATT_HEADS, WINDOW, 2 * WINDOW),
            pl.BlockSpec(memory_space=pltpu.SMEM),
            full(tt, tt),
            full(HG_WIDTH, HG_WIDTH),
            full(HG_WIDTH, HG_WIDTH),
        ],
        out_specs=[
            pl.BlockSpec((None, 2 * tt, D_MODEL), mix_pair),
            pl.BlockSpec((None, POOL_BUF, POOL_WIDTH), mix_seq(0, 0)),
            pl.BlockSpec((None, HG_HEADS, HG_DK, HG_DK), mix_seq(0, 0, 0)),
            pl.BlockSpec((None, WINDOW, KV_HEADS * HEAD_DIM), mix_seq(0, 0)),
            pl.BlockSpec((None, WINDOW, KV_HEADS * HEAD_DIM), mix_seq(0, 0)),
        ],
        out_shape=[
            jax.ShapeDtypeStruct((bsz, seq, D_MODEL), F32),
            jax.ShapeDtypeStruct((bsz, POOL_BUF, POOL_WIDTH), F32),
            jax.ShapeDtypeStruct((bsz, HG_HEADS, HG_DK, HG_DK), F32),
            jax.ShapeDtypeStruct((bsz, WINDOW, KV_HEADS * HEAD_DIM), F32),
            jax.ShapeDtypeStruct((bsz, WINDOW, KV_HEADS * HEAD_DIM), F32),
        ],
        scratch_shapes=[
            pltpu.VMEM((tt, IN_WIDTH), F32),
            pltpu.VMEM((tt, IN_WIDTH), F32),
            pltpu.VMEM((16 + tt, POOL_WIDTH), F32),
            pltpu.VMEM((KV_HEADS, WINDOW + tt, HEAD_DIM), BF16),
            pltpu.VMEM((KV_HEADS, WINDOW + tt, HEAD_DIM), BF16),
            pltpu.VMEM((HG_WIDTH, HG_WIDTH), F32),
            pltpu.VMEM((tt, D_MODEL), BF16),
        ],
        compiler_params=pltpu.CompilerParams(dimension_semantics=("arbitrary", "arbitrary"),
                                             vmem_limit_bytes=VMEM_LIMIT),
        name="prompt_mixer",
    )(x, x, mod, norm.reshape(DEPTH, 1, D_MODEL), w_in, w_out, pool_bd, pool_scale.reshape(DEPTH, 1, POOL_WIDTH),
      lbs.reshape(DEPTH, 1, HG_WIDTH), hg_norm, bm, sinks, tri, ones_blk, bdm)


def _sample_mixer_kernel(x_ref, mod_ref, n_ref, win_ref, wout_ref, pbd_ref, pscale_ref, lb_ref, hgn_ref,
                         bms_ref, sinkc_ref, ones_ref, sp_ref, s0_ref, kt_ref, vt_ref,
                         xo_ref, pool_o, s_o, kt_o, vt_o,
                         z_ref, cat_ref, qs_ref, os_ref, kn_ref, vn_ref, qint_ref, koutt_ref, vtt_ref, gt_ref,
                         oin_ref, oacc_ref, *, steps, nb, gb, rb, pos0):
    g = pl.program_id(0)
    last = pl.num_programs(0) - 1
    base = pl.multiple_of(g * gb, gb)
    kvw = KV_HEADS * HEAD_DIM

    def rows(t):
        return slice(t * nb, (t + 1) * nb)

    @pl.when(g == 0)
    def _():
        for t in range(steps):
            h = _norm_mod(x_ref[t], n_ref[...], mod_ref[0], mod_ref[1]).astype(BF16)
            z_ref[rows(t), :] = _dot(h, win_ref[...])
        lane = lax.broadcasted_iota(jnp.int32, (1, POOL_WIDTH), 1)

        ext = [sp_ref[r] for r in range(POOL_BUF)] + [z_ref[rows(t), OFF_POOL:OFF_POOL + POOL_WIDTH] for t in range(steps)]
        for r in range(POOL_BUF):
            pool_o[r] = ext[r + steps]
        sums = {1: ext}
        w = 1
        while w < 16:
            prev = sums[w]
            sums[2 * w] = [None if (r < 2 * w - 1) else prev[r] + prev[r - w] for r in range(len(ext))]
            w *= 2
        dl = []
        for t in range(steps):
            r = POOL_BUF + t
            sel = jnp.where(lane < 64, sums[2][r], jnp.where(lane < 128, sums[4][r], jnp.where(lane < 192, sums[8][r], sums[16][r])))
            cnt = [float(min(pos0 + t + 1, wd)) for wd in POOL_WINDOWS]
            cl = jnp.where(lane < 64, cnt[0], jnp.where(lane < 128, cnt[1], jnp.where(lane < 192, cnt[2], cnt[3]))).astype(F32)
            dl.append(sel / cl - ext[r])
        ypool = _dot(jnp.concatenate(dl, axis=0).astype(BF16), pbd_ref[...]) * pscale_ref[...]
        cat_ref[:, 0:POOL_WIDTH] = ypool

        lb = lb_ref[...]
        qv, kv_, vv, bv = [], [], [], []
        bacc = None
        for t in range(steps):
            qv.append(_silu(z_ref[rows(t), OFF_HQ:OFF_HQ + HG_WIDTH]))
            f = lb + (1.0 - lb) * _sigmoid(z_ref[rows(t), OFF_HF:OFF_HF + HG_WIDTH])
            kv_.append(1.0 - f)
            vv.append(z_ref[rows(t), OFF_HI:OFF_HI + HG_WIDTH])
            bacc = jnp.log(f) if bacc is None else bacc + jnp.log(f)
            bv.append(bacc)
        pairs = [(t, s) for t in range(steps) for s in range(t + 1)]
        xs = jnp.concatenate([qv[t] * kv_[s] * jnp.exp(bv[t] - bv[s]) for t, s in pairs], axis=0).astype(BF16)
        rr = _dot(xs, ones_ref[...])
        o_intra = [None] * steps
        for i, (t, s) in enumerate(pairs):
            term = rr[i * nb:(i + 1) * nb] * vv[s]
            o_intra[t] = term if o_intra[t] is None else o_intra[t] + term
        blast = bv[steps - 1]
        gt_ref[...] = jnp.exp(blast).T
        for t in range(steps):
            oin_ref[t] = o_intra[t]
            qint_ref[t] = (qv[t] * jnp.exp(bv[t])).T
            koutt_ref[t] = (kv_[t] * jnp.exp(blast - bv[t])).T
            vtt_ref[t] = vv[t].T
        oacc_ref[...] = jnp.zeros(oacc_ref.shape, F32)

        zeros64 = jnp.zeros((nb, HEAD_DIM), F32)
        kn_ref[...] = jnp.zeros(kn_ref.shape, F32)
        vn_ref[...] = jnp.zeros(vn_ref.shape, F32)
        for t in range(steps):
            kn_ref[:, t, :] = z_ref[rows(t), OFF_AK:OFF_AK + kvw]
            vn_ref[:, t, :] = z_ref[rows(t), OFF_AV:OFF_AV + kvw]
            for hd in range(ATT_HEADS):
                c0 = OFF_AQ + HEAD_DIM * hd
                qsl = z_ref[rows(t), c0:c0 + HEAD_DIM] * (HEAD_DIM ** -0.5)
                parts = [qsl, zeros64] if hd < GQA_GROUP else [zeros64, qsl]
                qs_ref[:, ATT_HEADS * t + hd, :] = jnp.concatenate(parts, axis=1)

    r0 = pl.multiple_of(g * rb, rb)
    hrow = pl.multiple_of((g * rb // HG_DK) * HG_DK, HG_DK)
    vts = [vtt_ref[t, pl.ds(hrow, HG_DK), :] for t in range(steps)]
    accs = [oacc_ref[t, pl.ds(hrow, HG_DK), :] for t in range(steps)]
    for i in range(rb):
        s0 = s0_ref[i]
        snew = gt_ref[pl.ds(r0 + i, 1), :] * s0
        for t in range(steps):
            accs[t] = accs[t] + qint_ref[t, pl.ds(r0 + i, 1), :] * s0
            snew = snew + koutt_ref[t, pl.ds(r0 + i, 1), :] * vts[t]
        s_o[i] = snew
    for t in range(steps):
        oacc_ref[t, pl.ds(hrow, HG_DK), :] = accs[t]

    bm_old = bms_ref[:, 0:WINDOW]
    bm_new = bms_ref[:, WINDOW:WINDOW + 8]
    sinkc = sinkc_ref[...]
    seqs = range(gb)
    qn = [qs_ref[base + n].astype(BF16) for n in seqs]
    l_old = [_dot(qn[n], kt_ref[n].astype(BF16)) for n in seqs]
    l_new = [_dot_nt(qn[n], kn_ref[base + n].astype(BF16)) for n in seqs]
    def new_columns(ref):
        new = ref[pl.ds(base, gb)].reshape(gb * 8, kvw)
        if gb * 8 < WINDOW:
            new = jnp.concatenate([new, jnp.zeros((WINDOW - gb * 8, kvw), F32)], axis=0)
        return new.T

    knt = new_columns(kn_ref)
    vnt = new_columns(vn_ref)
    for n in seqs:
        kt_o[n] = jnp.concatenate([kt_ref[n][:, steps:], knt[:, 8 * n:8 * n + steps]], axis=1)
        vt_o[n] = jnp.concatenate([vt_ref[n][:, steps:], vnt[:, 8 * n:8 * n + steps]], axis=1)
    p_old, p_new, den = [], [], []
    for n in seqs:
        lo_, ln_ = l_old[n] + bm_old, l_new[n] + bm_new
        m = jnp.maximum(jnp.maximum(jnp.max(lo_, axis=-1, keepdims=True), jnp.max(ln_, axis=-1, keepdims=True)), sinkc)
        po, pn = jnp.exp(lo_ - m), jnp.exp(ln_ - m)
        den.append(jnp.sum(po, axis=-1, keepdims=True) + jnp.sum(pn, axis=-1, keepdims=True) + jnp.exp(sinkc - m))
        p_old.append(po.astype(BF16))
        p_new.append(pn.astype(BF16))
    for n in seqs:
        o = _dot_nt(p_old[n], vt_ref[n].astype(BF16)) + _dot(p_new[n], vn_ref[base + n].astype(BF16))
        os_ref[n] = o / den[n]
    for t in range(steps):
        for hd in range(ATT_HEADS):
            c0 = HEAD_DIM * (hd // GQA_GROUP)
            cc = POOL_WIDTH + HG_WIDTH + HEAD_DIM * hd
            cat_ref[pl.ds(t * nb + base, gb), cc:cc + HEAD_DIM] = os_ref[:, ATT_HEADS * t + hd, c0:c0 + HEAD_DIM]

    @pl.when(g == last)
    def _():
        ones_blk = ones_ref[...]
        for t in range(steps):
            o = oin_ref[t] + oacc_ref[t].T
            sq_hi, sq_mid, _ = _split3(o * o)
            ms = (_dot(sq_hi, ones_blk) + _dot(sq_mid, ones_blk)) * (1.0 / HG_DK)
            gate_t = _silu(z_ref[rows(t), OFF_HG:OFF_HG + HG_WIDTH])
            cat_ref[rows(t), POOL_WIDTH:POOL_WIDTH + HG_WIDTH] = o * lax.rsqrt(ms + EPS) * hgn_ref[...] * gate_t
        for t in range(steps):
            y = _dot(cat_ref[rows(t), :].astype(BF16), wout_ref[...])
            xo_ref[t] = x_ref[t] + mod_ref[2] * y


N_SAMPLE_MIXER_INPUTS = 16


def _sample_mixer_kernel_into(*refs, n_stacks, **kw):
    _sample_mixer_kernel(*refs[:N_SAMPLE_MIXER_INPUTS], *refs[N_SAMPLE_MIXER_INPUTS + n_stacks:], **kw)


def _sample_mixer(x, mod, norm, w_in, w_out, pool_bd, pool_scale, lbs, hg_norm, bms, sink_col, ones_blk,
                  pool_hist, hgrn_state, cache_kt, cache_vt, stacks, layer, *, gb, pos0):
    steps, nb, _ = x.shape
    rows_th = steps * ATT_HEADS
    ngrp = nb // gb
    rb = HG_WIDTH // ngrp
    kvw = KV_HEADS * HEAD_DIM
    stacks = () if stacks is None else tuple(stacks)
    kern = functools.partial(_sample_mixer_kernel_into, n_stacks=len(stacks), steps=steps, nb=nb, gb=gb, rb=rb,
                             pos0=pos0)
    full = lambda *shape: pl.BlockSpec(shape, lambda g: (0,) * len(shape))
    lsel = lambda *shape: pl.BlockSpec((None,) + shape, lambda g: (layer,) + (0,) * len(shape))
    grp = lambda *shape: pl.BlockSpec((None,) + shape, lambda g: (layer, g) + (0,) * (len(shape) - 1))
    return pl.pallas_call(
        kern,
        grid=(ngrp,),
        in_specs=[
            full(steps, nb, D_MODEL),
            pl.BlockSpec((None, 3, nb, D_MODEL), lambda g: (layer, 1, 0, 0)),
            lsel(1, D_MODEL),
            lsel(D_MODEL, IN_WIDTH),
            lsel(D_MODEL, D_MODEL),
            lsel(POOL_WIDTH, POOL_WIDTH),
            lsel(1, POOL_WIDTH),
            lsel(1, HG_WIDTH),
            lsel(1, HG_WIDTH),
            full(rows_th, 2 * WINDOW),
            lsel(rows_th, 1),
            full(HG_WIDTH, HG_WIDTH),
            lsel(POOL_BUF, nb, POOL_WIDTH),
            grp(rb, HG_DK, nb),
            grp(gb, kvw, WINDOW),
            grp(gb, kvw, WINDOW),
        ] + [pl.BlockSpec(memory_space=pl.ANY)] * len(stacks),
        out_specs=[
            full(steps, nb, D_MODEL),
            lsel(POOL_BUF, nb, POOL_WIDTH),
            grp(rb, HG_DK, nb),
            grp(gb, kvw, WINDOW),
            grp(gb, kvw, WINDOW),
        ],
        out_shape=[
            jax.ShapeDtypeStruct((steps, nb, D_MODEL), F32),
            jax.ShapeDtypeStruct((DEPTH, POOL_BUF, nb, POOL_WIDTH), F32),
            jax.ShapeDtypeStruct((DEPTH, HG_WIDTH, HG_DK, nb), F32),
            jax.ShapeDtypeStruct((DEPTH, nb, kvw, WINDOW), F32),
            jax.ShapeDtypeStruct((DEPTH, nb, kvw, WINDOW), F32),
        ],
        input_output_aliases={N_SAMPLE_MIXER_INPUTS + i: 1 + i for i in range(len(stacks))},
        scratch_shapes=[
            pltpu.VMEM((steps * nb, IN_WIDTH), F32),
            pltpu.VMEM((steps * nb, D_MODEL), F32),
            pltpu.VMEM((nb, rows_th, kvw), F32),
            pltpu.VMEM((gb, rows_th, kvw), F32),
            pltpu.VMEM((nb, 8, kvw), F32),
            pltpu.VMEM((nb, 8, kvw), F32),
            pltpu.VMEM((steps, HG_WIDTH, nb), F32),
            pltpu.VMEM((steps, HG_WIDTH, nb), F32),
            pltpu.VMEM((steps, HG_WIDTH, nb), F32),
            pltpu.VMEM((HG_WIDTH, nb), F32),
            pltpu.VMEM((steps, nb, HG_WIDTH), F32),
            pltpu.VMEM((steps, HG_WIDTH, nb), F32),
        ],
        compiler_params=pltpu.CompilerParams(dimension_semantics=("arbitrary",), vmem_limit_bytes=VMEM_LIMIT),
        name="sample_mixer",
    )(x, mod, norm.reshape(DEPTH, 1, D_MODEL), w_in, w_out, pool_bd, pool_scale.reshape(DEPTH, 1, POOL_WIDTH),
      lbs.reshape(DEPTH, 1, HG_WIDTH), hg_norm, bms, sink_col, ones_blk, pool_hist, hgrn_state, cache_kt, cache_vt,
      *stacks)


FFN_TILE = 512
MIXER_TILE = 256
SAMPLE_GROUP = 8
PAST_LEN = 8192


def kernel(x_prompt, x_sample, c_prompt, c_sample, state_pool, state_hgrn, cache_k_win, cache_v_win, norm_ffn1, norm_mix, norm_ffn2, w_mod, b_mod, ffn1_w_gate, ffn1_w_up, ffn1_w_down, w_in, w_out, pool_w, pool_scale, hgrn_lower, hgrn_norm, attn_sinks, rel_bias, ffn2_w_gate, ffn2_w_up, ffn2_w_down, norm_final):
    bsz, seq, _ = x_prompt.shape
    nb, steps, _ = x_sample.shape
    kvw = KV_HEADS * HEAD_DIM

    mod = _modulation(jnp.concatenate([c_sample, c_prompt], axis=0), w_mod, b_mod)
    prompt_mod_block = nb // 8
    lbs = _lower_bounds(hgrn_lower)
    maps_p = np.stack([_bucket_map(WINDOW, 2 * WINDOW, WINDOW, 0, 2 * WINDOW),
                       _bucket_map(WINDOW, 2 * WINDOW, WINDOW, WINDOW, 2 * WINDOW)])
    bm_p = _bias_tables(jnp.asarray(maps_p), rel_bias)
    maps_s = _bucket_map(8, 2 * WINDOW, steps, 0, WINDOW + steps)[None]
    bm_s = _bias_tables(jnp.asarray(maps_s), rel_bias)[0]
    bm_s = jnp.swapaxes(bm_s, 0, 1).reshape(8 * ATT_HEADS, 2 * WINDOW)[:steps * ATT_HEADS]
    sink_col = jnp.tile(attn_sinks, (1, steps)).reshape(DEPTH, steps * ATT_HEADS, 1)
    consts = _mixer_consts(MIXER_TILE)
    hgn = jnp.tile(hgrn_norm, (1, HG_HEADS)).reshape(DEPTH, 1, HG_WIDTH)
    pool_bd = _pool_block_diag(pool_w)
    w_in_b, w_out_b = w_in.astype(BF16), w_out.astype(BF16)
    f1 = (ffn1_w_gate, ffn1_w_up, ffn1_w_down)
    f2 = (ffn2_w_gate, ffn2_w_up, ffn2_w_down)
    s_pool = jnp.transpose(state_pool, (0, 2, 1, 3))
    s_hgrn = jnp.transpose(state_hgrn, (0, 2, 3, 4, 1)).reshape(DEPTH, HG_WIDTH, HG_DK, nb)
    c_k = jnp.transpose(cache_k_win, (0, 1, 3, 4, 2)).reshape(DEPTH, nb, kvw, WINDOW)
    c_v = jnp.transpose(cache_v_win, (0, 1, 3, 4, 2)).reshape(DEPTH, nb, kvw, WINDOW)

    xp = x_prompt.reshape(bsz * seq, D_MODEL)
    xs = jnp.swapaxes(x_sample, 0, 1).reshape(steps * nb, D_MODEL)
    ffn_p = functools.partial(_ffn, tm=FFN_TILE, per_row=False, mod_rows=8, mod_row_block=prompt_mod_block)
    ffn_s = functools.partial(_ffn, tm=nb, per_row=True, mod_rows=nb, mod_row_block=0)
    st_p, stacks_s = [], None
    for l in range(DEPTH):
        xp = ffn_p(xp, mod, norm_ffn1, *f1, norm_final, l, 0)
        xs = ffn_s(xs, mod, norm_ffn1, *f1, norm_final, l, 0)
        xp, *sp = _prompt_mixer(xp.reshape(bsz, seq, D_MODEL), mod, norm_mix, w_in_b, w_out_b, pool_bd, pool_scale,
                                lbs, hgn, bm_p, attn_sinks, consts, l, tt=MIXER_TILE, mod_row_block=prompt_mod_block)
        xs, *stacks_s = _sample_mixer(xs.reshape(steps, nb, D_MODEL), mod, norm_mix, w_in_b, w_out_b, pool_bd,
                                      pool_scale, lbs, hgn, bm_s, sink_col, consts[1], s_pool, s_hgrn, c_k, c_v,
                                      stacks_s, l, gb=SAMPLE_GROUP, pos0=PAST_LEN)
        st_p.append(sp)
        last = l == DEPTH - 1
        xp = ffn_p(xp.reshape(bsz * seq, D_MODEL), mod, norm_ffn2, *f2, norm_final, l, 6, final=last)
        xs = ffn_s(xs.reshape(steps * nb, D_MODEL), mod, norm_ffn2, *f2, norm_final, l, 6, final=last)

    y_prompt = xp.reshape(bsz, seq, D_MODEL)
    y_sample = jnp.swapaxes(xs.reshape(steps, nb, D_MODEL), 0, 1)

    def stacked(states, i, shape):
        return jnp.stack([s[i] for s in states]).reshape((DEPTH,) + shape)

    outs = [stacked(st_p, 0, (bsz, POOL_BUF, POOL_WIDTH)),
            stacked(st_p, 1, (bsz, HG_HEADS, HG_DK, HG_DK)),
            stacked(st_p, 2, (bsz, WINDOW, KV_HEADS, HEAD_DIM)),
            stacked(st_p, 3, (bsz, WINDOW, KV_HEADS, HEAD_DIM)),
            jnp.transpose(stacks_s[0], (0, 2, 1, 3)),
            jnp.transpose(stacks_s[1].reshape(DEPTH, HG_HEADS, HG_DK, HG_DK, nb), (0, 4, 1, 2, 3)),
            jnp.transpose(stacks_s[2].reshape(DEPTH, nb, KV_HEADS, HEAD_DIM, WINDOW), (0, 1, 4, 2, 3)),
            jnp.transpose(stacks_s[3].reshape(DEPTH, nb, KV_HEADS, HEAD_DIM, WINDOW), (0, 1, 4, 2, 3))]
    return (y_prompt, y_sample, *outs)
```

```python
import functools
import math

import numpy as np
import jax
import jax.numpy as jnp
from jax import lax
from jax.experimental import pallas as pl
from jax.experimental.pallas import tpu as pltpu

F32 = jnp.float32
BF16 = jnp.bfloat16

D_MODEL = 1024
DEPTH = 4
N_MOD = 9
D_FF = 2816
EPS = 1e-6
POOL_WIDTH = 256
POOL_WINDOWS = (2, 4, 8, 16)
POOL_GW = 64
POOL_BUF = 15
HG_WIDTH = 256
HG_HEADS = 4
HG_DK = 64
ATT_WIDTH = 512
ATT_HEADS = 8
KV_HEADS = 2
GQA_GROUP = 4
HEAD_DIM = 64
WINDOW = 128
REL_BUCKETS = 32
REL_MAX_DIST = 128
OFF_POOL, OFF_HQ, OFF_HF, OFF_HI, OFF_HG, OFF_AQ, OFF_AK, OFF_AV = 0, 256, 512, 768, 1024, 1280, 1792, 1920
IN_WIDTH = 2048
NEG = -1e30

VMEM_LIMIT = 56 * 1024 * 1024


def _dot(a, b):
    return jnp.dot(a, b, preferred_element_type=F32)


def _dot_nt(a, b):
    return lax.dot_general(a, b, (((1,), (1,)), ((), ())), preferred_element_type=F32)


def _dot_tn(a, b):
    return lax.dot_general(a, b, (((0,), (0,)), ((), ())), preferred_element_type=F32)


def _sigmoid(x):
    return 1.0 / (1.0 + jnp.exp(-x))


def _silu(x):
    return x * _sigmoid(x)


def _norm_mod(x, g, shift, scale):
    r = lax.rsqrt(jnp.mean(x * x, axis=-1, keepdims=True) + EPS)
    return (x * r) * g * (1.0 + scale) + shift


def _split3(x):
    hi = x.astype(BF16)
    r1 = x - hi.astype(F32)
    mid = r1.astype(BF16)
    lo = (r1 - mid.astype(F32)).astype(BF16)
    return hi, mid, lo


def _mod_kernel(c_ref, w_ref, b_ref, o_ref):
    sc = _silu(c_ref[...]).astype(BF16)
    o_ref[...] = _dot(sc, w_ref[...].astype(BF16)) + b_ref[...]


def _modulation(c_all, w_mod, b_mod):
    rows = c_all.shape[0]
    return pl.pallas_call(
        _mod_kernel,
        grid=(DEPTH, N_MOD),
        in_specs=[
            pl.BlockSpec((rows, D_MODEL), lambda l, j: (0, 0)),
            pl.BlockSpec((None, D_MODEL, D_MODEL), lambda l, j: (l, 0, j)),
            pl.BlockSpec((None, None, 1, D_MODEL), lambda l, j: (l, j, 0, 0)),
        ],
        out_specs=pl.BlockSpec((None, None, rows, D_MODEL), lambda l, j: (l, j, 0, 0)),
        out_shape=jax.ShapeDtypeStruct((DEPTH, N_MOD, rows, D_MODEL), F32),
        compiler_params=pltpu.CompilerParams(dimension_semantics=("arbitrary", "arbitrary"),
                                             vmem_limit_bytes=VMEM_LIMIT),
        name="adaln_mod",
    )(c_all, w_mod, b_mod.reshape(DEPTH, N_MOD, 1, D_MODEL))


def _mod_rows(mod_ref, j, row, per_row):
    if per_row:
        return mod_ref[j]
    return mod_ref[j, pl.ds(row, 1), :]


FFN_CHUNK = 256
FFN_NCHUNK = D_FF // FFN_CHUNK


def _ffn_kernel(x_ref, mod_ref, n_ref, wg_hbm, wu_hbm, wd_hbm, nf_ref, o_ref,
                wg_ref, wu_ref, wd_ref, sup_ref, sdn_ref, sem, *, layer, j0, per_row, tiles_per_seq, final):
    i = pl.program_id(0)
    row = i // tiles_per_seq
    shift = _mod_rows(mod_ref, j0, row, per_row)
    scale = _mod_rows(mod_ref, j0 + 1, row, per_row)
    gate = _mod_rows(mod_ref, j0 + 2, row, per_row)
    x = x_ref[...]
    h = _norm_mod(x, n_ref[...], shift, scale).astype(BF16)

    up_jobs = [(w, c) for w in (0, 1) for c in range(D_MODEL // FFN_CHUNK)]

    def up_copy(j):
        w, c = up_jobs[j]
        src = (wg_hbm, wu_hbm)[w].at[layer, pl.ds(c * FFN_CHUNK, FFN_CHUNK), :]
        return pltpu.make_async_copy(src, sup_ref.at[j % 2], sem.at[0, j % 2])

    def down_copy(c):
        src = wd_hbm.at[layer, pl.ds(c * FFN_CHUNK, FFN_CHUNK), :]
        return pltpu.make_async_copy(src, sdn_ref.at[c % 2], sem.at[1, c % 2])

    def finish(acc):
        y = x + (0.5 * gate) * acc
        if final:
            y = (y * lax.rsqrt(jnp.mean(y * y, axis=-1, keepdims=True) + EPS)) * nf_ref[...]
        o_ref[...] = y

    @pl.when(i == 0)
    def _():
        n_up = D_MODEL // FFN_CHUNK
        per_wait = -(-FFN_NCHUNK // n_up)
        g_cols = [None] * FFN_NCHUNK
        u_cols = [None] * FFN_NCHUNK
        up_copy(0).start()
        for j, (w, c) in enumerate(up_jobs):
            if j + 1 < len(up_jobs):
                up_copy(j + 1).start()
            else:
                down_copy(0).start()
            up_copy(j).wait()
            rs = slice(c * FFN_CHUNK, (c + 1) * FFN_CHUNK)
            (wg_ref, wu_ref)[w][rs, :] = sup_ref[j % 2].astype(BF16)
            if w == 1:
                lo, hi = c * per_wait, min((c + 1) * per_wait, FFN_NCHUNK)
                g = _dot(h, wg_ref[:, lo * FFN_CHUNK:hi * FFN_CHUNK])
                for cc in range(lo, hi):
                    g_cols[cc] = g[:, (cc - lo) * FFN_CHUNK:(cc - lo + 1) * FFN_CHUNK]
        acc = None
        for c in range(FFN_NCHUNK):
            if c + 1 < FFN_NCHUNK:
                down_copy(c + 1).start()
            down_copy(c).wait()
            rs = slice(c * FFN_CHUNK, (c + 1) * FFN_CHUNK)
            wb = sdn_ref[c % 2].astype(BF16)
            wd_ref[rs, :] = wb
            if c % 2 == 0:
                hi = min(c + 2, FFN_NCHUNK)
                u = _dot(h, wu_ref[:, c * FFN_CHUNK:hi * FFN_CHUNK])
                for cc in range(c, hi):
                    u_cols[cc] = u[:, (cc - c) * FFN_CHUNK:(cc - c + 1) * FFN_CHUNK]
            a = (_silu(g_cols[c]) * u_cols[c]).astype(BF16)
            part = _dot(a, wb)
            acc = part if acc is None else acc + part
        finish(acc)

    @pl.when(i != 0)
    def _():
        a = (_silu(_dot(h, wg_ref[...])) * _dot(h, wu_ref[...])).astype(BF16)
        finish(_dot(a, wd_ref[...]))


def _ffn(x, mod, norm, wg, wu, wd, norm_final, layer, j0, *, tm, per_row, mod_rows, mod_row_block, final=False):
    rows = x.shape[0]
    tiles_per_seq = 1 if per_row else (rows // mod_rows) // tm
    kern = functools.partial(_ffn_kernel, layer=layer, j0=j0, per_row=per_row, tiles_per_seq=tiles_per_seq,
                             final=final)
    return pl.pallas_call(
        kern,
        grid=(rows // tm,),
        in_specs=[
            pl.BlockSpec((tm, D_MODEL), lambda i: (i, 0)),
            pl.BlockSpec((None, N_MOD, mod_rows, D_MODEL), lambda i: (layer, 0, mod_row_block, 0)),
            pl.BlockSpec((None, 1, D_MODEL), lambda i: (layer, 0, 0)),
            pl.BlockSpec(memory_space=pl.ANY),
            pl.BlockSpec(memory_space=pl.ANY),
            pl.BlockSpec(memory_space=pl.ANY),
            pl.BlockSpec((1, D_MODEL), lambda i: (0, 0)),
        ],
        out_specs=pl.BlockSpec((tm, D_MODEL), lambda i: (i, 0)),
        out_shape=jax.ShapeDtypeStruct((rows, D_MODEL), F32),
        scratch_shapes=[
            pltpu.VMEM((D_MODEL, D_FF), BF16),
            pltpu.VMEM((D_MODEL, D_FF), BF16),
            pltpu.VMEM((D_FF, D_MODEL), BF16),
            pltpu.VMEM((2, FFN_CHUNK, D_FF), F32),
            pltpu.VMEM((2, FFN_CHUNK, D_MODEL), F32),
            pltpu.SemaphoreType.DMA((2, 2)),
        ],
        compiler_params=pltpu.CompilerParams(dimension_semantics=("arbitrary",), vmem_limit_bytes=VMEM_LIMIT),
        name="swiglu_half_step",
    )(x, mod, norm.reshape(DEPTH, 1, D_MODEL), wg, wu, wd, norm_final.reshape(1, D_MODEL))


def _lower_bound_kernel(x_ref, o_ref):
    rows = [x_ref[l:l + 1, :] for l in range(DEPTH)]
    m = functools.reduce(jnp.maximum, rows)
    e = [jnp.exp(r - m) for r in rows]
    s = functools.reduce(lambda a, b: a + b, e)
    sm = [ei / s for ei in e]
    acc = sm[0]
    for l in range(DEPTH):
        if l > 0:
            acc = acc + sm[l]
        o_ref[l:l + 1, :] = acc - sm[0]


def _lower_bounds(hgrn_lower):
    return pl.pallas_call(
        _lower_bound_kernel,
        out_shape=jax.ShapeDtypeStruct((DEPTH, HG_WIDTH), F32),
        name="hgrn_lower_bounds",
    )(hgrn_lower)


def _bucket_map(rows, cols, valid_rows, col_lo, col_hi):
    r = np.arange(rows)[:, None]
    c = np.arange(cols)[None, :]
    rel = c - WINDOW - r
    n = np.maximum(-rel, 0)
    exact = REL_BUCKETS // 2
    nf = np.maximum(n, 1).astype(np.float32)
    large = exact + (np.log(nf / np.float32(exact)) / np.float32(math.log(REL_MAX_DIST / exact))
                     * np.float32(REL_BUCKETS - exact)).astype(np.int32)
    large = np.minimum(large, REL_BUCKETS - 1)
    bucket = np.where(n < exact, n, large)
    valid = (rel <= 0) & (rel > -WINDOW) & (r < valid_rows) & (c >= col_lo) & (c < col_hi)
    return np.where(valid, bucket, -1).astype(np.int32)


def _bias_table_kernel(bkt_ref, rb_ref, o_ref):
    h = pl.program_id(1)
    bkt = bkt_ref[...]
    acc = jnp.full(bkt.shape, NEG, F32)
    for b in range(REL_BUCKETS):
        acc = jnp.where(bkt == b, rb_ref[b, h], acc)
    o_ref[...] = acc


def _bias_tables(bucket_maps, rel_bias):
    m, r, c = bucket_maps.shape
    return pl.pallas_call(
        _bias_table_kernel,
        grid=(m, ATT_HEADS),
        in_specs=[pl.BlockSpec((None, r, c), lambda i, h: (i, 0, 0)),
                  pl.BlockSpec(memory_space=pltpu.SMEM)],
        out_specs=pl.BlockSpec((None, None, r, c), lambda i, h: (i, h, 0, 0)),
        out_shape=jax.ShapeDtypeStruct((m, ATT_HEADS, r, c), F32),
        compiler_params=pltpu.CompilerParams(dimension_semantics=("arbitrary", "arbitrary")),
        name="rel_bias_tables",
    )(bucket_maps, rel_bias)


def _block_ones(n, blk):
    i = np.arange(n)
    return (i[:, None] // blk == i[None, :] // blk)


def _mixer_consts(tt):
    tri = jnp.asarray(np.tril(np.ones((tt, tt), np.float32)), BF16)
    blk = _block_ones(HG_WIDTH, HG_DK)
    return tri, jnp.asarray(blk, BF16), jnp.asarray(blk, F32)


def _pool_block_diag(pool_w):
    out = jnp.zeros((DEPTH, POOL_WIDTH, POOL_WIDTH), BF16)
    for g in range(len(POOL_WINDOWS)):
        sl = slice(g * POOL_GW, (g + 1) * POOL_GW)
        out = out.at[:, sl, sl].set(pool_w[:, g].astype(BF16))
    return out


def _head_masks():
    lane = lax.broadcasted_iota(jnp.int32, (1, HG_WIDTH), 1)
    return [jnp.where((lane >= HG_DK * h) & (lane < HG_DK * (h + 1)), 1.0, 0.0).astype(F32) for h in range(HG_HEADS)]


def _prompt_mixer_kernel(x_ref, xn_ref, mod_ref, n_ref, win_ref, wout_ref, pbd_ref, pscale_ref, lb_ref, hgn_ref,
                         bm_ref, sink_ref, tri_ref, ones_ref, bdm_ref,
                         xo_ref, pool_o, s_o, k_o, v_o,
                         za_ref, zb_ref, ubuf, kbuf, vbuf, st_ref, cat_ref, *, layer, tt, nt, ntiles):
    b = pl.program_id(0)
    p = pl.program_id(1)
    last_pair = pl.num_programs(1) - 1
    nxt = jnp.minimum(b * nt + 2 * p + 2, ntiles - 1)
    bn = nxt // nt
    n_chunks = 8
    cw = IN_WIDTH // n_chunks

    def projector(x_rows, bsel, dst_ref):
        h = _norm_mod(x_rows, n_ref[...], mod_ref[3, pl.ds(bsel, 1), :], mod_ref[4, pl.ds(bsel, 1), :]).astype(BF16)
        state = {"k": 0}

        def step():
            k = state["k"]
            if k < n_chunks:
                dst_ref[:, k * cw:(k + 1) * cw] = _dot(h, win_ref[:, k * cw:(k + 1) * cw])
                state["k"] = k + 1

        return step

    @pl.when((b == 0) & (p == 0))
    def _():
        first_proj = projector(x_ref[0:tt, :], b, za_ref)
        for _ in range(n_chunks):
            first_proj()

    @pl.when(p == 0)
    def _():
        ubuf[0:16, :] = jnp.zeros((16, POOL_WIDTH), F32)
        kbuf[:, 0:WINDOW, :] = jnp.zeros((KV_HEADS, WINDOW, HEAD_DIM), BF16)
        vbuf[:, 0:WINDOW, :] = jnp.zeros((KV_HEADS, WINDOW, HEAD_DIM), BF16)
        st_ref[...] = jnp.zeros((HG_WIDTH, HG_WIDTH), F32)

    gate = mod_ref[5, pl.ds(b, 1), :]

    def mix(z_ref, x, row0, t, side):
        lane = lax.broadcasted_iota(jnp.int32, (1, POOL_WIDTH), 1)

        u = z_ref[:, OFF_POOL:OFF_POOL + POOL_WIDTH]
        ubuf[16:16 + tt, :] = u
        e = ubuf[...]
        s2 = e + pltpu.roll(e, 1, 0)
        s4 = s2 + pltpu.roll(s2, 2, 0)
        s8 = s4 + pltpu.roll(s4, 4, 0)
        s16 = s8 + pltpu.roll(s8, 8, 0)
        sel = jnp.where(lane < 64, s2, jnp.where(lane < 128, s4, jnp.where(lane < 192, s8, s16)))[16:]
        wl = jnp.where(lane < 64, 2.0, jnp.where(lane < 128, 4.0, jnp.where(lane < 192, 8.0, 16.0))).astype(F32)
        pos1 = (t * tt + 1 + lax.broadcasted_iota(jnp.int32, (tt, 1), 0)).astype(F32)
        dpool = sel / jnp.minimum(pos1, wl) - u
        ypool = _dot(dpool.astype(BF16), pbd_ref[...]) * pscale_ref[...]
        cat_ref[:, 0:POOL_WIDTH] = ypool.astype(BF16)
        ubuf[0:16, :] = ubuf[tt:tt + 16, :]
        side()


        for g in range(KV_HEADS):
            kbuf[g, WINDOW:WINDOW + tt, :] = z_ref[:, OFF_AK + HEAD_DIM * g:OFF_AK + HEAD_DIM * (g + 1)].astype(BF16)
            vbuf[g, WINDOW:WINDOW + tt, :] = z_ref[:, OFF_AV + HEAD_DIM * g:OFF_AV + HEAD_DIM * (g + 1)].astype(BF16)
        first = jnp.where(t == 0, 1, 0)
        bands = [(j, g) for j in range(tt // WINDOW) for g in range(KV_HEADS)]
        logits = []
        for j, g in bands:
            r0 = j * WINDOW
            c0 = OFF_AQ + HEAD_DIM * g * GQA_GROUP
            qg = jnp.concatenate([z_ref[r0:r0 + WINDOW, c0 + HEAD_DIM * i:c0 + HEAD_DIM * (i + 1)]
                                  for i in range(GQA_GROUP)], axis=0)
            qg = (qg * (HEAD_DIM ** -0.5)).astype(BF16)
            logits.append(_dot_nt(qg, kbuf[g, r0:r0 + 2 * WINDOW, :]))
        side()

        lb = lb_ref[...]
        q = _silu(z_ref[:, OFF_HQ:OFF_HQ + HG_WIDTH])
        f = lb + (1.0 - lb) * _sigmoid(z_ref[:, OFF_HF:OFF_HF + HG_WIDTH])
        k = 1.0 - f
        v = z_ref[:, OFF_HI:OFF_HI + HG_WIDTH]
        vb = v.astype(BF16)
        tri = tri_ref[...]
        hi, mid, lo = _split3(jnp.log(f))
        bcum = _dot(tri, hi) + _dot(tri, mid) + _dot(tri, lo)
        side()
        ones_blk = ones_ref[...]
        hm = _head_masks()

        n16 = tt // 16
        half_shape = (n16, 2, 8, HG_WIDTH)
        q4 = q.reshape(half_shape)
        b4 = bcum.reshape(half_shape)
        c4 = (bcum - jnp.log(k)).reshape(half_shape)
        v4 = v.reshape(half_shape)
        rowi = lax.broadcasted_iota(jnp.int32, (1, 8, 1), 1)
        o_up = jnp.zeros((n16, 8, HG_WIDTH), F32)
        o_dn = jnp.zeros((n16, 8, HG_WIDTH), F32)
        nh = n16 * 8
        for s in range(8):
            cs = c4[:, 0, s:s + 1, :]
            vs = v4[:, 0, s:s + 1, :]
            x_up = jnp.where(rowi >= s, q4[:, 0] * jnp.exp(b4[:, 0] - cs), 0.0)
            x_dn = q4[:, 1] * jnp.exp(b4[:, 1] - cs)
            xs = jnp.concatenate([x_up.reshape(nh, HG_WIDTH), x_dn.reshape(nh, HG_WIDTH)], axis=0)
            r = _dot(xs.astype(BF16), ones_blk)
            o_up = o_up + r[0:nh].reshape(n16, 8, HG_WIDTH) * vs
            o_dn = o_dn + r[nh:2 * nh].reshape(n16, 8, HG_WIDTH) * vs
            if s % 4 == 3:
                side()
        for s in range(8):
            cs = c4[:, 1, s:s + 1, :]
            x_dn = jnp.where(rowi >= s, q4[:, 1] * jnp.exp(b4[:, 1] - cs), 0.0)
            r = _dot(x_dn.reshape(nh, HG_WIDTH).astype(BF16), ones_blk)
            o_dn = o_dn + r.reshape(n16, 8, HG_WIDTH) * v4[:, 1, s:s + 1, :]
            if s % 4 == 3:
                side()
        o_tot = jnp.concatenate([o_up[:, None], o_dn[:, None]], axis=1).reshape(tt, HG_WIDTH)

        blocks = []
        blk = 32
        while blk <= tt:
            blocks += [(i * blk, i * blk + blk // 2, (i + 1) * blk) for i in range(tt // blk)]
            blk *= 2
        scores = []
        for lo_s, mid_s, hi_s in blocks:
            ref = bcum[mid_s - 1:mid_s, :]
            qt = q[mid_s:hi_s] * jnp.exp(bcum[mid_s:hi_s] - ref)
            kt = (k[lo_s:mid_s] * jnp.exp(ref - bcum[lo_s:mid_s])).astype(BF16)
            qs = jnp.concatenate([qt * hm[hh] for hh in range(HG_HEADS)], axis=0).astype(BF16)
            scores.append(_dot_nt(qs, kt).astype(BF16))

        st = st_ref[...]
        qin = (q * jnp.exp(bcum)).astype(BF16)
        o_tot = o_tot + _dot_nt(qin, st.astype(BF16))
        blast = bcum[tt - 1:tt, :]
        kout = (k * jnp.exp(blast - bcum)).astype(BF16)
        st_ref[...] = st * jnp.exp(blast) + _dot_tn(vb, kout) * bdm_ref[...]
        side()

        probs, dens = [], []
        for (j, g), lg_all in zip(bands, logits):
            h0 = g * GQA_GROUP
            bm = bm_ref[first, h0:h0 + GQA_GROUP] if j == 0 else bm_ref[0, h0:h0 + GQA_GROUP]
            ps = []
            for i in range(GQA_GROUP):
                lg = lg_all[i * WINDOW:(i + 1) * WINDOW] + bm[i]
                sink = sink_ref[layer, h0 + i]
                m = jnp.maximum(jnp.max(lg, axis=-1, keepdims=True), sink)
                p_ = jnp.exp(lg - m)
                dens.append(jnp.sum(p_, axis=-1, keepdims=True) + jnp.exp(sink - m))
                ps.append(p_.astype(BF16))
            probs.append(jnp.concatenate(ps, axis=0))
        for n, (j, g) in enumerate(bands):
            r0 = j * WINDOW
            og = _dot(probs[n], vbuf[g, r0:r0 + 2 * WINDOW, :])
            for i in range(GQA_GROUP):
                cc = POOL_WIDTH + HG_WIDTH + HEAD_DIM * (g * GQA_GROUP + i)
                oh = og[i * WINDOW:(i + 1) * WINDOW] / dens[n * GQA_GROUP + i]
                cat_ref[r0:r0 + WINDOW, cc:cc + HEAD_DIM] = oh.astype(BF16)
        for g in range(KV_HEADS):
            kbuf[g, 0:WINDOW, :] = kbuf[g, tt:tt + WINDOW, :]
            vbuf[g, 0:WINDOW, :] = vbuf[g, tt:tt + WINDOW, :]
        side()

        adds = {}
        for (lo_s, mid_s, hi_s), a in zip(blocks, scores):
            half = mid_s - lo_s
            ov = _dot(a, vb[lo_s:mid_s])
            oi = ov[0:half] * hm[0]
            for hh in range(1, HG_HEADS):
                oi = oi + ov[hh * half:(hh + 1) * half] * hm[hh]
            adds.setdefault(2 * half, []).extend([jnp.zeros((half, HG_WIDTH), F32), oi])
        for parts in adds.values():
            o_tot = o_tot + jnp.concatenate(parts, axis=0)

        sq_hi, sq_mid, _ = _split3(o_tot * o_tot)
        ms = (_dot(sq_hi, ones_blk) + _dot(sq_mid, ones_blk)) * (1.0 / HG_DK)
        y_hg = o_tot * lax.rsqrt(ms + EPS) * hgn_ref[...] * _silu(z_ref[:, OFF_HG:OFF_HG + HG_WIDTH])
        cat_ref[:, POOL_WIDTH:POOL_WIDTH + HG_WIDTH] = y_hg.astype(BF16)

        xo_ref[row0:row0 + tt, :] = x + gate * _dot(cat_ref[...], wout_ref[...])

        for _ in range(n_chunks):
            side()

    mix(za_ref, x_ref[0:tt, :], 0, 2 * p, projector(x_ref[tt:2 * tt, :], b, zb_ref))
    mix(zb_ref, x_ref[tt:2 * tt, :], tt, 2 * p + 1, projector(xn_ref[...], bn, za_ref))

    @pl.when(p == last_pair)
    def _():
        pool_o[...] = ubuf[pl.ds(1, POOL_BUF), :]
        s_kv = st_ref[...].T
        for hh in range(HG_HEADS):
            s_o[hh] = s_kv[HG_DK * hh:HG_DK * (hh + 1), HG_DK * hh:HG_DK * (hh + 1)]
        k_o[...] = zb_ref[tt - WINDOW:tt, OFF_AK:OFF_AV]
        v_o[...] = zb_ref[tt - WINDOW:tt, OFF_AV:IN_WIDTH]


def _prompt_mixer(x, mod, norm, w_in, w_out, pool_bd, pool_scale, lbs, hg_norm, bm, sinks, consts, layer, *,
                  tt, mod_row_block):
    bsz, seq, _ = x.shape
    tri, ones_blk, bdm = consts
    nt = seq // tt
    ntiles = bsz * nt
    kern = functools.partial(_prompt_mixer_kernel, layer=layer, tt=tt, nt=nt, ntiles=ntiles)
    full = lambda *shape: pl.BlockSpec(shape, lambda b, p: (0,) * len(shape))
    lsel = lambda *shape: pl.BlockSpec((None,) + shape, lambda b, p: (layer,) + (0,) * len(shape))
    mix_pair = lambda b, p: (b, p, 0)
    mix_seq = lambda *z: (lambda b, p: (b,) + z)

    def next_tile(b, p):
        nxt = jnp.minimum(b * nt + 2 * p + 2, ntiles - 1)
        return (nxt // nt, nxt % nt, 0)

    return pl.pallas_call(
        kern,
        grid=(bsz, nt // 2),
        in_specs=[
            pl.BlockSpec((None, 2 * tt, D_MODEL), mix_pair),
            pl.BlockSpec((None, tt, D_MODEL), next_tile),
            pl.BlockSpec((None, N_MOD, 8, D_MODEL), lambda b, p: (layer, 0, mod_row_block, 0)),
            lsel(1, D_MODEL),
            lsel(D_MODEL, IN_WIDTH),
            lsel(D_MODEL, D_MODEL),
            lsel(POOL_WIDTH, POOL_WIDTH),
            lsel(1, POOL_WIDTH),
            lsel(1, HG_WIDTH),
            lsel(1, HG_WIDTH),
            full(2, ATT_HEADS, WINDOW, 2 * WINDOW),
            pl.BlockSpec(memory_space=pltpu.SMEM),
            full(tt, tt),
            full(HG_WIDTH, HG_WIDTH),
            full(HG_WIDTH, HG_WIDTH),
        ],
        out_specs=[
            pl.BlockSpec((None, 2 * tt, D_MODEL), mix_pair),
            pl.BlockSpec((None, POOL_BUF, POOL_WIDTH), mix_seq(0, 0)),
            pl.BlockSpec((None, HG_HEADS, HG_DK, HG_DK), mix_seq(0, 0, 0)),
            pl.BlockSpec((None, WINDOW, KV_HEADS * HEAD_DIM), mix_seq(0, 0)),
            pl.BlockSpec((None, WINDOW, KV_HEADS * HEAD_DIM), mix_seq(0, 0)),
        ],
        out_shape=[
            jax.ShapeDtypeStruct((bsz, seq, D_MODEL), F32),
            jax.ShapeDtypeStruct((bsz, POOL_BUF, POOL_WIDTH), F32),
            jax.ShapeDtypeStruct((bsz, HG_HEADS, HG_DK, HG_DK), F32),
            jax.ShapeDtypeStruct((bsz, WINDOW, KV_HEADS * HEAD_DIM), F32),
            jax.ShapeDtypeStruct((bsz, WINDOW, KV_HEADS * HEAD_DIM), F32),
        ],
        scratch_shapes=[
            pltpu.VMEM((tt, IN_WIDTH), F32),
            pltpu.VMEM((tt, IN_WIDTH), F32),
            pltpu.VMEM((16 + tt, POOL_WIDTH), F32),
            pltpu.VMEM((KV_HEADS, WINDOW + tt, HEAD_DIM), BF16),
            pltpu.VMEM((KV_HEADS, WINDOW + tt, HEAD_DIM), BF16),
            pltpu.VMEM((HG_WIDTH, HG_WIDTH), F32),
            pltpu.VMEM((tt, D_MODEL), BF16),
        ],
        compiler_params=pltpu.CompilerParams(dimension_semantics=("arbitrary", "arbitrary"),
                                             vmem_limit_bytes=VMEM_LIMIT),
        name="prompt_mixer",
    )(x, x, mod, norm.reshape(DEPTH, 1, D_MODEL), w_in, w_out, pool_bd, pool_scale.reshape(DEPTH, 1, POOL_WIDTH),
      lbs.reshape(DEPTH, 1, HG_WIDTH), hg_norm, bm, sinks, tri, ones_blk, bdm)


def _sample_mixer_kernel(x_ref, mod_ref, n_ref, win_ref, wout_ref, pbd_ref, pscale_ref, lb_ref, hgn_ref,
                         bms_ref, sinkc_ref, ones_ref, sp_ref, s0_ref, kt_ref, vt_ref,
                         xo_ref, pool_o, s_o, kt_o, vt_o,
                         z_ref, cat_ref, qs_ref, os_ref, kn_ref, vn_ref, qint_ref, koutt_ref, vtt_ref, gt_ref,
                         oin_ref, oacc_ref, *, steps, nb, gb, rb, pos0):
    g = pl.program_id(0)
    last = pl.num_programs(0) - 1
    base = pl.multiple_of(g * gb, gb)
    kvw = KV_HEADS * HEAD_DIM

    def rows(t):
        return slice(t * nb, (t + 1) * nb)

    @pl.when(g == 0)
    def _():
        for t in range(steps):
            h = _norm_mod(x_ref[t], n_ref[...], mod_ref[0], mod_ref[1]).astype(BF16)
            z_ref[rows(t), :] = _dot(h, win_ref[...])
        lane = lax.broadcasted_iota(jnp.int32, (1, POOL_WIDTH), 1)

        ext = [sp_ref[r] for r in range(POOL_BUF)] + [z_ref[rows(t), OFF_POOL:OFF_POOL + POOL_WIDTH] for t in range(steps)]
        for r in range(POOL_BUF):
            pool_o[r] = ext[r + steps]
        sums = {1: ext}
        w = 1
        while w < 16:
            prev = sums[w]
            sums[2 * w] = [None if (r < 2 * w - 1) else prev[r] + prev[r - w] for r in range(len(ext))]
            w *= 2
        dl = []
        for t in range(steps):
            r = POOL_BUF + t
            sel = jnp.where(lane < 64, sums[2][r], jnp.where(lane < 128, sums[4][r], jnp.where(lane < 192, sums[8][r], sums[16][r])))
            cnt = [float(min(pos0 + t + 1, wd)) for wd in POOL_WINDOWS]
            cl = jnp.where(lane < 64, cnt[0], jnp.where(lane < 128, cnt[1], jnp.where(lane < 192, cnt[2], cnt[3]))).astype(F32)
            dl.append(sel / cl - ext[r])
        ypool = _dot(jnp.concatenate(dl, axis=0).astype(BF16), pbd_ref[...]) * pscale_ref[...]
        cat_ref[:, 0:POOL_WIDTH] = ypool

        lb = lb_ref[...]
        qv, kv_, vv, bv = [], [], [], []
        bacc = None
        for t in range(steps):
            qv.append(_silu(z_ref[rows(t), OFF_HQ:OFF_HQ + HG_WIDTH]))
            f = lb + (1.0 - lb) * _sigmoid(z_ref[rows(t), OFF_HF:OFF_HF + HG_WIDTH])
            kv_.append(1.0 - f)
            vv.append(z_ref[rows(t), OFF_HI:OFF_HI + HG_WIDTH])
            bacc = jnp.log(f) if bacc is None else bacc + jnp.log(f)
            bv.append(bacc)
        pairs = [(t, s) for t in range(steps) for s in range(t + 1)]
        xs = jnp.concatenate([qv[t] * kv_[s] * jnp.exp(bv[t] - bv[s]) for t, s in pairs], axis=0).astype(BF16)
        rr = _dot(xs, ones_ref[...])
        o_intra = [None] * steps
        for i, (t, s) in enumerate(pairs):
            term = rr[i * nb:(i + 1) * nb] * vv[s]
            o_intra[t] = term if o_intra[t] is None else o_intra[t] + term
        blast = bv[steps - 1]
        gt_ref[...] = jnp.exp(blast).T
        for t in range(steps):
            oin_ref[t] = o_intra[t]
            qint_ref[t] = (qv[t] * jnp.exp(bv[t])).T
            koutt_ref[t] = (kv_[t] * jnp.exp(blast - bv[t])).T
            vtt_ref[t] = vv[t].T
        oacc_ref[...] = jnp.zeros(oacc_ref.shape, F32)

        zeros64 = jnp.zeros((nb, HEAD_DIM), F32)
        kn_ref[...] = jnp.zeros(kn_ref.shape, F32)
        vn_ref[...] = jnp.zeros(vn_ref.shape, F32)
        for t in range(steps):
            kn_ref[:, t, :] = z_ref[rows(t), OFF_AK:OFF_AK + kvw]
            vn_ref[:, t, :] = z_ref[rows(t), OFF_AV:OFF_AV + kvw]
            for hd in range(ATT_HEADS):
                c0 = OFF_AQ + HEAD_DIM * hd
                qsl = z_ref[rows(t), c0:c0 + HEAD_DIM] * (HEAD_DIM ** -0.5)
                parts = [qsl, zeros64] if hd < GQA_GROUP else [zeros64, qsl]
                qs_ref[:, ATT_HEADS * t + hd, :] = jnp.concatenate(parts, axis=1)

    r0 = pl.multiple_of(g * rb, rb)
    hrow = pl.multiple_of((g * rb // HG_DK) * HG_DK, HG_DK)
    vts = [vtt_ref[t, pl.ds(hrow, HG_DK), :] for t in range(steps)]
    accs = [oacc_ref[t, pl.ds(hrow, HG_DK), :] for t in range(steps)]
    for i in range(rb):
        s0 = s0_ref[i]
        snew = gt_ref[pl.ds(r0 + i, 1), :] * s0
        for t in range(steps):
            accs[t] = accs[t] + qint_ref[t, pl.ds(r0 + i, 1), :] * s0
            snew = snew + koutt_ref[t, pl.ds(r0 + i, 1), :] * vts[t]
        s_o[i] = snew
    for t in range(steps):
        oacc_ref[t, pl.ds(hrow, HG_DK), :] = accs[t]

    bm_old = bms_ref[:, 0:WINDOW]
    bm_new = bms_ref[:, WINDOW:WINDOW + 8]
    sinkc = sinkc_ref[...]
    seqs = range(gb)
    qn = [qs_ref[base + n].astype(BF16) for n in seqs]
    l_old = [_dot(qn[n], kt_ref[n].astype(BF16)) for n in seqs]
    l_new = [_dot_nt(qn[n], kn_ref[base + n].astype(BF16)) for n in seqs]
    def new_columns(ref):
        new = ref[pl.ds(base, gb)].reshape(gb * 8, kvw)
        if gb * 8 < WINDOW:
            new = jnp.concatenate([new, jnp.zeros((WINDOW - gb * 8, kvw), F32)], axis=0)
        return new.T

    knt = new_columns(kn_ref)
    vnt = new_columns(vn_ref)
    for n in seqs:
        kt_o[n] = jnp.concatenate([kt_ref[n][:, steps:], knt[:, 8 * n:8 * n + steps]], axis=1)
        vt_o[n] = jnp.concatenate([vt_ref[n][:, steps:], vnt[:, 8 * n:8 * n + steps]], axis=1)
    p_old, p_new, den = [], [], []
    for n in seqs:
        lo_, ln_ = l_old[n] + bm_old, l_new[n] + bm_new
        m = jnp.maximum(jnp.maximum(jnp.max(lo_, axis=-1, keepdims=True), jnp.max(ln_, axis=-1, keepdims=True)), sinkc)
        po, pn = jnp.exp(lo_ - m), jnp.exp(ln_ - m)
        den.append(jnp.sum(po, axis=-1, keepdims=True) + jnp.sum(pn, axis=-1, keepdims=True) + jnp.exp(sinkc - m))
        p_old.append(po.astype(BF16))
        p_new.append(pn.astype(BF16))
    for n in seqs:
        o = _dot_nt(p_old[n], vt_ref[n].astype(BF16)) + _dot(p_new[n], vn_ref[base + n].astype(BF16))
        os_ref[n] = o / den[n]
    for t in range(steps):
        for hd in range(ATT_HEADS):
            c0 = HEAD_DIM * (hd // GQA_GROUP)
            cc = POOL_WIDTH + HG_WIDTH + HEAD_DIM * hd
            cat_ref[pl.ds(t * nb + base, gb), cc:cc + HEAD_DIM] = os_ref[:, ATT_HEADS * t + hd, c0:c0 + HEAD_DIM]

    @pl.when(g == last)
    def _():
        ones_blk = ones_ref[...]
        for t in range(steps):
            o = oin_ref[t] + oacc_ref[t].T
            sq_hi, sq_mid, _ = _split3(o * o)
            ms = (_dot(sq_hi, ones_blk) + _dot(sq_mid, ones_blk)) * (1.0 / HG_DK)
            gate_t = _silu(z_ref[rows(t), OFF_HG:OFF_HG + HG_WIDTH])
            cat_ref[rows(t), POOL_WIDTH:POOL_WIDTH + HG_WIDTH] = o * lax.rsqrt(ms + EPS) * hgn_ref[...] * gate_t
        for t in range(steps):
            y = _dot(cat_ref[rows(t), :].astype(BF16), wout_ref[...])
            xo_ref[t] = x_ref[t] + mod_ref[2] * y


N_SAMPLE_MIXER_INPUTS = 16


def _sample_mixer_kernel_into(*refs, n_stacks, **kw):
    _sample_mixer_kernel(*refs[:N_SAMPLE_MIXER_INPUTS], *refs[N_SAMPLE_MIXER_INPUTS + n_stacks:], **kw)


def _sample_mixer(x, mod, norm, w_in, w_out, pool_bd, pool_scale, lbs, hg_norm, bms, sink_col, ones_blk,
                  pool_hist, hgrn_state, cache_kt, cache_vt, stacks, layer, *, gb, pos0):
    steps, nb, _ = x.shape
    rows_th = steps * ATT_HEADS
    ngrp = nb // gb
    rb = HG_WIDTH // ngrp
    kvw = KV_HEADS * HEAD_DIM
    stacks = () if stacks is None else tuple(stacks)
    kern = functools.partial(_sample_mixer_kernel_into, n_stacks=len(stacks), steps=steps, nb=nb, gb=gb, rb=rb,
                             pos0=pos0)
    full = lambda *shape: pl.BlockSpec(shape, lambda g: (0,) * len(shape))
    lsel = lambda *shape: pl.BlockSpec((None,) + shape, lambda g: (layer,) + (0,) * len(shape))
    grp = lambda *shape: pl.BlockSpec((None,) + shape, lambda g: (layer, g) + (0,) * (len(shape) - 1))
    return pl.pallas_call(
        kern,
        grid=(ngrp,),
        in_specs=[
            full(steps, nb, D_MODEL),
            pl.BlockSpec((None, 3, nb, D_MODEL), lambda g: (layer, 1, 0, 0)),
            lsel(1, D_MODEL),
            lsel(D_MODEL, IN_WIDTH),
            lsel(D_MODEL, D_MODEL),
            lsel(POOL_WIDTH, POOL_WIDTH),
            lsel(1, POOL_WIDTH),
            lsel(1, HG_WIDTH),
            lsel(1, HG_WIDTH),
            full(rows_th, 2 * WINDOW),
            lsel(rows_th, 1),
            full(HG_WIDTH, HG_WIDTH),
            lsel(POOL_BUF, nb, POOL_WIDTH),
            grp(rb, HG_DK, nb),
            grp(gb, kvw, WINDOW),
            grp(gb, kvw, WINDOW),
        ] + [pl.BlockSpec(memory_space=pl.ANY)] * len(stacks),
        out_specs=[
            full(steps, nb, D_MODEL),
            lsel(POOL_BUF, nb, POOL_WIDTH),
            grp(rb, HG_DK, nb),
            grp(gb, kvw, WINDOW),
            grp(gb, kvw, WINDOW),
        ],
        out_shape=[
            jax.ShapeDtypeStruct((steps, nb, D_MODEL), F32),
            jax.ShapeDtypeStruct((DEPTH, POOL_BUF, nb, POOL_WIDTH), F32),
            jax.ShapeDtypeStruct((DEPTH, HG_WIDTH, HG_DK, nb), F32),
            jax.ShapeDtypeStruct((DEPTH, nb, kvw, WINDOW), F32),
            jax.ShapeDtypeStruct((DEPTH, nb, kvw, WINDOW), F32),
        ],
        input_output_aliases={N_SAMPLE_MIXER_INPUTS + i: 1 + i for i in range(len(stacks))},
        scratch_shapes=[
            pltpu.VMEM((steps * nb, IN_WIDTH), F32),
            pltpu.VMEM((steps * nb, D_MODEL), F32),
            pltpu.VMEM((nb, rows_th, kvw), F32),
            pltpu.VMEM((gb, rows_th, kvw), F32),
            pltpu.VMEM((nb, 8, kvw), F32),
            pltpu.VMEM((nb, 8, kvw), F32),
            pltpu.VMEM((steps, HG_WIDTH, nb), F32),
            pltpu.VMEM((steps, HG_WIDTH, nb), F32),
            pltpu.VMEM((steps, HG_WIDTH, nb), F32),
            pltpu.VMEM((HG_WIDTH, nb), F32),
            pltpu.VMEM((steps, nb, HG_WIDTH), F32),
            pltpu.VMEM((steps, HG_WIDTH, nb), F32),
        ],
        compiler_params=pltpu.CompilerParams(dimension_semantics=("arbitrary",), vmem_limit_bytes=VMEM_LIMIT),
        name="sample_mixer",
    )(x, mod, norm.reshape(DEPTH, 1, D_MODEL), w_in, w_out, pool_bd, pool_scale.reshape(DEPTH, 1, POOL_WIDTH),
      lbs.reshape(DEPTH, 1, HG_WIDTH), hg_norm, bms, sink_col, ones_blk, pool_hist, hgrn_state, cache_kt, cache_vt,
      *stacks)


FFN_TILE = 512
MIXER_TILE = 256
SAMPLE_GROUP = 8
PAST_LEN = 8192


def kernel(x_prompt, x_sample, c_prompt, c_sample, state_pool, state_hgrn, cache_k_win, cache_v_win, norm_ffn1, norm_mix, norm_ffn2, w_mod, b_mod, ffn1_w_gate, ffn1_w_up, ffn1_w_down, w_in, w_out, pool_w, pool_scale, hgrn_lower, hgrn_norm, attn_sinks, rel_bias, ffn2_w_gate, ffn2_w_up, ffn2_w_down, norm_final):
    bsz, seq, _ = x_prompt.shape
    nb, steps, _ = x_sample.shape
    kvw = KV_HEADS * HEAD_DIM

    mod = _modulation(jnp.concatenate([c_sample, c_prompt], axis=0), w_mod, b_mod)
    prompt_mod_block = nb // 8
    lbs = _lower_bounds(hgrn_lower)
    maps_p = np.stack([_bucket_map(WINDOW, 2 * WINDOW, WINDOW, 0, 2 * WINDOW),
                       _bucket_map(WINDOW, 2 * WINDOW, WINDOW, WINDOW, 2 * WINDOW)])
    bm_p = _bias_tables(jnp.asarray(maps_p), rel_bias)
    maps_s = _bucket_map(8, 2 * WINDOW, steps, 0, WINDOW + steps)[None]
    bm_s = _bias_tables(jnp.asarray(maps_s), rel_bias)[0]
    bm_s = jnp.swapaxes(bm_s, 0, 1).reshape(8 * ATT_HEADS, 2 * WINDOW)[:steps * ATT_HEADS]
    sink_col = jnp.tile(attn_sinks, (1, steps)).reshape(DEPTH, steps * ATT_HEADS, 1)
    consts = _mixer_consts(MIXER_TILE)
    hgn = jnp.tile(hgrn_norm, (1, HG_HEADS)).reshape(DEPTH, 1, HG_WIDTH)
    pool_bd = _pool_block_diag(pool_w)
    w_in_b, w_out_b = w_in.astype(BF16), w_out.astype(BF16)
    f1 = (ffn1_w_gate, ffn1_w_up, ffn1_w_down)
    f2 = (ffn2_w_gate, ffn2_w_up, ffn2_w_down)
    s_pool = jnp.transpose(state_pool, (0, 2, 1, 3))
    s_hgrn = jnp.transpose(state_hgrn, (0, 2, 3, 4, 1)).reshape(DEPTH, HG_WIDTH, HG_DK, nb)
    c_k = jnp.transpose(cache_k_win, (0, 1, 3, 4, 2)).reshape(DEPTH, nb, kvw, WINDOW)
    c_v = jnp.transpose(cache_v_win, (0, 1, 3, 4, 2)).reshape(DEPTH, nb, kvw, WINDOW)

    xp = x_prompt.reshape(bsz * seq, D_MODEL)
    xs = jnp.swapaxes(x_sample, 0, 1).reshape(steps * nb, D_MODEL)
    ffn_p = functools.partial(_ffn, tm=FFN_TILE, per_row=False, mod_rows=8, mod_row_block=prompt_mod_block)
    ffn_s = functools.partial(_ffn, tm=nb, per_row=True, mod_rows=nb, mod_row_block=0)
    st_p, stacks_s = [], None
    for l in range(DEPTH):
        xp = ffn_p(xp, mod, norm_ffn1, *f1, norm_final, l, 0)
        xs = ffn_s(xs, mod, norm_ffn1, *f1, norm_final, l, 0)
        xp, *sp = _prompt_mixer(xp.reshape(bsz, seq, D_MODEL), mod, norm_mix, w_in_b, w_out_b, pool_bd, pool_scale,
                                lbs, hgn, bm_p, attn_sinks, consts, l, tt=MIXER_TILE, mod_row_block=prompt_mod_block)
        xs, *stacks_s = _sample_mixer(xs.reshape(steps, nb, D_MODEL), mod, norm_mix, w_in_b, w_out_b, pool_bd,
                                      pool_scale, lbs, hgn, bm_s, sink_col, consts[1], s_pool, s_hgrn, c_k, c_v,
                                      stacks_s, l, gb=SAMPLE_GROUP, pos0=PAST_LEN)
        st_p.append(sp)
        last = l == DEPTH - 1
        xp = ffn_p(xp.reshape(bsz * seq, D_MODEL), mod, norm_ffn2, *f2, norm_final, l, 6, final=last)
        xs = ffn_s(xs.reshape(steps * nb, D_MODEL), mod, norm_ffn2, *f2, norm_final, l, 6, final=last)

    y_prompt = xp.reshape(bsz, seq, D_MODEL)
    y_sample = jnp.swapaxes(xs.reshape(steps, nb, D_MODEL), 0, 1)

    def stacked(states, i, shape):
        return jnp.stack([s[i] for s in states]).reshape((DEPTH,) + shape)

    outs = [stacked(st_p, 0, (bsz, POOL_BUF, POOL_WIDTH)),
            stacked(st_p, 1, (bsz, HG_HEADS, HG_DK, HG_DK)),
            stacked(st_p, 2, (bsz, WINDOW, KV_HEADS, HEAD_DIM)),
            stacked(st_p, 3, (bsz, WINDOW, KV_HEADS, HEAD_DIM)),
            jnp.transpose(stacks_s[0], (0, 2, 1, 3)),
            jnp.transpose(stacks_s[1].reshape(DEPTH, HG_HEADS, HG_DK, HG_DK, nb), (0, 4, 1, 2, 3)),
            jnp.transpose(stacks_s[2].reshape(DEPTH, nb, KV_HEADS, HEAD_DIM, WINDOW), (0, 1, 4, 2, 3)),
            jnp.transpose(stacks_s[3].reshape(DEPTH, nb, KV_HEADS, HEAD_DIM, WINDOW), (0, 1, 4, 2, 3))]
    return (y_prompt, y_sample, *outs)
```

```python
import functools
import math

import numpy as np
import jax
import jax.numpy as jnp
from jax import lax
from jax.experimental import pallas as pl
from jax.experimental.pallas import tpu as pltpu

F32 = jnp.float32
BF16 = jnp.bfloat16

D_MODEL = 1024
DEPTH = 4
N_MOD = 9
D_FF = 2816
EPS = 1e-6
POOL_WIDTH = 256
POOL_WINDOWS = (2, 4, 8, 16)
POOL_GW = 64
POOL_BUF = 15
HG_WIDTH = 256
HG_HEADS = 4
HG_DK = 64
ATT_WIDTH = 512
ATT_HEADS = 8
KV_HEADS = 2
GQA_GROUP = 4
HEAD_DIM = 64
WINDOW = 128
REL_BUCKETS = 32
REL_MAX_DIST = 128
OFF_POOL, OFF_HQ, OFF_HF, OFF_HI, OFF_HG, OFF_AQ, OFF_AK, OFF_AV = 0, 256, 512, 768, 1024, 1280, 1792, 1920
IN_WIDTH = 2048
NEG = -1e30

VMEM_LIMIT = 56 * 1024 * 1024


def _dot(a, b):
    return jnp.dot(a, b, preferred_element_type=F32)


def _dot_nt(a, b):
    return lax.dot_general(a, b, (((1,), (1,)), ((), ())), preferred_element_type=F32)


def _dot_tn(a, b):
    return lax.dot_general(a, b, (((0,), (0,)), ((), ())), preferred_element_type=F32)


def _sigmoid(x):
    return 1.0 / (1.0 + jnp.exp(-x))


def _silu(x):
    return x * _sigmoid(x)


def _norm_mod(x, g, shift, scale):
    r = lax.rsqrt(jnp.mean(x * x, axis=-1, keepdims=True) + EPS)
    return (x * r) * g * (1.0 + scale) + shift


def _split3(x):
    hi = x.astype(BF16)
    r1 = x - hi.astype(F32)
    mid = r1.astype(BF16)
    lo = (r1 - mid.astype(F32)).astype(BF16)
    return hi, mid, lo


def _mod_kernel(c_ref, w_ref, b_ref, o_ref):
    sc = _silu(c_ref[...]).astype(BF16)
    o_ref[...] = _dot(sc, w_ref[...].astype(BF16)) + b_ref[...]


def _modulation(c_all, w_mod, b_mod):
    rows = c_all.shape[0]
    return pl.pallas_call(
        _mod_kernel,
        grid=(DEPTH, N_MOD),
        in_specs=[
            pl.BlockSpec((rows, D_MODEL), lambda l, j: (0, 0)),
            pl.BlockSpec((None, D_MODEL, D_MODEL), lambda l, j: (l, 0, j)),
            pl.BlockSpec((None, None, 1, D_MODEL), lambda l, j: (l, j, 0, 0)),
        ],
        out_specs=pl.BlockSpec((None, None, rows, D_MODEL), lambda l, j: (l, j, 0, 0)),
        out_shape=jax.ShapeDtypeStruct((DEPTH, N_MOD, rows, D_MODEL), F32),
        compiler_params=pltpu.CompilerParams(dimension_semantics=("arbitrary", "arbitrary"),
                                             vmem_limit_bytes=VMEM_LIMIT),
        name="adaln_mod",
    )(c_all, w_mod, b_mod.reshape(DEPTH, N_MOD, 1, D_MODEL))


def _swiglu_tile(x, g_norm, shift, scale, gate, wg_ref, wu_ref, wd_ref, nf_ref, final):
    h = _norm_mod(x, g_norm, shift, scale).astype(BF16)
    a = (_silu(_dot(h, wg_ref[...])) * _dot(h, wu_ref[...])).astype(BF16)
    y = x + (0.5 * gate) * _dot(a, wd_ref[...])
    if final:
        y = (y * lax.rsqrt(jnp.mean(y * y, axis=-1, keepdims=True) + EPS)) * nf_ref[...]
    return y


def _ffn_kernel(xp_ref, xs_ref, modp_ref, mods_ref, n_ref, wg_ref, wu_ref, wd_ref, nf_ref, op_ref, os_ref, *,
                tiles_per_seq, prompt_tiles, sample_reps, final):
    i = pl.program_id(0)

    @pl.when(i < prompt_tiles)
    def _():
        row = i // tiles_per_seq
        shift, scale, gate = (modp_ref[j, pl.ds(row, 1), :] for j in range(3))
        op_ref[...] = _swiglu_tile(xp_ref[...], n_ref[...], shift, scale, gate, wg_ref, wu_ref, wd_ref, nf_ref, final)

    @pl.when(i == prompt_tiles)
    def _():
        shift, scale, gate = (jnp.concatenate([mods_ref[j]] * sample_reps, axis=0) for j in range(3))
        os_ref[...] = _swiglu_tile(xs_ref[...], n_ref[...], shift, scale, gate, wg_ref, wu_ref, wd_ref, nf_ref, final)


def _ffn(xp, xs, mod, norm, wg, wu, wd, norm_final, layer, j0, *, tm, seqs_p, seqs_s, final=False):
    rows_p, rows_s = xp.shape[0], xs.shape[0]
    prompt_tiles = rows_p // tm
    kern = functools.partial(_ffn_kernel, tiles_per_seq=(rows_p // seqs_p) // tm, prompt_tiles=prompt_tiles,
                             sample_reps=rows_s // seqs_s, final=final)
    ptile = lambda i: (jnp.minimum(i, prompt_tiles - 1), 0)
    weight = lambda *shape: pl.BlockSpec((None,) + shape, lambda i: (layer, 0, 0), pipeline_mode=pl.Buffered(1))
    return pl.pallas_call(
        kern,
        grid=(prompt_tiles + 1,),
        in_specs=[
            pl.BlockSpec((tm, D_MODEL), ptile),
            pl.BlockSpec((rows_s, D_MODEL), lambda i: (0, 0)),
            pl.BlockSpec((None, 3, seqs_p, D_MODEL), lambda i: (layer, j0 // 3, seqs_s // seqs_p, 0)),
            pl.BlockSpec((None, 3, seqs_s, D_MODEL), lambda i: (layer, j0 // 3, 0, 0)),
            pl.BlockSpec((None, 1, D_MODEL), lambda i: (layer, 0, 0)),
            weight(D_MODEL, D_FF),
            weight(D_MODEL, D_FF),
            weight(D_FF, D_MODEL),
            pl.BlockSpec((1, D_MODEL), lambda i: (0, 0)),
        ],
        out_specs=[pl.BlockSpec((tm, D_MODEL), ptile), pl.BlockSpec((rows_s, D_MODEL), lambda i: (0, 0))],
        out_shape=[jax.ShapeDtypeStruct((rows_p, D_MODEL), F32), jax.ShapeDtypeStruct((rows_s, D_MODEL), F32)],
        compiler_params=pltpu.CompilerParams(dimension_semantics=("arbitrary",), vmem_limit_bytes=VMEM_LIMIT),
        name="swiglu_half_step",
    )(xp, xs, mod, mod, norm.reshape(DEPTH, 1, D_MODEL), wg, wu, wd, norm_final.reshape(1, D_MODEL))


def _lower_bound_kernel(x_ref, o_ref):
    rows = [x_ref[l:l + 1, :] for l in range(DEPTH)]
    m = functools.reduce(jnp.maximum, rows)
    e = [jnp.exp(r - m) for r in rows]
    s = functools.reduce(lambda a, b: a + b, e)
    sm = [ei / s for ei in e]
    acc = sm[0]
    for l in range(DEPTH):
        if l > 0:
            acc = acc + sm[l]
        o_ref[l:l + 1, :] = acc - sm[0]


def _lower_bounds(hgrn_lower):
    return pl.pallas_call(
        _lower_bound_kernel,
        out_shape=jax.ShapeDtypeStruct((DEPTH, HG_WIDTH), F32),
        name="hgrn_lower_bounds",
    )(hgrn_lower)


def _bucket_map(rows, cols, valid_rows, col_lo, col_hi):
    r = np.arange(rows)[:, None]
    c = np.arange(cols)[None, :]
    rel = c - WINDOW - r
    n = np.maximum(-rel, 0)
    exact = REL_BUCKETS // 2
    nf = np.maximum(n, 1).astype(np.float32)
    large = exact + (np.log(nf / np.float32(exact)) / np.float32(math.log(REL_MAX_DIST / exact))
                     * np.float32(REL_BUCKETS - exact)).astype(np.int32)
    large = np.minimum(large, REL_BUCKETS - 1)
    bucket = np.where(n < exact, n, large)
    valid = (rel <= 0) & (rel > -WINDOW) & (r < valid_rows) & (c >= col_lo) & (c < col_hi)
    return np.where(valid, bucket, -1).astype(np.int32)


def _bias_table_kernel(bkt_ref, rb_ref, o_ref):
    h = pl.program_id(1)
    bkt = bkt_ref[...]
    acc = jnp.full(bkt.shape, NEG, F32)
    for b in range(REL_BUCKETS):
        acc = jnp.where(bkt == b, rb_ref[b, h], acc)
    o_ref[...] = acc


def _bias_tables(bucket_maps, rel_bias):
    m, r, c = bucket_maps.shape
    return pl.pallas_call(
        _bias_table_kernel,
        grid=(m, ATT_HEADS),
        in_specs=[pl.BlockSpec((None, r, c), lambda i, h: (i, 0, 0)),
                  pl.BlockSpec(memory_space=pltpu.SMEM)],
        out_specs=pl.BlockSpec((None, None, r, c), lambda i, h: (i, h, 0, 0)),
        out_shape=jax.ShapeDtypeStruct((m, ATT_HEADS, r, c), F32),
        compiler_params=pltpu.CompilerParams(dimension_semantics=("arbitrary", "arbitrary")),
        name="rel_bias_tables",
    )(bucket_maps, rel_bias)


def _block_ones(n, blk):
    i = np.arange(n)
    return (i[:, None] // blk == i[None, :] // blk)


def _mixer_consts(tt):
    tri = jnp.asarray(np.tril(np.ones((tt, tt), np.float32)), BF16)
    blk = _block_ones(HG_WIDTH, HG_DK)
    return tri, jnp.asarray(blk, BF16), jnp.asarray(blk, F32)


def _pool_block_diag(pool_w):
    out = jnp.zeros((DEPTH, POOL_WIDTH, POOL_WIDTH), BF16)
    for g in range(len(POOL_WINDOWS)):
        sl = slice(g * POOL_GW, (g + 1) * POOL_GW)
        out = out.at[:, sl, sl].set(pool_w[:, g].astype(BF16))
    return out


def _head_masks():
    lane = lax.broadcasted_iota(jnp.int32, (1, HG_WIDTH), 1)
    return [jnp.where((lane >= HG_DK * h) & (lane < HG_DK * (h + 1)), 1.0, 0.0).astype(F32) for h in range(HG_HEADS)]


def _prompt_mixer_kernel(x_ref, xn_ref, mod_ref, n_ref, win_ref, wout_ref, pbd_ref, pscale_ref, lb_ref, hgn_ref,
                         bm_ref, sink_ref, tri_ref, ones_ref, bdm_ref,
                         xo_ref, pool_o, s_o, k_o, v_o,
                         za_ref, zb_ref, ubuf, kbuf, vbuf, st_ref, cat_ref, *, layer, tt, nt, ntiles):
    b = pl.program_id(0)
    p = pl.program_id(1)
    last_pair = pl.num_programs(1) - 1
    nxt = jnp.minimum(b * nt + 2 * p + 2, ntiles - 1)
    bn = nxt // nt
    n_chunks = 8
    cw = IN_WIDTH // n_chunks

    def projector(x_rows, bsel, dst_ref):
        h = _norm_mod(x_rows, n_ref[...], mod_ref[3, pl.ds(bsel, 1), :], mod_ref[4, pl.ds(bsel, 1), :]).astype(BF16)
        state = {"k": 0}

        def step():
            k = state["k"]
            if k < n_chunks:
                dst_ref[:, k * cw:(k + 1) * cw] = _dot(h, win_ref[:, k * cw:(k + 1) * cw])
                state["k"] = k + 1

        return step

    @pl.when((b == 0) & (p == 0))
    def _():
        first_proj = projector(x_ref[0:tt, :], b, za_ref)
        for _ in range(n_chunks):
            first_proj()

    @pl.when(p == 0)
    def _():
        ubuf[0:16, :] = jnp.zeros((16, POOL_WIDTH), F32)
        kbuf[:, 0:WINDOW, :] = jnp.zeros((KV_HEADS, WINDOW, HEAD_DIM), BF16)
        vbuf[:, 0:WINDOW, :] = jnp.zeros((KV_HEADS, WINDOW, HEAD_DIM), BF16)
        st_ref[...] = jnp.zeros((HG_WIDTH, HG_WIDTH), F32)

    gate = mod_ref[5, pl.ds(b, 1), :]

    def mix(z_ref, x, row0, t, side):
        lane = lax.broadcasted_iota(jnp.int32, (1, POOL_WIDTH), 1)

        u = z_ref[:, OFF_POOL:OFF_POOL + POOL_WIDTH]
        ubuf[16:16 + tt, :] = u
        e = ubuf[...]
        s2 = e + pltpu.roll(e, 1, 0)
        s4 = s2 + pltpu.roll(s2, 2, 0)
        s8 = s4 + pltpu.roll(s4, 4, 0)
        s16 = s8 + pltpu.roll(s8, 8, 0)
        sel = jnp.where(lane < 64, s2, jnp.where(lane < 128, s4, jnp.where(lane < 192, s8, s16)))[16:]
        wl = jnp.where(lane < 64, 2.0, jnp.where(lane < 128, 4.0, jnp.where(lane < 192, 8.0, 16.0))).astype(F32)
        pos1 = (t * tt + 1 + lax.broadcasted_iota(jnp.int32, (tt, 1), 0)).astype(F32)
        dpool = sel / jnp.minimum(pos1, wl) - u
        ypool = _dot(dpool.astype(BF16), pbd_ref[...]) * pscale_ref[...]
        cat_ref[:, 0:POOL_WIDTH] = ypool.astype(BF16)
        ubuf[0:16, :] = ubuf[tt:tt + 16, :]
        side()


        for g in range(KV_HEADS):
            kbuf[g, WINDOW:WINDOW + tt, :] = z_ref[:, OFF_AK + HEAD_DIM * g:OFF_AK + HEAD_DIM * (g + 1)].astype(BF16)
            vbuf[g, WINDOW:WINDOW + tt, :] = z_ref[:, OFF_AV + HEAD_DIM * g:OFF_AV + HEAD_DIM * (g + 1)].astype(BF16)
        first = jnp.where(t == 0, 1, 0)
        bands = [(j, g) for j in range(tt // WINDOW) for g in range(KV_HEADS)]
        logits = []
        for j, g in bands:
            r0 = j * WINDOW
            c0 = OFF_AQ + HEAD_DIM * g * GQA_GROUP
            qg = jnp.concatenate([z_ref[r0:r0 + WINDOW, c0 + HEAD_DIM * i:c0 + HEAD_DIM * (i + 1)]
                                  for i in range(GQA_GROUP)], axis=0)
            qg = (qg * (HEAD_DIM ** -0.5)).astype(BF16)
            logits.append(_dot_nt(qg, kbuf[g, r0:r0 + 2 * WINDOW, :]))
        side()

        lb = lb_ref[...]
        q = _silu(z_ref[:, OFF_HQ:OFF_HQ + HG_WIDTH])
        f = lb + (1.0 - lb) * _sigmoid(z_ref[:, OFF_HF:OFF_HF + HG_WIDTH])
        k = 1.0 - f
        v = z_ref[:, OFF_HI:OFF_HI + HG_WIDTH]
        vb = v.astype(BF16)
        tri = tri_ref[...]
        hi, mid, lo = _split3(jnp.log(f))
        bcum = _dot(tri, hi) + _dot(tri, mid) + _dot(tri, lo)
        side()
        ones_blk = ones_ref[...]
        hm = _head_masks()

        n16 = tt // 16
        half_shape = (n16, 2, 8, HG_WIDTH)
        q4 = q.reshape(half_shape)
        b4 = bcum.reshape(half_shape)
        c4 = (bcum - jnp.log(k)).reshape(half_shape)
        v4 = v.reshape(half_shape)
        rowi = lax.broadcasted_iota(jnp.int32, (1, 8, 1), 1)
        o_up = jnp.zeros((n16, 8, HG_WIDTH), F32)
        o_dn = jnp.zeros((n16, 8, HG_WIDTH), F32)
        nh = n16 * 8
        for s in range(8):
            cs = c4[:, 0, s:s + 1, :]
            vs = v4[:, 0, s:s + 1, :]
            x_up = jnp.where(rowi >= s, q4[:, 0] * jnp.exp(b4[:, 0] - cs), 0.0)
            x_dn = q4[:, 1] * jnp.exp(b4[:, 1] - cs)
            xs = jnp.concatenate([x_up.reshape(nh, HG_WIDTH), x_dn.reshape(nh, HG_WIDTH)], axis=0)
            r = _dot(xs.astype(BF16), ones_blk)
            o_up = o_up + r[0:nh].reshape(n16, 8, HG_WIDTH) * vs
            o_dn = o_dn + r[nh:2 * nh].reshape(n16, 8, HG_WIDTH) * vs
            if s % 4 == 3:
                side()
        for s in range(8):
            cs = c4[:, 1, s:s + 1, :]
            x_dn = jnp.where(rowi >= s, q4[:, 1] * jnp.exp(b4[:, 1] - cs), 0.0)
            r = _dot(x_dn.reshape(nh, HG_WIDTH).astype(BF16), ones_blk)
            o_dn = o_dn + r.reshape(n16, 8, HG_WIDTH) * v4[:, 1, s:s + 1, :]
            if s % 4 == 3:
                side()
        o_tot = jnp.concatenate([o_up[:, None], o_dn[:, None]], axis=1).reshape(tt, HG_WIDTH)

        blocks = []
        blk = 32
        while blk <= tt:
            blocks += [(i * blk, i * blk + blk // 2, (i + 1) * blk) for i in range(tt // blk)]
            blk *= 2
        scores = []
        for lo_s, mid_s, hi_s in blocks:
            ref = bcum[mid_s - 1:mid_s, :]
            qt = q[mid_s:hi_s] * jnp.exp(bcum[mid_s:hi_s] - ref)
            kt = (k[lo_s:mid_s] * jnp.exp(ref - bcum[lo_s:mid_s])).astype(BF16)
            qs = jnp.concatenate([qt * hm[hh] for hh in range(HG_HEADS)], axis=0).astype(BF16)
            scores.append(_dot_nt(qs, kt).astype(BF16))

        st = st_ref[...]
        qin = (q * jnp.exp(bcum)).astype(BF16)
        o_tot = o_tot + _dot_nt(qin, st.astype(BF16))
        blast = bcum[tt - 1:tt, :]
        kout = (k * jnp.exp(blast - bcum)).astype(BF16)
        st_ref[...] = st * jnp.exp(blast) + _dot_tn(vb, kout) * bdm_ref[...]
        side()

        probs, dens = [], []
        for (j, g), lg_all in zip(bands, logits):
            h0 = g * GQA_GROUP
            bm = bm_ref[first, h0:h0 + GQA_GROUP] if j == 0 else bm_ref[0, h0:h0 + GQA_GROUP]
            ps = []
            for i in range(GQA_GROUP):
                lg = lg_all[i * WINDOW:(i + 1) * WINDOW] + bm[i]
                sink = sink_ref[layer, h0 + i]
                m = jnp.maximum(jnp.max(lg, axis=-1, keepdims=True), sink)
                p_ = jnp.exp(lg - m)
                dens.append(jnp.sum(p_, axis=-1, keepdims=True) + jnp.exp(sink - m))
                ps.append(p_.astype(BF16))
            probs.append(jnp.concatenate(ps, axis=0))
        for n, (j, g) in enumerate(bands):
            r0 = j * WINDOW
            og = _dot(probs[n], vbuf[g, r0:r0 + 2 * WINDOW, :])
            for i in range(GQA_GROUP):
                cc = POOL_WIDTH + HG_WIDTH + HEAD_DIM * (g * GQA_GROUP + i)
                oh = og[i * WINDOW:(i + 1) * WINDOW] / dens[n * GQA_GROUP + i]
                cat_ref[r0:r0 + WINDOW, cc:cc + HEAD_DIM] = oh.astype(BF16)
        for g in range(KV_HEADS):
            kbuf[g, 0:WINDOW, :] = kbuf[g, tt:tt + WINDOW, :]
            vbuf[g, 0:WINDOW, :] = vbuf[g, tt:tt + WINDOW, :]
        side()

        adds = {}
        for (lo_s, mid_s, hi_s), a in zip(blocks, scores):
            half = mid_s - lo_s
            ov = _dot(a, vb[lo_s:mid_s])
            oi = ov[0:half] * hm[0]
            for hh in range(1, HG_HEADS):
                oi = oi + ov[hh * half:(hh + 1) * half] * hm[hh]
            adds.setdefault(2 * half, []).extend([jnp.zeros((half, HG_WIDTH), F32), oi])
        for parts in adds.values():
            o_tot = o_tot + jnp.concatenate(parts, axis=0)

        sq_hi, sq_mid, _ = _split3(o_tot * o_tot)
        ms = (_dot(sq_hi, ones_blk) + _dot(sq_mid, ones_blk)) * (1.0 / HG_DK)
        y_hg = o_tot * lax.rsqrt(ms + EPS) * hgn_ref[...] * _silu(z_ref[:, OFF_HG:OFF_HG + HG_WIDTH])
        cat_ref[:, POOL_WIDTH:POOL_WIDTH + HG_WIDTH] = y_hg.astype(BF16)

        xo_ref[row0:row0 + tt, :] = x + gate * _dot(cat_ref[...], wout_ref[...])

        for _ in range(n_chunks):
            side()

    mix(za_ref, x_ref[0:tt, :], 0, 2 * p, projector(x_ref[tt:2 * tt, :], b, zb_ref))
    mix(zb_ref, x_ref[tt:2 * tt, :], tt, 2 * p + 1, projector(xn_ref[...], bn, za_ref))

    @pl.when(p == last_pair)
    def _():
        pool_o[...] = ubuf[pl.ds(1, POOL_BUF), :]
        s_kv = st_ref[...].T
        for hh in range(HG_HEADS):
            s_o[hh] = s_kv[HG_DK * hh:HG_DK * (hh + 1), HG_DK * hh:HG_DK * (hh + 1)]
        k_o[...] = zb_ref[tt - WINDOW:tt, OFF_AK:OFF_AV]
        v_o[...] = zb_ref[tt - WINDOW:tt, OFF_AV:IN_WIDTH]


def _prompt_mixer(x, mod, norm, w_in, w_out, pool_bd, pool_scale, lbs, hg_norm, bm, sinks, consts, layer, *,
                  tt, mod_row_block):
    bsz, seq, _ = x.shape
    tri, ones_blk, bdm = consts
    nt = seq // tt
    ntiles = bsz * nt
    kern = functools.partial(_prompt_mixer_kernel, layer=layer, tt=tt, nt=nt, ntiles=ntiles)
    full = lambda *shape: pl.BlockSpec(shape, lambda b, p: (0,) * len(shape))
    lsel = lambda *shape: pl.BlockSpec((None,) + shape, lambda b, p: (layer,) + (0,) * len(shape))
    mix_pair = lambda b, p: (b, p, 0)
    mix_seq = lambda *z: (lambda b, p: (b,) + z)

    def next_tile(b, p):
        nxt = jnp.minimum(b * nt + 2 * p + 2, ntiles - 1)
        return (nxt // nt, nxt % nt, 0)

    return pl.pallas_call(
        kern,
        grid=(bsz, nt // 2),
        in_specs=[
            pl.BlockSpec((None, 2 * tt, D_MODEL), mix_pair),
            pl.BlockSpec((None, tt, D_MODEL), next_tile),
            pl.BlockSpec((None, N_MOD, 8, D_MODEL), lambda b, p: (layer, 0, mod_row_block, 0)),
            lsel(1, D_MODEL),
            lsel(D_MODEL, IN_WIDTH),
            lsel(D_MODEL, D_MODEL),
            lsel(POOL_WIDTH, POOL_WIDTH),
            lsel(1, POOL_WIDTH),
            lsel(1, HG_WIDTH),
            lsel(1, HG_WIDTH),
            full(2, ATT_HEADS, WINDOW, 2 * WINDOW),
            pl.BlockSpec(memory_space=pltpu.SMEM),
            full(tt, tt),
            full(HG_WIDTH, HG_WIDTH),
            full(HG_WIDTH, HG_WIDTH),
        ],
        out_specs=[
            pl.BlockSpec((None, 2 * tt, D_MODEL), mix_pair),
            pl.BlockSpec((None, POOL_BUF, POOL_WIDTH), mix_seq(0, 0)),
            pl.BlockSpec((None, HG_HEADS, HG_DK, HG_DK), mix_seq(0, 0, 0)),
            pl.BlockSpec((None, WINDOW, KV_HEADS * HEAD_DIM), mix_seq(0, 0)),
            pl.BlockSpec((None, WINDOW, KV_HEADS * HEAD_DIM), mix_seq(0, 0)),
        ],
        out_shape=[
            jax.ShapeDtypeStruct((bsz, seq, D_MODEL), F32),
            jax.ShapeDtypeStruct((bsz, POOL_BUF, POOL_WIDTH), F32),
            jax.ShapeDtypeStruct((bsz, HG_HEADS, HG_DK, HG_DK), F32),
            jax.ShapeDtypeStruct((bsz, WINDOW, KV_HEADS * HEAD_DIM), F32),
            jax.ShapeDtypeStruct((bsz, WINDOW, KV_HEADS * HEAD_DIM), F32),
        ],
        scratch_shapes=[
            pltpu.VMEM((tt, IN_WIDTH), F32),
            pltpu.VMEM((tt, IN_WIDTH), F32),
            pltpu.VMEM((16 + tt, POOL_WIDTH), F32),
            pltpu.VMEM((KV_HEADS, WINDOW + tt, HEAD_DIM), BF16),
            pltpu.VMEM((KV_HEADS, WINDOW + tt, HEAD_DIM), BF16),
            pltpu.VMEM((HG_WIDTH, HG_WIDTH), F32),
            pltpu.VMEM((tt, D_MODEL), BF16),
        ],
        compiler_params=pltpu.CompilerParams(dimension_semantics=("arbitrary", "arbitrary"),
                                             vmem_limit_bytes=VMEM_LIMIT),
        name="prompt_mixer",
    )(x, x, mod, norm.reshape(DEPTH, 1, D_MODEL), w_in, w_out, pool_bd, pool_scale.reshape(DEPTH, 1, POOL_WIDTH),
      lbs.reshape(DEPTH, 1, HG_WIDTH), hg_norm, bm, sinks, tri, ones_blk, bdm)


def _sample_mixer_kernel(x_ref, mod_ref, n_ref, win_ref, wout_ref, pbd_ref, pscale_ref, lb_ref, hgn_ref,
                         bms_ref, sinkc_ref, ones_ref, sp_ref, s0_ref, kt_ref, vt_ref,
                         xo_ref, pool_o, s_o, kt_o, vt_o,
                         z_ref, cat_ref, qs_ref, os_ref, kn_ref, vn_ref, qint_ref, koutt_ref, vtt_ref, gt_ref,
                         oin_ref, oacc_ref, *, steps, nb, gb, rb, pos0):
    g = pl.program_id(0)
    last = pl.num_programs(0) - 1
    base = pl.multiple_of(g * gb, gb)
    kvw = KV_HEADS * HEAD_DIM

    def rows(t):
        return slice(t * nb, (t + 1) * nb)

    @pl.when(g == 0)
    def _():
        for t in range(steps):
            h = _norm_mod(x_ref[t], n_ref[...], mod_ref[0], mod_ref[1]).astype(BF16)
            z_ref[rows(t), :] = _dot(h, win_ref[...])
        lane = lax.broadcasted_iota(jnp.int32, (1, POOL_WIDTH), 1)

        ext = [sp_ref[r] for r in range(POOL_BUF)] + [z_ref[rows(t), OFF_POOL:OFF_POOL + POOL_WIDTH] for t in range(steps)]
        for r in range(POOL_BUF):
            pool_o[r] = ext[r + steps]
        sums = {1: ext}
        w = 1
        while w < 16:
            prev = sums[w]
            sums[2 * w] = [None if (r < 2 * w - 1) else prev[r] + prev[r - w] for r in range(len(ext))]
            w *= 2
        dl = []
        for t in range(steps):
            r = POOL_BUF + t
            sel = jnp.where(lane < 64, sums[2][r], jnp.where(lane < 128, sums[4][r], jnp.where(lane < 192, sums[8][r], sums[16][r])))
            cnt = [float(min(pos0 + t + 1, wd)) for wd in POOL_WINDOWS]
            cl = jnp.where(lane < 64, cnt[0], jnp.where(lane < 128, cnt[1], jnp.where(lane < 192, cnt[2], cnt[3]))).astype(F32)
            dl.append(sel / cl - ext[r])
        ypool = _dot(jnp.concatenate(dl, axis=0).astype(BF16), pbd_ref[...]) * pscale_ref[...]
        cat_ref[:, 0:POOL_WIDTH] = ypool

        lb = lb_ref[...]
        qv, kv_, vv, bv = [], [], [], []
        bacc = None
        for t in range(steps):
            qv.append(_silu(z_ref[rows(t), OFF_HQ:OFF_HQ + HG_WIDTH]))
            f = lb + (1.0 - lb) * _sigmoid(z_ref[rows(t), OFF_HF:OFF_HF + HG_WIDTH])
            kv_.append(1.0 - f)
            vv.append(z_ref[rows(t), OFF_HI:OFF_HI + HG_WIDTH])
            bacc = jnp.log(f) if bacc is None else bacc + jnp.log(f)
            bv.append(bacc)
        pairs = [(t, s) for t in range(steps) for s in range(t + 1)]
        xs = jnp.concatenate([qv[t] * kv_[s] * jnp.exp(bv[t] - bv[s]) for t, s in pairs], axis=0).astype(BF16)
        rr = _dot(xs, ones_ref[...])
        o_intra = [None] * steps
        for i, (t, s) in enumerate(pairs):
            term = rr[i * nb:(i + 1) * nb] * vv[s]
            o_intra[t] = term if o_intra[t] is None else o_intra[t] + term
        blast = bv[steps - 1]
        gt_ref[...] = jnp.exp(blast).T
        for t in range(steps):
            oin_ref[t] = o_intra[t]
            qint_ref[t] = (qv[t] * jnp.exp(bv[t])).T
            koutt_ref[t] = (kv_[t] * jnp.exp(blast - bv[t])).T
            vtt_ref[t] = vv[t].T
        oacc_ref[...] = jnp.zeros(oacc_ref.shape, F32)

        zeros64 = jnp.zeros((nb, HEAD_DIM), F32)
        kn_ref[...] = jnp.zeros(kn_ref.shape, F32)
        vn_ref[...] = jnp.zeros(vn_ref.shape, F32)
        for t in range(steps):
            kn_ref[:, t, :] = z_ref[rows(t), OFF_AK:OFF_AK + kvw]
            vn_ref[:, t, :] = z_ref[rows(t), OFF_AV:OFF_AV + kvw]
            for hd in range(ATT_HEADS):
                c0 = OFF_AQ + HEAD_DIM * hd
                qsl = z_ref[rows(t), c0:c0 + HEAD_DIM] * (HEAD_DIM ** -0.5)
                parts = [qsl, zeros64] if hd < GQA_GROUP else [zeros64, qsl]
                qs_ref[:, ATT_HEADS * t + hd, :] = jnp.concatenate(parts, axis=1)

    r0 = pl.multiple_of(g * rb, rb)
    hrow = pl.multiple_of((g * rb // HG_DK) * HG_DK, HG_DK)
    vts = [vtt_ref[t, pl.ds(hrow, HG_DK), :] for t in range(steps)]
    accs = [oacc_ref[t, pl.ds(hrow, HG_DK), :] for t in range(steps)]
    for i in range(rb):
        s0 = s0_ref[i]
        snew = gt_ref[pl.ds(r0 + i, 1), :] * s0
        for t in range(steps):
            accs[t] = accs[t] + qint_ref[t, pl.ds(r0 + i, 1), :] * s0
            snew = snew + koutt_ref[t, pl.ds(r0 + i, 1), :] * vts[t]
        s_o[i] = snew
    for t in range(steps):
        oacc_ref[t, pl.ds(hrow, HG_DK), :] = accs[t]

    bm_old = bms_ref[:, 0:WINDOW]
    bm_new = bms_ref[:, WINDOW:WINDOW + 8]
    sinkc = sinkc_ref[...]
    seqs = range(gb)
    qn = [qs_ref[base + n].astype(BF16) for n in seqs]
    l_old = [_dot(qn[n], kt_ref[n].astype(BF16)) for n in seqs]
    l_new = [_dot_nt(qn[n], kn_ref[base + n].astype(BF16)) for n in seqs]
    def new_columns(ref):
        new = ref[pl.ds(base, gb)].reshape(gb * 8, kvw)
        if gb * 8 < WINDOW:
            new = jnp.concatenate([new, jnp.zeros((WINDOW - gb * 8, kvw), F32)], axis=0)
        return new.T

    knt = new_columns(kn_ref)
    vnt = new_columns(vn_ref)
    for n in seqs:
        kt_o[n] = jnp.concatenate([kt_ref[n][:, steps:], knt[:, 8 * n:8 * n + steps]], axis=1)
        vt_o[n] = jnp.concatenate([vt_ref[n][:, steps:], vnt[:, 8 * n:8 * n + steps]], axis=1)
    p_old, p_new, den = [], [], []
    for n in seqs:
        lo_, ln_ = l_old[n] + bm_old, l_new[n] + bm_new
        m = jnp.maximum(jnp.maximum(jnp.max(lo_, axis=-1, keepdims=True), jnp.max(ln_, axis=-1, keepdims=True)), sinkc)
        po, pn = jnp.exp(lo_ - m), jnp.exp(ln_ - m)
        den.append(jnp.sum(po, axis=-1, keepdims=True) + jnp.sum(pn, axis=-1, keepdims=True) + jnp.exp(sinkc - m))
        p_old.append(po.astype(BF16))
        p_new.append(pn.astype(BF16))
    for n in seqs:
        o = _dot_nt(p_old[n], vt_ref[n].astype(BF16)) + _dot(p_new[n], vn_ref[base + n].astype(BF16))
        os_ref[n] = o / den[n]
    for t in range(steps):
        for hd in range(ATT_HEADS):
            c0 = HEAD_DIM * (hd // GQA_GROUP)
            cc = POOL_WIDTH + HG_WIDTH + HEAD_DIM * hd
            cat_ref[pl.ds(t * nb + base, gb), cc:cc + HEAD_DIM] = os_ref[:, ATT_HEADS * t + hd, c0:c0 + HEAD_DIM]

    @pl.when(g == last)
    def _():
        ones_blk = ones_ref[...]
        for t in range(steps):
            o = oin_ref[t] + oacc_ref[t].T
            sq_hi, sq_mid, _ = _split3(o * o)
            ms = (_dot(sq_hi, ones_blk) + _dot(sq_mid, ones_blk)) * (1.0 / HG_DK)
            gate_t = _silu(z_ref[rows(t), OFF_HG:OFF_HG + HG_WIDTH])
            cat_ref[rows(t), POOL_WIDTH:POOL_WIDTH + HG_WIDTH] = o * lax.rsqrt(ms + EPS) * hgn_ref[...] * gate_t
        for t in range(steps):
            y = _dot(cat_ref[rows(t), :].astype(BF16), wout_ref[...])
            xo_ref[t] = x_ref[t] + mod_ref[2] * y


N_SAMPLE_MIXER_INPUTS = 16


def _sample_mixer_kernel_into(*refs, n_stacks, **kw):
    _sample_mixer_kernel(*refs[:N_SAMPLE_MIXER_INPUTS], *refs[N_SAMPLE_MIXER_INPUTS + n_stacks:], **kw)


def _sample_mixer(x, mod, norm, w_in, w_out, pool_bd, pool_scale, lbs, hg_norm, bms, sink_col, ones_blk,
                  pool_hist, hgrn_state, cache_kt, cache_vt, stacks, layer, *, gb, pos0):
    steps, nb, _ = x.shape
    rows_th = steps * ATT_HEADS
    ngrp = nb // gb
    rb = HG_WIDTH // ngrp
    kvw = KV_HEADS * HEAD_DIM
    stacks = () if stacks is None else tuple(stacks)
    kern = functools.partial(_sample_mixer_kernel_into, n_stacks=len(stacks), steps=steps, nb=nb, gb=gb, rb=rb,
                             pos0=pos0)
    full = lambda *shape: pl.BlockSpec(shape, lambda g: (0,) * len(shape))
    lsel = lambda *shape: pl.BlockSpec((None,) + shape, lambda g: (layer,) + (0,) * len(shape))
    grp = lambda *shape: pl.BlockSpec((None,) + shape, lambda g: (layer, g) + (0,) * (len(shape) - 1))
    return pl.pallas_call(
        kern,
        grid=(ngrp,),
        in_specs=[
            full(steps, nb, D_MODEL),
            pl.BlockSpec((None, 3, nb, D_MODEL), lambda g: (layer, 1, 0, 0)),
            lsel(1, D_MODEL),
            lsel(D_MODEL, IN_WIDTH),
            lsel(D_MODEL, D_MODEL),
            lsel(POOL_WIDTH, POOL_WIDTH),
            lsel(1, POOL_WIDTH),
            lsel(1, HG_WIDTH),
            lsel(1, HG_WIDTH),
            full(rows_th, 2 * WINDOW),
            lsel(rows_th, 1),
            full(HG_WIDTH, HG_WIDTH),
            lsel(POOL_BUF, nb, POOL_WIDTH),
            grp(rb, HG_DK, nb),
            grp(gb, kvw, WINDOW),
            grp(gb, kvw, WINDOW),
        ] + [pl.BlockSpec(memory_space=pl.ANY)] * len(stacks),
        out_specs=[
            full(steps, nb, D_MODEL),
            lsel(POOL_BUF, nb, POOL_WIDTH),
            grp(rb, HG_DK, nb),
            grp(gb, kvw, WINDOW),
            grp(gb, kvw, WINDOW),
        ],
        out_shape=[
            jax.ShapeDtypeStruct((steps, nb, D_MODEL), F32),
            jax.ShapeDtypeStruct((DEPTH, POOL_BUF, nb, POOL_WIDTH), F32),
            jax.ShapeDtypeStruct((DEPTH, HG_WIDTH, HG_DK, nb), F32),
            jax.ShapeDtypeStruct((DEPTH, nb, kvw, WINDOW), F32),
            jax.ShapeDtypeStruct((DEPTH, nb, kvw, WINDOW), F32),
        ],
        input_output_aliases={N_SAMPLE_MIXER_INPUTS + i: 1 + i for i in range(len(stacks))},
        scratch_shapes=[
            pltpu.VMEM((steps * nb, IN_WIDTH), F32),
            pltpu.VMEM((steps * nb, D_MODEL), F32),
            pltpu.VMEM((nb, rows_th, kvw), F32),
            pltpu.VMEM((gb, rows_th, kvw), F32),
            pltpu.VMEM((nb, 8, kvw), F32),
            pltpu.VMEM((nb, 8, kvw), F32),
            pltpu.VMEM((steps, HG_WIDTH, nb), F32),
            pltpu.VMEM((steps, HG_WIDTH, nb), F32),
            pltpu.VMEM((steps, HG_WIDTH, nb), F32),
            pltpu.VMEM((HG_WIDTH, nb), F32),
            pltpu.VMEM((steps, nb, HG_WIDTH), F32),
            pltpu.VMEM((steps, HG_WIDTH, nb), F32),
        ],
        compiler_params=pltpu.CompilerParams(dimension_semantics=("arbitrary",), vmem_limit_bytes=VMEM_LIMIT),
        name="sample_mixer",
    )(x, mod, norm.reshape(DEPTH, 1, D_MODEL), w_in, w_out, pool_bd, pool_scale.reshape(DEPTH, 1, POOL_WIDTH),
      lbs.reshape(DEPTH, 1, HG_WIDTH), hg_norm, bms, sink_col, ones_blk, pool_hist, hgrn_state, cache_kt, cache_vt,
      *stacks)


FFN_TILE = 512
MIXER_TILE = 256
SAMPLE_GROUP = 8
PAST_LEN = 8192


def kernel(x_prompt, x_sample, c_prompt, c_sample, state_pool, state_hgrn, cache_k_win, cache_v_win, norm_ffn1, norm_mix, norm_ffn2, w_mod, b_mod, ffn1_w_gate, ffn1_w_up, ffn1_w_down, w_in, w_out, pool_w, pool_scale, hgrn_lower, hgrn_norm, attn_sinks, rel_bias, ffn2_w_gate, ffn2_w_up, ffn2_w_down, norm_final):
    bsz, seq, _ = x_prompt.shape
    nb, steps, _ = x_sample.shape
    kvw = KV_HEADS * HEAD_DIM

    mod = _modulation(jnp.concatenate([c_sample, c_prompt], axis=0), w_mod, b_mod)
    prompt_mod_block = nb // 8
    lbs = _lower_bounds(hgrn_lower)
    maps_p = np.stack([_bucket_map(WINDOW, 2 * WINDOW, WINDOW, 0, 2 * WINDOW),
                       _bucket_map(WINDOW, 2 * WINDOW, WINDOW, WINDOW, 2 * WINDOW)])
    bm_p = _bias_tables(jnp.asarray(maps_p), rel_bias)
    maps_s = _bucket_map(8, 2 * WINDOW, steps, 0, WINDOW + steps)[None]
    bm_s = _bias_tables(jnp.asarray(maps_s), rel_bias)[0]
    bm_s = jnp.swapaxes(bm_s, 0, 1).reshape(8 * ATT_HEADS, 2 * WINDOW)[:steps * ATT_HEADS]
    sink_col = jnp.tile(attn_sinks, (1, steps)).reshape(DEPTH, steps * ATT_HEADS, 1)
    consts = _mixer_consts(MIXER_TILE)
    hgn = jnp.tile(hgrn_norm, (1, HG_HEADS)).reshape(DEPTH, 1, HG_WIDTH)
    pool_bd = _pool_block_diag(pool_w)
    w_in_b, w_out_b = w_in.astype(BF16), w_out.astype(BF16)
    f1 = (ffn1_w_gate.astype(BF16), ffn1_w_up.astype(BF16), ffn1_w_down.astype(BF16))
    f2 = (ffn2_w_gate.astype(BF16), ffn2_w_up.astype(BF16), ffn2_w_down.astype(BF16))
    s_pool = jnp.transpose(state_pool, (0, 2, 1, 3))
    s_hgrn = jnp.transpose(state_hgrn, (0, 2, 3, 4, 1)).reshape(DEPTH, HG_WIDTH, HG_DK, nb)
    c_k = jnp.transpose(cache_k_win, (0, 1, 3, 4, 2)).reshape(DEPTH, nb, kvw, WINDOW)
    c_v = jnp.transpose(cache_v_win, (0, 1, 3, 4, 2)).reshape(DEPTH, nb, kvw, WINDOW)

    xp = x_prompt.reshape(bsz * seq, D_MODEL)
    xs = jnp.swapaxes(x_sample, 0, 1).reshape(steps * nb, D_MODEL)
    ffn = functools.partial(_ffn, tm=FFN_TILE, seqs_p=bsz, seqs_s=nb)
    st_p, stacks_s = [], None
    for l in range(DEPTH):
        xp, xs = ffn(xp, xs, mod, norm_ffn1, *f1, norm_final, l, 0)
        xp, *sp = _prompt_mixer(xp.reshape(bsz, seq, D_MODEL), mod, norm_mix, w_in_b, w_out_b, pool_bd, pool_scale,
                                lbs, hgn, bm_p, attn_sinks, consts, l, tt=MIXER_TILE, mod_row_block=prompt_mod_block)
        xs, *stacks_s = _sample_mixer(xs.reshape(steps, nb, D_MODEL), mod, norm_mix, w_in_b, w_out_b, pool_bd,
                                      pool_scale, lbs, hgn, bm_s, sink_col, consts[1], s_pool, s_hgrn, c_k, c_v,
                                      stacks_s, l, gb=SAMPLE_GROUP, pos0=PAST_LEN)
        st_p.append(sp)
        last = l == DEPTH - 1
        xp, xs = ffn(xp.reshape(bsz * seq, D_MODEL), xs.reshape(steps * nb, D_MODEL), mod, norm_ffn2, *f2,
                     norm_final, l, 6, final=last)

    y_prompt = xp.reshape(bsz, seq, D_MODEL)
    y_sample = jnp.swapaxes(xs.reshape(steps, nb, D_MODEL), 0, 1)

    def stacked(states, i, shape):
        return jnp.stack([s[i] for s in states]).reshape((DEPTH,) + shape)

    outs = [stacked(st_p, 0, (bsz, POOL_BUF, POOL_WIDTH)),
            stacked(st_p, 1, (bsz, HG_HEADS, HG_DK, HG_DK)),
            stacked(st_p, 2, (bsz, WINDOW, KV_HEADS, HEAD_DIM)),
            stacked(st_p, 3, (bsz, WINDOW, KV_HEADS, HEAD_DIM)),
            jnp.transpose(stacks_s[0], (0, 2, 1, 3)),
            jnp.transpose(stacks_s[1].reshape(DEPTH, HG_HEADS, HG_DK, HG_DK, nb), (0, 4, 1, 2, 3)),
            jnp.transpose(stacks_s[2].reshape(DEPTH, nb, KV_HEADS, HEAD_DIM, WINDOW), (0, 1, 4, 2, 3)),
            jnp.transpose(stacks_s[3].reshape(DEPTH, nb, KV_HEADS, HEAD_DIM, WINDOW), (0, 1, 4, 2, 3))]
    return (y_prompt, y_sample, *outs)
```

```python
import functools
import math

import numpy as np
import jax
import jax.numpy as jnp
from jax import lax
from jax.experimental import pallas as pl
from jax.experimental.pallas import tpu as pltpu

F32 = jnp.float32
BF16 = jnp.bfloat16

D_MODEL = 1024
DEPTH = 4
N_MOD = 9
D_FF = 2816
EPS = 1e-6
POOL_WIDTH = 256
POOL_WINDOWS = (2, 4, 8, 16)
POOL_GW = 64
POOL_BUF = 15
HG_WIDTH = 256
HG_HEADS = 4
HG_DK = 64
ATT_WIDTH = 512
ATT_HEADS = 8
KV_HEADS = 2
GQA_GROUP = 4
HEAD_DIM = 64
WINDOW = 128
REL_BUCKETS = 32
REL_MAX_DIST = 128
OFF_POOL, OFF_HQ, OFF_HF, OFF_HI, OFF_HG, OFF_AQ, OFF_AK, OFF_AV = 0, 256, 512, 768, 1024, 1280, 1792, 1920
IN_WIDTH = 2048
NEG = -1e30

VMEM_LIMIT = 56 * 1024 * 1024


def _dot(a, b):
    return jnp.dot(a, b, preferred_element_type=F32)


def _dot_nt(a, b):
    return lax.dot_general(a, b, (((1,), (1,)), ((), ())), preferred_element_type=F32)


def _dot_tn(a, b):
    return lax.dot_general(a, b, (((0,), (0,)), ((), ())), preferred_element_type=F32)


def _sigmoid(x):
    return 1.0 / (1.0 + jnp.exp(-x))


def _silu(x):
    return x * _sigmoid(x)


def _norm_mod(x, g, shift, scale):
    r = lax.rsqrt(jnp.mean(x * x, axis=-1, keepdims=True) + EPS)
    return (x * r) * g * (1.0 + scale) + shift


def _split3(x):
    hi = x.astype(BF16)
    r1 = x - hi.astype(F32)
    mid = r1.astype(BF16)
    lo = (r1 - mid.astype(F32)).astype(BF16)
    return hi, mid, lo


def _mod_kernel(c_ref, w_ref, b_ref, o_ref):
    sc = _silu(c_ref[...]).astype(BF16)
    o_ref[...] = _dot(sc, w_ref[...].astype(BF16)) + b_ref[...]


def _modulation(c_all, w_mod, b_mod):
    rows = c_all.shape[0]
    return pl.pallas_call(
        _mod_kernel,
        grid=(DEPTH, N_MOD),
        in_specs=[
            pl.BlockSpec((rows, D_MODEL), lambda l, j: (0, 0)),
            pl.BlockSpec((None, D_MODEL, D_MODEL), lambda l, j: (l, 0, j)),
            pl.BlockSpec((None, None, 1, D_MODEL), lambda l, j: (l, j, 0, 0)),
        ],
        out_specs=pl.BlockSpec((None, None, rows, D_MODEL), lambda l, j: (l, j, 0, 0)),
        out_shape=jax.ShapeDtypeStruct((DEPTH, N_MOD, rows, D_MODEL), F32),
        compiler_params=pltpu.CompilerParams(dimension_semantics=("arbitrary", "arbitrary"),
                                             vmem_limit_bytes=VMEM_LIMIT),
        name="adaln_mod",
    )(c_all, w_mod, b_mod.reshape(DEPTH, N_MOD, 1, D_MODEL))


def _swiglu_tile(x, g_norm, shift, scale, gate, wg_ref, wu_ref, wd_ref, nf_ref, final):
    h = _norm_mod(x, g_norm, shift, scale).astype(BF16)
    a = (_silu(_dot(h, wg_ref[...])) * _dot(h, wu_ref[...])).astype(BF16)
    y = x + (0.5 * gate) * _dot(a, wd_ref[...])
    if final:
        y = (y * lax.rsqrt(jnp.mean(y * y, axis=-1, keepdims=True) + EPS)) * nf_ref[...]
    return y


DOWN_CAST_ROWS = 128


def _ffn_kernel(xp_ref, xs_ref, modp_ref, mods_ref, n_ref, wg_ref, wu_ref, wd_ref, nf_ref, *rest,
                tiles_per_seq, prompt_tiles, sample_reps, final, cast_layer):
    i = pl.program_id(0)
    if cast_layer is None:
        op_ref, os_ref = rest
    else:
        (ng_hbm, nu_hbm, nd_hbm, op_ref, os_ref, og_hbm, ou_hbm, od_hbm,
         sg_ref, su_ref, sd_ref, tg_ref, tu_ref, td_ref, sem) = rest
        up_rows = D_MODEL // prompt_tiles
        down_steps = D_FF // DOWN_CAST_ROWS

        def up_copies(step, into_vmem):
            rows = pl.ds(pl.multiple_of(step * up_rows, up_rows), up_rows)
            if into_vmem:
                return (pltpu.make_async_copy(ng_hbm.at[cast_layer, rows, :], sg_ref, sem.at[0]),
                        pltpu.make_async_copy(nu_hbm.at[cast_layer, rows, :], su_ref, sem.at[1]))
            return (pltpu.make_async_copy(tg_ref, og_hbm.at[rows, :], sem.at[2]),
                    pltpu.make_async_copy(tu_ref, ou_hbm.at[rows, :], sem.at[3]))

        def down_copy(step, into_vmem):
            rows = pl.ds(pl.multiple_of(step * DOWN_CAST_ROWS, DOWN_CAST_ROWS), DOWN_CAST_ROWS)
            if into_vmem:
                return pltpu.make_async_copy(nd_hbm.at[cast_layer, rows, :], sd_ref, sem.at[4])
            return pltpu.make_async_copy(td_ref, od_hbm.at[rows, :], sem.at[5])

        @pl.when((i >= 1) & (i <= prompt_tiles))
        def _():
            for cp in up_copies(i - 1, False):
                cp.wait()

        @pl.when((i >= 1) & (i <= down_steps))
        def _():
            down_copy(i - 1, False).wait()

        @pl.when(i < prompt_tiles)
        def _():
            for cp in up_copies(i, True):
                cp.start()

        @pl.when(i < down_steps)
        def _():
            down_copy(i, True).start()

    @pl.when(i < prompt_tiles)
    def _():
        row = i // tiles_per_seq
        shift, scale, gate = (modp_ref[j, pl.ds(row, 1), :] for j in range(3))
        op_ref[...] = _swiglu_tile(xp_ref[...], n_ref[...], shift, scale, gate, wg_ref, wu_ref, wd_ref, nf_ref, final)

    @pl.when(i == prompt_tiles)
    def _():
        shift, scale, gate = (jnp.concatenate([mods_ref[j]] * sample_reps, axis=0) for j in range(3))
        os_ref[...] = _swiglu_tile(xs_ref[...], n_ref[...], shift, scale, gate, wg_ref, wu_ref, wd_ref, nf_ref, final)

    if cast_layer is not None:
        @pl.when(i < prompt_tiles)
        def _():
            for cp in up_copies(i, True):
                cp.wait()
            tg_ref[...] = sg_ref[...].astype(BF16)
            tu_ref[...] = su_ref[...].astype(BF16)
            for cp in up_copies(i, False):
                cp.start()

        @pl.when(i < down_steps)
        def _():
            down_copy(i, True).wait()
            td_ref[...] = sd_ref[...].astype(BF16)
            down_copy(i, False).start()


def _ffn(xp, xs, mod, norm, wg, wu, wd, next_weights, norm_final, layer, j0, *, tm, seqs_p, seqs_s, final=False):
    rows_p, rows_s = xp.shape[0], xs.shape[0]
    prompt_tiles = rows_p // tm
    cast_layer = None if next_weights is None else next_weights[3]
    kern = functools.partial(_ffn_kernel, tiles_per_seq=(rows_p // seqs_p) // tm, prompt_tiles=prompt_tiles,
                             sample_reps=rows_s // seqs_s, final=final, cast_layer=cast_layer)
    ptile = lambda i: (jnp.minimum(i, prompt_tiles - 1), 0)
    weight = lambda *shape: pl.BlockSpec(shape, lambda i: (0, 0), pipeline_mode=pl.Buffered(1))
    hbm = pl.BlockSpec(memory_space=pl.ANY)
    in_specs = [
        pl.BlockSpec((tm, D_MODEL), ptile),
        pl.BlockSpec((rows_s, D_MODEL), lambda i: (0, 0)),
        pl.BlockSpec((None, 3, seqs_p, D_MODEL), lambda i: (layer, j0 // 3, seqs_s // seqs_p, 0)),
        pl.BlockSpec((None, 3, seqs_s, D_MODEL), lambda i: (layer, j0 // 3, 0, 0)),
        pl.BlockSpec((None, 1, D_MODEL), lambda i: (layer, 0, 0)),
        weight(D_MODEL, D_FF),
        weight(D_MODEL, D_FF),
        weight(D_FF, D_MODEL),
        pl.BlockSpec((1, D_MODEL), lambda i: (0, 0)),
    ]
    out_specs = [pl.BlockSpec((tm, D_MODEL), ptile), pl.BlockSpec((rows_s, D_MODEL), lambda i: (0, 0))]
    out_shape = [jax.ShapeDtypeStruct((rows_p, D_MODEL), F32), jax.ShapeDtypeStruct((rows_s, D_MODEL), F32)]
    operands = [xp, xs, mod, mod, norm.reshape(DEPTH, 1, D_MODEL), wg, wu, wd, norm_final.reshape(1, D_MODEL)]
    scratch = []
    if next_weights is not None:
        assert D_MODEL % prompt_tiles == 0 and (D_MODEL // prompt_tiles) % 16 == 0
        assert D_FF % DOWN_CAST_ROWS == 0 and D_FF // DOWN_CAST_ROWS <= prompt_tiles
        up_rows = D_MODEL // prompt_tiles
        in_specs += [hbm, hbm, hbm]
        operands += list(next_weights[:3])
        out_specs += [hbm, hbm, hbm]
        out_shape += [jax.ShapeDtypeStruct((D_MODEL, D_FF), BF16), jax.ShapeDtypeStruct((D_MODEL, D_FF), BF16),
                      jax.ShapeDtypeStruct((D_FF, D_MODEL), BF16)]
        scratch = [pltpu.VMEM((up_rows, D_FF), F32), pltpu.VMEM((up_rows, D_FF), F32),
                   pltpu.VMEM((DOWN_CAST_ROWS, D_MODEL), F32),
                   pltpu.VMEM((up_rows, D_FF), BF16), pltpu.VMEM((up_rows, D_FF), BF16),
                   pltpu.VMEM((DOWN_CAST_ROWS, D_MODEL), BF16),
                   pltpu.SemaphoreType.DMA((6,))]
    return pl.pallas_call(
        kern,
        grid=(prompt_tiles + 1,),
        in_specs=in_specs,
        out_specs=out_specs,
        out_shape=out_shape,
        scratch_shapes=scratch,
        compiler_params=pltpu.CompilerParams(dimension_semantics=("arbitrary",), vmem_limit_bytes=VMEM_LIMIT),
        name="swiglu_half_step",
    )(*operands)


def _lower_bound_kernel(x_ref, o_ref):
    rows = [x_ref[l:l + 1, :] for l in range(DEPTH)]
    m = functools.reduce(jnp.maximum, rows)
    e = [jnp.exp(r - m) for r in rows]
    s = functools.reduce(lambda a, b: a + b, e)
    sm = [ei / s for ei in e]
    acc = sm[0]
    for l in range(DEPTH):
        if l > 0:
            acc = acc + sm[l]
        o_ref[l:l + 1, :] = acc - sm[0]


def _lower_bounds(hgrn_lower):
    return pl.pallas_call(
        _lower_bound_kernel,
        out_shape=jax.ShapeDtypeStruct((DEPTH, HG_WIDTH), F32),
        name="hgrn_lower_bounds",
    )(hgrn_lower)


def _bucket_map(rows, cols, valid_rows, col_lo, col_hi):
    r = np.arange(rows)[:, None]
    c = np.arange(cols)[None, :]
    rel = c - WINDOW - r
    n = np.maximum(-rel, 0)
    exact = REL_BUCKETS // 2
    nf = np.maximum(n, 1).astype(np.float32)
    large = exact + (np.log(nf / np.float32(exact)) / np.float32(math.log(REL_MAX_DIST / exact))
                     * np.float32(REL_BUCKETS - exact)).astype(np.int32)
    large = np.minimum(large, REL_BUCKETS - 1)
    bucket = np.where(n < exact, n, large)
    valid = (rel <= 0) & (rel > -WINDOW) & (r < valid_rows) & (c >= col_lo) & (c < col_hi)
    return np.where(valid, bucket, -1).astype(np.int32)


def _bias_table_kernel(bkt_ref, rb_ref, o_ref):
    h = pl.program_id(1)
    bkt = bkt_ref[...]
    acc = jnp.full(bkt.shape, NEG, F32)
    for b in range(REL_BUCKETS):
        acc = jnp.where(bkt == b, rb_ref[b, h], acc)
    o_ref[...] = acc


def _bias_tables(bucket_maps, rel_bias):
    m, r, c = bucket_maps.shape
    return pl.pallas_call(
        _bias_table_kernel,
        grid=(m, ATT_HEADS),
        in_specs=[pl.BlockSpec((None, r, c), lambda i, h: (i, 0, 0)),
                  pl.BlockSpec(memory_space=pltpu.SMEM)],
        out_specs=pl.BlockSpec((None, None, r, c), lambda i, h: (i, h, 0, 0)),
        out_shape=jax.ShapeDtypeStruct((m, ATT_HEADS, r, c), F32),
        compiler_params=pltpu.CompilerParams(dimension_semantics=("arbitrary", "arbitrary")),
        name="rel_bias_tables",
    )(bucket_maps, rel_bias)


def _block_ones(n, blk):
    i = np.arange(n)
    return (i[:, None] // blk == i[None, :] // blk)


def _mixer_consts(tt):
    tri = jnp.asarray(np.tril(np.ones((tt, tt), np.float32)), BF16)
    blk = _block_ones(HG_WIDTH, HG_DK)
    return tri, jnp.asarray(blk, BF16), jnp.asarray(blk, F32)


def _pool_block_diag(pool_w):
    out = jnp.zeros((DEPTH, POOL_WIDTH, POOL_WIDTH), BF16)
    for g in range(len(POOL_WINDOWS)):
        sl = slice(g * POOL_GW, (g + 1) * POOL_GW)
        out = out.at[:, sl, sl].set(pool_w[:, g].astype(BF16))
    return out


def _head_masks():
    lane = lax.broadcasted_iota(jnp.int32, (1, HG_WIDTH), 1)
    return [jnp.where((lane >= HG_DK * h) & (lane < HG_DK * (h + 1)), 1.0, 0.0).astype(F32) for h in range(HG_HEADS)]


def _prompt_mixer_kernel(x_ref, xn_ref, mod_ref, n_ref, win_ref, wout_ref, pbd_ref, pscale_ref, lb_ref, hgn_ref,
                         bm_ref, sink_ref, tri_ref, ones_ref, bdm_ref,
                         xo_ref, pool_o, s_o, k_o, v_o,
                         za_ref, zb_ref, ubuf, kbuf, vbuf, st_ref, cat_ref, *, layer, tt, nt, ntiles):
    b = pl.program_id(0)
    p = pl.program_id(1)
    last_pair = pl.num_programs(1) - 1
    nxt = jnp.minimum(b * nt + 2 * p + 2, ntiles - 1)
    bn = nxt // nt
    n_chunks = 8
    cw = IN_WIDTH // n_chunks

    def projector(x_rows, bsel, dst_ref):
        h = _norm_mod(x_rows, n_ref[...], mod_ref[3, pl.ds(bsel, 1), :], mod_ref[4, pl.ds(bsel, 1), :]).astype(BF16)
        state = {"k": 0}

        def step():
            k = state["k"]
            if k < n_chunks:
                dst_ref[:, k * cw:(k + 1) * cw] = _dot(h, win_ref[:, k * cw:(k + 1) * cw])
                state["k"] = k + 1

        return step

    @pl.when((b == 0) & (p == 0))
    def _():
        first_proj = projector(x_ref[0:tt, :], b, za_ref)
        for _ in range(n_chunks):
            first_proj()

    @pl.when(p == 0)
    def _():
        ubuf[0:16, :] = jnp.zeros((16, POOL_WIDTH), F32)
        kbuf[:, 0:WINDOW, :] = jnp.zeros((KV_HEADS, WINDOW, HEAD_DIM), BF16)
        vbuf[:, 0:WINDOW, :] = jnp.zeros((KV_HEADS, WINDOW, HEAD_DIM), BF16)
        st_ref[...] = jnp.zeros((HG_WIDTH, HG_WIDTH), F32)

    gate = mod_ref[5, pl.ds(b, 1), :]

    def mix(z_ref, x, row0, t, side):
        lane = lax.broadcasted_iota(jnp.int32, (1, POOL_WIDTH), 1)

        u = z_ref[:, OFF_POOL:OFF_POOL + POOL_WIDTH]
        ubuf[16:16 + tt, :] = u
        e = ubuf[...]
        s2 = e + pltpu.roll(e, 1, 0)
        s4 = s2 + pltpu.roll(s2, 2, 0)
        s8 = s4 + pltpu.roll(s4, 4, 0)
        s16 = s8 + pltpu.roll(s8, 8, 0)
        sel = jnp.where(lane < 64, s2, jnp.where(lane < 128, s4, jnp.where(lane < 192, s8, s16)))[16:]
        wl = jnp.where(lane < 64, 2.0, jnp.where(lane < 128, 4.0, jnp.where(lane < 192, 8.0, 16.0))).astype(F32)
        pos1 = (t * tt + 1 + lax.broadcasted_iota(jnp.int32, (tt, 1), 0)).astype(F32)
        dpool = sel / jnp.minimum(pos1, wl) - u
        ypool = _dot(dpool.astype(BF16), pbd_ref[...]) * pscale_ref[...]
        cat_ref[:, 0:POOL_WIDTH] = ypool.astype(BF16)
        ubuf[0:16, :] = ubuf[tt:tt + 16, :]
        side()


        for g in range(KV_HEADS):
            kbuf[g, WINDOW:WINDOW + tt, :] = z_ref[:, OFF_AK + HEAD_DIM * g:OFF_AK + HEAD_DIM * (g + 1)].astype(BF16)
            vbuf[g, WINDOW:WINDOW + tt, :] = z_ref[:, OFF_AV + HEAD_DIM * g:OFF_AV + HEAD_DIM * (g + 1)].astype(BF16)
        first = jnp.where(t == 0, 1, 0)
        bands = [(j, g) for j in range(tt // WINDOW) for g in range(KV_HEADS)]
        logits = []
        for j, g in bands:
            r0 = j * WINDOW
            c0 = OFF_AQ + HEAD_DIM * g * GQA_GROUP
            qg = jnp.concatenate([z_ref[r0:r0 + WINDOW, c0 + HEAD_DIM * i:c0 + HEAD_DIM * (i + 1)]
                                  for i in range(GQA_GROUP)], axis=0)
            qg = (qg * (HEAD_DIM ** -0.5)).astype(BF16)
            logits.append(_dot_nt(qg, kbuf[g, r0:r0 + 2 * WINDOW, :]))
        side()

        lb = lb_ref[...]
        q = _silu(z_ref[:, OFF_HQ:OFF_HQ + HG_WIDTH])
        f = lb + (1.0 - lb) * _sigmoid(z_ref[:, OFF_HF:OFF_HF + HG_WIDTH])
        k = 1.0 - f
        v = z_ref[:, OFF_HI:OFF_HI + HG_WIDTH]
        vb = v.astype(BF16)
        tri = tri_ref[...]
        hi, mid, lo = _split3(jnp.log(f))
        bcum = _dot(tri, hi) + _dot(tri, mid) + _dot(tri, lo)
        side()
        ones_blk = ones_ref[...]
        hm = _head_masks()

        n16 = tt // 16
        half_shape = (n16, 2, 8, HG_WIDTH)
        q4 = q.reshape(half_shape)
        b4 = bcum.reshape(half_shape)
        c4 = (bcum - jnp.log(k)).reshape(half_shape)
        v4 = v.reshape(half_shape)
        rowi = lax.broadcasted_iota(jnp.int32, (1, 8, 1), 1)
        o_up = jnp.zeros((n16, 8, HG_WIDTH), F32)
        o_dn = jnp.zeros((n16, 8, HG_WIDTH), F32)
        nh = n16 * 8
        for s in range(8):
            cs = c4[:, 0, s:s + 1, :]
            vs = v4[:, 0, s:s + 1, :]
            x_up = jnp.where(rowi >= s, q4[:, 0] * jnp.exp(b4[:, 0] - cs), 0.0)
            x_dn = q4[:, 1] * jnp.exp(b4[:, 1] - cs)
            xs = jnp.concatenate([x_up.reshape(nh, HG_WIDTH), x_dn.reshape(nh, HG_WIDTH)], axis=0)
            r = _dot(xs.astype(BF16), ones_blk)
            o_up = o_up + r[0:nh].reshape(n16, 8, HG_WIDTH) * vs
            o_dn = o_dn + r[nh:2 * nh].reshape(n16, 8, HG_WIDTH) * vs
            if s % 4 == 3:
                side()
        for s in range(8):
            cs = c4[:, 1, s:s + 1, :]
            x_dn = jnp.where(rowi >= s, q4[:, 1] * jnp.exp(b4[:, 1] - cs), 0.0)
            r = _dot(x_dn.reshape(nh, HG_WIDTH).astype(BF16), ones_blk)
            o_dn = o_dn + r.reshape(n16, 8, HG_WIDTH) * v4[:, 1, s:s + 1, :]
            if s % 4 == 3:
                side()
        o_tot = jnp.concatenate([o_up[:, None], o_dn[:, None]], axis=1).reshape(tt, HG_WIDTH)

        blocks = []
        blk = 32
        while blk <= tt:
            blocks += [(i * blk, i * blk + blk // 2, (i + 1) * blk) for i in range(tt // blk)]
            blk *= 2
        scores = []
        for lo_s, mid_s, hi_s in blocks:
            ref = bcum[mid_s - 1:mid_s, :]
            qt = q[mid_s:hi_s] * jnp.exp(bcum[mid_s:hi_s] - ref)
            kt = (k[lo_s:mid_s] * jnp.exp(ref - bcum[lo_s:mid_s])).astype(BF16)
            qs = jnp.concatenate([qt * hm[hh] for hh in range(HG_HEADS)], axis=0).astype(BF16)
            scores.append(_dot_nt(qs, kt).astype(BF16))

        st = st_ref[...]
        qin = (q * jnp.exp(bcum)).astype(BF16)
        o_tot = o_tot + _dot_nt(qin, st.astype(BF16))
        blast = bcum[tt - 1:tt, :]
        kout = (k * jnp.exp(blast - bcum)).astype(BF16)
        st_ref[...] = st * jnp.exp(blast) + _dot_tn(vb, kout) * bdm_ref[...]
        side()

        probs, dens = [], []
        for (j, g), lg_all in zip(bands, logits):
            h0 = g * GQA_GROUP
            bm = bm_ref[first, h0:h0 + GQA_GROUP] if j == 0 else bm_ref[0, h0:h0 + GQA_GROUP]
            ps = []
            for i in range(GQA_GROUP):
                lg = lg_all[i * WINDOW:(i + 1) * WINDOW] + bm[i]
                sink = sink_ref[layer, h0 + i]
                m = jnp.maximum(jnp.max(lg, axis=-1, keepdims=True), sink)
                p_ = jnp.exp(lg - m)
                dens.append(jnp.sum(p_, axis=-1, keepdims=True) + jnp.exp(sink - m))
                ps.append(p_.astype(BF16))
            probs.append(jnp.concatenate(ps, axis=0))
        for n, (j, g) in enumerate(bands):
            r0 = j * WINDOW
            og = _dot(probs[n], vbuf[g, r0:r0 + 2 * WINDOW, :])
            for i in range(GQA_GROUP):
                cc = POOL_WIDTH + HG_WIDTH + HEAD_DIM * (g * GQA_GROUP + i)
                oh = og[i * WINDOW:(i + 1) * WINDOW] / dens[n * GQA_GROUP + i]
                cat_ref[r0:r0 + WINDOW, cc:cc + HEAD_DIM] = oh.astype(BF16)
        for g in range(KV_HEADS):
            kbuf[g, 0:WINDOW, :] = kbuf[g, tt:tt + WINDOW, :]
            vbuf[g, 0:WINDOW, :] = vbuf[g, tt:tt + WINDOW, :]
        side()

        adds = {}
        for (lo_s, mid_s, hi_s), a in zip(blocks, scores):
            half = mid_s - lo_s
            ov = _dot(a, vb[lo_s:mid_s])
            oi = ov[0:half] * hm[0]
            for hh in range(1, HG_HEADS):
                oi = oi + ov[hh * half:(hh + 1) * half] * hm[hh]
            adds.setdefault(2 * half, []).extend([jnp.zeros((half, HG_WIDTH), F32), oi])
        for parts in adds.values():
            o_tot = o_tot + jnp.concatenate(parts, axis=0)

        sq_hi, sq_mid, _ = _split3(o_tot * o_tot)
        ms = (_dot(sq_hi, ones_blk) + _dot(sq_mid, ones_blk)) * (1.0 / HG_DK)
        y_hg = o_tot * lax.rsqrt(ms + EPS) * hgn_ref[...] * _silu(z_ref[:, OFF_HG:OFF_HG + HG_WIDTH])
        cat_ref[:, POOL_WIDTH:POOL_WIDTH + HG_WIDTH] = y_hg.astype(BF16)

        xo_ref[row0:row0 + tt, :] = x + gate * _dot(cat_ref[...], wout_ref[...])

        for _ in range(n_chunks):
            side()

    mix(za_ref, x_ref[0:tt, :], 0, 2 * p, projector(x_ref[tt:2 * tt, :], b, zb_ref))
    mix(zb_ref, x_ref[tt:2 * tt, :], tt, 2 * p + 1, projector(xn_ref[...], bn, za_ref))

    @pl.when(p == last_pair)
    def _():
        pool_o[...] = ubuf[pl.ds(1, POOL_BUF), :]
        s_kv = st_ref[...].T
        for hh in range(HG_HEADS):
            s_o[hh] = s_kv[HG_DK * hh:HG_DK * (hh + 1), HG_DK * hh:HG_DK * (hh + 1)]
        k_o[...] = zb_ref[tt - WINDOW:tt, OFF_AK:OFF_AV]
        v_o[...] = zb_ref[tt - WINDOW:tt, OFF_AV:IN_WIDTH]


def _prompt_mixer(x, mod, norm, w_in, w_out, pool_bd, pool_scale, lbs, hg_norm, bm, sinks, consts, layer, *,
                  tt, mod_row_block):
    bsz, seq, _ = x.shape
    tri, ones_blk, bdm = consts
    nt = seq // tt
    ntiles = bsz * nt
    kern = functools.partial(_prompt_mixer_kernel, layer=layer, tt=tt, nt=nt, ntiles=ntiles)
    full = lambda *shape: pl.BlockSpec(shape, lambda b, p: (0,) * len(shape))
    lsel = lambda *shape: pl.BlockSpec((None,) + shape, lambda b, p: (layer,) + (0,) * len(shape))
    mix_pair = lambda b, p: (b, p, 0)
    mix_seq = lambda *z: (lambda b, p: (b,) + z)

    def next_tile(b, p):
        nxt = jnp.minimum(b * nt + 2 * p + 2, ntiles - 1)
        return (nxt // nt, nxt % nt, 0)

    return pl.pallas_call(
        kern,
        grid=(bsz, nt // 2),
        in_specs=[
            pl.BlockSpec((None, 2 * tt, D_MODEL), mix_pair),
            pl.BlockSpec((None, tt, D_MODEL), next_tile),
            pl.BlockSpec((None, N_MOD, 8, D_MODEL), lambda b, p: (layer, 0, mod_row_block, 0)),
            lsel(1, D_MODEL),
            lsel(D_MODEL, IN_WIDTH),
            lsel(D_MODEL, D_MODEL),
            lsel(POOL_WIDTH, POOL_WIDTH),
            lsel(1, POOL_WIDTH),
            lsel(1, HG_WIDTH),
            lsel(1, HG_WIDTH),
            full(2, ATT_HEADS, WINDOW, 2 * WINDOW),
            pl.BlockSpec(memory_space=pltpu.SMEM),
            full(tt, tt),
            full(HG_WIDTH, HG_WIDTH),
            full(HG_WIDTH, HG_WIDTH),
        ],
        out_specs=[
            pl.BlockSpec((None, 2 * tt, D_MODEL), mix_pair),
            pl.BlockSpec((None, POOL_BUF, POOL_WIDTH), mix_seq(0, 0)),
            pl.BlockSpec((None, HG_HEADS, HG_DK, HG_DK), mix_seq(0, 0, 0)),
            pl.BlockSpec((None, WINDOW, KV_HEADS * HEAD_DIM), mix_seq(0, 0)),
            pl.BlockSpec((None, WINDOW, KV_HEADS * HEAD_DIM), mix_seq(0, 0)),
        ],
        out_shape=[
            jax.ShapeDtypeStruct((bsz, seq, D_MODEL), F32),
            jax.ShapeDtypeStruct((bsz, POOL_BUF, POOL_WIDTH), F32),
            jax.ShapeDtypeStruct((bsz, HG_HEADS, HG_DK, HG_DK), F32),
            jax.ShapeDtypeStruct((bsz, WINDOW, KV_HEADS * HEAD_DIM), F32),
            jax.ShapeDtypeStruct((bsz, WINDOW, KV_HEADS * HEAD_DIM), F32),
        ],
        scratch_shapes=[
            pltpu.VMEM((tt, IN_WIDTH), F32),
            pltpu.VMEM((tt, IN_WIDTH), F32),
            pltpu.VMEM((16 + tt, POOL_WIDTH), F32),
            pltpu.VMEM((KV_HEADS, WINDOW + tt, HEAD_DIM), BF16),
            pltpu.VMEM((KV_HEADS, WINDOW + tt, HEAD_DIM), BF16),
            pltpu.VMEM((HG_WIDTH, HG_WIDTH), F32),
            pltpu.VMEM((tt, D_MODEL), BF16),
        ],
        compiler_params=pltpu.CompilerParams(dimension_semantics=("arbitrary", "arbitrary"),
                                             vmem_limit_bytes=VMEM_LIMIT),
        name="prompt_mixer",
    )(x, x, mod, norm.reshape(DEPTH, 1, D_MODEL), w_in, w_out, pool_bd, pool_scale.reshape(DEPTH, 1, POOL_WIDTH),
      lbs.reshape(DEPTH, 1, HG_WIDTH), hg_norm, bm, sinks, tri, ones_blk, bdm)


def _sample_mixer_kernel(x_ref, mod_ref, n_ref, win_ref, wout_ref, pbd_ref, pscale_ref, lb_ref, hgn_ref,
                         bms_ref, sinkc_ref, ones_ref, sp_ref, s0_ref, kt_ref, vt_ref,
                         xo_ref, pool_o, s_o, kt_o, vt_o,
                         z_ref, cat_ref, qs_ref, os_ref, kn_ref, vn_ref, qint_ref, koutt_ref, vtt_ref, gt_ref,
                         oin_ref, oacc_ref, *, steps, nb, gb, rb, pos0):
    g = pl.program_id(0)
    last = pl.num_programs(0) - 1
    base = pl.multiple_of(g * gb, gb)
    kvw = KV_HEADS * HEAD_DIM

    def rows(t):
        return slice(t * nb, (t + 1) * nb)

    @pl.when(g == 0)
    def _():
        for t in range(steps):
            h = _norm_mod(x_ref[t], n_ref[...], mod_ref[0], mod_ref[1]).astype(BF16)
            z_ref[rows(t), :] = _dot(h, win_ref[...])
        lane = lax.broadcasted_iota(jnp.int32, (1, POOL_WIDTH), 1)

        ext = [sp_ref[r] for r in range(POOL_BUF)] + [z_ref[rows(t), OFF_POOL:OFF_POOL + POOL_WIDTH] for t in range(steps)]
        for r in range(POOL_BUF):
            pool_o[r] = ext[r + steps]
        sums = {1: ext}
        w = 1
        while w < 16:
            prev = sums[w]
            sums[2 * w] = [None if (r < 2 * w - 1) else prev[r] + prev[r - w] for r in range(len(ext))]
            w *= 2
        dl = []
        for t in range(steps):
            r = POOL_BUF + t
            sel = jnp.where(lane < 64, sums[2][r], jnp.where(lane < 128, sums[4][r], jnp.where(lane < 192, sums[8][r], sums[16][r])))
            cnt = [float(min(pos0 + t + 1, wd)) for wd in POOL_WINDOWS]
            cl = jnp.where(lane < 64, cnt[0], jnp.where(lane < 128, cnt[1], jnp.where(lane < 192, cnt[2], cnt[3]))).astype(F32)
            dl.append(sel / cl - ext[r])
        ypool = _dot(jnp.concatenate(dl, axis=0).astype(BF16), pbd_ref[...]) * pscale_ref[...]
        cat_ref[:, 0:POOL_WIDTH] = ypool

        lb = lb_ref[...]
        qv, kv_, vv, bv = [], [], [], []
        bacc = None
        for t in range(steps):
            qv.append(_silu(z_ref[rows(t), OFF_HQ:OFF_HQ + HG_WIDTH]))
            f = lb + (1.0 - lb) * _sigmoid(z_ref[rows(t), OFF_HF:OFF_HF + HG_WIDTH])
            kv_.append(1.0 - f)
            vv.append(z_ref[rows(t), OFF_HI:OFF_HI + HG_WIDTH])
            bacc = jnp.log(f) if bacc is None else bacc + jnp.log(f)
            bv.append(bacc)
        pairs = [(t, s) for t in range(steps) for s in range(t + 1)]
        xs = jnp.concatenate([qv[t] * kv_[s] * jnp.exp(bv[t] - bv[s]) for t, s in pairs], axis=0).astype(BF16)
        rr = _dot(xs, ones_ref[...])
        o_intra = [None] * steps
        for i, (t, s) in enumerate(pairs):
            term = rr[i * nb:(i + 1) * nb] * vv[s]
            o_intra[t] = term if o_intra[t] is None else o_intra[t] + term
        blast = bv[steps - 1]
        gt_ref[...] = jnp.exp(blast).T
        for t in range(steps):
            oin_ref[t] = o_intra[t]
            qint_ref[t] = (qv[t] * jnp.exp(bv[t])).T
            koutt_ref[t] = (kv_[t] * jnp.exp(blast - bv[t])).T
            vtt_ref[t] = vv[t].T
        oacc_ref[...] = jnp.zeros(oacc_ref.shape, F32)

        zeros64 = jnp.zeros((nb, HEAD_DIM), F32)
        kn_ref[...] = jnp.zeros(kn_ref.shape, F32)
        vn_ref[...] = jnp.zeros(vn_ref.shape, F32)
        for t in range(steps):
            kn_ref[:, t, :] = z_ref[rows(t), OFF_AK:OFF_AK + kvw]
            vn_ref[:, t, :] = z_ref[rows(t), OFF_AV:OFF_AV + kvw]
            for hd in range(ATT_HEADS):
                c0 = OFF_AQ + HEAD_DIM * hd
                qsl = z_ref[rows(t), c0:c0 + HEAD_DIM] * (HEAD_DIM ** -0.5)
                parts = [qsl, zeros64] if hd < GQA_GROUP else [zeros64, qsl]
                qs_ref[:, ATT_HEADS * t + hd, :] = jnp.concatenate(parts, axis=1)

    r0 = pl.multiple_of(g * rb, rb)
    hrow = pl.multiple_of((g * rb // HG_DK) * HG_DK, HG_DK)
    vts = [vtt_ref[t, pl.ds(hrow, HG_DK), :] for t in range(steps)]
    accs = [oacc_ref[t, pl.ds(hrow, HG_DK), :] for t in range(steps)]
    for i in range(rb):
        s0 = s0_ref[i]
        snew = gt_ref[pl.ds(r0 + i, 1), :] * s0
        for t in range(steps):
            accs[t] = accs[t] + qint_ref[t, pl.ds(r0 + i, 1), :] * s0
            snew = snew + koutt_ref[t, pl.ds(r0 + i, 1), :] * vts[t]
        s_o[i] = snew
    for t in range(steps):
        oacc_ref[t, pl.ds(hrow, HG_DK), :] = accs[t]

    bm_old = bms_ref[:, 0:WINDOW]
    bm_new = bms_ref[:, WINDOW:WINDOW + 8]
    sinkc = sinkc_ref[...]
    seqs = range(gb)
    qn = [qs_ref[base + n].astype(BF16) for n in seqs]
    l_old = [_dot(qn[n], kt_ref[n].astype(BF16)) for n in seqs]
    l_new = [_dot_nt(qn[n], kn_ref[base + n].astype(BF16)) for n in seqs]
    def new_columns(ref):
        new = ref[pl.ds(base, gb)].reshape(gb * 8, kvw)
        if gb * 8 < WINDOW:
            new = jnp.concatenate([new, jnp.zeros((WINDOW - gb * 8, kvw), F32)], axis=0)
        return new.T

    knt = new_columns(kn_ref)
    vnt = new_columns(vn_ref)
    for n in seqs:
        kt_o[n] = jnp.concatenate([kt_ref[n][:, steps:], knt[:, 8 * n:8 * n + steps]], axis=1)
        vt_o[n] = jnp.concatenate([vt_ref[n][:, steps:], vnt[:, 8 * n:8 * n + steps]], axis=1)
    p_old, p_new, den = [], [], []
    for n in seqs:
        lo_, ln_ = l_old[n] + bm_old, l_new[n] + bm_new
        m = jnp.maximum(jnp.maximum(jnp.max(lo_, axis=-1, keepdims=True), jnp.max(ln_, axis=-1, keepdims=True)), sinkc)
        po, pn = jnp.exp(lo_ - m), jnp.exp(ln_ - m)
        den.append(jnp.sum(po, axis=-1, keepdims=True) + jnp.sum(pn, axis=-1, keepdims=True) + jnp.exp(sinkc - m))
        p_old.append(po.astype(BF16))
        p_new.append(pn.astype(BF16))
    for n in seqs:
        o = _dot_nt(p_old[n], vt_ref[n].astype(BF16)) + _dot(p_new[n], vn_ref[base + n].astype(BF16))
        os_ref[n] = o / den[n]
    for t in range(steps):
        for hd in range(ATT_HEADS):
            c0 = HEAD_DIM * (hd // GQA_GROUP)
            cc = POOL_WIDTH + HG_WIDTH + HEAD_DIM * hd
            cat_ref[pl.ds(t * nb + base, gb), cc:cc + HEAD_DIM] = os_ref[:, ATT_HEADS * t + hd, c0:c0 + HEAD_DIM]

    @pl.when(g == last)
    def _():
        ones_blk = ones_ref[...]
        for t in range(steps):
            o = oin_ref[t] + oacc_ref[t].T
            sq_hi, sq_mid, _ = _split3(o * o)
            ms = (_dot(sq_hi, ones_blk) + _dot(sq_mid, ones_blk)) * (1.0 / HG_DK)
            gate_t = _silu(z_ref[rows(t), OFF_HG:OFF_HG + HG_WIDTH])
            cat_ref[rows(t), POOL_WIDTH:POOL_WIDTH + HG_WIDTH] = o * lax.rsqrt(ms + EPS) * hgn_ref[...] * gate_t
        for t in range(steps):
            y = _dot(cat_ref[rows(t), :].astype(BF16), wout_ref[...])
            xo_ref[t] = x_ref[t] + mod_ref[2] * y


N_SAMPLE_MIXER_INPUTS = 16


def _sample_mixer_kernel_into(*refs, n_stacks, **kw):
    _sample_mixer_kernel(*refs[:N_SAMPLE_MIXER_INPUTS], *refs[N_SAMPLE_MIXER_INPUTS + n_stacks:], **kw)


def _sample_mixer(x, mod, norm, w_in, w_out, pool_bd, pool_scale, lbs, hg_norm, bms, sink_col, ones_blk,
                  pool_hist, hgrn_state, cache_kt, cache_vt, stacks, layer, *, gb, pos0):
    steps, nb, _ = x.shape
    rows_th = steps * ATT_HEADS
    ngrp = nb // gb
    rb = HG_WIDTH // ngrp
    kvw = KV_HEADS * HEAD_DIM
    stacks = () if stacks is None else tuple(stacks)
    kern = functools.partial(_sample_mixer_kernel_into, n_stacks=len(stacks), steps=steps, nb=nb, gb=gb, rb=rb,
                             pos0=pos0)
    full = lambda *shape: pl.BlockSpec(shape, lambda g: (0,) * len(shape))
    lsel = lambda *shape: pl.BlockSpec((None,) + shape, lambda g: (layer,) + (0,) * len(shape))
    grp = lambda *shape: pl.BlockSpec((None,) + shape, lambda g: (layer, g) + (0,) * (len(shape) - 1))
    return pl.pallas_call(
        kern,
        grid=(ngrp,),
        in_specs=[
            full(steps, nb, D_MODEL),
            pl.BlockSpec((None, 3, nb, D_MODEL), lambda g: (layer, 1, 0, 0)),
            lsel(1, D_MODEL),
            lsel(D_MODEL, IN_WIDTH),
            lsel(D_MODEL, D_MODEL),
            lsel(POOL_WIDTH, POOL_WIDTH),
            lsel(1, POOL_WIDTH),
            lsel(1, HG_WIDTH),
            lsel(1, HG_WIDTH),
            full(rows_th, 2 * WINDOW),
            lsel(rows_th, 1),
            full(HG_WIDTH, HG_WIDTH),
            lsel(POOL_BUF, nb, POOL_WIDTH),
            grp(rb, HG_DK, nb),
            grp(gb, kvw, WINDOW),
            grp(gb, kvw, WINDOW),
        ] + [pl.BlockSpec(memory_space=pl.ANY)] * len(stacks),
        out_specs=[
            full(steps, nb, D_MODEL),
            lsel(POOL_BUF, nb, POOL_WIDTH),
            grp(rb, HG_DK, nb),
            grp(gb, kvw, WINDOW),
            grp(gb, kvw, WINDOW),
        ],
        out_shape=[
            jax.ShapeDtypeStruct((steps, nb, D_MODEL), F32),
            jax.ShapeDtypeStruct((DEPTH, POOL_BUF, nb, POOL_WIDTH), F32),
            jax.ShapeDtypeStruct((DEPTH, HG_WIDTH, HG_DK, nb), F32),
            jax.ShapeDtypeStruct((DEPTH, nb, kvw, WINDOW), F32),
            jax.ShapeDtypeStruct((DEPTH, nb, kvw, WINDOW), F32),
        ],
        input_output_aliases={N_SAMPLE_MIXER_INPUTS + i: 1 + i for i in range(len(stacks))},
        scratch_shapes=[
            pltpu.VMEM((steps * nb, IN_WIDTH), F32),
            pltpu.VMEM((steps * nb, D_MODEL), F32),
            pltpu.VMEM((nb, rows_th, kvw), F32),
            pltpu.VMEM((gb, rows_th, kvw), F32),
            pltpu.VMEM((nb, 8, kvw), F32),
            pltpu.VMEM((nb, 8, kvw), F32),
            pltpu.VMEM((steps, HG_WIDTH, nb), F32),
            pltpu.VMEM((steps, HG_WIDTH, nb), F32),
            pltpu.VMEM((steps, HG_WIDTH, nb), F32),
            pltpu.VMEM((HG_WIDTH, nb), F32),
            pltpu.VMEM((steps, nb, HG_WIDTH), F32),
            pltpu.VMEM((steps, HG_WIDTH, nb), F32),
        ],
        compiler_params=pltpu.CompilerParams(dimension_semantics=("arbitrary",), vmem_limit_bytes=VMEM_LIMIT),
        name="sample_mixer",
    )(x, mod, norm.reshape(DEPTH, 1, D_MODEL), w_in, w_out, pool_bd, pool_scale.reshape(DEPTH, 1, POOL_WIDTH),
      lbs.reshape(DEPTH, 1, HG_WIDTH), hg_norm, bms, sink_col, ones_blk, pool_hist, hgrn_state, cache_kt, cache_vt,
      *stacks)


FFN_TILE = 512
MIXER_TILE = 256
SAMPLE_GROUP = 8
PAST_LEN = 8192


def kernel(x_prompt, x_sample, c_prompt, c_sample, state_pool, state_hgrn, cache_k_win, cache_v_win, norm_ffn1, norm_mix, norm_ffn2, w_mod, b_mod, ffn1_w_gate, ffn1_w_up, ffn1_w_down, w_in, w_out, pool_w, pool_scale, hgrn_lower, hgrn_norm, attn_sinks, rel_bias, ffn2_w_gate, ffn2_w_up, ffn2_w_down, norm_final):
    bsz, seq, _ = x_prompt.shape
    nb, steps, _ = x_sample.shape
    kvw = KV_HEADS * HEAD_DIM

    mod = _modulation(jnp.concatenate([c_sample, c_prompt], axis=0), w_mod, b_mod)
    prompt_mod_block = nb // 8
    lbs = _lower_bounds(hgrn_lower)
    maps_p = np.stack([_bucket_map(WINDOW, 2 * WINDOW, WINDOW, 0, 2 * WINDOW),
                       _bucket_map(WINDOW, 2 * WINDOW, WINDOW, WINDOW, 2 * WINDOW)])
    bm_p = _bias_tables(jnp.asarray(maps_p), rel_bias)
    maps_s = _bucket_map(8, 2 * WINDOW, steps, 0, WINDOW + steps)[None]
    bm_s = _bias_tables(jnp.asarray(maps_s), rel_bias)[0]
    bm_s = jnp.swapaxes(bm_s, 0, 1).reshape(8 * ATT_HEADS, 2 * WINDOW)[:steps * ATT_HEADS]
    sink_col = jnp.tile(attn_sinks, (1, steps)).reshape(DEPTH, steps * ATT_HEADS, 1)
    consts = _mixer_consts(MIXER_TILE)
    hgn = jnp.tile(hgrn_norm, (1, HG_HEADS)).reshape(DEPTH, 1, HG_WIDTH)
    pool_bd = _pool_block_diag(pool_w)
    w_in_b, w_out_b = w_in.astype(BF16), w_out.astype(BF16)
    f1 = (ffn1_w_gate, ffn1_w_up, ffn1_w_down)
    f2 = (ffn2_w_gate, ffn2_w_up, ffn2_w_down)
    w_ffn = tuple(w[0].astype(BF16) for w in f1)
    s_pool = jnp.transpose(state_pool, (0, 2, 1, 3))
    s_hgrn = jnp.transpose(state_hgrn, (0, 2, 3, 4, 1)).reshape(DEPTH, HG_WIDTH, HG_DK, nb)
    c_k = jnp.transpose(cache_k_win, (0, 1, 3, 4, 2)).reshape(DEPTH, nb, kvw, WINDOW)
    c_v = jnp.transpose(cache_v_win, (0, 1, 3, 4, 2)).reshape(DEPTH, nb, kvw, WINDOW)

    xp = x_prompt.reshape(bsz * seq, D_MODEL)
    xs = jnp.swapaxes(x_sample, 0, 1).reshape(steps * nb, D_MODEL)
    ffn = functools.partial(_ffn, tm=FFN_TILE, seqs_p=bsz, seqs_s=nb)
    st_p, stacks_s = [], None
    for l in range(DEPTH):
        xp, xs, *w_ffn = ffn(xp, xs, mod, norm_ffn1, *w_ffn, f2 + (l,), norm_final, l, 0)
        xp, *sp = _prompt_mixer(xp.reshape(bsz, seq, D_MODEL), mod, norm_mix, w_in_b, w_out_b, pool_bd, pool_scale,
                                lbs, hgn, bm_p, attn_sinks, consts, l, tt=MIXER_TILE, mod_row_block=prompt_mod_block)
        xs, *stacks_s = _sample_mixer(xs.reshape(steps, nb, D_MODEL), mod, norm_mix, w_in_b, w_out_b, pool_bd,
                                      pool_scale, lbs, hgn, bm_s, sink_col, consts[1], s_pool, s_hgrn, c_k, c_v,
                                      stacks_s, l, gb=SAMPLE_GROUP, pos0=PAST_LEN)
        st_p.append(sp)
        last = l == DEPTH - 1
        xp, xs, *w_ffn = ffn(xp.reshape(bsz * seq, D_MODEL), xs.reshape(steps * nb, D_MODEL), mod, norm_ffn2, *w_ffn,
                             None if last else f1 + (l + 1,), norm_final, l, 6, final=last)

    y_prompt = xp.reshape(bsz, seq, D_MODEL)
    y_sample = jnp.swapaxes(xs.reshape(steps, nb, D_MODEL), 0, 1)

    def stacked(states, i, shape):
        return jnp.stack([s[i] for s in states]).reshape((DEPTH,) + shape)

    outs = [stacked(st_p, 0, (bsz, POOL_BUF, POOL_WIDTH)),
            stacked(st_p, 1, (bsz, HG_HEADS, HG_DK, HG_DK)),
            stacked(st_p, 2, (bsz, WINDOW, KV_HEADS, HEAD_DIM)),
            stacked(st_p, 3, (bsz, WINDOW, KV_HEADS, HEAD_DIM)),
            jnp.transpose(stacks_s[0], (0, 2, 1, 3)),
            jnp.transpose(stacks_s[1].reshape(DEPTH, HG_HEADS, HG_DK, HG_DK, nb), (0, 4, 1, 2, 3)),
            jnp.transpose(stacks_s[2].reshape(DEPTH, nb, KV_HEADS, HEAD_DIM, WINDOW), (0, 1, 4, 2, 3)),
            jnp.transpose(stacks_s[3].reshape(DEPTH, nb, KV_HEADS, HEAD_DIM, WINDOW), (0, 1, 4, 2, 3))]
    return (y_prompt, y_sample, *outs)
```

```python
import functools
import math

import numpy as np
import jax
import jax.numpy as jnp
from jax import lax
from jax.experimental import pallas as pl
from jax.experimental.pallas import tpu as pltpu

F32 = jnp.float32
BF16 = jnp.bfloat16

D_MODEL = 1024
DEPTH = 4
N_MOD = 9
D_FF = 2816
EPS = 1e-6
POOL_WIDTH = 256
POOL_WINDOWS = (2, 4, 8, 16)
POOL_GW = 64
POOL_BUF = 15
HG_WIDTH = 256
HG_HEADS = 4
HG_DK = 64
ATT_WIDTH = 512
ATT_HEADS = 8
KV_HEADS = 2
GQA_GROUP = 4
HEAD_DIM = 64
WINDOW = 128
REL_BUCKETS = 32
REL_MAX_DIST = 128
OFF_POOL, OFF_HQ, OFF_HF, OFF_HI, OFF_HG, OFF_AQ, OFF_AK, OFF_AV = 0, 256, 512, 768, 1024, 1280, 1792, 1920
IN_WIDTH = 2048
NEG = -1e30

VMEM_LIMIT = 56 * 1024 * 1024


def _dot(a, b):
    return jnp.dot(a, b, preferred_element_type=F32)


def _dot_nt(a, b):
    return lax.dot_general(a, b, (((1,), (1,)), ((), ())), preferred_element_type=F32)


def _dot_tn(a, b):
    return lax.dot_general(a, b, (((0,), (0,)), ((), ())), preferred_element_type=F32)


def _sigmoid(x):
    return 1.0 / (1.0 + jnp.exp(-x))


def _silu(x):
    return x * _sigmoid(x)


def _norm_mod(x, g, shift, scale):
    r = lax.rsqrt(jnp.mean(x * x, axis=-1, keepdims=True) + EPS)
    return (x * r) * g * (1.0 + scale) + shift


def _split3(x):
    hi = x.astype(BF16)
    r1 = x - hi.astype(F32)
    mid = r1.astype(BF16)
    lo = (r1 - mid.astype(F32)).astype(BF16)
    return hi, mid, lo


def _mod_kernel(c_ref, w_ref, b_ref, o_ref):
    sc = _silu(c_ref[...]).astype(BF16)
    o_ref[...] = _dot(sc, w_ref[...].astype(BF16)) + b_ref[...]


def _modulation(c_all, w_mod, b_mod):
    rows = c_all.shape[0]
    return pl.pallas_call(
        _mod_kernel,
        grid=(DEPTH, N_MOD),
        in_specs=[
            pl.BlockSpec((rows, D_MODEL), lambda l, j: (0, 0)),
            pl.BlockSpec((None, D_MODEL, D_MODEL), lambda l, j: (l, 0, j)),
            pl.BlockSpec((None, None, 1, D_MODEL), lambda l, j: (l, j, 0, 0)),
        ],
        out_specs=pl.BlockSpec((None, None, rows, D_MODEL), lambda l, j: (l, j, 0, 0)),
        out_shape=jax.ShapeDtypeStruct((DEPTH, N_MOD, rows, D_MODEL), F32),
        compiler_params=pltpu.CompilerParams(dimension_semantics=("arbitrary", "arbitrary"),
                                             vmem_limit_bytes=VMEM_LIMIT),
        name="adaln_mod",
    )(c_all, w_mod, b_mod.reshape(DEPTH, N_MOD, 1, D_MODEL))


def _swiglu_tile(x, g_norm, shift, scale, gate, wg_ref, wu_ref, wd_ref, nf_ref, final):
    h = _norm_mod(x, g_norm, shift, scale).astype(BF16)
    a = (_silu(_dot(h, wg_ref[...])) * _dot(h, wu_ref[...])).astype(BF16)
    y = x + (0.5 * gate) * _dot(a, wd_ref[...])
    if final:
        y = (y * lax.rsqrt(jnp.mean(y * y, axis=-1, keepdims=True) + EPS)) * nf_ref[...]
    return y


DOWN_CAST_ROWS = 128


def _ffn_kernel(xp_ref, xs_ref, modp_ref, mods_ref, n_ref, wg_ref, wu_ref, wd_ref, nf_ref, *rest,
                tiles_per_seq, prompt_tiles, sample_reps, final, cast_layer):
    i = pl.program_id(0)
    if cast_layer is None:
        op_ref, os_ref = rest
    else:
        (ng_hbm, nu_hbm, nd_hbm, op_ref, os_ref, og_hbm, ou_hbm, od_hbm,
         sg_ref, su_ref, sd_ref, tg_ref, tu_ref, td_ref, sem) = rest
        up_rows = D_MODEL // prompt_tiles
        down_steps = D_FF // DOWN_CAST_ROWS

        def up_copies(step, into_vmem):
            rows = pl.ds(pl.multiple_of(step * up_rows, up_rows), up_rows)
            if into_vmem:
                return (pltpu.make_async_copy(ng_hbm.at[cast_layer, rows, :], sg_ref, sem.at[0]),
                        pltpu.make_async_copy(nu_hbm.at[cast_layer, rows, :], su_ref, sem.at[1]))
            return (pltpu.make_async_copy(tg_ref, og_hbm.at[rows, :], sem.at[2]),
                    pltpu.make_async_copy(tu_ref, ou_hbm.at[rows, :], sem.at[3]))

        def down_copy(step, into_vmem):
            rows = pl.ds(pl.multiple_of(step * DOWN_CAST_ROWS, DOWN_CAST_ROWS), DOWN_CAST_ROWS)
            if into_vmem:
                return pltpu.make_async_copy(nd_hbm.at[cast_layer, rows, :], sd_ref, sem.at[4])
            return pltpu.make_async_copy(td_ref, od_hbm.at[rows, :], sem.at[5])

        @pl.when(i < prompt_tiles)
        def _():
            for cp in up_copies(i, True):
                cp.start()

        @pl.when(i < down_steps)
        def _():
            down_copy(i, True).start()

    @pl.when(i < prompt_tiles)
    def _():
        row = i // tiles_per_seq
        shift, scale, gate = (modp_ref[j, pl.ds(row, 1), :] for j in range(3))
        op_ref[...] = _swiglu_tile(xp_ref[...], n_ref[...], shift, scale, gate, wg_ref, wu_ref, wd_ref, nf_ref, final)

    @pl.when(i == prompt_tiles)
    def _():
        shift, scale, gate = (jnp.concatenate([mods_ref[j]] * sample_reps, axis=0) for j in range(3))
        os_ref[...] = _swiglu_tile(xs_ref[...], n_ref[...], shift, scale, gate, wg_ref, wu_ref, wd_ref, nf_ref, final)

    if cast_layer is not None:
        @pl.when((i >= 1) & (i <= prompt_tiles))
        def _():
            for cp in up_copies(i - 1, False):
                cp.wait()

        @pl.when((i >= 1) & (i <= down_steps))
        def _():
            down_copy(i - 1, False).wait()

        @pl.when(i < prompt_tiles)
        def _():
            for cp in up_copies(i, True):
                cp.wait()
            tg_ref[...] = sg_ref[...].astype(BF16)
            tu_ref[...] = su_ref[...].astype(BF16)
            for cp in up_copies(i, False):
                cp.start()

        @pl.when(i < down_steps)
        def _():
            down_copy(i, True).wait()
            td_ref[...] = sd_ref[...].astype(BF16)
            down_copy(i, False).start()


def _ffn(xp, xs, mod, norm, wg, wu, wd, next_weights, norm_final, layer, j0, *, tm, seqs_p, seqs_s, final=False):
    rows_p, rows_s = xp.shape[0], xs.shape[0]
    prompt_tiles = rows_p // tm
    cast_layer = None if next_weights is None else next_weights[3]
    kern = functools.partial(_ffn_kernel, tiles_per_seq=(rows_p // seqs_p) // tm, prompt_tiles=prompt_tiles,
                             sample_reps=rows_s // seqs_s, final=final, cast_layer=cast_layer)
    ptile = lambda i: (jnp.minimum(i, prompt_tiles - 1), 0)
    weight = lambda *shape: pl.BlockSpec(shape, lambda i: (0, 0), pipeline_mode=pl.Buffered(1))
    hbm = pl.BlockSpec(memory_space=pl.ANY)
    in_specs = [
        pl.BlockSpec((tm, D_MODEL), ptile),
        pl.BlockSpec((rows_s, D_MODEL), lambda i: (0, 0)),
        pl.BlockSpec((None, 3, seqs_p, D_MODEL), lambda i: (layer, j0 // 3, seqs_s // seqs_p, 0)),
        pl.BlockSpec((None, 3, seqs_s, D_MODEL), lambda i: (layer, j0 // 3, 0, 0)),
        pl.BlockSpec((None, 1, D_MODEL), lambda i: (layer, 0, 0)),
        weight(D_MODEL, D_FF),
        weight(D_MODEL, D_FF),
        weight(D_FF, D_MODEL),
        pl.BlockSpec((1, D_MODEL), lambda i: (0, 0)),
    ]
    out_specs = [pl.BlockSpec((tm, D_MODEL), ptile), pl.BlockSpec((rows_s, D_MODEL), lambda i: (0, 0))]
    out_shape = [jax.ShapeDtypeStruct((rows_p, D_MODEL), F32), jax.ShapeDtypeStruct((rows_s, D_MODEL), F32)]
    operands = [xp, xs, mod, mod, norm.reshape(DEPTH, 1, D_MODEL), wg, wu, wd, norm_final.reshape(1, D_MODEL)]
    scratch = []
    if next_weights is not None:
        assert D_MODEL % prompt_tiles == 0 and (D_MODEL // prompt_tiles) % 16 == 0
        assert D_FF % DOWN_CAST_ROWS == 0 and D_FF // DOWN_CAST_ROWS <= prompt_tiles
        up_rows = D_MODEL // prompt_tiles
        in_specs += [hbm, hbm, hbm]
        operands += list(next_weights[:3])
        out_specs += [hbm, hbm, hbm]
        out_shape += [jax.ShapeDtypeStruct((D_MODEL, D_FF), BF16), jax.ShapeDtypeStruct((D_MODEL, D_FF), BF16),
                      jax.ShapeDtypeStruct((D_FF, D_MODEL), BF16)]
        scratch = [pltpu.VMEM((up_rows, D_FF), F32), pltpu.VMEM((up_rows, D_FF), F32),
                   pltpu.VMEM((DOWN_CAST_ROWS, D_MODEL), F32),
                   pltpu.VMEM((up_rows, D_FF), BF16), pltpu.VMEM((up_rows, D_FF), BF16),
                   pltpu.VMEM((DOWN_CAST_ROWS, D_MODEL), BF16),
                   pltpu.SemaphoreType.DMA((6,))]
    return pl.pallas_call(
        kern,
        grid=(prompt_tiles + 1,),
        in_specs=in_specs,
        out_specs=out_specs,
        out_shape=out_shape,
        scratch_shapes=scratch,
        compiler_params=pltpu.CompilerParams(dimension_semantics=("arbitrary",), vmem_limit_bytes=VMEM_LIMIT),
        name="swiglu_half_step",
    )(*operands)


def _lower_bound_kernel(x_ref, o_ref):
    rows = [x_ref[l:l + 1, :] for l in range(DEPTH)]
    m = functools.reduce(jnp.maximum, rows)
    e = [jnp.exp(r - m) for r in rows]
    s = functools.reduce(lambda a, b: a + b, e)
    sm = [ei / s for ei in e]
    acc = sm[0]
    for l in range(DEPTH):
        if l > 0:
            acc = acc + sm[l]
        o_ref[l:l + 1, :] = acc - sm[0]


def _lower_bounds(hgrn_lower):
    return pl.pallas_call(
        _lower_bound_kernel,
        out_shape=jax.ShapeDtypeStruct((DEPTH, HG_WIDTH), F32),
        name="hgrn_lower_bounds",
    )(hgrn_lower)


def _bucket_map(rows, cols, valid_rows, col_lo, col_hi):
    r = np.arange(rows)[:, None]
    c = np.arange(cols)[None, :]
    rel = c - WINDOW - r
    n = np.maximum(-rel, 0)
    exact = REL_BUCKETS // 2
    nf = np.maximum(n, 1).astype(np.float32)
    large = exact + (np.log(nf / np.float32(exact)) / np.float32(math.log(REL_MAX_DIST / exact))
                     * np.float32(REL_BUCKETS - exact)).astype(np.int32)
    large = np.minimum(large, REL_BUCKETS - 1)
    bucket = np.where(n < exact, n, large)
    valid = (rel <= 0) & (rel > -WINDOW) & (r < valid_rows) & (c >= col_lo) & (c < col_hi)
    return np.where(valid, bucket, -1).astype(np.int32)


def _bias_table_kernel(bkt_ref, rb_ref, o_ref):
    h = pl.program_id(1)
    bkt = bkt_ref[...]
    acc = jnp.full(bkt.shape, NEG, F32)
    for b in range(REL_BUCKETS):
        acc = jnp.where(bkt == b, rb_ref[b, h], acc)
    o_ref[...] = acc


def _bias_tables(bucket_maps, rel_bias):
    m, r, c = bucket_maps.shape
    return pl.pallas_call(
        _bias_table_kernel,
        grid=(m, ATT_HEADS),
        in_specs=[pl.BlockSpec((None, r, c), lambda i, h: (i, 0, 0)),
                  pl.BlockSpec(memory_space=pltpu.SMEM)],
        out_specs=pl.BlockSpec((None, None, r, c), lambda i, h: (i, h, 0, 0)),
        out_shape=jax.ShapeDtypeStruct((m, ATT_HEADS, r, c), F32),
        compiler_params=pltpu.CompilerParams(dimension_semantics=("arbitrary", "arbitrary")),
        name="rel_bias_tables",
    )(bucket_maps, rel_bias)


def _block_ones(n, blk):
    i = np.arange(n)
    return (i[:, None] // blk == i[None, :] // blk)


def _mixer_consts(tt):
    tri = jnp.asarray(np.tril(np.ones((tt, tt), np.float32)), BF16)
    blk = _block_ones(HG_WIDTH, HG_DK)
    return tri, jnp.asarray(blk, BF16), jnp.asarray(blk, F32)


def _pool_block_diag(pool_w):
    out = jnp.zeros((DEPTH, POOL_WIDTH, POOL_WIDTH), BF16)
    for g in range(len(POOL_WINDOWS)):
        sl = slice(g * POOL_GW, (g + 1) * POOL_GW)
        out = out.at[:, sl, sl].set(pool_w[:, g].astype(BF16))
    return out


def _head_masks():
    lane = lax.broadcasted_iota(jnp.int32, (1, HG_WIDTH), 1)
    return [jnp.where((lane >= HG_DK * h) & (lane < HG_DK * (h + 1)), 1.0, 0.0).astype(F32) for h in range(HG_HEADS)]


def _prompt_mixer_kernel(x_ref, xn_ref, mod_ref, n_ref, win_ref, wout_ref, pbd_ref, pscale_ref, lb_ref, hgn_ref,
                         bm_ref, sink_ref, tri_ref, ones_ref, bdm_ref,
                         xo_ref, pool_o, s_o, k_o, v_o,
                         za_ref, zb_ref, ubuf, kbuf, vbuf, st_ref, cat_ref, *, layer, tt, nt, ntiles):
    b = pl.program_id(0)
    p = pl.program_id(1)
    last_pair = pl.num_programs(1) - 1
    nxt = jnp.minimum(b * nt + 2 * p + 2, ntiles - 1)
    bn = nxt // nt
    n_chunks = 8
    cw = IN_WIDTH // n_chunks

    def projector(x_rows, bsel, dst_ref):
        h = _norm_mod(x_rows, n_ref[...], mod_ref[3, pl.ds(bsel, 1), :], mod_ref[4, pl.ds(bsel, 1), :]).astype(BF16)
        state = {"k": 0}

        def step():
            k = state["k"]
            if k < n_chunks:
                dst_ref[:, k * cw:(k + 1) * cw] = _dot(h, win_ref[:, k * cw:(k + 1) * cw])
                state["k"] = k + 1

        return step

    @pl.when((b == 0) & (p == 0))
    def _():
        first_proj = projector(x_ref[0:tt, :], b, za_ref)
        for _ in range(n_chunks):
            first_proj()

    @pl.when(p == 0)
    def _():
        ubuf[0:16, :] = jnp.zeros((16, POOL_WIDTH), F32)
        kbuf[:, 0:WINDOW, :] = jnp.zeros((KV_HEADS, WINDOW, HEAD_DIM), BF16)
        vbuf[:, 0:WINDOW, :] = jnp.zeros((KV_HEADS, WINDOW, HEAD_DIM), BF16)
        st_ref[...] = jnp.zeros((HG_WIDTH, HG_WIDTH), F32)

    gate = mod_ref[5, pl.ds(b, 1), :]

    def mix(z_ref, x, row0, t, side):
        lane = lax.broadcasted_iota(jnp.int32, (1, POOL_WIDTH), 1)

        u = z_ref[:, OFF_POOL:OFF_POOL + POOL_WIDTH]
        ubuf[16:16 + tt, :] = u
        e = ubuf[...]
        s2 = e + pltpu.roll(e, 1, 0)
        s4 = s2 + pltpu.roll(s2, 2, 0)
        s8 = s4 + pltpu.roll(s4, 4, 0)
        s16 = s8 + pltpu.roll(s8, 8, 0)
        sel = jnp.where(lane < 64, s2, jnp.where(lane < 128, s4, jnp.where(lane < 192, s8, s16)))[16:]
        wl = jnp.where(lane < 64, 2.0, jnp.where(lane < 128, 4.0, jnp.where(lane < 192, 8.0, 16.0))).astype(F32)
        pos1 = (t * tt + 1 + lax.broadcasted_iota(jnp.int32, (tt, 1), 0)).astype(F32)
        dpool = sel / jnp.minimum(pos1, wl) - u
        ypool = _dot(dpool.astype(BF16), pbd_ref[...]) * pscale_ref[...]
        cat_ref[:, 0:POOL_WIDTH] = ypool.astype(BF16)
        ubuf[0:16, :] = ubuf[tt:tt + 16, :]
        side()


        for g in range(KV_HEADS):
            kbuf[g, WINDOW:WINDOW + tt, :] = z_ref[:, OFF_AK + HEAD_DIM * g:OFF_AK + HEAD_DIM * (g + 1)].astype(BF16)
            vbuf[g, WINDOW:WINDOW + tt, :] = z_ref[:, OFF_AV + HEAD_DIM * g:OFF_AV + HEAD_DIM * (g + 1)].astype(BF16)
        first = jnp.where(t == 0, 1, 0)
        bands = [(j, g) for j in range(tt // WINDOW) for g in range(KV_HEADS)]
        logits = []
        for j, g in bands:
            r0 = j * WINDOW
            c0 = OFF_AQ + HEAD_DIM * g * GQA_GROUP
            qg = jnp.concatenate([z_ref[r0:r0 + WINDOW, c0 + HEAD_DIM * i:c0 + HEAD_DIM * (i + 1)]
                                  for i in range(GQA_GROUP)], axis=0)
            qg = (qg * (HEAD_DIM ** -0.5)).astype(BF16)
            logits.append(_dot_nt(qg, kbuf[g, r0:r0 + 2 * WINDOW, :]))
        side()

        lb = lb_ref[...]
        q = _silu(z_ref[:, OFF_HQ:OFF_HQ + HG_WIDTH])
        f = lb + (1.0 - lb) * _sigmoid(z_ref[:, OFF_HF:OFF_HF + HG_WIDTH])
        k = 1.0 - f
        v = z_ref[:, OFF_HI:OFF_HI + HG_WIDTH]
        vb = v.astype(BF16)
        tri = tri_ref[...]
        hi, mid, lo = _split3(jnp.log(f))
        bcum = _dot(tri, hi) + _dot(tri, mid) + _dot(tri, lo)
        side()
        ones_blk = ones_ref[...]
        hm = _head_masks()

        n16 = tt // 16
        half_shape = (n16, 2, 8, HG_WIDTH)
        q4 = q.reshape(half_shape)
        b4 = bcum.reshape(half_shape)
        c4 = (bcum - jnp.log(k)).reshape(half_shape)
        v4 = v.reshape(half_shape)
        rowi = lax.broadcasted_iota(jnp.int32, (1, 8, 1), 1)
        o_up = jnp.zeros((n16, 8, HG_WIDTH), F32)
        o_dn = jnp.zeros((n16, 8, HG_WIDTH), F32)
        nh = n16 * 8
        for s in range(8):
            cs = c4[:, 0, s:s + 1, :]
            vs = v4[:, 0, s:s + 1, :]
            x_up = jnp.where(rowi >= s, q4[:, 0] * jnp.exp(b4[:, 0] - cs), 0.0)
            x_dn = q4[:, 1] * jnp.exp(b4[:, 1] - cs)
            xs = jnp.concatenate([x_up.reshape(nh, HG_WIDTH), x_dn.reshape(nh, HG_WIDTH)], axis=0)
            r = _dot(xs.astype(BF16), ones_blk)
            o_up = o_up + r[0:nh].reshape(n16, 8, HG_WIDTH) * vs
            o_dn = o_dn + r[nh:2 * nh].reshape(n16, 8, HG_WIDTH) * vs
            if s % 4 == 3:
                side()
        for s in range(8):
            cs = c4[:, 1, s:s + 1, :]
            x_dn = jnp.where(rowi >= s, q4[:, 1] * jnp.exp(b4[:, 1] - cs), 0.0)
            r = _dot(x_dn.reshape(nh, HG_WIDTH).astype(BF16), ones_blk)
            o_dn = o_dn + r.reshape(n16, 8, HG_WIDTH) * v4[:, 1, s:s + 1, :]
            if s % 4 == 3:
                side()
        o_tot = jnp.concatenate([o_up[:, None], o_dn[:, None]], axis=1).reshape(tt, HG_WIDTH)

        blocks = []
        blk = 32
        while blk <= tt:
            blocks += [(i * blk, i * blk + blk // 2, (i + 1) * blk) for i in range(tt // blk)]
            blk *= 2
        scores = []
        for lo_s, mid_s, hi_s in blocks:
            ref = bcum[mid_s - 1:mid_s, :]
            qt = q[mid_s:hi_s] * jnp.exp(bcum[mid_s:hi_s] - ref)
            kt = (k[lo_s:mid_s] * jnp.exp(ref - bcum[lo_s:mid_s])).astype(BF16)
            qs = jnp.concatenate([qt * hm[hh] for hh in range(HG_HEADS)], axis=0).astype(BF16)
            scores.append(_dot_nt(qs, kt).astype(BF16))

        st = st_ref[...]
        qin = (q * jnp.exp(bcum)).astype(BF16)
        o_tot = o_tot + _dot_nt(qin, st.astype(BF16))
        blast = bcum[tt - 1:tt, :]
        kout = (k * jnp.exp(blast - bcum)).astype(BF16)
        st_ref[...] = st * jnp.exp(blast) + _dot_tn(vb, kout) * bdm_ref[...]
        side()

        probs, dens = [], []
        for (j, g), lg_all in zip(bands, logits):
            h0 = g * GQA_GROUP
            bm = bm_ref[first, h0:h0 + GQA_GROUP] if j == 0 else bm_ref[0, h0:h0 + GQA_GROUP]
            ps = []
            for i in range(GQA_GROUP):
                lg = lg_all[i * WINDOW:(i + 1) * WINDOW] + bm[i]
                sink = sink_ref[layer, h0 + i]
                m = jnp.maximum(jnp.max(lg, axis=-1, keepdims=True), sink)
                p_ = jnp.exp(lg - m)
                dens.append(jnp.sum(p_, axis=-1, keepdims=True) + jnp.exp(sink - m))
                ps.append(p_.astype(BF16))
            probs.append(jnp.concatenate(ps, axis=0))
        for n, (j, g) in enumerate(bands):
            r0 = j * WINDOW
            og = _dot(probs[n], vbuf[g, r0:r0 + 2 * WINDOW, :])
            for i in range(GQA_GROUP):
                cc = POOL_WIDTH + HG_WIDTH + HEAD_DIM * (g * GQA_GROUP + i)
                oh = og[i * WINDOW:(i + 1) * WINDOW] / dens[n * GQA_GROUP + i]
                cat_ref[r0:r0 + WINDOW, cc:cc + HEAD_DIM] = oh.astype(BF16)
        for g in range(KV_HEADS):
            kbuf[g, 0:WINDOW, :] = kbuf[g, tt:tt + WINDOW, :]
            vbuf[g, 0:WINDOW, :] = vbuf[g, tt:tt + WINDOW, :]
        side()

        adds = {}
        for (lo_s, mid_s, hi_s), a in zip(blocks, scores):
            half = mid_s - lo_s
            ov = _dot(a, vb[lo_s:mid_s])
            oi = ov[0:half] * hm[0]
            for hh in range(1, HG_HEADS):
                oi = oi + ov[hh * half:(hh + 1) * half] * hm[hh]
            adds.setdefault(2 * half, []).extend([jnp.zeros((half, HG_WIDTH), F32), oi])
        for parts in adds.values():
            o_tot = o_tot + jnp.concatenate(parts, axis=0)

        sq_hi, sq_mid, _ = _split3(o_tot * o_tot)
        ms = (_dot(sq_hi, ones_blk) + _dot(sq_mid, ones_blk)) * (1.0 / HG_DK)
        y_hg = o_tot * lax.rsqrt(ms + EPS) * hgn_ref[...] * _silu(z_ref[:, OFF_HG:OFF_HG + HG_WIDTH])
        cat_ref[:, POOL_WIDTH:POOL_WIDTH + HG_WIDTH] = y_hg.astype(BF16)

        xo_ref[row0:row0 + tt, :] = x + gate * _dot(cat_ref[...], wout_ref[...])

        for _ in range(n_chunks):
            side()

    mix(za_ref, x_ref[0:tt, :], 0, 2 * p, projector(x_ref[tt:2 * tt, :], b, zb_ref))
    mix(zb_ref, x_ref[tt:2 * tt, :], tt, 2 * p + 1, projector(xn_ref[...], bn, za_ref))

    @pl.when(p == last_pair)
    def _():
        pool_o[...] = ubuf[pl.ds(1, POOL_BUF), :]
        s_kv = st_ref[...].T
        for hh in range(HG_HEADS):
            s_o[hh] = s_kv[HG_DK * hh:HG_DK * (hh + 1), HG_DK * hh:HG_DK * (hh + 1)]
        k_o[...] = zb_ref[tt - WINDOW:tt, OFF_AK:OFF_AV]
        v_o[...] = zb_ref[tt - WINDOW:tt, OFF_AV:IN_WIDTH]


def _prompt_mixer(x, mod, norm, w_in, w_out, pool_bd, pool_scale, lbs, hg_norm, bm, sinks, consts, layer, *,
                  tt, mod_row_block):
    bsz, seq, _ = x.shape
    tri, ones_blk, bdm = consts
    nt = seq // tt
    ntiles = bsz * nt
    kern = functools.partial(_prompt_mixer_kernel, layer=layer, tt=tt, nt=nt, ntiles=ntiles)
    full = lambda *shape: pl.BlockSpec(shape, lambda b, p: (0,) * len(shape))
    lsel = lambda *shape: pl.BlockSpec((None,) + shape, lambda b, p: (layer,) + (0,) * len(shape))
    mix_pair = lambda b, p: (b, p, 0)
    mix_seq = lambda *z: (lambda b, p: (b,) + z)

    def next_tile(b, p):
        nxt = jnp.minimum(b * nt + 2 * p + 2, ntiles - 1)
        return (nxt // nt, nxt % nt, 0)

    return pl.pallas_call(
        kern,
        grid=(bsz, nt // 2),
        in_specs=[
            pl.BlockSpec((None, 2 * tt, D_MODEL), mix_pair),
            pl.BlockSpec((None, tt, D_MODEL), next_tile),
            pl.BlockSpec((None, N_MOD, 8, D_MODEL), lambda b, p: (layer, 0, mod_row_block, 0)),
            lsel(1, D_MODEL),
            lsel(D_MODEL, IN_WIDTH),
            lsel(D_MODEL, D_MODEL),
            lsel(POOL_WIDTH, POOL_WIDTH),
            lsel(1, POOL_WIDTH),
            lsel(1, HG_WIDTH),
            lsel(1, HG_WIDTH),
            full(2, ATT_HEADS, WINDOW, 2 * WINDOW),
            pl.BlockSpec(memory_space=pltpu.SMEM),
            full(tt, tt),
            full(HG_WIDTH, HG_WIDTH),
            full(HG_WIDTH, HG_WIDTH),
        ],
        out_specs=[
            pl.BlockSpec((None, 2 * tt, D_MODEL), mix_pair),
            pl.BlockSpec((None, POOL_BUF, POOL_WIDTH), mix_seq(0, 0)),
            pl.BlockSpec((None, HG_HEADS, HG_DK, HG_DK), mix_seq(0, 0, 0)),
            pl.BlockSpec((None, WINDOW, KV_HEADS * HEAD_DIM), mix_seq(0, 0)),
            pl.BlockSpec((None, WINDOW, KV_HEADS * HEAD_DIM), mix_seq(0, 0)),
        ],
        out_shape=[
            jax.ShapeDtypeStruct((bsz, seq, D_MODEL), F32),
            jax.ShapeDtypeStruct((bsz, POOL_BUF, POOL_WIDTH), F32),
            jax.ShapeDtypeStruct((bsz, HG_HEADS, HG_DK, HG_DK), F32),
            jax.ShapeDtypeStruct((bsz, WINDOW, KV_HEADS * HEAD_DIM), F32),
            jax.ShapeDtypeStruct((bsz, WINDOW, KV_HEADS * HEAD_DIM), F32),
        ],
        scratch_shapes=[
            pltpu.VMEM((tt, IN_WIDTH), F32),
            pltpu.VMEM((tt, IN_WIDTH), F32),
            pltpu.VMEM((16 + tt, POOL_WIDTH), F32),
            pltpu.VMEM((KV_HEADS, WINDOW + tt, HEAD_DIM), BF16),
            pltpu.VMEM((KV_HEADS, WINDOW + tt, HEAD_DIM), BF16),
            pltpu.VMEM((HG_WIDTH, HG_WIDTH), F32),
            pltpu.VMEM((tt, D_MODEL), BF16),
        ],
        compiler_params=pltpu.CompilerParams(dimension_semantics=("arbitrary", "arbitrary"),
                                             vmem_limit_bytes=VMEM_LIMIT),
        name="prompt_mixer",
    )(x, x, mod, norm.reshape(DEPTH, 1, D_MODEL), w_in, w_out, pool_bd, pool_scale.reshape(DEPTH, 1, POOL_WIDTH),
      lbs.reshape(DEPTH, 1, HG_WIDTH), hg_norm, bm, sinks, tri, ones_blk, bdm)


def _sample_mixer_kernel(x_ref, mod_ref, n_ref, win_ref, wout_ref, pbd_ref, pscale_ref, lb_ref, hgn_ref,
                         bms_ref, sinkc_ref, ones_ref, sp_ref, s0_ref, kt_ref, vt_ref,
                         xo_ref, pool_o, s_o, kt_o, vt_o,
                         z_ref, cat_ref, qs_ref, os_ref, kn_ref, vn_ref, qint_ref, koutt_ref, vtt_ref, gt_ref,
                         oin_ref, oacc_ref, *, steps, nb, gb, rb, pos0):
    g = pl.program_id(0)
    last = pl.num_programs(0) - 1
    base = pl.multiple_of(g * gb, gb)
    kvw = KV_HEADS * HEAD_DIM

    def rows(t):
        return slice(t * nb, (t + 1) * nb)

    @pl.when(g == 0)
    def _():
        for t in range(steps):
            h = _norm_mod(x_ref[t], n_ref[...], mod_ref[0], mod_ref[1]).astype(BF16)
            z_ref[rows(t), :] = _dot(h, win_ref[...])
        lane = lax.broadcasted_iota(jnp.int32, (1, POOL_WIDTH), 1)

        ext = [sp_ref[r] for r in range(POOL_BUF)] + [z_ref[rows(t), OFF_POOL:OFF_POOL + POOL_WIDTH] for t in range(steps)]
        for r in range(POOL_BUF):
            pool_o[r] = ext[r + steps]
        sums = {1: ext}
        w = 1
        while w < 16:
            prev = sums[w]
            sums[2 * w] = [None if (r < 2 * w - 1) else prev[r] + prev[r - w] for r in range(len(ext))]
            w *= 2
        dl = []
        for t in range(steps):
            r = POOL_BUF + t
            sel = jnp.where(lane < 64, sums[2][r], jnp.where(lane < 128, sums[4][r], jnp.where(lane < 192, sums[8][r], sums[16][r])))
            cnt = [float(min(pos0 + t + 1, wd)) for wd in POOL_WINDOWS]
            cl = jnp.where(lane < 64, cnt[0], jnp.where(lane < 128, cnt[1], jnp.where(lane < 192, cnt[2], cnt[3]))).astype(F32)
            dl.append(sel / cl - ext[r])
        ypool = _dot(jnp.concatenate(dl, axis=0).astype(BF16), pbd_ref[...]) * pscale_ref[...]
        cat_ref[:, 0:POOL_WIDTH] = ypool

        lb = lb_ref[...]
        qv, kv_, vv, bv = [], [], [], []
        bacc = None
        for t in range(steps):
            qv.append(_silu(z_ref[rows(t), OFF_HQ:OFF_HQ + HG_WIDTH]))
            f = lb + (1.0 - lb) * _sigmoid(z_ref[rows(t), OFF_HF:OFF_HF + HG_WIDTH])
            kv_.append(1.0 - f)
            vv.append(z_ref[rows(t), OFF_HI:OFF_HI + HG_WIDTH])
            bacc = jnp.log(f) if bacc is None else bacc + jnp.log(f)
            bv.append(bacc)
        pairs = [(t, s) for t in range(steps) for s in range(t + 1)]
        xs = jnp.concatenate([qv[t] * kv_[s] * jnp.exp(bv[t] - bv[s]) for t, s in pairs], axis=0).astype(BF16)
        rr = _dot(xs, ones_ref[...])
        o_intra = [None] * steps
        for i, (t, s) in enumerate(pairs):
            term = rr[i * nb:(i + 1) * nb] * vv[s]
            o_intra[t] = term if o_intra[t] is None else o_intra[t] + term
        blast = bv[steps - 1]
        gt_ref[...] = jnp.exp(blast).T
        for t in range(steps):
            oin_ref[t] = o_intra[t]
            qint_ref[t] = (qv[t] * jnp.exp(bv[t])).T
            koutt_ref[t] = (kv_[t] * jnp.exp(blast - bv[t])).T
            vtt_ref[t] = vv[t].T
        oacc_ref[...] = jnp.zeros(oacc_ref.shape, F32)

        zeros64 = jnp.zeros((nb, HEAD_DIM), F32)
        kn_ref[...] = jnp.zeros(kn_ref.shape, F32)
        vn_ref[...] = jnp.zeros(vn_ref.shape, F32)
        for t in range(steps):
            kn_ref[:, t, :] = z_ref[rows(t), OFF_AK:OFF_AK + kvw]
            vn_ref[:, t, :] = z_ref[rows(t), OFF_AV:OFF_AV + kvw]
            for hd in range(ATT_HEADS):
                c0 = OFF_AQ + HEAD_DIM * hd
                qsl = z_ref[rows(t), c0:c0 + HEAD_DIM] * (HEAD_DIM ** -0.5)
                parts = [qsl, zeros64] if hd < GQA_GROUP else [zeros64, qsl]
                qs_ref[:, ATT_HEADS * t + hd, :] = jnp.concatenate(parts, axis=1)

    r0 = pl.multiple_of(g * rb, rb)
    hrow = pl.multiple_of((g * rb // HG_DK) * HG_DK, HG_DK)
    vts = [vtt_ref[t, pl.ds(hrow, HG_DK), :] for t in range(steps)]
    accs = [oacc_ref[t, pl.ds(hrow, HG_DK), :] for t in range(steps)]
    for i in range(rb):
        s0 = s0_ref[i]
        snew = gt_ref[pl.ds(r0 + i, 1), :] * s0
        for t in range(steps):
            accs[t] = accs[t] + qint_ref[t, pl.ds(r0 + i, 1), :] * s0
            snew = snew + koutt_ref[t, pl.ds(r0 + i, 1), :] * vts[t]
        s_o[i] = snew
    for t in range(steps):
        oacc_ref[t, pl.ds(hrow, HG_DK), :] = accs[t]

    bm_old = bms_ref[:, 0:WINDOW]
    bm_new = bms_ref[:, WINDOW:WINDOW + 8]
    sinkc = sinkc_ref[...]
    seqs = range(gb)
    qn = [qs_ref[base + n].astype(BF16) for n in seqs]
    l_old = [_dot(qn[n], kt_ref[n].astype(BF16)) for n in seqs]
    l_new = [_dot_nt(qn[n], kn_ref[base + n].astype(BF16)) for n in seqs]
    def new_columns(ref):
        new = ref[pl.ds(base, gb)].reshape(gb * 8, kvw)
        if gb * 8 < WINDOW:
            new = jnp.concatenate([new, jnp.zeros((WINDOW - gb * 8, kvw), F32)], axis=0)
        return new.T

    knt = new_columns(kn_ref)
    vnt = new_columns(vn_ref)
    for n in seqs:
        kt_o[n] = jnp.concatenate([kt_ref[n][:, steps:], knt[:, 8 * n:8 * n + steps]], axis=1)
        vt_o[n] = jnp.concatenate([vt_ref[n][:, steps:], vnt[:, 8 * n:8 * n + steps]], axis=1)
    p_old, p_new, den = [], [], []
    for n in seqs:
        lo_, ln_ = l_old[n] + bm_old, l_new[n] + bm_new
        m = jnp.maximum(jnp.maximum(jnp.max(lo_, axis=-1, keepdims=True), jnp.max(ln_, axis=-1, keepdims=True)), sinkc)
        po, pn = jnp.exp(lo_ - m), jnp.exp(ln_ - m)
        den.append(jnp.sum(po, axis=-1, keepdims=True) + jnp.sum(pn, axis=-1, keepdims=True) + jnp.exp(sinkc - m))
        p_old.append(po.astype(BF16))
        p_new.append(pn.astype(BF16))
    for n in seqs:
        o = _dot_nt(p_old[n], vt_ref[n].astype(BF16)) + _dot(p_new[n], vn_ref[base + n].astype(BF16))
        os_ref[n] = o / den[n]
    for t in range(steps):
        for hd in range(ATT_HEADS):
            c0 = HEAD_DIM * (hd // GQA_GROUP)
            cc = POOL_WIDTH + HG_WIDTH + HEAD_DIM * hd
            cat_ref[pl.ds(t * nb + base, gb), cc:cc + HEAD_DIM] = os_ref[:, ATT_HEADS * t + hd, c0:c0 + HEAD_DIM]

    @pl.when(g == last)
    def _():
        ones_blk = ones_ref[...]
        for t in range(steps):
            o = oin_ref[t] + oacc_ref[t].T
            sq_hi, sq_mid, _ = _split3(o * o)
            ms = (_dot(sq_hi, ones_blk) + _dot(sq_mid, ones_blk)) * (1.0 / HG_DK)
            gate_t = _silu(z_ref[rows(t), OFF_HG:OFF_HG + HG_WIDTH])
            cat_ref[rows(t), POOL_WIDTH:POOL_WIDTH + HG_WIDTH] = o * lax.rsqrt(ms + EPS) * hgn_ref[...] * gate_t
        for t in range(steps):
            y = _dot(cat_ref[rows(t), :].astype(BF16), wout_ref[...])
            xo_ref[t] = x_ref[t] + mod_ref[2] * y


N_SAMPLE_MIXER_INPUTS = 16


def _sample_mixer_kernel_into(*refs, n_stacks, **kw):
    _sample_mixer_kernel(*refs[:N_SAMPLE_MIXER_INPUTS], *refs[N_SAMPLE_MIXER_INPUTS + n_stacks:], **kw)


def _sample_mixer(x, mod, norm, w_in, w_out, pool_bd, pool_scale, lbs, hg_norm, bms, sink_col, ones_blk,
                  pool_hist, hgrn_state, cache_kt, cache_vt, stacks, layer, *, gb, pos0):
    steps, nb, _ = x.shape
    rows_th = steps * ATT_HEADS
    ngrp = nb // gb
    rb = HG_WIDTH // ngrp
    kvw = KV_HEADS * HEAD_DIM
    stacks = () if stacks is None else tuple(stacks)
    kern = functools.partial(_sample_mixer_kernel_into, n_stacks=len(stacks), steps=steps, nb=nb, gb=gb, rb=rb,
                             pos0=pos0)
    full = lambda *shape: pl.BlockSpec(shape, lambda g: (0,) * len(shape))
    lsel = lambda *shape: pl.BlockSpec((None,) + shape, lambda g: (layer,) + (0,) * len(shape))
    grp = lambda *shape: pl.BlockSpec((None,) + shape, lambda g: (layer, g) + (0,) * (len(shape) - 1))
    return pl.pallas_call(
        kern,
        grid=(ngrp,),
        in_specs=[
            full(steps, nb, D_MODEL),
            pl.BlockSpec((None, 3, nb, D_MODEL), lambda g: (layer, 1, 0, 0)),
            lsel(1, D_MODEL),
            lsel(D_MODEL, IN_WIDTH),
            lsel(D_MODEL, D_MODEL),
            lsel(POOL_WIDTH, POOL_WIDTH),
            lsel(1, POOL_WIDTH),
            lsel(1, HG_WIDTH),
            lsel(1, HG_WIDTH),
            full(rows_th, 2 * WINDOW),
            lsel(rows_th, 1),
            full(HG_WIDTH, HG_WIDTH),
            lsel(POOL_BUF, nb, POOL_WIDTH),
            grp(rb, HG_DK, nb),
            grp(gb, kvw, WINDOW),
            grp(gb, kvw, WINDOW),
        ] + [pl.BlockSpec(memory_space=pl.ANY)] * len(stacks),
        out_specs=[
            full(steps, nb, D_MODEL),
            lsel(POOL_BUF, nb, POOL_WIDTH),
            grp(rb, HG_DK, nb),
            grp(gb, kvw, WINDOW),
            grp(gb, kvw, WINDOW),
        ],
        out_shape=[
            jax.ShapeDtypeStruct((steps, nb, D_MODEL), F32),
            jax.ShapeDtypeStruct((DEPTH, POOL_BUF, nb, POOL_WIDTH), F32),
            jax.ShapeDtypeStruct((DEPTH, HG_WIDTH, HG_DK, nb), F32),
            jax.ShapeDtypeStruct((DEPTH, nb, kvw, WINDOW), F32),
            jax.ShapeDtypeStruct((DEPTH, nb, kvw, WINDOW), F32),
        ],
        input_output_aliases={N_SAMPLE_MIXER_INPUTS + i: 1 + i for i in range(len(stacks))},
        scratch_shapes=[
            pltpu.VMEM((steps * nb, IN_WIDTH), F32),
            pltpu.VMEM((steps * nb, D_MODEL), F32),
            pltpu.VMEM((nb, rows_th, kvw), F32),
            pltpu.VMEM((gb, rows_th, kvw), F32),
            pltpu.VMEM((nb, 8, kvw), F32),
            pltpu.VMEM((nb, 8, kvw), F32),
            pltpu.VMEM((steps, HG_WIDTH, nb), F32),
            pltpu.VMEM((steps, HG_WIDTH, nb), F32),
            pltpu.VMEM((steps, HG_WIDTH, nb), F32),
            pltpu.VMEM((HG_WIDTH, nb), F32),
            pltpu.VMEM((steps, nb, HG_WIDTH), F32),
            pltpu.VMEM((steps, HG_WIDTH, nb), F32),
        ],
        compiler_params=pltpu.CompilerParams(dimension_semantics=("arbitrary",), vmem_limit_bytes=VMEM_LIMIT),
        name="sample_mixer",
    )(x, mod, norm.reshape(DEPTH, 1, D_MODEL), w_in, w_out, pool_bd, pool_scale.reshape(DEPTH, 1, POOL_WIDTH),
      lbs.reshape(DEPTH, 1, HG_WIDTH), hg_norm, bms, sink_col, ones_blk, pool_hist, hgrn_state, cache_kt, cache_vt,
      *stacks)


FFN_TILE = 512
MIXER_TILE = 256
SAMPLE_GROUP = 8
PAST_LEN = 8192


def kernel(x_prompt, x_sample, c_prompt, c_sample, state_pool, state_hgrn, cache_k_win, cache_v_win, norm_ffn1, norm_mix, norm_ffn2, w_mod, b_mod, ffn1_w_gate, ffn1_w_up, ffn1_w_down, w_in, w_out, pool_w, pool_scale, hgrn_lower, hgrn_norm, attn_sinks, rel_bias, ffn2_w_gate, ffn2_w_up, ffn2_w_down, norm_final):
    bsz, seq, _ = x_prompt.shape
    nb, steps, _ = x_sample.shape
    kvw = KV_HEADS * HEAD_DIM

    mod = _modulation(jnp.concatenate([c_sample, c_prompt], axis=0), w_mod, b_mod)
    prompt_mod_block = nb // 8
    lbs = _lower_bounds(hgrn_lower)
    maps_p = np.stack([_bucket_map(WINDOW, 2 * WINDOW, WINDOW, 0, 2 * WINDOW),
                       _bucket_map(WINDOW, 2 * WINDOW, WINDOW, WINDOW, 2 * WINDOW)])
    bm_p = _bias_tables(jnp.asarray(maps_p), rel_bias)
    maps_s = _bucket_map(8, 2 * WINDOW, steps, 0, WINDOW + steps)[None]
    bm_s = _bias_tables(jnp.asarray(maps_s), rel_bias)[0]
    bm_s = jnp.swapaxes(bm_s, 0, 1).reshape(8 * ATT_HEADS, 2 * WINDOW)[:steps * ATT_HEADS]
    sink_col = jnp.tile(attn_sinks, (1, steps)).reshape(DEPTH, steps * ATT_HEADS, 1)
    consts = _mixer_consts(MIXER_TILE)
    hgn = jnp.tile(hgrn_norm, (1, HG_HEADS)).reshape(DEPTH, 1, HG_WIDTH)
    pool_bd = _pool_block_diag(pool_w)
    w_in_b, w_out_b = w_in.astype(BF16), w_out.astype(BF16)
    f1 = (ffn1_w_gate, ffn1_w_up, ffn1_w_down)
    f2 = (ffn2_w_gate, ffn2_w_up, ffn2_w_down)
    w_ffn = tuple(w[0].astype(BF16) for w in f1)
    s_pool = jnp.transpose(state_pool, (0, 2, 1, 3))
    s_hgrn = jnp.transpose(state_hgrn, (0, 2, 3, 4, 1)).reshape(DEPTH, HG_WIDTH, HG_DK, nb)
    c_k = jnp.transpose(cache_k_win, (0, 1, 3, 4, 2)).reshape(DEPTH, nb, kvw, WINDOW)
    c_v = jnp.transpose(cache_v_win, (0, 1, 3, 4, 2)).reshape(DEPTH, nb, kvw, WINDOW)

    xp = x_prompt.reshape(bsz * seq, D_MODEL)
    xs = jnp.swapaxes(x_sample, 0, 1).reshape(steps * nb, D_MODEL)
    ffn = functools.partial(_ffn, tm=FFN_TILE, seqs_p=bsz, seqs_s=nb)
    st_p, stacks_s = [], None
    for l in range(DEPTH):
        xp, xs, *w_ffn = ffn(xp, xs, mod, norm_ffn1, *w_ffn, f2 + (l,), norm_final, l, 0)
        xp, *sp = _prompt_mixer(xp.reshape(bsz, seq, D_MODEL), mod, norm_mix, w_in_b, w_out_b, pool_bd, pool_scale,
                                lbs, hgn, bm_p, attn_sinks, consts, l, tt=MIXER_TILE, mod_row_block=prompt_mod_block)
        xs, *stacks_s = _sample_mixer(xs.reshape(steps, nb, D_MODEL), mod, norm_mix, w_in_b, w_out_b, pool_bd,
                                      pool_scale, lbs, hgn, bm_s, sink_col, consts[1], s_pool, s_hgrn, c_k, c_v,
                                      stacks_s, l, gb=SAMPLE_GROUP, pos0=PAST_LEN)
        st_p.append(sp)
        last = l == DEPTH - 1
        xp, xs, *w_ffn = ffn(xp.reshape(bsz * seq, D_MODEL), xs.reshape(steps * nb, D_MODEL), mod, norm_ffn2, *w_ffn,
                             None if last else f1 + (l + 1,), norm_final, l, 6, final=last)

    y_prompt = xp.reshape(bsz, seq, D_MODEL)
    y_sample = jnp.swapaxes(xs.reshape(steps, nb, D_MODEL), 0, 1)

    def stacked(states, i, shape):
        return jnp.stack([s[i] for s in states]).reshape((DEPTH,) + shape)

    outs = [stacked(st_p, 0, (bsz, POOL_BUF, POOL_WIDTH)),
            stacked(st_p, 1, (bsz, HG_HEADS, HG_DK, HG_DK)),
            stacked(st_p, 2, (bsz, WINDOW, KV_HEADS, HEAD_DIM)),
            stacked(st_p, 3, (bsz, WINDOW, KV_HEADS, HEAD_DIM)),
            jnp.transpose(stacks_s[0], (0, 2, 1, 3)),
            jnp.transpose(stacks_s[1].reshape(DEPTH, HG_HEADS, HG_DK, HG_DK, nb), (0, 4, 1, 2, 3)),
            jnp.transpose(stacks_s[2].reshape(DEPTH, nb, KV_HEADS, HEAD_DIM, WINDOW), (0, 1, 4, 2, 3)),
            jnp.transpose(stacks_s[3].reshape(DEPTH, nb, KV_HEADS, HEAD_DIM, WINDOW), (0, 1, 4, 2, 3))]
    return (y_prompt, y_sample, *outs)
```

```python
import functools
import math

import numpy as np
import jax
import jax.numpy as jnp
from jax import lax
from jax.experimental import pallas as pl
from jax.experimental.pallas import tpu as pltpu

F32 = jnp.float32
BF16 = jnp.bfloat16

D_MODEL = 1024
DEPTH = 4
N_MOD = 9
D_FF = 2816
EPS = 1e-6
POOL_WIDTH = 256
POOL_WINDOWS = (2, 4, 8, 16)
POOL_GW = 64
POOL_BUF = 15
HG_WIDTH = 256
HG_HEADS = 4
HG_DK = 64
ATT_WIDTH = 512
ATT_HEADS = 8
KV_HEADS = 2
GQA_GROUP = 4
HEAD_DIM = 64
WINDOW = 128
REL_BUCKETS = 32
REL_MAX_DIST = 128
OFF_POOL, OFF_HQ, OFF_HF, OFF_HI, OFF_HG, OFF_AQ, OFF_AK, OFF_AV = 0, 256, 512, 768, 1024, 1280, 1792, 1920
IN_WIDTH = 2048
NEG = -1e30

VMEM_LIMIT = 56 * 1024 * 1024


def _dot(a, b):
    return jnp.dot(a, b, preferred_element_type=F32)


def _dot_nt(a, b):
    return lax.dot_general(a, b, (((1,), (1,)), ((), ())), preferred_element_type=F32)


def _dot_tn(a, b):
    return lax.dot_general(a, b, (((0,), (0,)), ((), ())), preferred_element_type=F32)


def _sigmoid(x):
    return 1.0 / (1.0 + jnp.exp(-x))


def _silu(x):
    return x * _sigmoid(x)


def _norm_mod(x, g, shift, scale):
    r = lax.rsqrt(jnp.mean(x * x, axis=-1, keepdims=True) + EPS)
    return (x * r) * g * (1.0 + scale) + shift


def _split3(x):
    hi = x.astype(BF16)
    r1 = x - hi.astype(F32)
    mid = r1.astype(BF16)
    lo = (r1 - mid.astype(F32)).astype(BF16)
    return hi, mid, lo


MOD_GROUP = 3


def _mod_kernel(c_ref, w_ref, b_ref, o_ref):
    sc = _silu(c_ref[...]).astype(BF16)
    for j in range(MOD_GROUP):
        cols = slice(j * D_MODEL, (j + 1) * D_MODEL)
        o_ref[j] = _dot(sc, w_ref[:, cols].astype(BF16)) + b_ref[j]


def _modulation(c_all, w_mod, b_mod):
    rows = c_all.shape[0]
    return pl.pallas_call(
        _mod_kernel,
        grid=(DEPTH, N_MOD // MOD_GROUP),
        in_specs=[
            pl.BlockSpec((rows, D_MODEL), lambda l, j: (0, 0)),
            pl.BlockSpec((None, D_MODEL, MOD_GROUP * D_MODEL), lambda l, j: (l, 0, j)),
            pl.BlockSpec((None, MOD_GROUP, 1, D_MODEL), lambda l, j: (l, j, 0, 0)),
        ],
        out_specs=pl.BlockSpec((None, MOD_GROUP, rows, D_MODEL), lambda l, j: (l, j, 0, 0)),
        out_shape=jax.ShapeDtypeStruct((DEPTH, N_MOD, rows, D_MODEL), F32),
        compiler_params=pltpu.CompilerParams(dimension_semantics=("arbitrary", "arbitrary"),
                                             vmem_limit_bytes=VMEM_LIMIT),
        name="adaln_mod",
    )(c_all, w_mod, b_mod.reshape(DEPTH, N_MOD, 1, D_MODEL))


def _swiglu_tile(x, g_norm, shift, scale, gate, wg_ref, wu_ref, wd_ref, nf_ref, final):
    h = _norm_mod(x, g_norm, shift, scale).astype(BF16)
    a = (_silu(_dot(h, wg_ref[...])) * _dot(h, wu_ref[...])).astype(BF16)
    y = x + (0.5 * gate) * _dot(a, wd_ref[...])
    if final:
        y = (y * lax.rsqrt(jnp.mean(y * y, axis=-1, keepdims=True) + EPS)) * nf_ref[...]
    return y


DOWN_CAST_ROWS = 128


def _ffn_kernel(xp_ref, xs_ref, modp_ref, mods_ref, n_ref, wg_ref, wu_ref, wd_ref, nf_ref, *rest,
                tiles_per_seq, prompt_tiles, sample_reps, final, cast_layer):
    i = pl.program_id(0)
    if cast_layer is None:
        op_ref, os_ref = rest
    else:
        (ng_hbm, nu_hbm, nd_hbm, op_ref, os_ref, og_hbm, ou_hbm, od_hbm,
         sg_ref, su_ref, sd_ref, tg_ref, tu_ref, td_ref, sem) = rest
        up_rows = D_MODEL // prompt_tiles
        down_steps = D_FF // DOWN_CAST_ROWS

        def up_copies(step, into_vmem):
            rows = pl.ds(pl.multiple_of(step * up_rows, up_rows), up_rows)
            if into_vmem:
                return (pltpu.make_async_copy(ng_hbm.at[cast_layer, rows, :], sg_ref, sem.at[0]),
                        pltpu.make_async_copy(nu_hbm.at[cast_layer, rows, :], su_ref, sem.at[1]))
            return (pltpu.make_async_copy(tg_ref, og_hbm.at[rows, :], sem.at[2]),
                    pltpu.make_async_copy(tu_ref, ou_hbm.at[rows, :], sem.at[3]))

        def down_copy(step, into_vmem):
            rows = pl.ds(pl.multiple_of(step * DOWN_CAST_ROWS, DOWN_CAST_ROWS), DOWN_CAST_ROWS)
            if into_vmem:
                return pltpu.make_async_copy(nd_hbm.at[cast_layer, rows, :], sd_ref, sem.at[4])
            return pltpu.make_async_copy(td_ref, od_hbm.at[rows, :], sem.at[5])

        @pl.when(i < prompt_tiles)
        def _():
            for cp in up_copies(i, True):
                cp.start()

        @pl.when(i < down_steps)
        def _():
            down_copy(i, True).start()

    @pl.when(i < prompt_tiles)
    def _():
        row = i // tiles_per_seq
        shift, scale, gate = (modp_ref[j, pl.ds(row, 1), :] for j in range(3))
        op_ref[...] = _swiglu_tile(xp_ref[...], n_ref[...], shift, scale, gate, wg_ref, wu_ref, wd_ref, nf_ref, final)

    @pl.when(i == prompt_tiles)
    def _():
        shift, scale, gate = (jnp.concatenate([mods_ref[j]] * sample_reps, axis=0) for j in range(3))
        os_ref[...] = _swiglu_tile(xs_ref[...], n_ref[...], shift, scale, gate, wg_ref, wu_ref, wd_ref, nf_ref, final)

    if cast_layer is not None:
        @pl.when((i >= 1) & (i <= prompt_tiles))
        def _():
            for cp in up_copies(i - 1, False):
                cp.wait()

        @pl.when((i >= 1) & (i <= down_steps))
        def _():
            down_copy(i - 1, False).wait()

        @pl.when(i < prompt_tiles)
        def _():
            for cp in up_copies(i, True):
                cp.wait()
            tg_ref[...] = sg_ref[...].astype(BF16)
            tu_ref[...] = su_ref[...].astype(BF16)
            for cp in up_copies(i, False):
                cp.start()

        @pl.when(i < down_steps)
        def _():
            down_copy(i, True).wait()
            td_ref[...] = sd_ref[...].astype(BF16)
            down_copy(i, False).start()


def _ffn(xp, xs, mod, norm, wg, wu, wd, next_weights, norm_final, layer, j0, *, tm, seqs_p, seqs_s, final=False):
    rows_p, rows_s = xp.shape[0], xs.shape[0]
    prompt_tiles = rows_p // tm
    cast_layer = None if next_weights is None else next_weights[3]
    kern = functools.partial(_ffn_kernel, tiles_per_seq=(rows_p // seqs_p) // tm, prompt_tiles=prompt_tiles,
                             sample_reps=rows_s // seqs_s, final=final, cast_layer=cast_layer)
    ptile = lambda i: (jnp.minimum(i, prompt_tiles - 1), 0)
    weight = lambda *shape: pl.BlockSpec(shape, lambda i: (0, 0), pipeline_mode=pl.Buffered(1))
    hbm = pl.BlockSpec(memory_space=pl.ANY)
    in_specs = [
        pl.BlockSpec((tm, D_MODEL), ptile),
        pl.BlockSpec((rows_s, D_MODEL), lambda i: (0, 0)),
        pl.BlockSpec((None, 3, seqs_p, D_MODEL), lambda i: (layer, j0 // 3, seqs_s // seqs_p, 0)),
        pl.BlockSpec((None, 3, seqs_s, D_MODEL), lambda i: (layer, j0 // 3, 0, 0)),
        pl.BlockSpec((None, 1, D_MODEL), lambda i: (layer, 0, 0)),
        weight(D_MODEL, D_FF),
        weight(D_MODEL, D_FF),
        weight(D_FF, D_MODEL),
        pl.BlockSpec((1, D_MODEL), lambda i: (0, 0)),
    ]
    out_specs = [pl.BlockSpec((tm, D_MODEL), ptile), pl.BlockSpec((rows_s, D_MODEL), lambda i: (0, 0))]
    out_shape = [jax.ShapeDtypeStruct((rows_p, D_MODEL), F32), jax.ShapeDtypeStruct((rows_s, D_MODEL), F32)]
    operands = [xp, xs, mod, mod, norm.reshape(DEPTH, 1, D_MODEL), wg, wu, wd, norm_final.reshape(1, D_MODEL)]
    scratch = []
    if next_weights is not None:
        assert D_MODEL % prompt_tiles == 0 and (D_MODEL // prompt_tiles) % 16 == 0
        assert D_FF % DOWN_CAST_ROWS == 0 and D_FF // DOWN_CAST_ROWS <= prompt_tiles
        up_rows = D_MODEL // prompt_tiles
        in_specs += [hbm, hbm, hbm]
        operands += list(next_weights[:3])
        out_specs += [hbm, hbm, hbm]
        out_shape += [jax.ShapeDtypeStruct((D_MODEL, D_FF), BF16), jax.ShapeDtypeStruct((D_MODEL, D_FF), BF16),
                      jax.ShapeDtypeStruct((D_FF, D_MODEL), BF16)]
        scratch = [pltpu.VMEM((up_rows, D_FF), F32), pltpu.VMEM((up_rows, D_FF), F32),
                   pltpu.VMEM((DOWN_CAST_ROWS, D_MODEL), F32),
                   pltpu.VMEM((up_rows, D_FF), BF16), pltpu.VMEM((up_rows, D_FF), BF16),
                   pltpu.VMEM((DOWN_CAST_ROWS, D_MODEL), BF16),
                   pltpu.SemaphoreType.DMA((6,))]
    return pl.pallas_call(
        kern,
        grid=(prompt_tiles + 1,),
        in_specs=in_specs,
        out_specs=out_specs,
        out_shape=out_shape,
        scratch_shapes=scratch,
        compiler_params=pltpu.CompilerParams(dimension_semantics=("arbitrary",), vmem_limit_bytes=VMEM_LIMIT),
        name="swiglu_half_step",
    )(*operands)


def _lower_bound_kernel(x_ref, o_ref):
    rows = [x_ref[l:l + 1, :] for l in range(DEPTH)]
    m = functools.reduce(jnp.maximum, rows)
    e = [jnp.exp(r - m) for r in rows]
    s = functools.reduce(lambda a, b: a + b, e)
    sm = [ei / s for ei in e]
    acc = sm[0]
    for l in range(DEPTH):
        if l > 0:
            acc = acc + sm[l]
        o_ref[l:l + 1, :] = acc - sm[0]


def _lower_bounds(hgrn_lower):
    return pl.pallas_call(
        _lower_bound_kernel,
        out_shape=jax.ShapeDtypeStruct((DEPTH, HG_WIDTH), F32),
        name="hgrn_lower_bounds",
    )(hgrn_lower)


def _bucket_map(rows, cols, valid_rows, col_lo, col_hi):
    r = np.arange(rows)[:, None]
    c = np.arange(cols)[None, :]
    rel = c - WINDOW - r
    n = np.maximum(-rel, 0)
    exact = REL_BUCKETS // 2
    nf = np.maximum(n, 1).astype(np.float32)
    large = exact + (np.log(nf / np.float32(exact)) / np.float32(math.log(REL_MAX_DIST / exact))
                     * np.float32(REL_BUCKETS - exact)).astype(np.int32)
    large = np.minimum(large, REL_BUCKETS - 1)
    bucket = np.where(n < exact, n, large)
    valid = (rel <= 0) & (rel > -WINDOW) & (r < valid_rows) & (c >= col_lo) & (c < col_hi)
    return np.where(valid, bucket, -1).astype(np.int32)


def _bias_table_kernel(bkt_ref, rb_ref, o_ref):
    h = pl.program_id(1)
    bkt = bkt_ref[...]
    acc = jnp.full(bkt.shape, NEG, F32)
    for b in range(REL_BUCKETS):
        acc = jnp.where(bkt == b, rb_ref[b, h], acc)
    o_ref[...] = acc


def _bias_tables(bucket_maps, rel_bias):
    m, r, c = bucket_maps.shape
    return pl.pallas_call(
        _bias_table_kernel,
        grid=(m, ATT_HEADS),
        in_specs=[pl.BlockSpec((None, r, c), lambda i, h: (i, 0, 0)),
                  pl.BlockSpec(memory_space=pltpu.SMEM)],
        out_specs=pl.BlockSpec((None, None, r, c), lambda i, h: (i, h, 0, 0)),
        out_shape=jax.ShapeDtypeStruct((m, ATT_HEADS, r, c), F32),
        compiler_params=pltpu.CompilerParams(dimension_semantics=("arbitrary", "arbitrary")),
        name="rel_bias_tables",
    )(bucket_maps, rel_bias)


def _block_ones(n, blk):
    i = np.arange(n)
    return (i[:, None] // blk == i[None, :] // blk)


def _mixer_consts(tt):
    tri = jnp.asarray(np.tril(np.ones((tt, tt), np.float32)), BF16)
    blk = _block_ones(HG_WIDTH, HG_DK)
    return tri, jnp.asarray(blk, BF16), jnp.asarray(blk, F32)


def _pool_block_diag(pool_w):
    out = jnp.zeros((DEPTH, POOL_WIDTH, POOL_WIDTH), BF16)
    for g in range(len(POOL_WINDOWS)):
        sl = slice(g * POOL_GW, (g + 1) * POOL_GW)
        out = out.at[:, sl, sl].set(pool_w[:, g].astype(BF16))
    return out


def _head_masks():
    lane = lax.broadcasted_iota(jnp.int32, (1, HG_WIDTH), 1)
    return [jnp.where((lane >= HG_DK * h) & (lane < HG_DK * (h + 1)), 1.0, 0.0).astype(F32) for h in range(HG_HEADS)]


def _prompt_mixer_kernel(x_ref, xn_ref, mod_ref, n_ref, win_ref, wout_ref, pbd_ref, pscale_ref, lb_ref, hgn_ref,
                         bm_ref, sink_ref, tri_ref, ones_ref, bdm_ref,
                         xo_ref, pool_o, s_o, k_o, v_o,
                         za_ref, zb_ref, ubuf, kbuf, vbuf, st_ref, cat_ref, *, layer, tt, nt, ntiles):
    b = pl.program_id(0)
    p = pl.program_id(1)
    last_pair = pl.num_programs(1) - 1
    nxt = jnp.minimum(b * nt + 2 * p + 2, ntiles - 1)
    bn = nxt // nt
    n_chunks = 8
    cw = IN_WIDTH // n_chunks

    def projector(x_rows, bsel, dst_ref):
        h = _norm_mod(x_rows, n_ref[...], mod_ref[3, pl.ds(bsel, 1), :], mod_ref[4, pl.ds(bsel, 1), :]).astype(BF16)
        state = {"k": 0}

        def step():
            k = state["k"]
            if k < n_chunks:
                dst_ref[:, k * cw:(k + 1) * cw] = _dot(h, win_ref[:, k * cw:(k + 1) * cw])
                state["k"] = k + 1

        return step

    @pl.when((b == 0) & (p == 0))
    def _():
        first_proj = projector(x_ref[0:tt, :], b, za_ref)
        for _ in range(n_chunks):
            first_proj()

    @pl.when(p == 0)
    def _():
        ubuf[0:16, :] = jnp.zeros((16, POOL_WIDTH), F32)
        kbuf[:, 0:WINDOW, :] = jnp.zeros((KV_HEADS, WINDOW, HEAD_DIM), BF16)
        vbuf[:, 0:WINDOW, :] = jnp.zeros((KV_HEADS, WINDOW, HEAD_DIM), BF16)
        st_ref[...] = jnp.zeros((HG_WIDTH, HG_WIDTH), F32)

    gate = mod_ref[5, pl.ds(b, 1), :]

    def mix(z_ref, x, row0, t, side):
        lane = lax.broadcasted_iota(jnp.int32, (1, POOL_WIDTH), 1)

        u = z_ref[:, OFF_POOL:OFF_POOL + POOL_WIDTH]
        ubuf[16:16 + tt, :] = u
        e = ubuf[...]
        s2 = e + pltpu.roll(e, 1, 0)
        s4 = s2 + pltpu.roll(s2, 2, 0)
        s8 = s4 + pltpu.roll(s4, 4, 0)
        s16 = s8 + pltpu.roll(s8, 8, 0)
        sel = jnp.where(lane < 64, s2, jnp.where(lane < 128, s4, jnp.where(lane < 192, s8, s16)))[16:]
        wl = jnp.where(lane < 64, 2.0, jnp.where(lane < 128, 4.0, jnp.where(lane < 192, 8.0, 16.0))).astype(F32)
        pos1 = (t * tt + 1 + lax.broadcasted_iota(jnp.int32, (tt, 1), 0)).astype(F32)
        dpool = sel / jnp.minimum(pos1, wl) - u
        ypool = _dot(dpool.astype(BF16), pbd_ref[...]) * pscale_ref[...]
        cat_ref[:, 0:POOL_WIDTH] = ypool.astype(BF16)
        ubuf[0:16, :] = ubuf[tt:tt + 16, :]
        side()


        for g in range(KV_HEADS):
            kbuf[g, WINDOW:WINDOW + tt, :] = z_ref[:, OFF_AK + HEAD_DIM * g:OFF_AK + HEAD_DIM * (g + 1)].astype(BF16)
            vbuf[g, WINDOW:WINDOW + tt, :] = z_ref[:, OFF_AV + HEAD_DIM * g:OFF_AV + HEAD_DIM * (g + 1)].astype(BF16)
        first = jnp.where(t == 0, 1, 0)
        bands = [(j, g) for j in range(tt // WINDOW) for g in range(KV_HEADS)]
        logits = []
        for j, g in bands:
            r0 = j * WINDOW
            c0 = OFF_AQ + HEAD_DIM * g * GQA_GROUP
            qg = jnp.concatenate([z_ref[r0:r0 + WINDOW, c0 + HEAD_DIM * i:c0 + HEAD_DIM * (i + 1)]
                                  for i in range(GQA_GROUP)], axis=0)
            qg = (qg * (HEAD_DIM ** -0.5)).astype(BF16)
            logits.append(_dot_nt(qg, kbuf[g, r0:r0 + 2 * WINDOW, :]))
        side()

        lb = lb_ref[...]
        q = _silu(z_ref[:, OFF_HQ:OFF_HQ + HG_WIDTH])
        f = lb + (1.0 - lb) * _sigmoid(z_ref[:, OFF_HF:OFF_HF + HG_WIDTH])
        k = 1.0 - f
        v = z_ref[:, OFF_HI:OFF_HI + HG_WIDTH]
        vb = v.astype(BF16)
        tri = tri_ref[...]
        hi, mid, lo = _split3(jnp.log(f))
        bcum = _dot(tri, hi) + _dot(tri, mid) + _dot(tri, lo)
        side()
        ones_blk = ones_ref[...]
        hm = _head_masks()

        n16 = tt // 16
        half_shape = (n16, 2, 8, HG_WIDTH)
        q4 = q.reshape(half_shape)
        b4 = bcum.reshape(half_shape)
        c4 = (bcum - jnp.log(k)).reshape(half_shape)
        v4 = v.reshape(half_shape)
        rowi = lax.broadcasted_iota(jnp.int32, (1, 8, 1), 1)
        o_up = jnp.zeros((n16, 8, HG_WIDTH), F32)
        o_dn = jnp.zeros((n16, 8, HG_WIDTH), F32)
        nh = n16 * 8
        for s in range(8):
            cs = c4[:, 0, s:s + 1, :]
            vs = v4[:, 0, s:s + 1, :]
            x_up = jnp.where(rowi >= s, q4[:, 0] * jnp.exp(b4[:, 0] - cs), 0.0)
            x_dn = q4[:, 1] * jnp.exp(b4[:, 1] - cs)
            xs = jnp.concatenate([x_up.reshape(nh, HG_WIDTH), x_dn.reshape(nh, HG_WIDTH)], axis=0)
            r = _dot(xs.astype(BF16), ones_blk)
            o_up = o_up + r[0:nh].reshape(n16, 8, HG_WIDTH) * vs
            o_dn = o_dn + r[nh:2 * nh].reshape(n16, 8, HG_WIDTH) * vs
            if s % 4 == 3:
                side()
        for s in range(8):
            cs = c4[:, 1, s:s + 1, :]
            x_dn = jnp.where(rowi >= s, q4[:, 1] * jnp.exp(b4[:, 1] - cs), 0.0)
            r = _dot(x_dn.reshape(nh, HG_WIDTH).astype(BF16), ones_blk)
            o_dn = o_dn + r.reshape(n16, 8, HG_WIDTH) * v4[:, 1, s:s + 1, :]
            if s % 4 == 3:
                side()
        o_tot = jnp.concatenate([o_up[:, None], o_dn[:, None]], axis=1).reshape(tt, HG_WIDTH)

        blocks = []
        blk = 32
        while blk <= tt:
            blocks += [(i * blk, i * blk + blk // 2, (i + 1) * blk) for i in range(tt // blk)]
            blk *= 2
        scores = []
        for lo_s, mid_s, hi_s in blocks:
            ref = bcum[mid_s - 1:mid_s, :]
            qt = q[mid_s:hi_s] * jnp.exp(bcum[mid_s:hi_s] - ref)
            kt = (k[lo_s:mid_s] * jnp.exp(ref - bcum[lo_s:mid_s])).astype(BF16)
            qs = jnp.concatenate([qt * hm[hh] for hh in range(HG_HEADS)], axis=0).astype(BF16)
            scores.append(_dot_nt(qs, kt).astype(BF16))

        st = st_ref[...]
        qin = (q * jnp.exp(bcum)).astype(BF16)
        o_tot = o_tot + _dot_nt(qin, st.astype(BF16))
        blast = bcum[tt - 1:tt, :]
        kout = (k * jnp.exp(blast - bcum)).astype(BF16)
        st_ref[...] = st * jnp.exp(blast) + _dot_tn(vb, kout) * bdm_ref[...]
        side()

        probs, dens = [], []
        for (j, g), lg_all in zip(bands, logits):
            h0 = g * GQA_GROUP
            bm = bm_ref[first, h0:h0 + GQA_GROUP] if j == 0 else bm_ref[0, h0:h0 + GQA_GROUP]
            ps = []
            for i in range(GQA_GROUP):
                lg = lg_all[i * WINDOW:(i + 1) * WINDOW] + bm[i]
                sink = sink_ref[layer, h0 + i]
                m = jnp.maximum(jnp.max(lg, axis=-1, keepdims=True), sink)
                p_ = jnp.exp(lg - m)
                dens.append(jnp.sum(p_, axis=-1, keepdims=True) + jnp.exp(sink - m))
                ps.append(p_.astype(BF16))
            probs.append(jnp.concatenate(ps, axis=0))
        for n, (j, g) in enumerate(bands):
            r0 = j * WINDOW
            og = _dot(probs[n], vbuf[g, r0:r0 + 2 * WINDOW, :])
            for i in range(GQA_GROUP):
                cc = POOL_WIDTH + HG_WIDTH + HEAD_DIM * (g * GQA_GROUP + i)
                oh = og[i * WINDOW:(i + 1) * WINDOW] / dens[n * GQA_GROUP + i]
                cat_ref[r0:r0 + WINDOW, cc:cc + HEAD_DIM] = oh.astype(BF16)
        for g in range(KV_HEADS):
            kbuf[g, 0:WINDOW, :] = kbuf[g, tt:tt + WINDOW, :]
            vbuf[g, 0:WINDOW, :] = vbuf[g, tt:tt + WINDOW, :]
        side()

        adds = {}
        for (lo_s, mid_s, hi_s), a in zip(blocks, scores):
            half = mid_s - lo_s
            ov = _dot(a, vb[lo_s:mid_s])
            oi = ov[0:half] * hm[0]
            for hh in range(1, HG_HEADS):
                oi = oi + ov[hh * half:(hh + 1) * half] * hm[hh]
            adds.setdefault(2 * half, []).extend([jnp.zeros((half, HG_WIDTH), F32), oi])
        for parts in adds.values():
            o_tot = o_tot + jnp.concatenate(parts, axis=0)

        sq_hi, sq_mid, _ = _split3(o_tot * o_tot)
        ms = (_dot(sq_hi, ones_blk) + _dot(sq_mid, ones_blk)) * (1.0 / HG_DK)
        y_hg = o_tot * lax.rsqrt(ms + EPS) * hgn_ref[...] * _silu(z_ref[:, OFF_HG:OFF_HG + HG_WIDTH])
        cat_ref[:, POOL_WIDTH:POOL_WIDTH + HG_WIDTH] = y_hg.astype(BF16)

        xo_ref[row0:row0 + tt, :] = x + gate * _dot(cat_ref[...], wout_ref[...])

        for _ in range(n_chunks):
            side()

    mix(za_ref, x_ref[0:tt, :], 0, 2 * p, projector(x_ref[tt:2 * tt, :], b, zb_ref))
    mix(zb_ref, x_ref[tt:2 * tt, :], tt, 2 * p + 1, projector(xn_ref[...], bn, za_ref))

    @pl.when(p == last_pair)
    def _():
        pool_o[...] = ubuf[pl.ds(1, POOL_BUF), :]
        s_kv = st_ref[...].T
        for hh in range(HG_HEADS):
            s_o[hh] = s_kv[HG_DK * hh:HG_DK * (hh + 1), HG_DK * hh:HG_DK * (hh + 1)]
        k_o[...] = zb_ref[tt - WINDOW:tt, OFF_AK:OFF_AV]
        v_o[...] = zb_ref[tt - WINDOW:tt, OFF_AV:IN_WIDTH]


def _prompt_mixer(x, mod, norm, w_in, w_out, pool_bd, pool_scale, lbs, hg_norm, bm, sinks, consts, layer, *,
                  tt, mod_row_block):
    bsz, seq, _ = x.shape
    tri, ones_blk, bdm = consts
    nt = seq // tt
    ntiles = bsz * nt
    kern = functools.partial(_prompt_mixer_kernel, layer=layer, tt=tt, nt=nt, ntiles=ntiles)
    full = lambda *shape: pl.BlockSpec(shape, lambda b, p: (0,) * len(shape))
    lsel = lambda *shape: pl.BlockSpec((None,) + shape, lambda b, p: (layer,) + (0,) * len(shape))
    mix_pair = lambda b, p: (b, p, 0)
    mix_seq = lambda *z: (lambda b, p: (b,) + z)

    def next_tile(b, p):
        nxt = jnp.minimum(b * nt + 2 * p + 2, ntiles - 1)
        return (nxt // nt, nxt % nt, 0)

    return pl.pallas_call(
        kern,
        grid=(bsz, nt // 2),
        in_specs=[
            pl.BlockSpec((None, 2 * tt, D_MODEL), mix_pair),
            pl.BlockSpec((None, tt, D_MODEL), next_tile),
            pl.BlockSpec((None, N_MOD, 8, D_MODEL), lambda b, p: (layer, 0, mod_row_block, 0)),
            lsel(1, D_MODEL),
            lsel(D_MODEL, IN_WIDTH),
            lsel(D_MODEL, D_MODEL),
            lsel(POOL_WIDTH, POOL_WIDTH),
            lsel(1, POOL_WIDTH),
            lsel(1, HG_WIDTH),
            lsel(1, HG_WIDTH),
            full(2, ATT_HEADS, WINDOW, 2 * WINDOW),
            pl.BlockSpec(memory_space=pltpu.SMEM),
            full(tt, tt),
            full(HG_WIDTH, HG_WIDTH),
            full(HG_WIDTH, HG_WIDTH),
        ],
        out_specs=[
            pl.BlockSpec((None, 2 * tt, D_MODEL), mix_pair),
            pl.BlockSpec((None, POOL_BUF, POOL_WIDTH), mix_seq(0, 0)),
            pl.BlockSpec((None, HG_HEADS, HG_DK, HG_DK), mix_seq(0, 0, 0)),
            pl.BlockSpec((None, WINDOW, KV_HEADS * HEAD_DIM), mix_seq(0, 0)),
            pl.BlockSpec((None, WINDOW, KV_HEADS * HEAD_DIM), mix_seq(0, 0)),
        ],
        out_shape=[
            jax.ShapeDtypeStruct((bsz, seq, D_MODEL), F32),
            jax.ShapeDtypeStruct((bsz, POOL_BUF, POOL_WIDTH), F32),
            jax.ShapeDtypeStruct((bsz, HG_HEADS, HG_DK, HG_DK), F32),
            jax.ShapeDtypeStruct((bsz, WINDOW, KV_HEADS * HEAD_DIM), F32),
            jax.ShapeDtypeStruct((bsz, WINDOW, KV_HEADS * HEAD_DIM), F32),
        ],
        scratch_shapes=[
            pltpu.VMEM((tt, IN_WIDTH), F32),
            pltpu.VMEM((tt, IN_WIDTH), F32),
            pltpu.VMEM((16 + tt, POOL_WIDTH), F32),
            pltpu.VMEM((KV_HEADS, WINDOW + tt, HEAD_DIM), BF16),
            pltpu.VMEM((KV_HEADS, WINDOW + tt, HEAD_DIM), BF16),
            pltpu.VMEM((HG_WIDTH, HG_WIDTH), F32),
            pltpu.VMEM((tt, D_MODEL), BF16),
        ],
        compiler_params=pltpu.CompilerParams(dimension_semantics=("arbitrary", "arbitrary"),
                                             vmem_limit_bytes=VMEM_LIMIT),
        name="prompt_mixer",
    )(x, x, mod, norm.reshape(DEPTH, 1, D_MODEL), w_in, w_out, pool_bd, pool_scale.reshape(DEPTH, 1, POOL_WIDTH),
      lbs.reshape(DEPTH, 1, HG_WIDTH), hg_norm, bm, sinks, tri, ones_blk, bdm)


def _sample_mixer_kernel(x_ref, mod_ref, n_ref, win_ref, wout_ref, pbd_ref, pscale_ref, lb_ref, hgn_ref,
                         bms_ref, sinkc_ref, ones_ref, sp_ref, s0_ref, kt_ref, vt_ref,
                         xo_ref, pool_o, s_o, kt_o, vt_o,
                         z_ref, cat_ref, qs_ref, os_ref, kn_ref, vn_ref, qint_ref, koutt_ref, vtt_ref, gt_ref,
                         oin_ref, oacc_ref, *, steps, nb, gb, rb, pos0):
    g = pl.program_id(0)
    last = pl.num_programs(0) - 1
    base = pl.multiple_of(g * gb, gb)
    kvw = KV_HEADS * HEAD_DIM

    def rows(t):
        return slice(t * nb, (t + 1) * nb)

    @pl.when(g == 0)
    def _():
        for t in range(steps):
            h = _norm_mod(x_ref[t], n_ref[...], mod_ref[0], mod_ref[1]).astype(BF16)
            z_ref[rows(t), :] = _dot(h, win_ref[...])
        lane = lax.broadcasted_iota(jnp.int32, (1, POOL_WIDTH), 1)

        ext = [sp_ref[r] for r in range(POOL_BUF)] + [z_ref[rows(t), OFF_POOL:OFF_POOL + POOL_WIDTH] for t in range(steps)]
        for r in range(POOL_BUF):
            pool_o[r] = ext[r + steps]
        sums = {1: ext}
        w = 1
        while w < 16:
            prev = sums[w]
            sums[2 * w] = [None if (r < 2 * w - 1) else prev[r] + prev[r - w] for r in range(len(ext))]
            w *= 2
        dl = []
        for t in range(steps):
            r = POOL_BUF + t
            sel = jnp.where(lane < 64, sums[2][r], jnp.where(lane < 128, sums[4][r], jnp.where(lane < 192, sums[8][r], sums[16][r])))
            cnt = [float(min(pos0 + t + 1, wd)) for wd in POOL_WINDOWS]
            cl = jnp.where(lane < 64, cnt[0], jnp.where(lane < 128, cnt[1], jnp.where(lane < 192, cnt[2], cnt[3]))).astype(F32)
            dl.append(sel / cl - ext[r])
        ypool = _dot(jnp.concatenate(dl, axis=0).astype(BF16), pbd_ref[...]) * pscale_ref[...]
        cat_ref[:, 0:POOL_WIDTH] = ypool

        lb = lb_ref[...]
        qv, kv_, vv, bv = [], [], [], []
        bacc = None
        for t in range(steps):
            qv.append(_silu(z_ref[rows(t), OFF_HQ:OFF_HQ + HG_WIDTH]))
            f = lb + (1.0 - lb) * _sigmoid(z_ref[rows(t), OFF_HF:OFF_HF + HG_WIDTH])
            kv_.append(1.0 - f)
            vv.append(z_ref[rows(t), OFF_HI:OFF_HI + HG_WIDTH])
            bacc = jnp.log(f) if bacc is None else bacc + jnp.log(f)
            bv.append(bacc)
        pairs = [(t, s) for t in range(steps) for s in range(t + 1)]
        xs = jnp.concatenate([qv[t] * kv_[s] * jnp.exp(bv[t] - bv[s]) for t, s in pairs], axis=0).astype(BF16)
        rr = _dot(xs, ones_ref[...])
        o_intra = [None] * steps
        for i, (t, s) in enumerate(pairs):
            term = rr[i * nb:(i + 1) * nb] * vv[s]
            o_intra[t] = term if o_intra[t] is None else o_intra[t] + term
        blast = bv[steps - 1]
        gt_ref[...] = jnp.exp(blast).T
        for t in range(steps):
            oin_ref[t] = o_intra[t]
            qint_ref[t] = (qv[t] * jnp.exp(bv[t])).T
            koutt_ref[t] = (kv_[t] * jnp.exp(blast - bv[t])).T
            vtt_ref[t] = vv[t].T
        oacc_ref[...] = jnp.zeros(oacc_ref.shape, F32)

        zeros64 = jnp.zeros((nb, HEAD_DIM), F32)
        kn_ref[...] = jnp.zeros(kn_ref.shape, F32)
        vn_ref[...] = jnp.zeros(vn_ref.shape, F32)
        for t in range(steps):
            kn_ref[:, t, :] = z_ref[rows(t), OFF_AK:OFF_AK + kvw]
            vn_ref[:, t, :] = z_ref[rows(t), OFF_AV:OFF_AV + kvw]
            for hd in range(ATT_HEADS):
                c0 = OFF_AQ + HEAD_DIM * hd
                qsl = z_ref[rows(t), c0:c0 + HEAD_DIM] * (HEAD_DIM ** -0.5)
                parts = [qsl, zeros64] if hd < GQA_GROUP else [zeros64, qsl]
                qs_ref[:, ATT_HEADS * t + hd, :] = jnp.concatenate(parts, axis=1)

    r0 = pl.multiple_of(g * rb, rb)
    hrow = pl.multiple_of((g * rb // HG_DK) * HG_DK, HG_DK)
    vts = [vtt_ref[t, pl.ds(hrow, HG_DK), :] for t in range(steps)]
    accs = [oacc_ref[t, pl.ds(hrow, HG_DK), :] for t in range(steps)]
    for i in range(rb):
        s0 = s0_ref[i]
        snew = gt_ref[pl.ds(r0 + i, 1), :] * s0
        for t in range(steps):
            accs[t] = accs[t] + qint_ref[t, pl.ds(r0 + i, 1), :] * s0
            snew = snew + koutt_ref[t, pl.ds(r0 + i, 1), :] * vts[t]
        s_o[i] = snew
    for t in range(steps):
        oacc_ref[t, pl.ds(hrow, HG_DK), :] = accs[t]

    bm_old = bms_ref[:, 0:WINDOW]
    bm_new = bms_ref[:, WINDOW:WINDOW + 8]
    sinkc = sinkc_ref[...]
    seqs = range(gb)
    qn = [qs_ref[base + n].astype(BF16) for n in seqs]
    l_old = [_dot(qn[n], kt_ref[n].astype(BF16)) for n in seqs]
    l_new = [_dot_nt(qn[n], kn_ref[base + n].astype(BF16)) for n in seqs]
    def new_columns(ref):
        new = ref[pl.ds(base, gb)].reshape(gb * 8, kvw)
        if gb * 8 < WINDOW:
            new = jnp.concatenate([new, jnp.zeros((WINDOW - gb * 8, kvw), F32)], axis=0)
        return new.T

    knt = new_columns(kn_ref)
    vnt = new_columns(vn_ref)
    for n in seqs:
        kt_o[n] = jnp.concatenate([kt_ref[n][:, steps:], knt[:, 8 * n:8 * n + steps]], axis=1)
        vt_o[n] = jnp.concatenate([vt_ref[n][:, steps:], vnt[:, 8 * n:8 * n + steps]], axis=1)
    p_old, p_new, den = [], [], []
    for n in seqs:
        lo_, ln_ = l_old[n] + bm_old, l_new[n] + bm_new
        m = jnp.maximum(jnp.maximum(jnp.max(lo_, axis=-1, keepdims=True), jnp.max(ln_, axis=-1, keepdims=True)), sinkc)
        po, pn = jnp.exp(lo_ - m), jnp.exp(ln_ - m)
        den.append(jnp.sum(po, axis=-1, keepdims=True) + jnp.sum(pn, axis=-1, keepdims=True) + jnp.exp(sinkc - m))
        p_old.append(po.astype(BF16))
        p_new.append(pn.astype(BF16))
    for n in seqs:
        o = _dot_nt(p_old[n], vt_ref[n].astype(BF16)) + _dot(p_new[n], vn_ref[base + n].astype(BF16))
        os_ref[n] = o / den[n]
    for t in range(steps):
        for hd in range(ATT_HEADS):
            c0 = HEAD_DIM * (hd // GQA_GROUP)
            cc = POOL_WIDTH + HG_WIDTH + HEAD_DIM * hd
            cat_ref[pl.ds(t * nb + base, gb), cc:cc + HEAD_DIM] = os_ref[:, ATT_HEADS * t + hd, c0:c0 + HEAD_DIM]

    @pl.when(g == last)
    def _():
        ones_blk = ones_ref[...]
        for t in range(steps):
            o = oin_ref[t] + oacc_ref[t].T
            sq_hi, sq_mid, _ = _split3(o * o)
            ms = (_dot(sq_hi, ones_blk) + _dot(sq_mid, ones_blk)) * (1.0 / HG_DK)
            gate_t = _silu(z_ref[rows(t), OFF_HG:OFF_HG + HG_WIDTH])
            cat_ref[rows(t), POOL_WIDTH:POOL_WIDTH + HG_WIDTH] = o * lax.rsqrt(ms + EPS) * hgn_ref[...] * gate_t
        for t in range(steps):
            y = _dot(cat_ref[rows(t), :].astype(BF16), wout_ref[...])
            xo_ref[t] = x_ref[t] + mod_ref[2] * y


N_SAMPLE_MIXER_INPUTS = 16


def _sample_mixer_kernel_into(*refs, n_stacks, **kw):
    _sample_mixer_kernel(*refs[:N_SAMPLE_MIXER_INPUTS], *refs[N_SAMPLE_MIXER_INPUTS + n_stacks:], **kw)


def _sample_mixer(x, mod, norm, w_in, w_out, pool_bd, pool_scale, lbs, hg_norm, bms, sink_col, ones_blk,
                  pool_hist, hgrn_state, cache_kt, cache_vt, stacks, layer, *, gb, pos0):
    steps, nb, _ = x.shape
    rows_th = steps * ATT_HEADS
    ngrp = nb // gb
    rb = HG_WIDTH // ngrp
    kvw = KV_HEADS * HEAD_DIM
    stacks = () if stacks is None else tuple(stacks)
    kern = functools.partial(_sample_mixer_kernel_into, n_stacks=len(stacks), steps=steps, nb=nb, gb=gb, rb=rb,
                             pos0=pos0)
    full = lambda *shape: pl.BlockSpec(shape, lambda g: (0,) * len(shape))
    lsel = lambda *shape: pl.BlockSpec((None,) + shape, lambda g: (layer,) + (0,) * len(shape))
    grp = lambda *shape: pl.BlockSpec((None,) + shape, lambda g: (layer, g) + (0,) * (len(shape) - 1))
    return pl.pallas_call(
        kern,
        grid=(ngrp,),
        in_specs=[
            full(steps, nb, D_MODEL),
            pl.BlockSpec((None, 3, nb, D_MODEL), lambda g: (layer, 1, 0, 0)),
            lsel(1, D_MODEL),
            lsel(D_MODEL, IN_WIDTH),
            lsel(D_MODEL, D_MODEL),
            lsel(POOL_WIDTH, POOL_WIDTH),
            lsel(1, POOL_WIDTH),
            lsel(1, HG_WIDTH),
            lsel(1, HG_WIDTH),
            full(rows_th, 2 * WINDOW),
            lsel(rows_th, 1),
            full(HG_WIDTH, HG_WIDTH),
            lsel(POOL_BUF, nb, POOL_WIDTH),
            grp(rb, HG_DK, nb),
            grp(gb, kvw, WINDOW),
            grp(gb, kvw, WINDOW),
        ] + [pl.BlockSpec(memory_space=pl.ANY)] * len(stacks),
        out_specs=[
            full(steps, nb, D_MODEL),
            lsel(POOL_BUF, nb, POOL_WIDTH),
            grp(rb, HG_DK, nb),
            grp(gb, kvw, WINDOW),
            grp(gb, kvw, WINDOW),
        ],
        out_shape=[
            jax.ShapeDtypeStruct((steps, nb, D_MODEL), F32),
            jax.ShapeDtypeStruct((DEPTH, POOL_BUF, nb, POOL_WIDTH), F32),
            jax.ShapeDtypeStruct((DEPTH, HG_WIDTH, HG_DK, nb), F32),
            jax.ShapeDtypeStruct((DEPTH, nb, kvw, WINDOW), F32),
            jax.ShapeDtypeStruct((DEPTH, nb, kvw, WINDOW), F32),
        ],
        input_output_aliases={N_SAMPLE_MIXER_INPUTS + i: 1 + i for i in range(len(stacks))},
        scratch_shapes=[
            pltpu.VMEM((steps * nb, IN_WIDTH), F32),
            pltpu.VMEM((steps * nb, D_MODEL), F32),
            pltpu.VMEM((nb, rows_th, kvw), F32),
            pltpu.VMEM((gb, rows_th, kvw), F32),
            pltpu.VMEM((nb, 8, kvw), F32),
            pltpu.VMEM((nb, 8, kvw), F32),
            pltpu.VMEM((steps, HG_WIDTH, nb), F32),
            pltpu.VMEM((steps, HG_WIDTH, nb), F32),
            pltpu.VMEM((steps, HG_WIDTH, nb), F32),
            pltpu.VMEM((HG_WIDTH, nb), F32),
            pltpu.VMEM((steps, nb, HG_WIDTH), F32),
            pltpu.VMEM((steps, HG_WIDTH, nb), F32),
        ],
        compiler_params=pltpu.CompilerParams(dimension_semantics=("arbitrary",), vmem_limit_bytes=VMEM_LIMIT),
        name="sample_mixer",
    )(x, mod, norm.reshape(DEPTH, 1, D_MODEL), w_in, w_out, pool_bd, pool_scale.reshape(DEPTH, 1, POOL_WIDTH),
      lbs.reshape(DEPTH, 1, HG_WIDTH), hg_norm, bms, sink_col, ones_blk, pool_hist, hgrn_state, cache_kt, cache_vt,
      *stacks)


FFN_TILE = 512
MIXER_TILE = 256
SAMPLE_GROUP = 8
PAST_LEN = 8192


def kernel(x_prompt, x_sample, c_prompt, c_sample, state_pool, state_hgrn, cache_k_win, cache_v_win, norm_ffn1, norm_mix, norm_ffn2, w_mod, b_mod, ffn1_w_gate, ffn1_w_up, ffn1_w_down, w_in, w_out, pool_w, pool_scale, hgrn_lower, hgrn_norm, attn_sinks, rel_bias, ffn2_w_gate, ffn2_w_up, ffn2_w_down, norm_final):
    bsz, seq, _ = x_prompt.shape
    nb, steps, _ = x_sample.shape
    kvw = KV_HEADS * HEAD_DIM

    mod = _modulation(jnp.concatenate([c_sample, c_prompt], axis=0), w_mod, b_mod)
    prompt_mod_block = nb // 8
    lbs = _lower_bounds(hgrn_lower)
    maps_p = np.stack([_bucket_map(WINDOW, 2 * WINDOW, WINDOW, 0, 2 * WINDOW),
                       _bucket_map(WINDOW, 2 * WINDOW, WINDOW, WINDOW, 2 * WINDOW)])
    bm_p = _bias_tables(jnp.asarray(maps_p), rel_bias)
    maps_s = _bucket_map(8, 2 * WINDOW, steps, 0, WINDOW + steps)[None]
    bm_s = _bias_tables(jnp.asarray(maps_s), rel_bias)[0]
    bm_s = jnp.swapaxes(bm_s, 0, 1).reshape(8 * ATT_HEADS, 2 * WINDOW)[:steps * ATT_HEADS]
    sink_col = jnp.tile(attn_sinks, (1, steps)).reshape(DEPTH, steps * ATT_HEADS, 1)
    consts = _mixer_consts(MIXER_TILE)
    hgn = jnp.tile(hgrn_norm, (1, HG_HEADS)).reshape(DEPTH, 1, HG_WIDTH)
    pool_bd = _pool_block_diag(pool_w)
    w_in_b, w_out_b = w_in.astype(BF16), w_out.astype(BF16)
    f1 = (ffn1_w_gate, ffn1_w_up, ffn1_w_down)
    f2 = (ffn2_w_gate, ffn2_w_up, ffn2_w_down)
    w_ffn = tuple(w[0].astype(BF16) for w in f1)
    s_pool = jnp.transpose(state_pool, (0, 2, 1, 3))
    s_hgrn = jnp.transpose(state_hgrn, (0, 2, 3, 4, 1)).reshape(DEPTH, HG_WIDTH, HG_DK, nb)
    c_k = jnp.transpose(cache_k_win, (0, 1, 3, 4, 2)).reshape(DEPTH, nb, kvw, WINDOW)
    c_v = jnp.transpose(cache_v_win, (0, 1, 3, 4, 2)).reshape(DEPTH, nb, kvw, WINDOW)

    xp = x_prompt.reshape(bsz * seq, D_MODEL)
    xs = jnp.swapaxes(x_sample, 0, 1).reshape(steps * nb, D_MODEL)
    ffn = functools.partial(_ffn, tm=FFN_TILE, seqs_p=bsz, seqs_s=nb)
    st_p, stacks_s = [], None
    for l in range(DEPTH):
        xp, xs, *w_ffn = ffn(xp, xs, mod, norm_ffn1, *w_ffn, f2 + (l,), norm_final, l, 0)
        xp, *sp = _prompt_mixer(xp.reshape(bsz, seq, D_MODEL), mod, norm_mix, w_in_b, w_out_b, pool_bd, pool_scale,
                                lbs, hgn, bm_p, attn_sinks, consts, l, tt=MIXER_TILE, mod_row_block=prompt_mod_block)
        xs, *stacks_s = _sample_mixer(xs.reshape(steps, nb, D_MODEL), mod, norm_mix, w_in_b, w_out_b, pool_bd,
                                      pool_scale, lbs, hgn, bm_s, sink_col, consts[1], s_pool, s_hgrn, c_k, c_v,
                                      stacks_s, l, gb=SAMPLE_GROUP, pos0=PAST_LEN)
        st_p.append(sp)
        last = l == DEPTH - 1
        xp, xs, *w_ffn = ffn(xp.reshape(bsz * seq, D_MODEL), xs.reshape(steps * nb, D_MODEL), mod, norm_ffn2, *w_ffn,
                             None if last else f1 + (l + 1,), norm_final, l, 6, final=last)

    y_prompt = xp.reshape(bsz, seq, D_MODEL)
    y_sample = jnp.swapaxes(xs.reshape(steps, nb, D_MODEL), 0, 1)

    def stacked(states, i, shape):
        return jnp.stack([s[i] for s in states]).reshape((DEPTH,) + shape)

    outs = [stacked(st_p, 0, (bsz, POOL_BUF, POOL_WIDTH)),
            stacked(st_p, 1, (bsz, HG_HEADS, HG_DK, HG_DK)),
            stacked(st_p, 2, (bsz, WINDOW, KV_HEADS, HEAD_DIM)),
            stacked(st_p, 3, (bsz, WINDOW, KV_HEADS, HEAD_DIM)),
            jnp.transpose(stacks_s[0], (0, 2, 1, 3)),
            jnp.transpose(stacks_s[1].reshape(DEPTH, HG_HEADS, HG_DK, HG_DK, nb), (0, 4, 1, 2, 3)),
            jnp.transpose(stacks_s[2].reshape(DEPTH, nb, KV_HEADS, HEAD_DIM, WINDOW), (0, 1, 4, 2, 3)),
            jnp.transpose(stacks_s[3].reshape(DEPTH, nb, KV_HEADS, HEAD_DIM, WINDOW), (0, 1, 4, 2, 3))]
    return (y_prompt, y_sample, *outs)
```

```python
import functools
import math

import numpy as np
import jax
import jax.numpy as jnp
from jax import lax
from jax.experimental import pallas as pl
from jax.experimental.pallas import tpu as pltpu

F32 = jnp.float32
BF16 = jnp.bfloat16

D_MODEL = 1024
DEPTH = 4
N_MOD = 9
D_FF = 2816
EPS = 1e-6
POOL_WIDTH = 256
POOL_WINDOWS = (2, 4, 8, 16)
POOL_GW = 64
POOL_BUF = 15
HG_WIDTH = 256
HG_HEADS = 4
HG_DK = 64
ATT_WIDTH = 512
ATT_HEADS = 8
KV_HEADS = 2
GQA_GROUP = 4
HEAD_DIM = 64
WINDOW = 128
REL_BUCKETS = 32
REL_MAX_DIST = 128
OFF_POOL, OFF_HQ, OFF_HF, OFF_HI, OFF_HG, OFF_AQ, OFF_AK, OFF_AV = 0, 256, 512, 768, 1024, 1280, 1792, 1920
IN_WIDTH = 2048
NEG = -1e30

VMEM_LIMIT = 56 * 1024 * 1024


def _dot(a, b):
    return jnp.dot(a, b, preferred_element_type=F32)


def _dot_nt(a, b):
    return lax.dot_general(a, b, (((1,), (1,)), ((), ())), preferred_element_type=F32)


def _dot_tn(a, b):
    return lax.dot_general(a, b, (((0,), (0,)), ((), ())), preferred_element_type=F32)


def _sigmoid(x):
    return 1.0 / (1.0 + jnp.exp(-x))


def _silu(x):
    return x * _sigmoid(x)


def _norm_mod(x, g, shift, scale):
    r = lax.rsqrt(jnp.mean(x * x, axis=-1, keepdims=True) + EPS)
    return (x * r) * g * (1.0 + scale) + shift


def _split3(x):
    hi = x.astype(BF16)
    r1 = x - hi.astype(F32)
    mid = r1.astype(BF16)
    lo = (r1 - mid.astype(F32)).astype(BF16)
    return hi, mid, lo


def _mod_kernel(c_ref, w_ref, b_ref, o_ref):
    sc = _silu(c_ref[...]).astype(BF16)
    o_ref[...] = _dot(sc, w_ref[...].astype(BF16)) + b_ref[...]


def _modulation(c_all, w_mod, b_mod):
    rows = c_all.shape[0]
    return pl.pallas_call(
        _mod_kernel,
        grid=(DEPTH, N_MOD),
        in_specs=[
            pl.BlockSpec((rows, D_MODEL), lambda l, j: (0, 0)),
            pl.BlockSpec((None, D_MODEL, D_MODEL), lambda l, j: (l, 0, j)),
            pl.BlockSpec((None, None, 1, D_MODEL), lambda l, j: (l, j, 0, 0)),
        ],
        out_specs=pl.BlockSpec((None, None, rows, D_MODEL), lambda l, j: (l, j, 0, 0)),
        out_shape=jax.ShapeDtypeStruct((DEPTH, N_MOD, rows, D_MODEL), F32),
        compiler_params=pltpu.CompilerParams(dimension_semantics=("arbitrary", "arbitrary"),
                                             vmem_limit_bytes=VMEM_LIMIT),
        name="adaln_mod",
    )(c_all, w_mod, b_mod.reshape(DEPTH, N_MOD, 1, D_MODEL))


def _swiglu_tile(x, g_norm, shift, scale, gate, wg_ref, wu_ref, wd_ref, nf_ref, final):
    h = _norm_mod(x, g_norm, shift, scale).astype(BF16)
    a = (_silu(_dot(h, wg_ref[...])) * _dot(h, wu_ref[...])).astype(BF16)
    y = x + (0.5 * gate) * _dot(a, wd_ref[...])
    if final:
        y = (y * lax.rsqrt(jnp.mean(y * y, axis=-1, keepdims=True) + EPS)) * nf_ref[...]
    return y


DOWN_CAST_ROWS = 128


def _ffn_kernel(xp_ref, xs_ref, modp_ref, mods_ref, n_ref, wg_ref, wu_ref, wd_ref, nf_ref, *rest,
                tiles_per_seq, prompt_tiles, sample_reps, final, cast_layer):
    i = pl.program_id(0)
    if cast_layer is None:
        op_ref, os_ref = rest
    else:
        (ng_hbm, nu_hbm, nd_hbm, op_ref, os_ref, og_hbm, ou_hbm, od_hbm,
         sg_ref, su_ref, sd_ref, tg_ref, tu_ref, td_ref, sem) = rest
        up_rows = D_MODEL // prompt_tiles
        down_steps = D_FF // DOWN_CAST_ROWS

        def up_copies(step, into_vmem):
            rows = pl.ds(pl.multiple_of(step * up_rows, up_rows), up_rows)
            if into_vmem:
                return (pltpu.make_async_copy(ng_hbm.at[cast_layer, rows, :], sg_ref, sem.at[0]),
                        pltpu.make_async_copy(nu_hbm.at[cast_layer, rows, :], su_ref, sem.at[1]))
            return (pltpu.make_async_copy(tg_ref, og_hbm.at[rows, :], sem.at[2]),
                    pltpu.make_async_copy(tu_ref, ou_hbm.at[rows, :], sem.at[3]))

        def down_copy(step, into_vmem):
            rows = pl.ds(pl.multiple_of(step * DOWN_CAST_ROWS, DOWN_CAST_ROWS), DOWN_CAST_ROWS)
            if into_vmem:
                return pltpu.make_async_copy(nd_hbm.at[cast_layer, rows, :], sd_ref, sem.at[4])
            return pltpu.make_async_copy(td_ref, od_hbm.at[rows, :], sem.at[5])

        @pl.when(i < prompt_tiles)
        def _():
            for cp in up_copies(i, True):
                cp.start()

        @pl.when(i < down_steps)
        def _():
            down_copy(i, True).start()

    @pl.when(i < prompt_tiles)
    def _():
        row = i // tiles_per_seq
        shift, scale, gate = (modp_ref[j, pl.ds(row, 1), :] for j in range(3))
        op_ref[...] = _swiglu_tile(xp_ref[...], n_ref[...], shift, scale, gate, wg_ref, wu_ref, wd_ref, nf_ref, final)

    @pl.when(i == prompt_tiles)
    def _():
        shift, scale, gate = (jnp.concatenate([mods_ref[j]] * sample_reps, axis=0) for j in range(3))
        os_ref[...] = _swiglu_tile(xs_ref[...], n_ref[...], shift, scale, gate, wg_ref, wu_ref, wd_ref, nf_ref, final)

    if cast_layer is not None:
        @pl.when((i >= 1) & (i <= prompt_tiles))
        def _():
            for cp in up_copies(i - 1, False):
                cp.wait()

        @pl.when((i >= 1) & (i <= down_steps))
        def _():
            down_copy(i - 1, False).wait()

        @pl.when(i < prompt_tiles)
        def _():
            for cp in up_copies(i, True):
                cp.wait()
            tg_ref[...] = sg_ref[...].astype(BF16)
            tu_ref[...] = su_ref[...].astype(BF16)
            for cp in up_copies(i, False):
                cp.start()

        @pl.when(i < down_steps)
        def _():
            down_copy(i, True).wait()
            td_ref[...] = sd_ref[...].astype(BF16)
            down_copy(i, False).start()


def _ffn(xp, xs, mod, norm, wg, wu, wd, next_weights, norm_final, layer, j0, *, tm, seqs_p, seqs_s, final=False):
    rows_p, rows_s = xp.shape[0], xs.shape[0]
    prompt_tiles = rows_p // tm
    cast_layer = None if next_weights is None else next_weights[3]
    kern = functools.partial(_ffn_kernel, tiles_per_seq=(rows_p // seqs_p) // tm, prompt_tiles=prompt_tiles,
                             sample_reps=rows_s // seqs_s, final=final, cast_layer=cast_layer)
    ptile = lambda i: (jnp.minimum(i, prompt_tiles - 1), 0)
    weight = lambda *shape: pl.BlockSpec(shape, lambda i: (0, 0), pipeline_mode=pl.Buffered(1))
    hbm = pl.BlockSpec(memory_space=pl.ANY)
    in_specs = [
        pl.BlockSpec((tm, D_MODEL), ptile),
        pl.BlockSpec((rows_s, D_MODEL), lambda i: (0, 0)),
        pl.BlockSpec((None, 3, seqs_p, D_MODEL), lambda i: (layer, j0 // 3, seqs_s // seqs_p, 0)),
        pl.BlockSpec((None, 3, seqs_s, D_MODEL), lambda i: (layer, j0 // 3, 0, 0)),
        pl.BlockSpec((None, 1, D_MODEL), lambda i: (layer, 0, 0)),
        weight(D_MODEL, D_FF),
        weight(D_MODEL, D_FF),
        weight(D_FF, D_MODEL),
        pl.BlockSpec((1, D_MODEL), lambda i: (0, 0)),
    ]
    out_specs = [pl.BlockSpec((tm, D_MODEL), ptile), pl.BlockSpec((rows_s, D_MODEL), lambda i: (0, 0))]
    out_shape = [jax.ShapeDtypeStruct((rows_p, D_MODEL), F32), jax.ShapeDtypeStruct((rows_s, D_MODEL), F32)]
    operands = [xp, xs, mod, mod, norm.reshape(DEPTH, 1, D_MODEL), wg, wu, wd, norm_final.reshape(1, D_MODEL)]
    scratch = []
    if next_weights is not None:
        assert D_MODEL % prompt_tiles == 0 and (D_MODEL // prompt_tiles) % 16 == 0
        assert D_FF % DOWN_CAST_ROWS == 0 and D_FF // DOWN_CAST_ROWS <= prompt_tiles
        up_rows = D_MODEL // prompt_tiles
        in_specs += [hbm, hbm, hbm]
        operands += list(next_weights[:3])
        out_specs += [hbm, hbm, hbm]
        out_shape += [jax.ShapeDtypeStruct((D_MODEL, D_FF), BF16), jax.ShapeDtypeStruct((D_MODEL, D_FF), BF16),
                      jax.ShapeDtypeStruct((D_FF, D_MODEL), BF16)]
        scratch = [pltpu.VMEM((up_rows, D_FF), F32), pltpu.VMEM((up_rows, D_FF), F32),
                   pltpu.VMEM((DOWN_CAST_ROWS, D_MODEL), F32),
                   pltpu.VMEM((up_rows, D_FF), BF16), pltpu.VMEM((up_rows, D_FF), BF16),
                   pltpu.VMEM((DOWN_CAST_ROWS, D_MODEL), BF16),
                   pltpu.SemaphoreType.DMA((6,))]
    return pl.pallas_call(
        kern,
        grid=(prompt_tiles + 1,),
        in_specs=in_specs,
        out_specs=out_specs,
        out_shape=out_shape,
        scratch_shapes=scratch,
        compiler_params=pltpu.CompilerParams(dimension_semantics=("arbitrary",), vmem_limit_bytes=VMEM_LIMIT),
        name="swiglu_half_step",
    )(*operands)


def _lower_bound_kernel(x_ref, o_ref):
    rows = [x_ref[l:l + 1, :] for l in range(DEPTH)]
    m = functools.reduce(jnp.maximum, rows)
    e = [jnp.exp(r - m) for r in rows]
    s = functools.reduce(lambda a, b: a + b, e)
    sm = [ei / s for ei in e]
    acc = sm[0]
    for l in range(DEPTH):
        if l > 0:
            acc = acc + sm[l]
        o_ref[l:l + 1, :] = acc - sm[0]


def _lower_bounds(hgrn_lower):
    return pl.pallas_call(
        _lower_bound_kernel,
        out_shape=jax.ShapeDtypeStruct((DEPTH, HG_WIDTH), F32),
        name="hgrn_lower_bounds",
    )(hgrn_lower)


def _bucket_map(rows, cols, valid_rows, col_lo, col_hi):
    r = np.arange(rows)[:, None]
    c = np.arange(cols)[None, :]
    rel = c - WINDOW - r
    n = np.maximum(-rel, 0)
    exact = REL_BUCKETS // 2
    nf = np.maximum(n, 1).astype(np.float32)
    large = exact + (np.log(nf / np.float32(exact)) / np.float32(math.log(REL_MAX_DIST / exact))
                     * np.float32(REL_BUCKETS - exact)).astype(np.int32)
    large = np.minimum(large, REL_BUCKETS - 1)
    bucket = np.where(n < exact, n, large)
    valid = (rel <= 0) & (rel > -WINDOW) & (r < valid_rows) & (c >= col_lo) & (c < col_hi)
    return np.where(valid, bucket, -1).astype(np.int32)


def _bias_table_kernel(bkt_ref, rb_ref, o_ref):
    h = pl.program_id(1)
    bkt = bkt_ref[...]
    acc = jnp.full(bkt.shape, NEG, F32)
    for b in range(REL_BUCKETS):
        acc = jnp.where(bkt == b, rb_ref[b, h], acc)
    o_ref[...] = acc


def _bias_tables(bucket_maps, rel_bias):
    m, r, c = bucket_maps.shape
    return pl.pallas_call(
        _bias_table_kernel,
        grid=(m, ATT_HEADS),
        in_specs=[pl.BlockSpec((None, r, c), lambda i, h: (i, 0, 0)),
                  pl.BlockSpec(memory_space=pltpu.SMEM)],
        out_specs=pl.BlockSpec((None, None, r, c), lambda i, h: (i, h, 0, 0)),
        out_shape=jax.ShapeDtypeStruct((m, ATT_HEADS, r, c), F32),
        compiler_params=pltpu.CompilerParams(dimension_semantics=("arbitrary", "arbitrary")),
        name="rel_bias_tables",
    )(bucket_maps, rel_bias)


def _block_ones(n, blk):
    i = np.arange(n)
    return (i[:, None] // blk == i[None, :] // blk)


def _mixer_consts(tt):
    tri = jnp.asarray(np.tril(np.ones((tt, tt), np.float32)), BF16)
    blk = _block_ones(HG_WIDTH, HG_DK)
    return tri, jnp.asarray(blk, BF16), jnp.asarray(blk, F32)


def _pool_block_diag(pool_w):
    out = jnp.zeros((DEPTH, POOL_WIDTH, POOL_WIDTH), BF16)
    for g in range(len(POOL_WINDOWS)):
        sl = slice(g * POOL_GW, (g + 1) * POOL_GW)
        out = out.at[:, sl, sl].set(pool_w[:, g].astype(BF16))
    return out


def _head_masks():
    lane = lax.broadcasted_iota(jnp.int32, (1, HG_WIDTH), 1)
    return [jnp.where((lane >= HG_DK * h) & (lane < HG_DK * (h + 1)), 1.0, 0.0).astype(F32) for h in range(HG_HEADS)]


def _prompt_mixer_kernel(x_ref, xn_ref, mod_ref, n_ref, win_ref, wout_ref, pbd_ref, pscale_ref, lb_ref, hgn_ref,
                         bm_ref, sink_ref, tri_ref, ones_ref, bdm_ref,
                         xo_ref, pool_o, s_o, k_o, v_o,
                         za_ref, zb_ref, ubuf, kbuf, vbuf, st_ref, cat_ref, catt_ref, *, layer, tt, nt, ntiles):
    b = pl.program_id(0)
    p = pl.program_id(1)
    last_pair = pl.num_programs(1) - 1
    nxt = jnp.minimum(b * nt + 2 * p + 2, ntiles - 1)
    bn = nxt // nt
    n_chunks = 8
    cw = IN_WIDTH // n_chunks

    def projector(x_rows, bsel, dst_ref):
        h = _norm_mod(x_rows, n_ref[...], mod_ref[3, pl.ds(bsel, 1), :], mod_ref[4, pl.ds(bsel, 1), :]).astype(BF16)
        state = {"k": 0}

        def step():
            k = state["k"]
            if k < n_chunks:
                dst_ref[:, k * cw:(k + 1) * cw] = _dot(h, win_ref[:, k * cw:(k + 1) * cw])
                state["k"] = k + 1

        return step

    @pl.when((b == 0) & (p == 0))
    def _():
        first_proj = projector(x_ref[0:tt, :], b, za_ref)
        for _ in range(n_chunks):
            first_proj()

    @pl.when(p == 0)
    def _():
        ubuf[0:16, :] = jnp.zeros((16, POOL_WIDTH), F32)
        kbuf[:, 0:WINDOW, :] = jnp.zeros((KV_HEADS, WINDOW, HEAD_DIM), BF16)
        vbuf[:, :, 0:WINDOW] = jnp.zeros((KV_HEADS, HEAD_DIM, WINDOW), BF16)
        st_ref[...] = jnp.zeros((HG_WIDTH, HG_WIDTH), F32)

    gate = mod_ref[5, pl.ds(b, 1), :]

    def mix(z_ref, x, row0, t, side):
        lane = lax.broadcasted_iota(jnp.int32, (1, POOL_WIDTH), 1)

        u = z_ref[:, OFF_POOL:OFF_POOL + POOL_WIDTH]
        ubuf[16:16 + tt, :] = u
        e = ubuf[...]
        s2 = e + pltpu.roll(e, 1, 0)
        s4 = s2 + pltpu.roll(s2, 2, 0)
        s8 = s4 + pltpu.roll(s4, 4, 0)
        s16 = s8 + pltpu.roll(s8, 8, 0)
        sel = jnp.where(lane < 64, s2, jnp.where(lane < 128, s4, jnp.where(lane < 192, s8, s16)))[16:]
        wl = jnp.where(lane < 64, 2.0, jnp.where(lane < 128, 4.0, jnp.where(lane < 192, 8.0, 16.0))).astype(F32)
        pos1 = (t * tt + 1 + lax.broadcasted_iota(jnp.int32, (tt, 1), 0)).astype(F32)
        dpool = sel / jnp.minimum(pos1, wl) - u
        ypool = _dot(dpool.astype(BF16), pbd_ref[...]) * pscale_ref[...]
        cat_ref[:, 0:POOL_WIDTH] = ypool.astype(BF16)
        ubuf[0:16, :] = ubuf[tt:tt + 16, :]
        side()


        vt_all = z_ref[:, OFF_AV:OFF_AV + KV_HEADS * HEAD_DIM].T
        for g in range(KV_HEADS):
            kbuf[g, WINDOW:WINDOW + tt, :] = z_ref[:, OFF_AK + HEAD_DIM * g:OFF_AK + HEAD_DIM * (g + 1)].astype(BF16)
            vbuf[g, :, WINDOW:WINDOW + tt] = vt_all[HEAD_DIM * g:HEAD_DIM * (g + 1)].astype(BF16)
        first = jnp.where(t == 0, 1, 0)
        bands = [(j, g) for j in range(tt // WINDOW) for g in range(KV_HEADS)]
        logits = []
        for j, g in bands:
            r0 = j * WINDOW
            c0 = OFF_AQ + HEAD_DIM * g * GQA_GROUP
            qg = jnp.concatenate([z_ref[r0:r0 + WINDOW, c0 + HEAD_DIM * i:c0 + HEAD_DIM * (i + 1)]
                                  for i in range(GQA_GROUP)], axis=0)
            qg = (qg * (HEAD_DIM ** -0.5)).astype(BF16)
            logits.append(_dot_nt(kbuf[g, r0:r0 + 2 * WINDOW, :], qg))
        side()

        lb = lb_ref[...]
        q = _silu(z_ref[:, OFF_HQ:OFF_HQ + HG_WIDTH])
        f = lb + (1.0 - lb) * _sigmoid(z_ref[:, OFF_HF:OFF_HF + HG_WIDTH])
        k = 1.0 - f
        v = z_ref[:, OFF_HI:OFF_HI + HG_WIDTH]
        vb = v.astype(BF16)
        tri = tri_ref[...]
        hi, mid, lo = _split3(jnp.log(f))
        bcum = _dot(tri, hi) + _dot(tri, mid) + _dot(tri, lo)
        side()
        ones_blk = ones_ref[...]
        hm = _head_masks()

        n16 = tt // 16
        half_shape = (n16, 2, 8, HG_WIDTH)
        q4 = q.reshape(half_shape)
        b4 = bcum.reshape(half_shape)
        c4 = (bcum - jnp.log(k)).reshape(half_shape)
        v4 = v.reshape(half_shape)
        rowi = lax.broadcasted_iota(jnp.int32, (1, 8, 1), 1)
        o_up = jnp.zeros((n16, 8, HG_WIDTH), F32)
        o_dn = jnp.zeros((n16, 8, HG_WIDTH), F32)
        nh = n16 * 8
        for s in range(8):
            cs = c4[:, 0, s:s + 1, :]
            vs = v4[:, 0, s:s + 1, :]
            x_up = jnp.where(rowi >= s, q4[:, 0] * jnp.exp(b4[:, 0] - cs), 0.0)
            x_dn = q4[:, 1] * jnp.exp(b4[:, 1] - cs)
            xs = jnp.concatenate([x_up.reshape(nh, HG_WIDTH), x_dn.reshape(nh, HG_WIDTH)], axis=0)
            r = _dot(xs.astype(BF16), ones_blk)
            o_up = o_up + r[0:nh].reshape(n16, 8, HG_WIDTH) * vs
            o_dn = o_dn + r[nh:2 * nh].reshape(n16, 8, HG_WIDTH) * vs
            if s % 4 == 3:
                side()
        for s in range(8):
            cs = c4[:, 1, s:s + 1, :]
            x_dn = jnp.where(rowi >= s, q4[:, 1] * jnp.exp(b4[:, 1] - cs), 0.0)
            r = _dot(x_dn.reshape(nh, HG_WIDTH).astype(BF16), ones_blk)
            o_dn = o_dn + r.reshape(n16, 8, HG_WIDTH) * v4[:, 1, s:s + 1, :]
            if s % 4 == 3:
                side()
        o_tot = jnp.concatenate([o_up[:, None], o_dn[:, None]], axis=1).reshape(tt, HG_WIDTH)

        blocks = []
        blk = 32
        while blk <= tt:
            blocks += [(i * blk, i * blk + blk // 2, (i + 1) * blk) for i in range(tt // blk)]
            blk *= 2
        scores = []
        for lo_s, mid_s, hi_s in blocks:
            ref = bcum[mid_s - 1:mid_s, :]
            qt = q[mid_s:hi_s] * jnp.exp(bcum[mid_s:hi_s] - ref)
            kt = (k[lo_s:mid_s] * jnp.exp(ref - bcum[lo_s:mid_s])).astype(BF16)
            qs = jnp.concatenate([qt * hm[hh] for hh in range(HG_HEADS)], axis=0).astype(BF16)
            scores.append(_dot_nt(qs, kt).astype(BF16))

        st = st_ref[...]
        qin = (q * jnp.exp(bcum)).astype(BF16)
        o_tot = o_tot + _dot_nt(qin, st.astype(BF16))
        blast = bcum[tt - 1:tt, :]
        kout = (k * jnp.exp(blast - bcum)).astype(BF16)
        st_ref[...] = st * jnp.exp(blast) + _dot_tn(vb, kout) * bdm_ref[...]
        side()

        probs, dens = [], []
        for (j, g), lg_all in zip(bands, logits):
            h0 = g * GQA_GROUP
            bm = bm_ref[first, g] if j == 0 else bm_ref[0, g]
            ps = []
            for i in range(GQA_GROUP):
                cols = slice(i * WINDOW, (i + 1) * WINDOW)
                lg = lg_all[:, cols] + bm[:, cols]
                sink = sink_ref[layer, h0 + i]
                m = jnp.maximum(jnp.max(lg, axis=0, keepdims=True), sink)
                p_ = jnp.exp(lg - m)
                dens.append(jnp.sum(p_, axis=0, keepdims=True) + jnp.exp(sink - m))
                ps.append(p_.astype(BF16))
            probs.append(jnp.concatenate(ps, axis=1))
        for n, (j, g) in enumerate(bands):
            r0 = j * WINDOW
            og = _dot(vbuf[g, :, r0:r0 + 2 * WINDOW], probs[n])
            for i in range(GQA_GROUP):
                hd = g * GQA_GROUP + i
                oh = og[:, i * WINDOW:(i + 1) * WINDOW] / dens[n * GQA_GROUP + i]
                catt_ref[HEAD_DIM * hd:HEAD_DIM * (hd + 1), r0:r0 + WINDOW] = oh.astype(BF16)
        for g in range(KV_HEADS):
            kbuf[g, 0:WINDOW, :] = kbuf[g, tt:tt + WINDOW, :]
            vbuf[g, :, 0:WINDOW] = vbuf[g, :, tt:tt + WINDOW]
        side()

        adds = {}
        for (lo_s, mid_s, hi_s), a in zip(blocks, scores):
            half = mid_s - lo_s
            ov = _dot(a, vb[lo_s:mid_s])
            oi = ov[0:half] * hm[0]
            for hh in range(1, HG_HEADS):
                oi = oi + ov[hh * half:(hh + 1) * half] * hm[hh]
            adds.setdefault(2 * half, []).extend([jnp.zeros((half, HG_WIDTH), F32), oi])
        for parts in adds.values():
            o_tot = o_tot + jnp.concatenate(parts, axis=0)

        sq_hi, sq_mid, _ = _split3(o_tot * o_tot)
        ms = (_dot(sq_hi, ones_blk) + _dot(sq_mid, ones_blk)) * (1.0 / HG_DK)
        y_hg = o_tot * lax.rsqrt(ms + EPS) * hgn_ref[...] * _silu(z_ref[:, OFF_HG:OFF_HG + HG_WIDTH])
        cat_ref[:, POOL_WIDTH:POOL_WIDTH + HG_WIDTH] = y_hg.astype(BF16)

        na = POOL_WIDTH + HG_WIDTH
        y = _dot(cat_ref[...], wout_ref[0:na, :]) + _dot_tn(catt_ref[...], wout_ref[na:D_MODEL, :])
        xo_ref[row0:row0 + tt, :] = x + gate * y

        for _ in range(n_chunks):
            side()

    mix(za_ref, x_ref[0:tt, :], 0, 2 * p, projector(x_ref[tt:2 * tt, :], b, zb_ref))
    mix(zb_ref, x_ref[tt:2 * tt, :], tt, 2 * p + 1, projector(xn_ref[...], bn, za_ref))

    @pl.when(p == last_pair)
    def _():
        pool_o[...] = ubuf[pl.ds(1, POOL_BUF), :]
        s_kv = st_ref[...].T
        for hh in range(HG_HEADS):
            s_o[hh] = s_kv[HG_DK * hh:HG_DK * (hh + 1), HG_DK * hh:HG_DK * (hh + 1)]
        k_o[...] = zb_ref[tt - WINDOW:tt, OFF_AK:OFF_AV]
        v_o[...] = zb_ref[tt - WINDOW:tt, OFF_AV:IN_WIDTH]


def _prompt_mixer(x, mod, norm, w_in, w_out, pool_bd, pool_scale, lbs, hg_norm, bm, sinks, consts, layer, *,
                  tt, mod_row_block):
    bsz, seq, _ = x.shape
    tri, ones_blk, bdm = consts
    nt = seq // tt
    ntiles = bsz * nt
    kern = functools.partial(_prompt_mixer_kernel, layer=layer, tt=tt, nt=nt, ntiles=ntiles)
    full = lambda *shape: pl.BlockSpec(shape, lambda b, p: (0,) * len(shape))
    lsel = lambda *shape: pl.BlockSpec((None,) + shape, lambda b, p: (layer,) + (0,) * len(shape))
    mix_pair = lambda b, p: (b, p, 0)
    mix_seq = lambda *z: (lambda b, p: (b,) + z)

    def next_tile(b, p):
        nxt = jnp.minimum(b * nt + 2 * p + 2, ntiles - 1)
        return (nxt // nt, nxt % nt, 0)

    return pl.pallas_call(
        kern,
        grid=(bsz, nt // 2),
        in_specs=[
            pl.BlockSpec((None, 2 * tt, D_MODEL), mix_pair),
            pl.BlockSpec((None, tt, D_MODEL), next_tile),
            pl.BlockSpec((None, N_MOD, 8, D_MODEL), lambda b, p: (layer, 0, mod_row_block, 0)),
            lsel(1, D_MODEL),
            lsel(D_MODEL, IN_WIDTH),
            lsel(D_MODEL, D_MODEL),
            lsel(POOL_WIDTH, POOL_WIDTH),
            lsel(1, POOL_WIDTH),
            lsel(1, HG_WIDTH),
            lsel(1, HG_WIDTH),
            full(2, KV_HEADS, 2 * WINDOW, GQA_GROUP * WINDOW),
            pl.BlockSpec(memory_space=pltpu.SMEM),
            full(tt, tt),
            full(HG_WIDTH, HG_WIDTH),
            full(HG_WIDTH, HG_WIDTH),
        ],
        out_specs=[
            pl.BlockSpec((None, 2 * tt, D_MODEL), mix_pair),
            pl.BlockSpec((None, POOL_BUF, POOL_WIDTH), mix_seq(0, 0)),
            pl.BlockSpec((None, HG_HEADS, HG_DK, HG_DK), mix_seq(0, 0, 0)),
            pl.BlockSpec((None, WINDOW, KV_HEADS * HEAD_DIM), mix_seq(0, 0)),
            pl.BlockSpec((None, WINDOW, KV_HEADS * HEAD_DIM), mix_seq(0, 0)),
        ],
        out_shape=[
            jax.ShapeDtypeStruct((bsz, seq, D_MODEL), F32),
            jax.ShapeDtypeStruct((bsz, POOL_BUF, POOL_WIDTH), F32),
            jax.ShapeDtypeStruct((bsz, HG_HEADS, HG_DK, HG_DK), F32),
            jax.ShapeDtypeStruct((bsz, WINDOW, KV_HEADS * HEAD_DIM), F32),
            jax.ShapeDtypeStruct((bsz, WINDOW, KV_HEADS * HEAD_DIM), F32),
        ],
        scratch_shapes=[
            pltpu.VMEM((tt, IN_WIDTH), F32),
            pltpu.VMEM((tt, IN_WIDTH), F32),
            pltpu.VMEM((16 + tt, POOL_WIDTH), F32),
            pltpu.VMEM((KV_HEADS, WINDOW + tt, HEAD_DIM), BF16),
            pltpu.VMEM((KV_HEADS, HEAD_DIM, WINDOW + tt), BF16),
            pltpu.VMEM((HG_WIDTH, HG_WIDTH), F32),
            pltpu.VMEM((tt, POOL_WIDTH + HG_WIDTH), BF16),
            pltpu.VMEM((ATT_WIDTH, tt), BF16),
        ],
        compiler_params=pltpu.CompilerParams(dimension_semantics=("arbitrary", "arbitrary"),
                                             vmem_limit_bytes=VMEM_LIMIT),
        name="prompt_mixer",
    )(x, x, mod, norm.reshape(DEPTH, 1, D_MODEL), w_in, w_out, pool_bd, pool_scale.reshape(DEPTH, 1, POOL_WIDTH),
      lbs.reshape(DEPTH, 1, HG_WIDTH), hg_norm, bm, sinks, tri, ones_blk, bdm)


def _sample_mixer_kernel(x_ref, mod_ref, n_ref, win_ref, wout_ref, pbd_ref, pscale_ref, lb_ref, hgn_ref,
                         bms_ref, sinkc_ref, ones_ref, sp_ref, s0_ref, kt_ref, vt_ref,
                         xo_ref, pool_o, s_o, kt_o, vt_o,
                         z_ref, cat_ref, qs_ref, os_ref, kn_ref, vn_ref, qint_ref, koutt_ref, vtt_ref, gt_ref,
                         oin_ref, oacc_ref, *, steps, nb, gb, rb, pos0):
    g = pl.program_id(0)
    last = pl.num_programs(0) - 1
    base = pl.multiple_of(g * gb, gb)
    kvw = KV_HEADS * HEAD_DIM

    def rows(t):
        return slice(t * nb, (t + 1) * nb)

    @pl.when(g == 0)
    def _():
        for t in range(steps):
            h = _norm_mod(x_ref[t], n_ref[...], mod_ref[0], mod_ref[1]).astype(BF16)
            z_ref[rows(t), :] = _dot(h, win_ref[...])
        lane = lax.broadcasted_iota(jnp.int32, (1, POOL_WIDTH), 1)

        ext = [sp_ref[r] for r in range(POOL_BUF)] + [z_ref[rows(t), OFF_POOL:OFF_POOL + POOL_WIDTH] for t in range(steps)]
        for r in range(POOL_BUF):
            pool_o[r] = ext[r + steps]
        sums = {1: ext}
        w = 1
        while w < 16:
            prev = sums[w]
            sums[2 * w] = [None if (r < 2 * w - 1) else prev[r] + prev[r - w] for r in range(len(ext))]
            w *= 2
        dl = []
        for t in range(steps):
            r = POOL_BUF + t
            sel = jnp.where(lane < 64, sums[2][r], jnp.where(lane < 128, sums[4][r], jnp.where(lane < 192, sums[8][r], sums[16][r])))
            cnt = [float(min(pos0 + t + 1, wd)) for wd in POOL_WINDOWS]
            cl = jnp.where(lane < 64, cnt[0], jnp.where(lane < 128, cnt[1], jnp.where(lane < 192, cnt[2], cnt[3]))).astype(F32)
            dl.append(sel / cl - ext[r])
        ypool = _dot(jnp.concatenate(dl, axis=0).astype(BF16), pbd_ref[...]) * pscale_ref[...]
        cat_ref[:, 0:POOL_WIDTH] = ypool

        lb = lb_ref[...]
        qv, kv_, vv, bv = [], [], [], []
        bacc = None
        for t in range(steps):
            qv.append(_silu(z_ref[rows(t), OFF_HQ:OFF_HQ + HG_WIDTH]))
            f = lb + (1.0 - lb) * _sigmoid(z_ref[rows(t), OFF_HF:OFF_HF + HG_WIDTH])
            kv_.append(1.0 - f)
            vv.append(z_ref[rows(t), OFF_HI:OFF_HI + HG_WIDTH])
            bacc = jnp.log(f) if bacc is None else bacc + jnp.log(f)
            bv.append(bacc)
        pairs = [(t, s) for t in range(steps) for s in range(t + 1)]
        xs = jnp.concatenate([qv[t] * kv_[s] * jnp.exp(bv[t] - bv[s]) for t, s in pairs], axis=0).astype(BF16)
        rr = _dot(xs, ones_ref[...])
        o_intra = [None] * steps
        for i, (t, s) in enumerate(pairs):
            term = rr[i * nb:(i + 1) * nb] * vv[s]
            o_intra[t] = term if o_intra[t] is None else o_intra[t] + term
        blast = bv[steps - 1]
        gt_ref[...] = jnp.exp(blast).T
        for t in range(steps):
            oin_ref[t] = o_intra[t]
            qint_ref[t] = (qv[t] * jnp.exp(bv[t])).T
            koutt_ref[t] = (kv_[t] * jnp.exp(blast - bv[t])).T
            vtt_ref[t] = vv[t].T
        oacc_ref[...] = jnp.zeros(oacc_ref.shape, F32)

        zeros64 = jnp.zeros((nb, HEAD_DIM), F32)
        kn_ref[...] = jnp.zeros(kn_ref.shape, F32)
        vn_ref[...] = jnp.zeros(vn_ref.shape, F32)
        for t in range(steps):
            kn_ref[:, t, :] = z_ref[rows(t), OFF_AK:OFF_AK + kvw]
            vn_ref[:, t, :] = z_ref[rows(t), OFF_AV:OFF_AV + kvw]
            for hd in range(ATT_HEADS):
                c0 = OFF_AQ + HEAD_DIM * hd
                qsl = z_ref[rows(t), c0:c0 + HEAD_DIM] * (HEAD_DIM ** -0.5)
                parts = [qsl, zeros64] if hd < GQA_GROUP else [zeros64, qsl]
                qs_ref[:, ATT_HEADS * t + hd, :] = jnp.concatenate(parts, axis=1)

    r0 = pl.multiple_of(g * rb, rb)
    hrow = pl.multiple_of((g * rb // HG_DK) * HG_DK, HG_DK)
    vts = [vtt_ref[t, pl.ds(hrow, HG_DK), :] for t in range(steps)]
    accs = [oacc_ref[t, pl.ds(hrow, HG_DK), :] for t in range(steps)]
    for i in range(rb):
        s0 = s0_ref[i]
        snew = gt_ref[pl.ds(r0 + i, 1), :] * s0
        for t in range(steps):
            accs[t] = accs[t] + qint_ref[t, pl.ds(r0 + i, 1), :] * s0
            snew = snew + koutt_ref[t, pl.ds(r0 + i, 1), :] * vts[t]
        s_o[i] = snew
    for t in range(steps):
        oacc_ref[t, pl.ds(hrow, HG_DK), :] = accs[t]

    bm_old = bms_ref[:, 0:WINDOW]
    bm_new = bms_ref[:, WINDOW:WINDOW + 8]
    sinkc = sinkc_ref[...]
    seqs = range(gb)
    qn = [qs_ref[base + n].astype(BF16) for n in seqs]
    l_old = [_dot(qn[n], kt_ref[n].astype(BF16)) for n in seqs]
    l_new = [_dot_nt(qn[n], kn_ref[base + n].astype(BF16)) for n in seqs]
    def new_columns(ref):
        new = ref[pl.ds(base, gb)].reshape(gb * 8, kvw)
        if gb * 8 < WINDOW:
            new = jnp.concatenate([new, jnp.zeros((WINDOW - gb * 8, kvw), F32)], axis=0)
        return new.T

    knt = new_columns(kn_ref)
    vnt = new_columns(vn_ref)
    for n in seqs:
        kt_o[n] = jnp.concatenate([kt_ref[n][:, steps:], knt[:, 8 * n:8 * n + steps]], axis=1)
        vt_o[n] = jnp.concatenate([vt_ref[n][:, steps:], vnt[:, 8 * n:8 * n + steps]], axis=1)
    p_old, p_new, den = [], [], []
    for n in seqs:
        lo_, ln_ = l_old[n] + bm_old, l_new[n] + bm_new
        m = jnp.maximum(jnp.maximum(jnp.max(lo_, axis=-1, keepdims=True), jnp.max(ln_, axis=-1, keepdims=True)), sinkc)
        po, pn = jnp.exp(lo_ - m), jnp.exp(ln_ - m)
        den.append(jnp.sum(po, axis=-1, keepdims=True) + jnp.sum(pn, axis=-1, keepdims=True) + jnp.exp(sinkc - m))
        p_old.append(po.astype(BF16))
        p_new.append(pn.astype(BF16))
    for n in seqs:
        o = _dot_nt(p_old[n], vt_ref[n].astype(BF16)) + _dot(p_new[n], vn_ref[base + n].astype(BF16))
        os_ref[n] = o / den[n]
    for t in range(steps):
        for hd in range(ATT_HEADS):
            c0 = HEAD_DIM * (hd // GQA_GROUP)
            cc = POOL_WIDTH + HG_WIDTH + HEAD_DIM * hd
            cat_ref[pl.ds(t * nb + base, gb), cc:cc + HEAD_DIM] = os_ref[:, ATT_HEADS * t + hd, c0:c0 + HEAD_DIM]

    @pl.when(g == last)
    def _():
        ones_blk = ones_ref[...]
        for t in range(steps):
            o = oin_ref[t] + oacc_ref[t].T
            sq_hi, sq_mid, _ = _split3(o * o)
            ms = (_dot(sq_hi, ones_blk) + _dot(sq_mid, ones_blk)) * (1.0 / HG_DK)
            gate_t = _silu(z_ref[rows(t), OFF_HG:OFF_HG + HG_WIDTH])
            cat_ref[rows(t), POOL_WIDTH:POOL_WIDTH + HG_WIDTH] = o * lax.rsqrt(ms + EPS) * hgn_ref[...] * gate_t
        for t in range(steps):
            y = _dot(cat_ref[rows(t), :].astype(BF16), wout_ref[...])
            xo_ref[t] = x_ref[t] + mod_ref[2] * y


N_SAMPLE_MIXER_INPUTS = 16


def _sample_mixer_kernel_into(*refs, n_stacks, **kw):
    _sample_mixer_kernel(*refs[:N_SAMPLE_MIXER_INPUTS], *refs[N_SAMPLE_MIXER_INPUTS + n_stacks:], **kw)


def _sample_mixer(x, mod, norm, w_in, w_out, pool_bd, pool_scale, lbs, hg_norm, bms, sink_col, ones_blk,
                  pool_hist, hgrn_state, cache_kt, cache_vt, stacks, layer, *, gb, pos0):
    steps, nb, _ = x.shape
    rows_th = steps * ATT_HEADS
    ngrp = nb // gb
    rb = HG_WIDTH // ngrp
    kvw = KV_HEADS * HEAD_DIM
    stacks = () if stacks is None else tuple(stacks)
    kern = functools.partial(_sample_mixer_kernel_into, n_stacks=len(stacks), steps=steps, nb=nb, gb=gb, rb=rb,
                             pos0=pos0)
    full = lambda *shape: pl.BlockSpec(shape, lambda g: (0,) * len(shape))
    lsel = lambda *shape: pl.BlockSpec((None,) + shape, lambda g: (layer,) + (0,) * len(shape))
    grp = lambda *shape: pl.BlockSpec((None,) + shape, lambda g: (layer, g) + (0,) * (len(shape) - 1))
    return pl.pallas_call(
        kern,
        grid=(ngrp,),
        in_specs=[
            full(steps, nb, D_MODEL),
            pl.BlockSpec((None, 3, nb, D_MODEL), lambda g: (layer, 1, 0, 0)),
            lsel(1, D_MODEL),
            lsel(D_MODEL, IN_WIDTH),
            lsel(D_MODEL, D_MODEL),
            lsel(POOL_WIDTH, POOL_WIDTH),
            lsel(1, POOL_WIDTH),
            lsel(1, HG_WIDTH),
            lsel(1, HG_WIDTH),
            full(rows_th, 2 * WINDOW),
            lsel(rows_th, 1),
            full(HG_WIDTH, HG_WIDTH),
            lsel(POOL_BUF, nb, POOL_WIDTH),
            grp(rb, HG_DK, nb),
            grp(gb, kvw, WINDOW),
            grp(gb, kvw, WINDOW),
        ] + [pl.BlockSpec(memory_space=pl.ANY)] * len(stacks),
        out_specs=[
            full(steps, nb, D_MODEL),
            lsel(POOL_BUF, nb, POOL_WIDTH),
            grp(rb, HG_DK, nb),
            grp(gb, kvw, WINDOW),
            grp(gb, kvw, WINDOW),
        ],
        out_shape=[
            jax.ShapeDtypeStruct((steps, nb, D_MODEL), F32),
            jax.ShapeDtypeStruct((DEPTH, POOL_BUF, nb, POOL_WIDTH), F32),
            jax.ShapeDtypeStruct((DEPTH, HG_WIDTH, HG_DK, nb), F32),
            jax.ShapeDtypeStruct((DEPTH, nb, kvw, WINDOW), F32),
            jax.ShapeDtypeStruct((DEPTH, nb, kvw, WINDOW), F32),
        ],
        input_output_aliases={N_SAMPLE_MIXER_INPUTS + i: 1 + i for i in range(len(stacks))},
        scratch_shapes=[
            pltpu.VMEM((steps * nb, IN_WIDTH), F32),
            pltpu.VMEM((steps * nb, D_MODEL), F32),
            pltpu.VMEM((nb, rows_th, kvw), F32),
            pltpu.VMEM((gb, rows_th, kvw), F32),
            pltpu.VMEM((nb, 8, kvw), F32),
            pltpu.VMEM((nb, 8, kvw), F32),
            pltpu.VMEM((steps, HG_WIDTH, nb), F32),
            pltpu.VMEM((steps, HG_WIDTH, nb), F32),
            pltpu.VMEM((steps, HG_WIDTH, nb), F32),
            pltpu.VMEM((HG_WIDTH, nb), F32),
            pltpu.VMEM((steps, nb, HG_WIDTH), F32),
            pltpu.VMEM((steps, HG_WIDTH, nb), F32),
        ],
        compiler_params=pltpu.CompilerParams(dimension_semantics=("arbitrary",), vmem_limit_bytes=VMEM_LIMIT),
        name="sample_mixer",
    )(x, mod, norm.reshape(DEPTH, 1, D_MODEL), w_in, w_out, pool_bd, pool_scale.reshape(DEPTH, 1, POOL_WIDTH),
      lbs.reshape(DEPTH, 1, HG_WIDTH), hg_norm, bms, sink_col, ones_blk, pool_hist, hgrn_state, cache_kt, cache_vt,
      *stacks)


FFN_TILE = 512
MIXER_TILE = 256
SAMPLE_GROUP = 8
PAST_LEN = 8192


def kernel(x_prompt, x_sample, c_prompt, c_sample, state_pool, state_hgrn, cache_k_win, cache_v_win, norm_ffn1, norm_mix, norm_ffn2, w_mod, b_mod, ffn1_w_gate, ffn1_w_up, ffn1_w_down, w_in, w_out, pool_w, pool_scale, hgrn_lower, hgrn_norm, attn_sinks, rel_bias, ffn2_w_gate, ffn2_w_up, ffn2_w_down, norm_final):
    bsz, seq, _ = x_prompt.shape
    nb, steps, _ = x_sample.shape
    kvw = KV_HEADS * HEAD_DIM

    mod = _modulation(jnp.concatenate([c_sample, c_prompt], axis=0), w_mod, b_mod)
    prompt_mod_block = nb // 8
    lbs = _lower_bounds(hgrn_lower)
    maps_p = np.stack([_bucket_map(WINDOW, 2 * WINDOW, WINDOW, 0, 2 * WINDOW),
                       _bucket_map(WINDOW, 2 * WINDOW, WINDOW, WINDOW, 2 * WINDOW)])
    bm_p = _bias_tables(jnp.asarray(maps_p), rel_bias)
    bm_p = jnp.swapaxes(bm_p.reshape(2, KV_HEADS, GQA_GROUP * WINDOW, 2 * WINDOW), 2, 3)
    maps_s = _bucket_map(8, 2 * WINDOW, steps, 0, WINDOW + steps)[None]
    bm_s = _bias_tables(jnp.asarray(maps_s), rel_bias)[0]
    bm_s = jnp.swapaxes(bm_s, 0, 1).reshape(8 * ATT_HEADS, 2 * WINDOW)[:steps * ATT_HEADS]
    sink_col = jnp.tile(attn_sinks, (1, steps)).reshape(DEPTH, steps * ATT_HEADS, 1)
    consts = _mixer_consts(MIXER_TILE)
    hgn = jnp.tile(hgrn_norm, (1, HG_HEADS)).reshape(DEPTH, 1, HG_WIDTH)
    pool_bd = _pool_block_diag(pool_w)
    w_in_b, w_out_b = w_in.astype(BF16), w_out.astype(BF16)
    f1 = (ffn1_w_gate, ffn1_w_up, ffn1_w_down)
    f2 = (ffn2_w_gate, ffn2_w_up, ffn2_w_down)
    w_ffn = tuple(w[0].astype(BF16) for w in f1)
    s_pool = jnp.transpose(state_pool, (0, 2, 1, 3))
    s_hgrn = jnp.transpose(state_hgrn, (0, 2, 3, 4, 1)).reshape(DEPTH, HG_WIDTH, HG_DK, nb)
    c_k = jnp.transpose(cache_k_win, (0, 1, 3, 4, 2)).reshape(DEPTH, nb, kvw, WINDOW)
    c_v = jnp.transpose(cache_v_win, (0, 1, 3, 4, 2)).reshape(DEPTH, nb, kvw, WINDOW)

    xp = x_prompt.reshape(bsz * seq, D_MODEL)
    xs = jnp.swapaxes(x_sample, 0, 1).reshape(steps * nb, D_MODEL)
    ffn = functools.partial(_ffn, tm=FFN_TILE, seqs_p=bsz, seqs_s=nb)
    st_p, stacks_s = [], None
    for l in range(DEPTH):
        xp, xs, *w_ffn = ffn(xp, xs, mod, norm_ffn1, *w_ffn, f2 + (l,), norm_final, l, 0)
        xp, *sp = _prompt_mixer(xp.reshape(bsz, seq, D_MODEL), mod, norm_mix, w_in_b, w_out_b, pool_bd, pool_scale,
                                lbs, hgn, bm_p, attn_sinks, consts, l, tt=MIXER_TILE, mod_row_block=prompt_mod_block)
        xs, *stacks_s = _sample_mixer(xs.reshape(steps, nb, D_MODEL), mod, norm_mix, w_in_b, w_out_b, pool_bd,
                                      pool_scale, lbs, hgn, bm_s, sink_col, consts[1], s_pool, s_hgrn, c_k, c_v,
                                      stacks_s, l, gb=SAMPLE_GROUP, pos0=PAST_LEN)
        st_p.append(sp)
        last = l == DEPTH - 1
        xp, xs, *w_ffn = ffn(xp.reshape(bsz * seq, D_MODEL), xs.reshape(steps * nb, D_MODEL), mod, norm_ffn2, *w_ffn,
                             None if last else f1 + (l + 1,), norm_final, l, 6, final=last)

    y_prompt = xp.reshape(bsz, seq, D_MODEL)
    y_sample = jnp.swapaxes(xs.reshape(steps, nb, D_MODEL), 0, 1)

    def stacked(states, i, shape):
        return jnp.stack([s[i] for s in states]).reshape((DEPTH,) + shape)

    outs = [stacked(st_p, 0, (bsz, POOL_BUF, POOL_WIDTH)),
            stacked(st_p, 1, (bsz, HG_HEADS, HG_DK, HG_DK)),
            stacked(st_p, 2, (bsz, WINDOW, KV_HEADS, HEAD_DIM)),
            stacked(st_p, 3, (bsz, WINDOW, KV_HEADS, HEAD_DIM)),
            jnp.transpose(stacks_s[0], (0, 2, 1, 3)),
            jnp.transpose(stacks_s[1].reshape(DEPTH, HG_HEADS, HG_DK, HG_DK, nb), (0, 4, 1, 2, 3)),
            jnp.transpose(stacks_s[2].reshape(DEPTH, nb, KV_HEADS, HEAD_DIM, WINDOW), (0, 1, 4, 2, 3)),
            jnp.transpose(stacks_s[3].reshape(DEPTH, nb, KV_HEADS, HEAD_DIM, WINDOW), (0, 1, 4, 2, 3))]
    return (y_prompt, y_sample, *outs)
```
